```python
import jax, jax.numpy as jnp
from jax import lax
import numpy as np

D_MODEL = 2048
BATCH = 4
SEQ = 2048
DEPTH = 1
DEC_BATCH = 128
DEC_SEQ = 4
PAST_LEN = 16384
PAGE_SIZE = 128

N_META = 16
CONV_W = 4
D_LRU = D_MODEL
LRU_HEADS = 16
LRU_HEAD_DIM = D_LRU // LRU_HEADS
LRU_C = 8.0
D_SSD = D_MODEL
SSD_HEAD_DIM = 64
SSD_HEADS = D_SSD // SSD_HEAD_DIM
SSD_GROUPS = 4
SSD_STATE = 128
SSD_CHUNK = 128
D_BC = SSD_GROUPS * SSD_STATE
D_CONV_SSD = D_SSD + 2 * D_BC
D_MIX = D_LRU + D_SSD
D_IN_PROJ = 2 * D_LRU + D_SSD + D_CONV_SSD + SSD_HEADS
D_FF = 4 * D_MODEL
EPS = 1e-6

kernel_name = "hymba_rglru_ssd_decoder_step"


def rmsnorm(x, g):
    xf = x.astype(jnp.float32)
    y = xf * lax.rsqrt(jnp.mean(xf * xf, axis=-1, keepdims=True) + EPS)
    return (y * g.astype(jnp.float32)).astype(x.dtype)


def causal_conv(u, w, b, buf):
    L = u.shape[1]
    ext = jnp.concatenate([buf.astype(u.dtype), u], axis=1)
    out = b + ext[:, 0:L] * w[0]
    for k in range(1, CONV_W):
        out = out + ext[:, k:k + L] * w[k]
    return out, ext[:, L:]


def _lin_combine(e1, e2):
    a1, b1 = e1
    a2, b2 = e2
    return a1 * a2, a2 * b1 + b2


def rglru(xc, w_a, b_a, w_x, b_x, lam, h0, reset_first):
    Bn, L, _ = xc.shape
    xf = xc.astype(jnp.float32)
    xh = xf.reshape(Bn, L, LRU_HEADS, LRU_HEAD_DIM)
    r = jax.nn.sigmoid(jnp.einsum('blhi,hij->blhj', xh, w_a.astype(jnp.float32)) + b_a.astype(jnp.float32)).reshape(Bn, L, D_LRU)
    i = jax.nn.sigmoid(jnp.einsum('blhi,hij->blhj', xh, w_x.astype(jnp.float32)) + b_x.astype(jnp.float32)).reshape(Bn, L, D_LRU)
    log_a = -LRU_C * r * jax.nn.softplus(-lam.astype(jnp.float32))
    a = jnp.exp(log_a)
    mult = jnp.sqrt(-jnp.expm1(2.0 * log_a))
    if reset_first:
        mult = mult.at[:, 0].set(1.0)
    bt = mult * i * xf
    bt = bt.at[:, 0].add(a[:, 0] * h0.astype(jnp.float32))
    _, h = lax.associative_scan(_lin_combine, (a, bt), axis=1)
    return h, h[:, -1]


def ssd_chunked(x, dt, A, Bm, Cm, h0, chunk):
    b, L, H, P = x.shape
    G, N = Bm.shape[2], Bm.shape[3]
    R = H // G
    nc = L // chunk
    xr = x.reshape(b, nc, chunk, G, R, P)
    dtc = dt.reshape(b, nc, chunk, H)
    Bc = Bm.reshape(b, nc, chunk, G, N)
    Cc = Cm.reshape(b, nc, chunk, G, N)
    acum = jnp.cumsum(dtc * A, axis=2)
    seg = acum[:, :, :, None, :] - acum[:, :, None, :, :]
    causal = jnp.tril(jnp.ones((chunk, chunk), dtype=bool))
    lmat = jnp.exp(jnp.where(causal[None, None, :, :, None], seg, -jnp.inf))
    lmat = (lmat * dtc[:, :, None, :, :]).reshape(b, nc, chunk, chunk, G, R)
    cb = jnp.einsum('bcign,bcjgn->bcgij', Cc, Bc)
    m = jnp.einsum('bcgij,bcijgr->bcijgr', cb, lmat)
    y_diag = jnp.einsum('bcijgr,bcjgrp->bcigrp', m, xr)
    w_end = (jnp.exp(acum[:, :, -1:, :] - acum) * dtc).reshape(b, nc, chunk, G, R)
    states = jnp.einsum('bcjgn,bcjgrp->bcgrpn', Bc, xr * w_end[..., None]).reshape(b, nc, H, P, N)
    chunk_decay = jnp.exp(acum[:, :, -1, :])

    def step(h, inp):
        st, dec = inp
        return dec[:, :, None, None] * h + st, h

    h_final, h_starts = lax.scan(step, h0.astype(jnp.float32),
                                 (jnp.swapaxes(states, 0, 1), jnp.swapaxes(chunk_decay, 0, 1)))
    h_starts = jnp.swapaxes(h_starts, 0, 1).reshape(b, nc, G, R, P, N)
    y_off = jnp.einsum('bcign,bcgrpn->bcigrp', Cc, h_starts) * jnp.exp(acum).reshape(b, nc, chunk, G, R)[..., None]
    return (y_diag + y_off).reshape(b, L, H, P), h_final


def mixer(h, lru_buf, lru_h0, ssd_buf, ssd_h0, w_in, conv_lru_w, conv_lru_b, lru_wa, lru_ba, lru_wx, lru_bx,
          lru_lambda, g_lru_out, conv_ssd_w, conv_ssd_b, dt_bias, a_log, d_skip, g_ssd_out, w_out,
          reset_first, pad, chunk):
    Bn, L, _ = h.shape
    proj = h @ w_in
    o1 = D_LRU
    o2 = o1 + D_LRU
    o3 = o2 + D_SSD
    o4 = o3 + D_CONV_SSD
    gate_lru, x_lru, z, xbc_raw, dt_raw = proj[..., :o1], proj[..., o1:o2], proj[..., o2:o3], proj[..., o3:o4], proj[..., o4:]
    xl, lru_buf_new = causal_conv(x_lru, conv_lru_w, conv_lru_b, lru_buf)
    hl, lru_h_new = rglru(xl, lru_wa, lru_ba, lru_wx, lru_bx, lru_lambda, lru_h0, reset_first)
    lru_out = rmsnorm(hl * jax.nn.gelu(gate_lru.astype(jnp.float32)), g_lru_out)
    xbc, ssd_buf_new = causal_conv(xbc_raw, conv_ssd_w, conv_ssd_b, ssd_buf)
    xbc = jax.nn.silu(xbc.astype(jnp.float32))
    xs = xbc[..., :D_SSD].reshape(Bn, L, SSD_HEADS, SSD_HEAD_DIM)
    Bm = xbc[..., D_SSD:D_SSD + D_BC].reshape(Bn, L, SSD_GROUPS, SSD_STATE)
    Cm = xbc[..., D_SSD + D_BC:].reshape(Bn, L, SSD_GROUPS, SSD_STATE)
    dt = jax.nn.softplus(dt_raw.astype(jnp.float32) + dt_bias.astype(jnp.float32))
    A = -jnp.exp(a_log.astype(jnp.float32))
    if pad > 0:
        xs_p = jnp.pad(xs, ((0, 0), (pad, 0), (0, 0), (0, 0)))
        dt_p = jnp.pad(dt, ((0, 0), (pad, 0), (0, 0)))
        B_p = jnp.pad(Bm, ((0, 0), (pad, 0), (0, 0), (0, 0)))
        C_p = jnp.pad(Cm, ((0, 0), (pad, 0), (0, 0), (0, 0)))
    else:
        xs_p, dt_p, B_p, C_p = xs, dt, Bm, Cm
    y, ssd_h_new = ssd_chunked(xs_p, dt_p, A, B_p, C_p, ssd_h0, chunk)
    y = y[:, pad:] + d_skip.astype(jnp.float32)[:, None] * xs
    yg = y.reshape(Bn, L, D_SSD) * jax.nn.silu(z.astype(jnp.float32))
    yg = yg.reshape(Bn, L, SSD_GROUPS, D_SSD // SSD_GROUPS)
    yg = yg * lax.rsqrt(jnp.mean(yg * yg, axis=-1, keepdims=True) + EPS)
    ssd_out = (yg.reshape(Bn, L, D_SSD) * g_ssd_out.astype(jnp.float32)).astype(h.dtype)
    mix = jnp.concatenate([lru_out.astype(h.dtype), ssd_out], axis=-1) @ w_out
    return mix, lru_h_new, lru_buf_new, ssd_h_new, ssd_buf_new


def setup_inputs(seed: int = 0) -> dict:
    key = jax.random.key(seed)
    ks = jax.random.split(key, 32)
    f32 = jnp.float32

    def nrm(k, shape, scale):
        return jax.random.normal(k, shape, f32) * scale

    a0 = jax.random.uniform(ks[10], (DEPTH, D_LRU), f32, 0.9, 0.999)
    s = a0 ** (1.0 / LRU_C)
    lru_lambda = jnp.log(s) - jnp.log1p(-s)
    dt0 = jnp.exp(jax.random.uniform(ks[14], (DEPTH, SSD_HEADS), f32, np.log(1e-3), np.log(1e-1)))
    dt_bias = dt0 + jnp.log(-jnp.expm1(-dt0))
    return {
        "x_prompt": nrm(ks[0], (BATCH, SEQ, D_MODEL), 1.0),
        "x_sample": nrm(ks[1], (DEC_BATCH, DEC_SEQ, D_MODEL), 1.0),
        "state_lru_h": nrm(ks[2], (DEPTH, DEC_BATCH, D_LRU), 0.5),
        "state_lru_conv": nrm(ks[3], (DEPTH, DEC_BATCH, CONV_W - 1, D_LRU), 1.0),
        "state_ssd": nrm(ks[4], (DEPTH, DEC_BATCH, SSD_HEADS, SSD_HEAD_DIM, SSD_STATE), 0.1),
        "state_ssd_conv": nrm(ks[5], (DEPTH, DEC_BATCH, CONV_W - 1, D_CONV_SSD), 1.0),
        "meta_tokens": nrm(ks[6], (N_META, D_MODEL), 1.0),
        "g_mix": 1.0 + nrm(ks[7], (DEPTH, D_MODEL), 0.02),
        "w_in": nrm(ks[8], (DEPTH, D_MODEL, D_IN_PROJ), D_MODEL ** -0.5),
        "conv_lru_w": nrm(ks[9], (DEPTH, CONV_W, D_LRU), CONV_W ** -0.5),
        "conv_lru_b": nrm(ks[11], (DEPTH, D_LRU), 0.02),
        "lru_wa": nrm(ks[12], (DEPTH, LRU_HEADS, LRU_HEAD_DIM, LRU_HEAD_DIM), LRU_HEAD_DIM ** -0.5),
        "lru_ba": nrm(ks[13], (DEPTH, LRU_HEADS, LRU_HEAD_DIM), 0.02),
        "lru_wx": nrm(ks[15], (DEPTH, LRU_HEADS, LRU_HEAD_DIM, LRU_HEAD_DIM), LRU_HEAD_DIM ** -0.5),
        "lru_bx": nrm(ks[16], (DEPTH, LRU_HEADS, LRU_HEAD_DIM), 0.02),
        "lru_lambda": lru_lambda,
        "g_lru_out": 1.0 + nrm(ks[17], (DEPTH, D_LRU), 0.02),
        "conv_ssd_w": nrm(ks[18], (DEPTH, CONV_W, D_CONV_SSD), CONV_W ** -0.5),
        "conv_ssd_b": nrm(ks[19], (DEPTH, D_CONV_SSD), 0.02),
        "dt_bias": dt_bias,
        "a_log": jnp.log(jax.random.uniform(ks[20], (DEPTH, SSD_HEADS), f32, 1.0, 16.0)),
        "d_skip": 1.0 + nrm(ks[21], (DEPTH, SSD_HEADS), 0.1),
        "g_ssd_out": 1.0 + nrm(ks[22], (DEPTH, D_SSD), 0.02),
        "w_out": nrm(ks[23], (DEPTH, D_MIX, D_MODEL), D_MIX ** -0.5),
        "g_mlp": 1.0 + nrm(ks[24], (DEPTH, D_MODEL), 0.02),
        "w_up": nrm(ks[25], (DEPTH, D_MODEL, D_FF), D_MODEL ** -0.5),
        "w_down": nrm(ks[26], (DEPTH, D_FF, D_MODEL), D_FF ** -0.5),
        "g_final": 1.0 + nrm(ks[27], (D_MODEL,), 0.02),
    }


def reference(x_prompt, x_sample, state_lru_h, state_lru_conv, state_ssd, state_ssd_conv, meta_tokens,
              g_mix, w_in, conv_lru_w, conv_lru_b, lru_wa, lru_ba, lru_wx, lru_bx, lru_lambda, g_lru_out,
              conv_ssd_w, conv_ssd_b, dt_bias, a_log, d_skip, g_ssd_out, w_out, g_mlp, w_up, w_down, g_final):
    Bp = x_prompt.shape[0]
    xp = jnp.concatenate([jnp.broadcast_to(meta_tokens.astype(x_prompt.dtype)[None], (Bp, N_META, D_MODEL)), x_prompt], axis=1)
    Lp = xp.shape[1]
    pad_p = (-Lp) % SSD_CHUNK
    xs = x_sample
    p_states = ([], [], [], [])
    s_states = ([], [], [], [])
    for l in range(DEPTH):
        w = (w_in[l], conv_lru_w[l], conv_lru_b[l], lru_wa[l], lru_ba[l], lru_wx[l], lru_bx[l], lru_lambda[l],
             g_lru_out[l], conv_ssd_w[l], conv_ssd_b[l], dt_bias[l], a_log[l], d_skip[l], g_ssd_out[l], w_out[l])
        mix_p, lh, lb, sh, sb = mixer(
            rmsnorm(xp, g_mix[l]),
            jnp.zeros((Bp, CONV_W - 1, D_LRU), xp.dtype), jnp.zeros((Bp, D_LRU), jnp.float32),
            jnp.zeros((Bp, CONV_W - 1, D_CONV_SSD), xp.dtype),
            jnp.zeros((Bp, SSD_HEADS, SSD_HEAD_DIM, SSD_STATE), jnp.float32),
            *w, reset_first=True, pad=pad_p, chunk=SSD_CHUNK)
        for lst, v in zip(p_states, (lh, lb, sh, sb)):
            lst.append(v)
        xp = xp + mix_p
        hp = rmsnorm(xp, g_mlp[l])
        xp = xp + jnp.square(jax.nn.relu(hp @ w_up[l])) @ w_down[l]
        mix_s, lh, lb, sh, sb = mixer(
            rmsnorm(xs, g_mix[l]), state_lru_conv[l], state_lru_h[l], state_ssd_conv[l], state_ssd[l],
            *w, reset_first=False, pad=0, chunk=xs.shape[1])
        for lst, v in zip(s_states, (lh, lb, sh, sb)):
            lst.append(v)
        xs = xs + mix_s
        hs = rmsnorm(xs, g_mlp[l])
        xs = xs + jnp.square(jax.nn.relu(hs @ w_up[l])) @ w_down[l]
    y_prompt = rmsnorm(xp, g_final)[:, N_META:]
    y_sample = rmsnorm(xs, g_final)
    dts = (state_lru_h.dtype, state_lru_conv.dtype, state_ssd.dtype, state_ssd_conv.dtype)
    p_lru_h, p_lru_conv, p_ssd, p_ssd_conv = [jnp.stack(v, axis=0).astype(d) for v, d in zip(p_states, dts)]
    s_lru_h, s_lru_conv, s_ssd, s_ssd_conv = [jnp.stack(v, axis=0).astype(d) for v, d in zip(s_states, dts)]
    return (y_prompt, y_sample, p_lru_h, p_lru_conv, p_ssd, p_ssd_conv, s_lru_h, s_lru_conv, s_ssd, s_ssd_conv)
```

```python
import functools

import jax
import jax.numpy as jnp
from jax import lax
from jax.experimental import pallas as pl
from jax.experimental.pallas import tpu as pltpu

F32 = jnp.float32
BF16 = jnp.bfloat16

EPS = 1e-6
LRU_C = 8.0
CONV_TAPS = 4

LANES = 128
SUBLANES = 8
VMEM_BYTES_V7X = 64 * 1024 * 1024
VMEM_TEMP_BYTES = 10 * 1024 * 1024
VMEM_CEILING_BYTES = VMEM_BYTES_V7X - 6 * 1024 * 1024

HALO = SUBLANES
SSD_CHUNK = 128
HEAD_PAD = LANES
SLAB_ROWS = SUBLANES


def _nbytes(shape, dtype):
    n = 1
    for s in shape:
        n *= s
    return n * jnp.dtype(dtype).itemsize


def _vmem_limit(pipelined, resident):
    est = 2 * sum(_nbytes(s, d) for s, d in pipelined) + sum(_nbytes(s, d) for s, d in resident)
    return int(min(est + VMEM_TEMP_BYTES, VMEM_CEILING_BYTES))


def _pick_tile(m, prefs):
    for t in prefs:
        if m % t == 0:
            return t
    return m


def _sigmoid(x):
    return 0.5 * jnp.tanh(0.5 * x) + 0.5


def _silu(x):
    return x * _sigmoid(x)


def _split3(x):
    hi = x.astype(BF16)
    r1 = x - hi.astype(F32)
    mid = r1.astype(BF16)
    lo = (r1 - mid.astype(F32)).astype(BF16)
    return jnp.concatenate([hi, mid, lo], axis=1)


def _expand_heads(x, e3_ref):
    return jnp.dot(_split3(x), e3_ref[...], preferred_element_type=F32)


def _rmsnorm_rows(x, g):
    ms = jnp.mean(x * x, axis=-1, keepdims=True)
    return x * lax.rsqrt(ms + EPS) * g


def _conv_taps(ext, s, cw_ref, cb_ref, sl):
    v = cb_ref[:, sl] + ext[s] * cw_ref[0:1, sl]
    for k in range(1, CONV_TAPS):
        v = v + ext[s + k] * cw_ref[k:k + 1, sl]
    return v


def _in_proj_kernel(x_ref, g_ref, w_ref, wdt_ref, o_ref, dt_ref, xn_ref):
    @pl.when(pl.program_id(1) == 0)
    def _():
        xn = _rmsnorm_rows(x_ref[...], g_ref[...]).astype(BF16)
        xn_ref[...] = xn
        dt_ref[...] = jnp.dot(xn, wdt_ref[...], preferred_element_type=F32)

    o_ref[...] = jnp.dot(xn_ref[...], w_ref[...], preferred_element_type=F32)


def _in_proj(x, g, w, wdt, *, tm, tn):
    m, k = x.shape
    nw = w.shape[1]
    assert m % tm == 0 and nw % tn == 0
    pipelined = [((tm, k), F32), ((k, tn), BF16), ((k, HEAD_PAD), BF16), ((tm, tn), F32), ((tm, HEAD_PAD), F32)]
    return pl.pallas_call(
        _in_proj_kernel,
        out_shape=(jax.ShapeDtypeStruct((m, nw), F32), jax.ShapeDtypeStruct((m, HEAD_PAD), F32)),
        grid=(m // tm, nw // tn),
        in_specs=[
            pl.BlockSpec((tm, k), lambda i, j: (i, 0)),
            pl.BlockSpec((1, k), lambda i, j: (0, 0)),
            pl.BlockSpec((k, tn), lambda i, j: (0, j)),
            pl.BlockSpec((k, HEAD_PAD), lambda i, j: (0, 0)),
        ],
        out_specs=(
            pl.BlockSpec((tm, tn), lambda i, j: (i, j)),
            pl.BlockSpec((tm, HEAD_PAD), lambda i, j: (i, 0)),
        ),
        scratch_shapes=[pltpu.VMEM((tm, k), BF16)],
        compiler_params=pltpu.CompilerParams(
            dimension_semantics=("parallel", "arbitrary"),
            vmem_limit_bytes=_vmem_limit(pipelined, [((tm, k), BF16)])),
        name="in_proj",
    )(x, g, w, wdt)


def _out_proj_kernel(a1_ref, a2_ref, w1_ref, w2_ref, res_ref, o_ref):
    acc = jnp.dot(a1_ref[...], w1_ref[...], preferred_element_type=F32)
    acc = acc + jnp.dot(a2_ref[...], w2_ref[...], preferred_element_type=F32)
    o_ref[...] = res_ref[...] + acc


def _out_proj(a1, a2, w1, w2, res, *, tm, tn):
    m, k1 = a1.shape
    k2 = a2.shape[1]
    n = w1.shape[1]
    assert m % tm == 0 and n % tn == 0
    pipelined = [((tm, k1), BF16), ((tm, k2), BF16), ((k1, tn), BF16), ((k2, tn), BF16), ((tm, tn), F32), ((tm, tn), F32)]
    return pl.pallas_call(
        _out_proj_kernel,
        out_shape=jax.ShapeDtypeStruct((m, n), F32),
        grid=(m // tm, n // tn),
        in_specs=[
            pl.BlockSpec((tm, k1), lambda i, j: (i, 0)),
            pl.BlockSpec((tm, k2), lambda i, j: (i, 0)),
            pl.BlockSpec((k1, tn), lambda i, j: (0, j)),
            pl.BlockSpec((k2, tn), lambda i, j: (0, j)),
            pl.BlockSpec((tm, tn), lambda i, j: (i, j)),
        ],
        out_specs=pl.BlockSpec((tm, tn), lambda i, j: (i, j)),
        compiler_params=pltpu.CompilerParams(
            dimension_semantics=("parallel", "parallel"),
            vmem_limit_bytes=_vmem_limit(pipelined, [])),
        name="out_proj",
    )(a1, a2, w1, w2, res)


def _mlp_up_kernel(x_ref, g_ref, w_ref, o_ref, xn_ref):
    @pl.when(pl.program_id(1) == 0)
    def _():
        xn_ref[...] = _rmsnorm_rows(x_ref[...], g_ref[...]).astype(BF16)

    acc = jnp.dot(xn_ref[...], w_ref[...], preferred_element_type=F32)
    o_ref[...] = jnp.square(jnp.maximum(acc, 0.0)).astype(o_ref.dtype)


def _mlp_up(x, g, w, *, tm, tn):
    m, k = x.shape
    n = w.shape[1]
    assert m % tm == 0 and n % tn == 0
    pipelined = [((tm, k), F32), ((k, tn), BF16), ((tm, tn), BF16)]
    return pl.pallas_call(
        _mlp_up_kernel,
        out_shape=jax.ShapeDtypeStruct((m, n), BF16),
        grid=(m // tm, n // tn),
        in_specs=[
            pl.BlockSpec((tm, k), lambda i, j: (i, 0)),
            pl.BlockSpec((1, k), lambda i, j: (0, 0)),
            pl.BlockSpec((k, tn), lambda i, j: (0, j)),
        ],
        out_specs=pl.BlockSpec((tm, tn), lambda i, j: (i, j)),
        scratch_shapes=[pltpu.VMEM((tm, k), BF16)],
        compiler_params=pltpu.CompilerParams(
            dimension_semantics=("parallel", "arbitrary"),
            vmem_limit_bytes=_vmem_limit(pipelined, [((tm, k), BF16)])),
        name="mlp_up",
    )(x, g, w)


def _mlp_down_kernel(h_ref, w_ref, res_ref, g_ref, o_ref, acc_ref):
    kk = pl.program_id(1)

    @pl.when(kk == 0)
    def _():
        acc_ref[...] = jnp.zeros_like(acc_ref)

    acc_ref[...] += jnp.dot(h_ref[...], w_ref[...], preferred_element_type=F32)

    @pl.when(kk == pl.num_programs(1) - 1)
    def _():
        o_ref[...] = _rmsnorm_rows(res_ref[...] + acc_ref[...], g_ref[...])


def _mlp_down_final(h, w, res, g, *, tm, tk):
    m, k = h.shape
    n = w.shape[1]
    assert m % tm == 0 and k % tk == 0
    pipelined = [((tm, tk), BF16), ((tk, n), BF16), ((tm, n), F32), ((tm, n), F32)]
    return pl.pallas_call(
        _mlp_down_kernel,
        out_shape=jax.ShapeDtypeStruct((m, n), F32),
        grid=(m // tm, k // tk),
        in_specs=[
            pl.BlockSpec((tm, tk), lambda i, j: (i, j)),
            pl.BlockSpec((tk, n), lambda i, j: (j, 0)),
            pl.BlockSpec((tm, n), lambda i, j: (i, 0)),
            pl.BlockSpec((1, n), lambda i, j: (0, 0)),
        ],
        out_specs=pl.BlockSpec((tm, n), lambda i, j: (i, 0)),
        scratch_shapes=[pltpu.VMEM((tm, n), F32)],
        compiler_params=pltpu.CompilerParams(
            dimension_semantics=("parallel", "arbitrary"),
            vmem_limit_bytes=_vmem_limit(pipelined, [((tm, n), F32)])),
        name="mlp_down",
    )(h, w, res, g)


def _lru_gates(xh, wa, wx, ba, bx, sp8):
    xb = xh.astype(BF16)
    r = _sigmoid(jnp.dot(xb, wa, preferred_element_type=F32) + ba)
    i = _sigmoid(jnp.dot(xb, wx, preferred_element_type=F32) + bx)
    log_a = -(r * sp8)
    a = jnp.exp(log_a)
    mult = jnp.sqrt(-jnp.tanh(log_a) * (1.0 + a * a))
    return a, mult, i


def _scan_rows(a, b, h_prev):
    t, hd = a.shape
    g = t // SUBLANES
    a3 = a.reshape(g, SUBLANES, hd)
    b3 = b.reshape(g, SUBLANES, hd)
    sub = lax.broadcasted_iota(jnp.int32, (g, SUBLANES, hd), 1)
    d = 1
    while d < SUBLANES:
        keep = sub >= d
        a_sh = jnp.where(keep, pltpu.roll(a3, d, axis=1), 1.0)
        b_sh = jnp.where(keep, pltpu.roll(b3, d, axis=1), 0.0)
        b3 = a3 * b_sh + b3
        a3 = a3 * a_sh
        d *= 2
    tiles = []
    h = h_prev
    for k in range(g):
        hk = a3[k] * h + b3[k]
        tiles.append(hk)
        h = hk[SUBLANES - 1:SUBLANES, :]
    return jnp.concatenate(tiles, axis=0), h


def _lru_seq_kernel(gate_ref, x_ref, cw_ref, cb_ref, wa_ref, wx_ref, ba_ref, bx_ref, lam_ref, g_ref,
                    h0_ref, tail0_ref, o_ref, hfin_ref, tailfin_ref, xe_ref, hc_ref, y_ref,
                    *, pad, reset_first):
    c = pl.program_id(1)
    t, dl = x_ref.shape
    nh, hd = wa_ref.shape[0], wa_ref.shape[1]

    @pl.when(c == 0)
    def _():
        xe_ref[0:HALO, :] = tail0_ref[...]
        hc_ref[...] = h0_ref[...]

    xe_ref[HALO:HALO + t, :] = x_ref[...]
    grow = c * t + lax.broadcasted_iota(jnp.int32, (t, hd), 0)
    ssq = jnp.zeros((t, hd), F32)
    for h in range(nh):
        sl = slice(h * hd, (h + 1) * hd)
        ext = [xe_ref[HALO - 3 + k:HALO - 3 + k + t, sl] for k in range(CONV_TAPS)]
        xh = _conv_taps(ext, 0, cw_ref, cb_ref, sl)
        sp8 = LRU_C * jax.nn.softplus(-lam_ref[:, sl])
        a, mult, i = _lru_gates(xh, wa_ref[h], wx_ref[h], ba_ref[:, sl], bx_ref[:, sl], sp8)
        if reset_first:
            mult = jnp.where(grow == pad, 1.0, mult)
        b = mult * i * xh
        if pad:
            a = jnp.where(grow >= pad, a, 1.0)
            b = jnp.where(grow >= pad, b, 0.0)
        hs, h_last = _scan_rows(a, b, hc_ref[:, sl])
        hc_ref[:, sl] = h_last
        y = hs * jax.nn.gelu(gate_ref[:, sl])
        y_ref[:, sl] = y
        ssq = ssq + y * y
    scale = lax.rsqrt(jnp.sum(ssq, axis=-1, keepdims=True) / dl + EPS)
    o_ref[...] = (y_ref[...] * scale * g_ref[...]).astype(o_ref.dtype)
    xe_ref[0:HALO, :] = xe_ref[t:t + HALO, :]
    hfin_ref[0] = hc_ref[...]
    tailfin_ref[0] = xe_ref[HALO - 3:HALO, :]


def _lru_seq(proj, params, h0, tail0, *, n_seq, seq_len, t, row0, pad=0, reset_first=False):
    cw, cb, wa, wx, ba, bx, lam, g = params
    dl = cw.shape[1]
    nh, hd = wa.shape[0], wa.shape[1]
    assert seq_len % t == 0 and row0 % t == 0 and t % SUBLANES == 0
    n_chunks = seq_len // t
    blk0 = row0 // t
    rows = lambda b, c: blk0 + b * n_chunks + c
    const2 = lambda b, c: (0, 0)
    pipelined = [((t, dl), F32), ((t, dl), F32), ((t, dl), BF16)]
    resident = [((t + HALO, dl), F32), ((t, dl), F32), ((4 * nh, hd, hd), BF16)]
    kern = functools.partial(_lru_seq_kernel, pad=pad, reset_first=reset_first)
    return pl.pallas_call(
        kern,
        out_shape=(jax.ShapeDtypeStruct((n_seq * seq_len, dl), BF16),
                   jax.ShapeDtypeStruct((n_seq, 1, dl), F32),
                   jax.ShapeDtypeStruct((n_seq, CONV_TAPS - 1, dl), F32)),
        grid=(n_seq, n_chunks),
        in_specs=[
            pl.BlockSpec((t, dl), lambda b, c: (rows(b, c), 0)),
            pl.BlockSpec((t, dl), lambda b, c: (rows(b, c), 1)),
            pl.BlockSpec((CONV_TAPS, dl), const2),
            pl.BlockSpec((1, dl), const2),
            pl.BlockSpec((nh, hd, hd), lambda b, c: (0, 0, 0)),
            pl.BlockSpec((nh, hd, hd), lambda b, c: (0, 0, 0)),
            pl.BlockSpec((1, dl), const2),
            pl.BlockSpec((1, dl), const2),
            pl.BlockSpec((1, dl), const2),
            pl.BlockSpec((1, dl), const2),
            pl.BlockSpec((1, dl), const2),
            pl.BlockSpec((HALO, dl), const2),
        ],
        out_specs=(
            pl.BlockSpec((t, dl), lambda b, c: (b * n_chunks + c, 0)),
            pl.BlockSpec((1, 1, dl), lambda b, c: (b, 0, 0)),
            pl.BlockSpec((1, CONV_TAPS - 1, dl), lambda b, c: (b, 0, 0)),
        ),
        scratch_shapes=[pltpu.VMEM((t + HALO, dl), F32), pltpu.VMEM((1, dl), F32), pltpu.VMEM((t, dl), F32)],
        compiler_params=pltpu.CompilerParams(
            dimension_semantics=("parallel", "arbitrary"),
            vmem_limit_bytes=_vmem_limit(pipelined, resident)),
        name="lru_seq",
    )(proj, proj, cw, cb, wa, wx, ba, bx, lam, g, h0, tail0)


def _lru_slab_kernel(gate_ref, x_ref, cw_ref, cb_ref, wa_ref, wx_ref, ba_ref, bx_ref, lam_ref, g_ref,
                     h0_ref, tail_ref, o_ref, hfin_ref, tailfin_ref, y_ref):
    ls, bs, dl = x_ref.shape
    nh, hd = wa_ref.shape[0], wa_ref.shape[1]
    ntail = CONV_TAPS - 1
    for h in range(nh):
        sl = slice(h * hd, (h + 1) * hd)
        ext = [tail_ref[:, k * dl + h * hd:k * dl + (h + 1) * hd] for k in range(ntail)]
        ext += [x_ref[s, :, sl] for s in range(ls)]
        sp8 = LRU_C * jax.nn.softplus(-lam_ref[:, sl])
        hcur = h0_ref[:, sl]
        for s in range(ls):
            xh = _conv_taps(ext, s, cw_ref, cb_ref, sl)
            a, mult, i = _lru_gates(xh, wa_ref[h], wx_ref[h], ba_ref[:, sl], bx_ref[:, sl], sp8)
            hcur = a * hcur + mult * i * xh
            y_ref[s, :, sl] = hcur * jax.nn.gelu(gate_ref[s, :, sl])
        hfin_ref[:, sl] = hcur
        for k in range(ntail):
            tailfin_ref[:, k * dl + h * hd:k * dl + (h + 1) * hd] = ext[ls + k]
    for s in range(ls):
        y = y_ref[s]
        scale = lax.rsqrt(jnp.mean(y * y, axis=-1, keepdims=True) + EPS)
        o_ref[s] = (y * scale * g_ref[...]).astype(o_ref.dtype)


def _lru_slab(proj3, params, h0, tail):
    cw, cb, wa, wx, ba, bx, lam, g = params
    ls, bs, _ = proj3.shape
    dl = cw.shape[1]
    nh, hd = wa.shape[0], wa.shape[1]
    ntail = CONV_TAPS - 1
    c2 = lambda i: (0, 0)
    c3 = lambda i: (0, 0, 0)
    pipelined = [((ls, bs, dl), F32)] * 2 + [((ls, bs, dl), BF16)] + [((bs, (2 * ntail + 2) * dl), F32)]
    return pl.pallas_call(
        _lru_slab_kernel,
        out_shape=(jax.ShapeDtypeStruct((ls, bs, dl), BF16),
                   jax.ShapeDtypeStruct((bs, dl), F32),
                   jax.ShapeDtypeStruct((bs, ntail * dl), F32)),
        grid=(1,),
        in_specs=[
            pl.BlockSpec((ls, bs, dl), lambda i: (0, 0, 0)),
            pl.BlockSpec((ls, bs, dl), lambda i: (0, 0, 1)),
            pl.BlockSpec((CONV_TAPS, dl), c2), pl.BlockSpec((1, dl), c2),
            pl.BlockSpec((nh, hd, hd), c3), pl.BlockSpec((nh, hd, hd), c3),
            pl.BlockSpec((1, dl), c2), pl.BlockSpec((1, dl), c2), pl.BlockSpec((1, dl), c2), pl.BlockSpec((1, dl), c2),
            pl.BlockSpec((bs, dl), c2), pl.BlockSpec((bs, ntail * dl), c2),
        ],
        out_specs=(pl.BlockSpec((ls, bs, dl), c3), pl.BlockSpec((bs, dl), c2), pl.BlockSpec((bs, ntail * dl), c2)),
        scratch_shapes=[pltpu.VMEM((ls, bs, dl), F32)],
        compiler_params=pltpu.CompilerParams(
            dimension_semantics=("arbitrary",),
            vmem_limit_bytes=_vmem_limit(pipelined, [((ls, bs, dl), F32)])),
        name="lru_slab",
    )(proj3, proj3, cw, cb, wa, wx, ba, bx, lam, g, h0, tail)


def _head_lane_mask(n_heads):
    return lax.broadcasted_iota(jnp.int32, (1, HEAD_PAD), 1) < n_heads


def _gated_group_norm(y, z, g):
    yg = y * _silu(z)
    scale = lax.rsqrt(jnp.mean(yg * yg, axis=-1, keepdims=True) + EPS)
    return yg * scale * g


def _ssd_seq_kernel(z_ref, xbc_ref, dt_ref, cw_ref, cb_ref, dtb_ref, alog_ref, dskip_ref, g_ref, e3_ref,
                    s0_ref, tail0_ref, o_ref, sfin_ref, tailfin_ref, xe_ref, xc_ref, ex_ref, y_ref, s_ref,
                    *, pad, n_heads, n_groups):
    c = pl.program_id(1)
    q, ds = z_ref.shape
    dc = xbc_ref.shape[1]
    gn = (dc - ds) // 2
    n = gn // n_groups
    p = ds // n_heads
    r = n_heads // n_groups
    gw = ds // n_groups
    hpb = LANES // p

    @pl.when(c == 0)
    def _():
        xe_ref[0:HALO, :] = tail0_ref[...]
        s_ref[...] = s0_ref[...]

    xe_ref[HALO:HALO + q, :] = xbc_ref[...]
    valid = (c * q + lax.broadcasted_iota(jnp.int32, (q, 1), 0)) >= pad

    cblk = 512 if dc % 512 == 0 else LANES
    for j in range(dc // cblk):
        sl = slice(j * cblk, (j + 1) * cblk)
        ext = [xe_ref[HALO - 3 + k:HALO - 3 + k + q, sl] for k in range(CONV_TAPS)]
        v = _silu(_conv_taps(ext, 0, cw_ref, cb_ref, sl))
        if pad and (j + 1) * cblk <= ds:
            v = jnp.where(valid, v, 0.0)
        xc_ref[:, sl] = v

    dtv = jnp.where(_head_lane_mask(n_heads), jax.nn.softplus(dt_ref[...] + dtb_ref[...]), 0.0)
    if pad:
        dtv = jnp.where(valid, dtv, 0.0)
    da = dtv * (-jnp.exp(alog_ref[...]))
    ri = lax.broadcasted_iota(jnp.int32, (q, q), 0)
    ci = lax.broadcasted_iota(jnp.int32, (q, q), 1)
    causal = ci <= ri
    tri = jnp.where(causal, 1.0, 0.0).astype(BF16)
    ac3 = jnp.dot(tri, _split3(da), preferred_element_type=F32)
    acum = ac3[:, 0:HEAD_PAD] + ac3[:, HEAD_PAD:2 * HEAD_PAD] + ac3[:, 2 * HEAD_PAD:3 * HEAD_PAD]
    alast = acum[q - 1:q, :]
    eac = jnp.exp(acum)
    wend = jnp.exp(alast - acum) * dtv
    cdec = jnp.broadcast_to(jnp.exp(alast), (SUBLANES, HEAD_PAD))
    ex_ref[...] = _expand_heads(jnp.concatenate([eac, wend, cdec], axis=0), e3_ref)
    acum_t = acum.T
    dt_t = dtv.T
    lane = lax.broadcasted_iota(jnp.int32, (q, LANES), 1)

    for g in range(n_groups):
        gsl = slice(g * gw, (g + 1) * gw)
        bg = xc_ref[:, ds + g * n:ds + (g + 1) * n].astype(BF16)
        cg = xc_ref[:, ds + gn + g * n:ds + gn + (g + 1) * n].astype(BF16)
        cbm = lax.dot_general(cg, bg, (((1,), (1,)), ((), ())), preferred_element_type=F32)
        yoff = jnp.dot(cg, s_ref[:, gsl].astype(BF16), preferred_element_type=F32)
        for k in range(gw // LANES):
            lsl = slice(g * gw + k * LANES, g * gw + (k + 1) * LANES)
            xs = xc_ref[:, lsl]
            ms = []
            xparts = []
            for u in range(hpb):
                h = g * r + k * hpb + u
                seg = acum[:, h:h + 1] - acum_t[h:h + 1, :]
                lm = jnp.where(causal, jnp.exp(seg), 0.0) * dt_t[h:h + 1, :]
                ms.append((cbm * lm).astype(BF16))
                inhead = (lane >= u * p) & (lane < (u + 1) * p)
                xparts.append(jnp.where(inhead, xs, 0.0).astype(BF16))
            ydiag = jnp.dot(jnp.concatenate(ms, axis=1), jnp.concatenate(xparts, axis=0),
                            preferred_element_type=F32)
            y = ydiag + yoff[:, k * LANES:(k + 1) * LANES] * ex_ref[0:q, lsl]
            y_ref[:, lsl] = y + dskip_ref[:, lsl] * xs
        o_ref[:, gsl] = _gated_group_norm(y_ref[:, gsl], z_ref[:, gsl], g_ref[:, gsl]).astype(o_ref.dtype)
        xw = (xc_ref[:, gsl] * ex_ref[q:2 * q, gsl]).astype(BF16)
        upd = lax.dot_general(bg, xw, (((0,), (0,)), ((), ())), preferred_element_type=F32)
        s_ref[:, gsl] = ex_ref[2 * q:2 * q + 1, gsl] * s_ref[:, gsl] + upd

    xe_ref[0:HALO, :] = xe_ref[q:q + HALO, :]
    sfin_ref[0] = s_ref[...]
    tailfin_ref[0] = xe_ref[HALO - 3:HALO, :]


def _ssd_seq(proj, dt, params, s0, tail0, *, n_seq, seq_len, row0, n_heads, n_groups, dl, pad=0):
    cw, cb, dtb, alog, dskip, g, e3 = params
    dc = cw.shape[1]
    ds = dskip.shape[1]
    n = s0.shape[0]
    q = SSD_CHUNK
    assert seq_len % q == 0 and row0 % q == 0
    assert (2 * dl) % ds == 0 and (2 * dl + ds) % dc == 0 and LANES % (ds // n_heads) == 0
    n_chunks = seq_len // q
    blk0 = row0 // q
    rows = lambda b, c: blk0 + b * n_chunks + c
    const2 = lambda b, c: (0, 0)
    pipelined = [((q, ds), F32), ((q, dc), F32), ((q, HEAD_PAD), F32), ((q, ds), BF16)]
    resident = [((q + HALO, dc), F32), ((q, dc), F32), ((2 * q + SUBLANES, ds), F32), ((q, ds), F32),
                ((3 * n, ds), F32), ((6 * HEAD_PAD, ds), BF16)]
    kern = functools.partial(_ssd_seq_kernel, pad=pad, n_heads=n_heads, n_groups=n_groups)
    return pl.pallas_call(
        kern,
        out_shape=(jax.ShapeDtypeStruct((n_seq * seq_len, ds), BF16),
                   jax.ShapeDtypeStruct((n_seq, n, ds), F32),
                   jax.ShapeDtypeStruct((n_seq, CONV_TAPS - 1, dc), F32)),
        grid=(n_seq, n_chunks),
        in_specs=[
            pl.BlockSpec((q, ds), lambda b, c: (rows(b, c), (2 * dl) // ds)),
            pl.BlockSpec((q, dc), lambda b, c: (rows(b, c), (2 * dl + ds) // dc)),
            pl.BlockSpec((q, HEAD_PAD), lambda b, c: (rows(b, c), 0)),
            pl.BlockSpec((CONV_TAPS, dc), const2),
            pl.BlockSpec((1, dc), const2),
            pl.BlockSpec((1, HEAD_PAD), const2),
            pl.BlockSpec((1, HEAD_PAD), const2),
            pl.BlockSpec((1, ds), const2),
            pl.BlockSpec((1, ds), const2),
            pl.BlockSpec((3 * HEAD_PAD, ds), const2),
            pl.BlockSpec((n, ds), const2),
            pl.BlockSpec((HALO, dc), const2),
        ],
        out_specs=(
            pl.BlockSpec((q, ds), lambda b, c: (b * n_chunks + c, 0)),
            pl.BlockSpec((1, n, ds), lambda b, c: (b, 0, 0)),
            pl.BlockSpec((1, CONV_TAPS - 1, dc), lambda b, c: (b, 0, 0)),
        ),
        scratch_shapes=[pltpu.VMEM((q + HALO, dc), F32), pltpu.VMEM((q, dc), F32),
                        pltpu.VMEM((2 * q + SUBLANES, ds), F32), pltpu.VMEM((q, ds), F32),
                        pltpu.VMEM((n, ds), F32)],
        compiler_params=pltpu.CompilerParams(
            dimension_semantics=("parallel", "arbitrary"),
            vmem_limit_bytes=_vmem_limit(pipelined, resident)),
        name="ssd_seq",
    )(proj, proj, dt, cw, cb, dtb, alog, dskip, g, e3, s0, tail0)


def _ssd_slab_pre_kernel(xbc_ref, dt_ref, tail_ref, cw_ref, cb_ref, dtb_ref, alog_ref, dskip_ref, e3_ref,
                         ypart_ref, eace_ref, c_ref, b_ref, xw_ref, cdec_ref, tailfin_ref, xc_ref,
                         *, n_heads, n_groups):
    ls, bs, dc = xbc_ref.shape
    ds = dskip_ref.shape[1]
    gn = (dc - ds) // 2
    n = gn // n_groups
    r = n_heads // n_groups
    ntail = CONV_TAPS - 1

    cblk = 512 if dc % 512 == 0 else LANES
    for j in range(dc // cblk):
        sl = slice(j * cblk, (j + 1) * cblk)
        ext = [tail_ref[:, k * dc + j * cblk:k * dc + (j + 1) * cblk] for k in range(ntail)]
        ext += [xbc_ref[s, :, sl] for s in range(ls)]
        for s in range(ls):
            xc_ref[s, :, sl] = _silu(_conv_taps(ext, s, cw_ref, cb_ref, sl))
        for k in range(ntail):
            tailfin_ref[:, k * dc + j * cblk:k * dc + (j + 1) * cblk] = ext[ls + k]

    hmask = _head_lane_mask(n_heads)
    a_neg = -jnp.exp(alog_ref[...])
    dtv, acum = [], []
    run = jnp.zeros((bs, HEAD_PAD), F32)
    for s in range(ls):
        d = jnp.where(hmask, jax.nn.softplus(dt_ref[s] + dtb_ref[...]), 0.0)
        run = run + d * a_neg
        dtv.append(d)
        acum.append(run)
    alast = acum[ls - 1]
    cdec_ref[...] = jnp.exp(alast)
    head_group = lax.broadcasted_iota(jnp.int32, (1, HEAD_PAD), 1) // r

    for s in range(ls):
        eace_ref[s] = _expand_heads(jnp.exp(acum[s]), e3_ref)
        wend_e = _expand_heads(jnp.exp(alast - acum[s]) * dtv[s], e3_ref)
        xw_ref[:, s * ds:(s + 1) * ds] = xc_ref[s, :, 0:ds] * wend_e
        b_ref[:, s * gn:(s + 1) * gn] = xc_ref[s, :, ds:ds + gn]
        c_ref[:, s * gn:(s + 1) * gn] = xc_ref[s, :, ds + gn:ds + 2 * gn]
        ypart = dskip_ref[...] * xc_ref[s, :, 0:ds]
        for j in range(s + 1):
            cbh = jnp.zeros((bs, HEAD_PAD), F32)
            for g in range(n_groups):
                cs = xc_ref[s, :, ds + gn + g * n:ds + gn + (g + 1) * n]
                bj = xc_ref[j, :, ds + g * n:ds + (g + 1) * n]
                cbg = jnp.sum(cs * bj, axis=-1, keepdims=True)
                cbh = cbh + jnp.where(head_group == g, cbg, 0.0)
            coef = cbh * (jnp.exp(acum[s] - acum[j]) * dtv[j])
            ypart = ypart + _expand_heads(coef, e3_ref) * xc_ref[j, :, 0:ds]
        ypart_ref[s] = ypart
    for s in range(ls, SLAB_ROWS):
        xw_ref[:, s * ds:(s + 1) * ds] = jnp.zeros((bs, ds), F32)
        b_ref[:, s * gn:(s + 1) * gn] = jnp.zeros((bs, gn), F32)
        c_ref[:, s * gn:(s + 1) * gn] = jnp.zeros((bs, gn), F32)


def _ssd_slab_pre(proj3, dt3, tail, params, *, n_heads, n_groups, dl):
    cw, cb, dtb, alog, dskip, _, e3 = params
    ls, bs, _ = proj3.shape
    dc = cw.shape[1]
    ds = dskip.shape[1]
    gn = (dc - ds) // 2
    ntail = CONV_TAPS - 1
    assert ls <= SLAB_ROWS and (2 * dl + ds) % dc == 0
    c2 = lambda i: (0, 0)
    c3 = lambda i: (0, 0, 0)
    pipelined = [((ls, bs, dc), F32), ((ls, bs, HEAD_PAD), F32), ((bs, 2 * ntail * dc), F32),
                 ((2 * ls, bs, ds), F32), ((bs, SLAB_ROWS * (2 * gn + ds)), F32), ((3 * HEAD_PAD, ds), BF16)]
    kern = functools.partial(_ssd_slab_pre_kernel, n_heads=n_heads, n_groups=n_groups)
    return pl.pallas_call(
        kern,
        out_shape=(jax.ShapeDtypeStruct((ls, bs, ds), F32),
                   jax.ShapeDtypeStruct((ls, bs, ds), F32),
                   jax.ShapeDtypeStruct((bs, SLAB_ROWS * gn), F32),
                   jax.ShapeDtypeStruct((bs, SLAB_ROWS * gn), F32),
                   jax.ShapeDtypeStruct((bs, SLAB_ROWS * ds), F32),
                   jax.ShapeDtypeStruct((bs, HEAD_PAD), F32),
                   jax.ShapeDtypeStruct((bs, ntail * dc), F32)),
        grid=(1,),
        in_specs=[
            pl.BlockSpec((ls, bs, dc), lambda i: (0, 0, (2 * dl + ds) // dc)),
            pl.BlockSpec((ls, bs, HEAD_PAD), c3),
            pl.BlockSpec((bs, ntail * dc), c2),
            pl.BlockSpec((CONV_TAPS, dc), c2), pl.BlockSpec((1, dc), c2),
            pl.BlockSpec((1, HEAD_PAD), c2), pl.BlockSpec((1, HEAD_PAD), c2),
            pl.BlockSpec((1, ds), c2), pl.BlockSpec((3 * HEAD_PAD, ds), c2),
        ],
        out_specs=(pl.BlockSpec((ls, bs, ds), c3), pl.BlockSpec((ls, bs, ds), c3),
                   pl.BlockSpec((bs, SLAB_ROWS * gn), c2), pl.BlockSpec((bs, SLAB_ROWS * gn), c2),
                   pl.BlockSpec((bs, SLAB_ROWS * ds), c2), pl.BlockSpec((bs, HEAD_PAD), c2),
                   pl.BlockSpec((bs, ntail * dc), c2)),
        scratch_shapes=[pltpu.VMEM((ls, bs, dc), F32)],
        compiler_params=pltpu.CompilerParams(
            dimension_semantics=("arbitrary",),
            vmem_limit_bytes=_vmem_limit(pipelined, [((ls, bs, dc), F32)])),
        name="ssd_slab_pre",
    )(proj3, dt3, tail, cw, cb, dtb, alog, dskip, e3)


def _ssd_state_kernel(cdec_ref, s_ref, c_ref, b_ref, xw_ref, snew_ref, yoff_ref, *, n_heads, n_groups):
    i = pl.program_id(0)
    hp, n = s_ref.shape
    p = hp // n_heads
    r = n_heads // n_groups
    gw = hp // n_groups
    for g in range(n_groups):
        gsl = slice(g * gw, (g + 1) * gw)
        sg = s_ref[gsl, :]
        cg = c_ref[:, g * n:(g + 1) * n].astype(BF16)
        yoff_ref[:, gsl] = lax.dot_general(cg, sg.astype(BF16), (((1,), (1,)), ((), ())),
                                           preferred_element_type=F32)
        upd = lax.dot_general(xw_ref[:, gsl].astype(BF16), b_ref[:, g * n:(g + 1) * n].astype(BF16),
                              (((0,), (0,)), ((), ())), preferred_element_type=F32)
        for u in range(r):
            h = g * r + u
            rows = slice(h * p, (h + 1) * p)
            snew_ref[rows, :] = cdec_ref[i, h] * s_ref[rows, :] + upd[u * p:(u + 1) * p, :]


def _ssd_state(cdec, state, c_rows, b_rows, xw_rows, *, n_heads, n_groups):
    bs, hp, n = state.shape
    gn = c_rows.shape[2]
    per_seq = lambda i: (i, 0, 0)
    pipelined = [((hp, n), F32)] * 2 + [((SLAB_ROWS, gn), F32)] * 2 + [((SLAB_ROWS, hp), F32)] * 2
    kern = functools.partial(_ssd_state_kernel, n_heads=n_heads, n_groups=n_groups)
    return pl.pallas_call(
        kern,
        out_shape=(jax.ShapeDtypeStruct((bs, hp, n), F32), jax.ShapeDtypeStruct((bs, SLAB_ROWS, hp), F32)),
        grid=(bs,),
        in_specs=[
            pl.BlockSpec(memory_space=pltpu.SMEM),
            pl.BlockSpec((None, hp, n), per_seq),
            pl.BlockSpec((None, SLAB_ROWS, gn), per_seq),
            pl.BlockSpec((None, SLAB_ROWS, gn), per_seq),
            pl.BlockSpec((None, SLAB_ROWS, hp), per_seq),
        ],
        out_specs=(pl.BlockSpec((None, hp, n), per_seq), pl.BlockSpec((None, SLAB_ROWS, hp), per_seq)),
        compiler_params=pltpu.CompilerParams(
            dimension_semantics=("parallel",),
            vmem_limit_bytes=_vmem_limit(pipelined, [])),
        name="ssd_state",
    )(cdec, state, c_rows, b_rows, xw_rows)


def _ssd_slab_post_kernel(ypart_ref, eace_ref, yoff_ref, z_ref, g_ref, o_ref, *, n_groups):
    ls, bs, ds = ypart_ref.shape
    gw = ds // n_groups
    for s in range(ls):
        for g in range(n_groups):
            gsl = slice(g * gw, (g + 1) * gw)
            y = ypart_ref[s, :, gsl] + eace_ref[s, :, gsl] * yoff_ref[:, s * ds + g * gw:s * ds + (g + 1) * gw]
            o_ref[s, :, gsl] = _gated_group_norm(y, z_ref[s, :, gsl], g_ref[:, gsl]).astype(o_ref.dtype)


def _ssd_slab_post(ypart, eace, yoff, proj3, g, *, n_groups, dl):
    ls, bs, ds = ypart.shape
    assert (2 * dl) % ds == 0
    c2 = lambda i: (0, 0)
    c3 = lambda i: (0, 0, 0)
    pipelined = [((ls, bs, ds), F32)] * 3 + [((bs, SLAB_ROWS * ds), F32), ((ls, bs, ds), BF16)]
    kern = functools.partial(_ssd_slab_post_kernel, n_groups=n_groups)
    return pl.pallas_call(
        kern,
        out_shape=jax.ShapeDtypeStruct((ls, bs, ds), BF16),
        grid=(1,),
        in_specs=[
            pl.BlockSpec((ls, bs, ds), c3), pl.BlockSpec((ls, bs, ds), c3),
            pl.BlockSpec((bs, SLAB_ROWS * ds), c2),
            pl.BlockSpec((ls, bs, ds), lambda i: (0, 0, (2 * dl) // ds)),
            pl.BlockSpec((1, ds), c2),
        ],
        out_specs=pl.BlockSpec((ls, bs, ds), c3),
        compiler_params=pltpu.CompilerParams(
            dimension_semantics=("arbitrary",),
            vmem_limit_bytes=_vmem_limit(pipelined, [])),
        name="ssd_slab_post",
    )(ypart, eace, yoff, proj3, g)


def _head_expansion(n_heads, head_dim):
    rows = lax.broadcasted_iota(jnp.int32, (HEAD_PAD, n_heads * head_dim), 0)
    cols = lax.broadcasted_iota(jnp.int32, (HEAD_PAD, n_heads * head_dim), 1)
    e = (cols // head_dim == rows).astype(BF16)
    return jnp.concatenate([e, e, e], axis=0)


def _pad_lanes(v, width):
    return jnp.pad(v, ((0, 0), (0, width - v.shape[1])))


def _tail_block(tail):
    return jnp.pad(tail, ((HALO - tail.shape[0], 0), (0, 0)))


def _mlp(x1, g_mlp, w_up, w_down, g_final):
    m = x1.shape[0]
    tm = _pick_tile(m, (1024, 512, 256, 128))
    hid = _mlp_up(x1, g_mlp, w_up, tm=tm, tn=_pick_tile(w_up.shape[1], (1024, 512, 256, 128)))
    tmd = _pick_tile(m, (512, 256, 128))
    return _mlp_down_final(hid, w_down, x1, g_final, tm=tmd, tk=_pick_tile(w_down.shape[0], (1024, 512, 256, 128)))


def kernel(x_prompt, x_sample, state_lru_h, state_lru_conv, state_ssd, state_ssd_conv, meta_tokens, g_mix, w_in, conv_lru_w, conv_lru_b, lru_wa, lru_ba, lru_wx, lru_bx, lru_lambda, g_lru_out, conv_ssd_w, conv_ssd_b, dt_bias, a_log, d_skip, g_ssd_out, w_out, g_mlp, w_up, w_down, g_final):
    depth = w_in.shape[0]
    assert depth == 1, "single-layer step"
    l = 0
    bp, lp, d = x_prompt.shape
    bs, ls, _ = x_sample.shape
    n_meta = meta_tokens.shape[0]
    dl = state_lru_h.shape[-1]
    n_heads, p, n = state_ssd.shape[-3:]
    ds = n_heads * p
    dc = state_ssd_conv.shape[-1]
    gn = (dc - ds) // 2
    n_groups = gn // n
    nw = 2 * dl + ds + dc
    ntail = CONV_TAPS - 1
    q = SSD_CHUNK
    meta_pad = (-n_meta) % q
    assert n_heads <= HEAD_PAD and (bs * ls) % q == 0 and lp % q == 0

    row = lambda v: v.reshape(1, -1).astype(F32)
    w_main = w_in[l][:, :nw].astype(BF16)
    w_dt = _pad_lanes(w_in[l][:, nw:], HEAD_PAD).astype(BF16)
    w_out_lru = w_out[l][:dl].astype(BF16)
    w_out_ssd = w_out[l][dl:].astype(BF16)
    w_up_b = w_up[l].astype(BF16)
    w_down_b = w_down[l].astype(BF16)
    lru_params = (conv_lru_w[l], row(conv_lru_b[l]), lru_wa[l].astype(BF16), lru_wx[l].astype(BF16),
                  row(lru_ba[l]), row(lru_bx[l]), row(lru_lambda[l]), row(g_lru_out[l]))
    ssd_params = (conv_ssd_w[l], row(conv_ssd_b[l]), _pad_lanes(row(dt_bias[l]), HEAD_PAD),
                  _pad_lanes(row(a_log[l]), HEAD_PAD), row(jnp.repeat(d_skip[l], p)), row(g_ssd_out[l]),
                  _head_expansion(n_heads, p))
    g_mix_r, g_mlp_r, g_final_r = row(g_mix[l]), row(g_mlp[l]), row(g_final)

    xs_tm = x_sample.transpose(1, 0, 2).reshape(ls * bs, d)
    x_side = jnp.concatenate([xs_tm, jnp.zeros((meta_pad, d), F32), meta_tokens.astype(F32)], axis=0)
    xp_rows = x_prompt.reshape(bp * lp, d)

    tn_in = _pick_tile(nw, (1024, 512, 256, 128))
    proj_side, dt_side = _in_proj(x_side, g_mix_r, w_main, w_dt, tm=x_side.shape[0], tn=tn_in)
    proj_p, dt_p = _in_proj(xp_rows, g_mix_r, w_main, w_dt, tm=_pick_tile(bp * lp, (1024, 512, 256, 128)), tn=tn_in)

    meta_row0 = ls * bs
    _, m_h, m_ltail = _lru_seq(proj_side, lru_params, jnp.zeros((1, dl), F32), jnp.zeros((HALO, dl), F32),
                               n_seq=1, seq_len=q, t=q, row0=meta_row0, pad=meta_pad, reset_first=True)
    _, m_s, m_stail = _ssd_seq(proj_side, dt_side, ssd_params, jnp.zeros((n, ds), F32), jnp.zeros((HALO, dc), F32),
                               n_seq=1, seq_len=q, row0=meta_row0, n_heads=n_heads, n_groups=n_groups, dl=dl,
                               pad=meta_pad)

    lru_p, p_h, p_ltail = _lru_seq(proj_p, lru_params, m_h[0], _tail_block(m_ltail[0]),
                                   n_seq=bp, seq_len=lp, t=_pick_tile(lp, (256, 128)), row0=0)
    ssd_p, p_s, p_stail = _ssd_seq(proj_p, dt_p, ssd_params, m_s[0], _tail_block(m_stail[0]),
                                   n_seq=bp, seq_len=lp, row0=0, n_heads=n_heads, n_groups=n_groups, dl=dl)
    x1_p = _out_proj(lru_p, ssd_p, w_out_lru, w_out_ssd, xp_rows,
                     tm=_pick_tile(bp * lp, (512, 256, 128)), tn=_pick_tile(d, (1024, 512, 256, 128)))
    y_p = _mlp(x1_p, g_mlp_r, w_up_b, w_down_b, g_final_r)

    proj_s3 = proj_side[:ls * bs].reshape(ls, bs, nw)
    dt_s3 = dt_side[:ls * bs].reshape(ls, bs, HEAD_PAD)
    lru_s, s_h, s_ltail = _lru_slab(proj_s3, lru_params, state_lru_h[l],
                                    state_lru_conv[l].reshape(bs, ntail * dl))
    ypart, eace, c_rows, b_rows, xw_rows, cdec, s_stail = _ssd_slab_pre(
        proj_s3, dt_s3, state_ssd_conv[l].reshape(bs, ntail * dc), ssd_params,
        n_heads=n_heads, n_groups=n_groups, dl=dl)
    s_new, yoff = _ssd_state(cdec, state_ssd[l].reshape(bs, ds, n),
                             c_rows.reshape(bs, SLAB_ROWS, gn), b_rows.reshape(bs, SLAB_ROWS, gn),
                             xw_rows.reshape(bs, SLAB_ROWS, ds), n_heads=n_heads, n_groups=n_groups)
    ssd_s = _ssd_slab_post(ypart, eace, yoff.reshape(bs, SLAB_ROWS * ds), proj_s3, ssd_params[5],
                           n_groups=n_groups, dl=dl)
    x1_s = _out_proj(lru_s.reshape(ls * bs, dl), ssd_s.reshape(ls * bs, ds), w_out_lru, w_out_ssd, xs_tm,
                     tm=_pick_tile(ls * bs, (512, 256, 128)), tn=_pick_tile(d, (1024, 512, 256, 128)))
    y_s = _mlp(x1_s, g_mlp_r, w_up_b, w_down_b, g_final_r)

    y_prompt = y_p.reshape(bp, lp, d)
    y_sample = y_s.reshape(ls, bs, d).transpose(1, 0, 2)
    p_lru_h = p_h.reshape(1, bp, dl)
    p_lru_conv = p_ltail.reshape(1, bp, ntail, dl)
    p_ssd = p_s.transpose(0, 2, 1).reshape(1, bp, n_heads, p, n)
    p_ssd_conv = p_stail.reshape(1, bp, ntail, dc)
    s_lru_h = s_h.reshape(1, bs, dl)
    s_lru_conv = s_ltail.reshape(1, bs, ntail, dl)
    s_ssd = s_new.reshape(1, bs, n_heads, p, n)
    s_ssd_conv = s_stail.reshape(1, bs, ntail, dc)
    return (y_prompt, y_sample, p_lru_h, p_lru_conv, p_ssd, p_ssd_conv, s_lru_h, s_lru_conv, s_ssd, s_ssd_conv)
```

```python
import functools

import jax
import jax.numpy as jnp
from jax import lax
from jax.experimental import pallas as pl
from jax.experimental.pallas import tpu as pltpu

F32 = jnp.float32
BF16 = jnp.bfloat16

EPS = 1e-6
LRU_C = 8.0
CONV_TAPS = 4

LANES = 128
SUBLANES = 8
VMEM_BYTES_V7X = 64 * 1024 * 1024
VMEM_TEMP_BYTES = 10 * 1024 * 1024
VMEM_CEILING_BYTES = VMEM_BYTES_V7X - 6 * 1024 * 1024

HALO = SUBLANES
SSD_CHUNK = 128
HEAD_PAD = LANES
SLAB_ROWS = SUBLANES
STATE_SEQS_PER_STEP = 4


def _nbytes(shape, dtype):
    n = 1
    for s in shape:
        n *= s
    return n * jnp.dtype(dtype).itemsize


def _vmem_limit(pipelined, resident):
    est = 2 * sum(_nbytes(s, d) for s, d in pipelined) + sum(_nbytes(s, d) for s, d in resident)
    return int(min(est + VMEM_TEMP_BYTES, VMEM_CEILING_BYTES))


def _pick_tile(m, prefs):
    for t in prefs:
        if m % t == 0:
            return t
    return m


def _silu(x):
    h = 0.5 * x
    return h * jnp.tanh(h) + h


def _split3(x):
    hi = x.astype(BF16)
    r1 = x - hi.astype(F32)
    mid = r1.astype(BF16)
    lo = (r1 - mid.astype(F32)).astype(BF16)
    return jnp.concatenate([hi, mid, lo], axis=1)


def _expand_heads(x, e3_ref):
    return jnp.dot(_split3(x), e3_ref[...], preferred_element_type=F32)


def _rmsnorm_rows(x, g):
    ms = jnp.mean(x * x, axis=-1, keepdims=True)
    return x * lax.rsqrt(ms + EPS) * g


def _conv_taps(ext, s, cw_ref, cb_ref, sl):
    v = cb_ref[:, sl] + ext[s] * cw_ref[0:1, sl]
    for k in range(1, CONV_TAPS):
        v = v + ext[s + k] * cw_ref[k:k + 1, sl]
    return v


def _in_proj_kernel(x_ref, g_ref, w_ref, wdt_ref, o_ref, dt_ref, xn_ref):
    @pl.when(pl.program_id(1) == 0)
    def _():
        xn = _rmsnorm_rows(x_ref[...], g_ref[...]).astype(BF16)
        xn_ref[...] = xn
        dt_ref[...] = jnp.dot(xn, wdt_ref[...], preferred_element_type=F32)

    o_ref[...] = jnp.dot(xn_ref[...], w_ref[...], preferred_element_type=F32)


def _in_proj(x, g, w, layer, wdt, *, nw, tm, tn):
    m, k = x.shape
    assert m % tm == 0 and nw % tn == 0 and nw <= w.shape[2]
    pipelined = [((tm, k), F32), ((k, tn), BF16), ((k, HEAD_PAD), BF16), ((tm, tn), F32), ((tm, HEAD_PAD), F32)]
    return pl.pallas_call(
        _in_proj_kernel,
        out_shape=(jax.ShapeDtypeStruct((m, nw), F32), jax.ShapeDtypeStruct((m, HEAD_PAD), F32)),
        grid=(m // tm, nw // tn),
        in_specs=[
            pl.BlockSpec((tm, k), lambda i, j: (i, 0)),
            pl.BlockSpec((1, k), lambda i, j: (0, 0)),
            pl.BlockSpec((None, k, tn), lambda i, j: (layer, 0, j)),
            pl.BlockSpec((k, HEAD_PAD), lambda i, j: (0, 0)),
        ],
        out_specs=(
            pl.BlockSpec((tm, tn), lambda i, j: (i, j)),
            pl.BlockSpec((tm, HEAD_PAD), lambda i, j: (i, 0)),
        ),
        scratch_shapes=[pltpu.VMEM((tm, k), BF16)],
        compiler_params=pltpu.CompilerParams(
            dimension_semantics=("parallel", "arbitrary"),
            vmem_limit_bytes=_vmem_limit(pipelined, [((tm, k), BF16)])),
        name="in_proj",
    )(x, g, w, wdt)


def _out_proj_kernel(a1_ref, a2_ref, w1_ref, w2_ref, res_ref, o_ref):
    acc = jnp.dot(a1_ref[...], w1_ref[...], preferred_element_type=F32)
    acc = acc + jnp.dot(a2_ref[...], w2_ref[...], preferred_element_type=F32)
    o_ref[...] = res_ref[...] + acc


def _out_proj(a1, a2, w, layer, res, *, tm, tn):
    m, k1 = a1.shape
    k2 = a2.shape[1]
    n = w.shape[2]
    assert m % tm == 0 and n % tn == 0 and k1 == k2 and w.shape[1] == k1 + k2
    pipelined = [((tm, k1), BF16), ((tm, k2), BF16), ((k1, tn), BF16), ((k2, tn), BF16), ((tm, tn), F32), ((tm, tn), F32)]
    return pl.pallas_call(
        _out_proj_kernel,
        out_shape=jax.ShapeDtypeStruct((m, n), F32),
        grid=(m // tm, n // tn),
        in_specs=[
            pl.BlockSpec((tm, k1), lambda i, j: (i, 0)),
            pl.BlockSpec((tm, k2), lambda i, j: (i, 0)),
            pl.BlockSpec((None, k1, tn), lambda i, j: (layer, 0, j)),
            pl.BlockSpec((None, k2, tn), lambda i, j: (layer, 1, j)),
            pl.BlockSpec((tm, tn), lambda i, j: (i, j)),
        ],
        out_specs=pl.BlockSpec((tm, tn), lambda i, j: (i, j)),
        compiler_params=pltpu.CompilerParams(
            dimension_semantics=("parallel", "parallel"),
            vmem_limit_bytes=_vmem_limit(pipelined, [])),
        name="out_proj",
    )(a1, a2, w, w, res)


def _mlp_up_kernel(x_ref, g_ref, w_ref, o_ref, xn_ref):
    @pl.when(pl.program_id(1) == 0)
    def _():
        xn_ref[...] = _rmsnorm_rows(x_ref[...], g_ref[...]).astype(BF16)

    acc = jnp.dot(xn_ref[...], w_ref[...], preferred_element_type=F32)
    o_ref[...] = jnp.square(jnp.maximum(acc, 0.0)).astype(o_ref.dtype)


def _mlp_up(x, g, w, *, tm, tn):
    m, k = x.shape
    n = w.shape[1]
    assert m % tm == 0 and n % tn == 0
    pipelined = [((tm, k), F32), ((k, tn), BF16), ((tm, tn), BF16)]
    return pl.pallas_call(
        _mlp_up_kernel,
        out_shape=jax.ShapeDtypeStruct((m, n), BF16),
        grid=(m // tm, n // tn),
        in_specs=[
            pl.BlockSpec((tm, k), lambda i, j: (i, 0)),
            pl.BlockSpec((1, k), lambda i, j: (0, 0)),
            pl.BlockSpec((k, tn), lambda i, j: (0, j)),
        ],
        out_specs=pl.BlockSpec((tm, tn), lambda i, j: (i, j)),
        scratch_shapes=[pltpu.VMEM((tm, k), BF16)],
        compiler_params=pltpu.CompilerParams(
            dimension_semantics=("parallel", "arbitrary"),
            vmem_limit_bytes=_vmem_limit(pipelined, [((tm, k), BF16)])),
        name="mlp_up",
    )(x, g, w)


def _mlp_down_kernel(h_ref, w_ref, res_ref, g_ref, o_ref):
    kk = pl.program_id(1)

    @pl.when(kk == 0)
    def _():
        o_ref[...] = res_ref[...]

    o_ref[...] += jnp.dot(h_ref[...], w_ref[...], preferred_element_type=F32)

    @pl.when(kk == pl.num_programs(1) - 1)
    def _():
        o_ref[...] = _rmsnorm_rows(o_ref[...], g_ref[...])


def _mlp_down_final(h, w, res, g, *, tm, tk):
    m, k = h.shape
    n = w.shape[1]
    assert m % tm == 0 and k % tk == 0
    pipelined = [((tm, tk), BF16), ((tk, n), BF16), ((tm, n), F32), ((tm, n), F32)]
    return pl.pallas_call(
        _mlp_down_kernel,
        out_shape=jax.ShapeDtypeStruct((m, n), F32),
        grid=(m // tm, k // tk),
        in_specs=[
            pl.BlockSpec((tm, tk), lambda i, j: (i, j)),
            pl.BlockSpec((tk, n), lambda i, j: (j, 0)),
            pl.BlockSpec((tm, n), lambda i, j: (i, 0)),
            pl.BlockSpec((1, n), lambda i, j: (0, 0)),
        ],
        out_specs=pl.BlockSpec((tm, n), lambda i, j: (i, 0)),
        compiler_params=pltpu.CompilerParams(
            dimension_semantics=("parallel", "arbitrary"),
            vmem_limit_bytes=_vmem_limit(pipelined, [])),
        name="mlp_down",
    )(h, w, res, g)


def _lru_gates(xh, wa_half, wx_half, ba_half, bx_half, hsp):
    xb = xh.astype(BF16)
    tr = jnp.tanh(jnp.dot(xb, wa_half, preferred_element_type=F32) + ba_half)
    ti = jnp.tanh(jnp.dot(xb, wx_half, preferred_element_type=F32) + bx_half)
    nla = tr * hsp + hsp
    a = jnp.exp(-nla)
    q = jnp.tanh(nla) * (1.0 + a * a)
    mult = jnp.where(q > 0.0, q * lax.rsqrt(q), 0.0)
    return a, mult, 0.5 * ti + 0.5


def _scan_rows(a, b, h_prev):
    t, hd = a.shape
    g = t // SUBLANES
    a3 = a.reshape(g, SUBLANES, hd)
    b3 = b.reshape(g, SUBLANES, hd)
    sub = lax.broadcasted_iota(jnp.int32, (g, SUBLANES, hd), 1)
    d = 1
    while d < SUBLANES:
        keep = sub >= d
        a_sh = jnp.where(keep, pltpu.roll(a3, d, axis=1), 1.0)
        b_sh = jnp.where(keep, pltpu.roll(b3, d, axis=1), 0.0)
        b3 = a3 * b_sh + b3
        a3 = a3 * a_sh
        d *= 2
    tiles = []
    h = h_prev
    for k in range(g):
        hk = a3[k] * h + b3[k]
        tiles.append(hk)
        h = hk[SUBLANES - 1:SUBLANES, :]
    return jnp.concatenate(tiles, axis=0), h


def _lru_seq_kernel(gate_ref, x_ref, cw_ref, cb_ref, wa_ref, wx_ref, ba_ref, bx_ref, lam_ref, g_ref,
                    h0_ref, tail0_ref, o_ref, hfin_ref, tailfin_ref, xe_ref, hc_ref, y_ref,
                    *, pad, reset_first):
    c = pl.program_id(1)
    t, dl = x_ref.shape
    nh, hd = wa_ref.shape[0], wa_ref.shape[1]

    @pl.when(c == 0)
    def _():
        xe_ref[0:HALO, :] = tail0_ref[...]
        hc_ref[...] = h0_ref[...]

    xe_ref[HALO:HALO + t, :] = x_ref[...]
    grow = c * t + lax.broadcasted_iota(jnp.int32, (t, hd), 0)
    ssq = jnp.zeros((t, hd), F32)
    for h in range(nh):
        sl = slice(h * hd, (h + 1) * hd)
        ext = [xe_ref[HALO - 3 + k:HALO - 3 + k + t, sl] for k in range(CONV_TAPS)]
        xh = _conv_taps(ext, 0, cw_ref, cb_ref, sl)
        hsp = (0.5 * LRU_C) * jax.nn.softplus(-lam_ref[:, sl])
        a, mult, i = _lru_gates(xh, wa_ref[h], wx_ref[h], ba_ref[:, sl], bx_ref[:, sl], hsp)
        if reset_first:
            mult = jnp.where(grow == pad, 1.0, mult)
        b = mult * i * xh
        if pad:
            a = jnp.where(grow >= pad, a, 1.0)
            b = jnp.where(grow >= pad, b, 0.0)
        hs, h_last = _scan_rows(a, b, hc_ref[:, sl])
        hc_ref[:, sl] = h_last
        y = hs * jax.nn.gelu(gate_ref[:, sl])
        y_ref[:, sl] = y
        ssq = ssq + y * y
    scale = lax.rsqrt(jnp.sum(ssq, axis=-1, keepdims=True) / dl + EPS)
    o_ref[...] = (y_ref[...] * scale * g_ref[...]).astype(o_ref.dtype)
    xe_ref[0:HALO, :] = xe_ref[t:t + HALO, :]
    hfin_ref[0] = hc_ref[...]
    tailfin_ref[0] = xe_ref[HALO - 3:HALO, :]


def _lru_seq(proj, params, h0, tail0, *, n_seq, seq_len, t, row0, pad=0, reset_first=False):
    cw, cb, wa, wx, ba, bx, lam, g = params
    dl = cw.shape[1]
    nh, hd = wa.shape[0], wa.shape[1]
    assert seq_len % t == 0 and row0 % t == 0 and t % SUBLANES == 0
    n_chunks = seq_len // t
    blk0 = row0 // t
    rows = lambda b, c: blk0 + b * n_chunks + c
    const2 = lambda b, c: (0, 0)
    pipelined = [((t, dl), F32), ((t, dl), F32), ((t, dl), BF16)]
    resident = [((t + HALO, dl), F32), ((t, dl), F32), ((4 * nh, hd, hd), BF16)]
    kern = functools.partial(_lru_seq_kernel, pad=pad, reset_first=reset_first)
    return pl.pallas_call(
        kern,
        out_shape=(jax.ShapeDtypeStruct((n_seq * seq_len, dl), BF16),
                   jax.ShapeDtypeStruct((n_seq, 1, dl), F32),
                   jax.ShapeDtypeStruct((n_seq, CONV_TAPS - 1, dl), F32)),
        grid=(n_seq, n_chunks),
        in_specs=[
            pl.BlockSpec((t, dl), lambda b, c: (rows(b, c), 0)),
            pl.BlockSpec((t, dl), lambda b, c: (rows(b, c), 1)),
            pl.BlockSpec((CONV_TAPS, dl), const2),
            pl.BlockSpec((1, dl), const2),
            pl.BlockSpec((nh, hd, hd), lambda b, c: (0, 0, 0)),
            pl.BlockSpec((nh, hd, hd), lambda b, c: (0, 0, 0)),
            pl.BlockSpec((1, dl), const2),
            pl.BlockSpec((1, dl), const2),
            pl.BlockSpec((1, dl), const2),
            pl.BlockSpec((1, dl), const2),
            pl.BlockSpec((1, dl), const2),
            pl.BlockSpec((HALO, dl), const2),
        ],
        out_specs=(
            pl.BlockSpec((t, dl), lambda b, c: (b * n_chunks + c, 0)),
            pl.BlockSpec((1, 1, dl), lambda b, c: (b, 0, 0)),
            pl.BlockSpec((1, CONV_TAPS - 1, dl), lambda b, c: (b, 0, 0)),
        ),
        scratch_shapes=[pltpu.VMEM((t + HALO, dl), F32), pltpu.VMEM((1, dl), F32), pltpu.VMEM((t, dl), F32)],
        compiler_params=pltpu.CompilerParams(
            dimension_semantics=("parallel", "arbitrary"),
            vmem_limit_bytes=_vmem_limit(pipelined, resident)),
        name="lru_seq",
    )(proj, proj, cw, cb, wa, wx, ba, bx, lam, g, h0, tail0)


def _lru_slab_kernel(gate_ref, x_ref, cw_ref, cb_ref, wa_ref, wx_ref, ba_ref, bx_ref, lam_ref, g_ref,
                     h0_ref, tail_ref, o_ref, hfin_ref, tailfin_ref, y_ref):
    ls, bs, dl = x_ref.shape
    nh, hd = wa_ref.shape[0], wa_ref.shape[1]
    ntail = CONV_TAPS - 1
    for h in range(nh):
        sl = slice(h * hd, (h + 1) * hd)
        ext = [tail_ref[:, k * dl + h * hd:k * dl + (h + 1) * hd] for k in range(ntail)]
        ext += [x_ref[s, :, sl] for s in range(ls)]
        hsp = (0.5 * LRU_C) * jax.nn.softplus(-lam_ref[:, sl])
        hcur = h0_ref[:, sl]
        for s in range(ls):
            xh = _conv_taps(ext, s, cw_ref, cb_ref, sl)
            a, mult, i = _lru_gates(xh, wa_ref[h], wx_ref[h], ba_ref[:, sl], bx_ref[:, sl], hsp)
            hcur = a * hcur + mult * i * xh
            y_ref[s, :, sl] = hcur * jax.nn.gelu(gate_ref[s, :, sl])
        hfin_ref[:, sl] = hcur
        for k in range(ntail):
            tailfin_ref[:, k * dl + h * hd:k * dl + (h + 1) * hd] = ext[ls + k]
    for s in range(ls):
        y = y_ref[s]
        scale = lax.rsqrt(jnp.mean(y * y, axis=-1, keepdims=True) + EPS)
        o_ref[s] = (y * scale * g_ref[...]).astype(o_ref.dtype)


def _lru_slab(proj3, params, h0, tail, *, ls):
    cw, cb, wa, wx, ba, bx, lam, g = params
    bs = proj3.shape[1]
    dl = cw.shape[1]
    nh, hd = wa.shape[0], wa.shape[1]
    ntail = CONV_TAPS - 1
    c2 = lambda i: (0, 0)
    c3 = lambda i: (0, 0, 0)
    pipelined = [((ls, bs, dl), F32)] * 2 + [((ls, bs, dl), BF16)] + [((bs, (2 * ntail + 2) * dl), F32)]
    return pl.pallas_call(
        _lru_slab_kernel,
        out_shape=(jax.ShapeDtypeStruct((ls, bs, dl), BF16),
                   jax.ShapeDtypeStruct((bs, dl), F32),
                   jax.ShapeDtypeStruct((bs, ntail * dl), F32)),
        grid=(1,),
        in_specs=[
            pl.BlockSpec((ls, bs, dl), lambda i: (0, 0, 0)),
            pl.BlockSpec((ls, bs, dl), lambda i: (0, 0, 1)),
            pl.BlockSpec((CONV_TAPS, dl), c2), pl.BlockSpec((1, dl), c2),
            pl.BlockSpec((nh, hd, hd), c3), pl.BlockSpec((nh, hd, hd), c3),
            pl.BlockSpec((1, dl), c2), pl.BlockSpec((1, dl), c2), pl.BlockSpec((1, dl), c2), pl.BlockSpec((1, dl), c2),
            pl.BlockSpec((bs, dl), c2), pl.BlockSpec((bs, ntail * dl), c2),
        ],
        out_specs=(pl.BlockSpec((ls, bs, dl), c3), pl.BlockSpec((bs, dl), c2), pl.BlockSpec((bs, ntail * dl), c2)),
        scratch_shapes=[pltpu.VMEM((ls, bs, dl), F32)],
        compiler_params=pltpu.CompilerParams(
            dimension_semantics=("arbitrary",),
            vmem_limit_bytes=_vmem_limit(pipelined, [((ls, bs, dl), F32)])),
        name="lru_slab",
    )(proj3, proj3, cw, cb, wa, wx, ba, bx, lam, g, h0, tail)


def _head_lane_mask(n_heads):
    return lax.broadcasted_iota(jnp.int32, (1, HEAD_PAD), 1) < n_heads


def _gated_group_norm(y, z, g):
    yg = y * _silu(z)
    scale = lax.rsqrt(jnp.mean(yg * yg, axis=-1, keepdims=True) + EPS)
    return yg * scale * g


def _ssd_seq_kernel(z_ref, xbc_ref, dt_ref, cw_ref, cb_ref, dtb_ref, alog_ref, dskip_ref, g_ref, e3_ref,
                    s0_ref, tail0_ref, o_ref, sfin_ref, tailfin_ref, xe_ref, xc_ref, ex_ref, y_ref, s_ref,
                    *, pad, n_heads, n_groups):
    c = pl.program_id(1)
    q, ds = z_ref.shape
    dc = xbc_ref.shape[1]
    gn = (dc - ds) // 2
    n = gn // n_groups
    p = ds // n_heads
    r = n_heads // n_groups
    gw = ds // n_groups
    hpb = LANES // p

    @pl.when(c == 0)
    def _():
        xe_ref[0:HALO, :] = tail0_ref[...]
        s_ref[...] = s0_ref[...]

    xe_ref[HALO:HALO + q, :] = xbc_ref[...]
    valid = (c * q + lax.broadcasted_iota(jnp.int32, (q, 1), 0)) >= pad

    cblk = 512 if dc % 512 == 0 else LANES
    for j in range(dc // cblk):
        sl = slice(j * cblk, (j + 1) * cblk)
        ext = [xe_ref[HALO - 3 + k:HALO - 3 + k + q, sl] for k in range(CONV_TAPS)]
        v = _silu(_conv_taps(ext, 0, cw_ref, cb_ref, sl))
        if pad and (j + 1) * cblk <= ds:
            v = jnp.where(valid, v, 0.0)
        xc_ref[:, sl] = v

    dtv = jnp.where(_head_lane_mask(n_heads), jax.nn.softplus(dt_ref[...] + dtb_ref[...]), 0.0)
    if pad:
        dtv = jnp.where(valid, dtv, 0.0)
    da = dtv * (-jnp.exp(alog_ref[...]))
    ri = lax.broadcasted_iota(jnp.int32, (q, q), 0)
    ci = lax.broadcasted_iota(jnp.int32, (q, q), 1)
    causal = ci <= ri
    tri = jnp.where(causal, 1.0, 0.0).astype(BF16)
    ac3 = jnp.dot(tri, _split3(da), preferred_element_type=F32)
    acum = ac3[:, 0:HEAD_PAD] + ac3[:, HEAD_PAD:2 * HEAD_PAD] + ac3[:, 2 * HEAD_PAD:3 * HEAD_PAD]
    alast = acum[q - 1:q, :]
    eac = jnp.exp(acum)
    wend = jnp.exp(alast - acum) * dtv
    cdec = jnp.broadcast_to(jnp.exp(alast), (SUBLANES, HEAD_PAD))
    ex_ref[...] = _expand_heads(jnp.concatenate([eac, wend, cdec], axis=0), e3_ref)
    acum_t = acum.T
    dt_t = dtv.T
    lane = lax.broadcasted_iota(jnp.int32, (q, LANES), 1)

    for g in range(n_groups):
        gsl = slice(g * gw, (g + 1) * gw)
        bg = xc_ref[:, ds + g * n:ds + (g + 1) * n].astype(BF16)
        cg = xc_ref[:, ds + gn + g * n:ds + gn + (g + 1) * n].astype(BF16)
        cbm = lax.dot_general(cg, bg, (((1,), (1,)), ((), ())), preferred_element_type=F32)
        yoff = jnp.dot(cg, s_ref[:, gsl].astype(BF16), preferred_element_type=F32)
        for k in range(gw // LANES):
            lsl = slice(g * gw + k * LANES, g * gw + (k + 1) * LANES)
            xs = xc_ref[:, lsl]
            ms = []
            xparts = []
            for u in range(hpb):
                h = g * r + k * hpb + u
                seg = acum[:, h:h + 1] - acum_t[h:h + 1, :]
                lm = jnp.where(causal, jnp.exp(seg), 0.0) * dt_t[h:h + 1, :]
                ms.append((cbm * lm).astype(BF16))
                inhead = (lane >= u * p) & (lane < (u + 1) * p)
                xparts.append(jnp.where(inhead, xs, 0.0).astype(BF16))
            ydiag = jnp.dot(jnp.concatenate(ms, axis=1), jnp.concatenate(xparts, axis=0),
                            preferred_element_type=F32)
            y = ydiag + yoff[:, k * LANES:(k + 1) * LANES] * ex_ref[0:q, lsl]
            y_ref[:, lsl] = y + dskip_ref[:, lsl] * xs
        o_ref[:, gsl] = _gated_group_norm(y_ref[:, gsl], z_ref[:, gsl], g_ref[:, gsl]).astype(o_ref.dtype)
        xw = (xc_ref[:, gsl] * ex_ref[q:2 * q, gsl]).astype(BF16)
        upd = lax.dot_general(bg, xw, (((0,), (0,)), ((), ())), preferred_element_type=F32)
        s_ref[:, gsl] = ex_ref[2 * q:2 * q + 1, gsl] * s_ref[:, gsl] + upd

    xe_ref[0:HALO, :] = xe_ref[q:q + HALO, :]
    sfin_ref[0] = s_ref[...]
    tailfin_ref[0] = xe_ref[HALO - 3:HALO, :]


def _ssd_seq(proj, dt, params, s0, tail0, *, n_seq, seq_len, row0, n_heads, n_groups, dl, pad=0):
    cw, cb, dtb, alog, dskip, g, e3 = params
    dc = cw.shape[1]
    ds = dskip.shape[1]
    n = s0.shape[0]
    q = SSD_CHUNK
    assert seq_len % q == 0 and row0 % q == 0
    assert (2 * dl) % ds == 0 and (2 * dl + ds) % dc == 0 and LANES % (ds // n_heads) == 0
    n_chunks = seq_len // q
    blk0 = row0 // q
    rows = lambda b, c: blk0 + b * n_chunks + c
    const2 = lambda b, c: (0, 0)
    pipelined = [((q, ds), F32), ((q, dc), F32), ((q, HEAD_PAD), F32), ((q, ds), BF16)]
    resident = [((q + HALO, dc), F32), ((q, dc), F32), ((2 * q + SUBLANES, ds), F32), ((q, ds), F32),
                ((3 * n, ds), F32), ((6 * HEAD_PAD, ds), BF16)]
    kern = functools.partial(_ssd_seq_kernel, pad=pad, n_heads=n_heads, n_groups=n_groups)
    return pl.pallas_call(
        kern,
        out_shape=(jax.ShapeDtypeStruct((n_seq * seq_len, ds), BF16),
                   jax.ShapeDtypeStruct((n_seq, n, ds), F32),
                   jax.ShapeDtypeStruct((n_seq, CONV_TAPS - 1, dc), F32)),
        grid=(n_seq, n_chunks),
        in_specs=[
            pl.BlockSpec((q, ds), lambda b, c: (rows(b, c), (2 * dl) // ds)),
            pl.BlockSpec((q, dc), lambda b, c: (rows(b, c), (2 * dl + ds) // dc)),
            pl.BlockSpec((q, HEAD_PAD), lambda b, c: (rows(b, c), 0)),
            pl.BlockSpec((CONV_TAPS, dc), const2),
            pl.BlockSpec((1, dc), const2),
            pl.BlockSpec((1, HEAD_PAD), const2),
            pl.BlockSpec((1, HEAD_PAD), const2),
            pl.BlockSpec((1, ds), const2),
            pl.BlockSpec((1, ds), const2),
            pl.BlockSpec((3 * HEAD_PAD, ds), const2),
            pl.BlockSpec((n, ds), const2),
            pl.BlockSpec((HALO, dc), const2),
        ],
        out_specs=(
            pl.BlockSpec((q, ds), lambda b, c: (b * n_chunks + c, 0)),
            pl.BlockSpec((1, n, ds), lambda b, c: (b, 0, 0)),
            pl.BlockSpec((1, CONV_TAPS - 1, dc), lambda b, c: (b, 0, 0)),
        ),
        scratch_shapes=[pltpu.VMEM((q + HALO, dc), F32), pltpu.VMEM((q, dc), F32),
                        pltpu.VMEM((2 * q + SUBLANES, ds), F32), pltpu.VMEM((q, ds), F32),
                        pltpu.VMEM((n, ds), F32)],
        compiler_params=pltpu.CompilerParams(
            dimension_semantics=("parallel", "arbitrary"),
            vmem_limit_bytes=_vmem_limit(pipelined, resident)),
        name="ssd_seq",
    )(proj, proj, dt, cw, cb, dtb, alog, dskip, g, e3, s0, tail0)


def _ssd_slab_pre_kernel(xbc_ref, dt_ref, tail_ref, cw_ref, cb_ref, dtb_ref, alog_ref, dskip_ref, e3_ref,
                         ypart_ref, eace_ref, c_ref, b_ref, xw_ref, cdec_ref, tailfin_ref, xc_ref,
                         *, n_heads, n_groups):
    ls, bs, dc = xbc_ref.shape
    ds = dskip_ref.shape[1]
    gn = (dc - ds) // 2
    n = gn // n_groups
    r = n_heads // n_groups
    ntail = CONV_TAPS - 1

    cblk = 512 if dc % 512 == 0 else LANES
    for j in range(dc // cblk):
        sl = slice(j * cblk, (j + 1) * cblk)
        ext = [tail_ref[:, k * dc + j * cblk:k * dc + (j + 1) * cblk] for k in range(ntail)]
        ext += [xbc_ref[s, :, sl] for s in range(ls)]
        for s in range(ls):
            xc_ref[s, :, sl] = _silu(_conv_taps(ext, s, cw_ref, cb_ref, sl))
        for k in range(ntail):
            tailfin_ref[:, k * dc + j * cblk:k * dc + (j + 1) * cblk] = ext[ls + k]

    hmask = _head_lane_mask(n_heads)
    a_neg = -jnp.exp(alog_ref[...])
    dtv, acum = [], []
    run = jnp.zeros((bs, HEAD_PAD), F32)
    for s in range(ls):
        d = jnp.where(hmask, jax.nn.softplus(dt_ref[s] + dtb_ref[...]), 0.0)
        run = run + d * a_neg
        dtv.append(d)
        acum.append(run)
    alast = acum[ls - 1]
    cdec_ref[...] = jnp.exp(alast)
    head_group = lax.broadcasted_iota(jnp.int32, (1, HEAD_PAD), 1) // r

    for s in range(ls):
        eace_ref[s] = _expand_heads(jnp.exp(acum[s]), e3_ref)
        wend_e = _expand_heads(jnp.exp(alast - acum[s]) * dtv[s], e3_ref)
        xw_ref[:, s * ds:(s + 1) * ds] = xc_ref[s, :, 0:ds] * wend_e
        b_ref[:, s * gn:(s + 1) * gn] = xc_ref[s, :, ds:ds + gn]
        c_ref[:, s * gn:(s + 1) * gn] = xc_ref[s, :, ds + gn:ds + 2 * gn]
        ypart = dskip_ref[...] * xc_ref[s, :, 0:ds]
        for j in range(s + 1):
            cbh = jnp.zeros((bs, HEAD_PAD), F32)
            for g in range(n_groups):
                cs = xc_ref[s, :, ds + gn + g * n:ds + gn + (g + 1) * n]
                bj = xc_ref[j, :, ds + g * n:ds + (g + 1) * n]
                cbg = jnp.sum(cs * bj, axis=-1, keepdims=True)
                cbh = cbh + jnp.where(head_group == g, cbg, 0.0)
            coef = cbh * (jnp.exp(acum[s] - acum[j]) * dtv[j])
            ypart = ypart + _expand_heads(coef, e3_ref) * xc_ref[j, :, 0:ds]
        ypart_ref[s] = ypart
    for s in range(ls, SLAB_ROWS):
        xw_ref[:, s * ds:(s + 1) * ds] = jnp.zeros((bs, ds), F32)
        b_ref[:, s * gn:(s + 1) * gn] = jnp.zeros((bs, gn), F32)
        c_ref[:, s * gn:(s + 1) * gn] = jnp.zeros((bs, gn), F32)


def _ssd_slab_pre(proj3, dt3, tail, params, *, ls, n_heads, n_groups, dl):
    cw, cb, dtb, alog, dskip, _, e3 = params
    bs = proj3.shape[1]
    dc = cw.shape[1]
    ds = dskip.shape[1]
    gn = (dc - ds) // 2
    ntail = CONV_TAPS - 1
    assert ls <= SLAB_ROWS and (2 * dl + ds) % dc == 0
    c2 = lambda i: (0, 0)
    c3 = lambda i: (0, 0, 0)
    pipelined = [((ls, bs, dc), F32), ((ls, bs, HEAD_PAD), F32), ((bs, 2 * ntail * dc), F32),
                 ((2 * ls, bs, ds), F32), ((bs, SLAB_ROWS * (2 * gn + ds)), F32), ((3 * HEAD_PAD, ds), BF16)]
    kern = functools.partial(_ssd_slab_pre_kernel, n_heads=n_heads, n_groups=n_groups)
    return pl.pallas_call(
        kern,
        out_shape=(jax.ShapeDtypeStruct((ls, bs, ds), F32),
                   jax.ShapeDtypeStruct((ls, bs, ds), F32),
                   jax.ShapeDtypeStruct((bs, SLAB_ROWS * gn), F32),
                   jax.ShapeDtypeStruct((bs, SLAB_ROWS * gn), F32),
                   jax.ShapeDtypeStruct((bs, SLAB_ROWS * ds), F32),
                   jax.ShapeDtypeStruct((bs, HEAD_PAD), F32),
                   jax.ShapeDtypeStruct((bs, ntail * dc), F32)),
        grid=(1,),
        in_specs=[
            pl.BlockSpec((ls, bs, dc), lambda i: (0, 0, (2 * dl + ds) // dc)),
            pl.BlockSpec((ls, bs, HEAD_PAD), c3),
            pl.BlockSpec((bs, ntail * dc), c2),
            pl.BlockSpec((CONV_TAPS, dc), c2), pl.BlockSpec((1, dc), c2),
            pl.BlockSpec((1, HEAD_PAD), c2), pl.BlockSpec((1, HEAD_PAD), c2),
            pl.BlockSpec((1, ds), c2), pl.BlockSpec((3 * HEAD_PAD, ds), c2),
        ],
        out_specs=(pl.BlockSpec((ls, bs, ds), c3), pl.BlockSpec((ls, bs, ds), c3),
                   pl.BlockSpec((bs, SLAB_ROWS * gn), c2), pl.BlockSpec((bs, SLAB_ROWS * gn), c2),
                   pl.BlockSpec((bs, SLAB_ROWS * ds), c2), pl.BlockSpec((bs, HEAD_PAD), c2),
                   pl.BlockSpec((bs, ntail * dc), c2)),
        scratch_shapes=[pltpu.VMEM((ls, bs, dc), F32)],
        compiler_params=pltpu.CompilerParams(
            dimension_semantics=("arbitrary",),
            vmem_limit_bytes=_vmem_limit(pipelined, [((ls, bs, dc), F32)])),
        name="ssd_slab_pre",
    )(proj3, dt3, tail, cw, cb, dtb, alog, dskip, e3)


def _ssd_state_kernel(cdec_ref, s_ref, c_ref, b_ref, xw_ref, snew_ref, yoff_ref, *, n_heads, n_groups):
    i = pl.program_id(0)
    sb, hp, n = s_ref.shape
    p = hp // n_heads
    r = n_heads // n_groups
    gw = hp // n_groups
    for q in range(sb):
        for g in range(n_groups):
            gsl = slice(g * gw, (g + 1) * gw)
            sg = s_ref[q, gsl, :]
            cg = c_ref[q, :, g * n:(g + 1) * n].astype(BF16)
            yoff_ref[q, :, gsl] = lax.dot_general(cg, sg.astype(BF16), (((1,), (1,)), ((), ())),
                                                  preferred_element_type=F32)
            upd = lax.dot_general(xw_ref[q, :, gsl].astype(BF16), b_ref[q, :, g * n:(g + 1) * n].astype(BF16),
                                  (((0,), (0,)), ((), ())), preferred_element_type=F32)
            for u in range(r):
                h = g * r + u
                rows = slice(h * p, (h + 1) * p)
                snew_ref[q, rows, :] = cdec_ref[i * sb + q, h] * s_ref[q, rows, :] + upd[u * p:(u + 1) * p, :]


def _ssd_state(cdec, state, c_rows, b_rows, xw_rows, *, n_heads, n_groups):
    bs, hp, n = state.shape
    gn = c_rows.shape[2]
    sb = STATE_SEQS_PER_STEP if bs % STATE_SEQS_PER_STEP == 0 else 1
    per_seq = lambda i: (i, 0, 0)
    pipelined = [((sb, hp, n), F32)] * 2 + [((sb, SLAB_ROWS, gn), F32)] * 2 + [((sb, SLAB_ROWS, hp), F32)] * 2
    kern = functools.partial(_ssd_state_kernel, n_heads=n_heads, n_groups=n_groups)
    return pl.pallas_call(
        kern,
        out_shape=(jax.ShapeDtypeStruct((bs, hp, n), F32), jax.ShapeDtypeStruct((bs, SLAB_ROWS, hp), F32)),
        grid=(bs // sb,),
        in_specs=[
            pl.BlockSpec(memory_space=pltpu.SMEM),
            pl.BlockSpec((sb, hp, n), per_seq),
            pl.BlockSpec((sb, SLAB_ROWS, gn), per_seq),
            pl.BlockSpec((sb, SLAB_ROWS, gn), per_seq),
            pl.BlockSpec((sb, SLAB_ROWS, hp), per_seq),
        ],
        out_specs=(pl.BlockSpec((sb, hp, n), per_seq), pl.BlockSpec((sb, SLAB_ROWS, hp), per_seq)),
        compiler_params=pltpu.CompilerParams(
            dimension_semantics=("parallel",),
            vmem_limit_bytes=_vmem_limit(pipelined, [])),
        name="ssd_state",
    )(cdec, state, c_rows, b_rows, xw_rows)


def _ssd_slab_post_kernel(ypart_ref, eace_ref, yoff_ref, z_ref, g_ref, o_ref, *, n_groups):
    ls, bs, ds = ypart_ref.shape
    gw = ds // n_groups
    for s in range(ls):
        for g in range(n_groups):
            gsl = slice(g * gw, (g + 1) * gw)
            y = ypart_ref[s, :, gsl] + eace_ref[s, :, gsl] * yoff_ref[:, s * ds + g * gw:s * ds + (g + 1) * gw]
            o_ref[s, :, gsl] = _gated_group_norm(y, z_ref[s, :, gsl], g_ref[:, gsl]).astype(o_ref.dtype)


def _ssd_slab_post(ypart, eace, yoff, proj3, g, *, n_groups, dl):
    ls, bs, ds = ypart.shape
    assert (2 * dl) % ds == 0
    c2 = lambda i: (0, 0)
    c3 = lambda i: (0, 0, 0)
    pipelined = [((ls, bs, ds), F32)] * 3 + [((bs, SLAB_ROWS * ds), F32), ((ls, bs, ds), BF16)]
    kern = functools.partial(_ssd_slab_post_kernel, n_groups=n_groups)
    return pl.pallas_call(
        kern,
        out_shape=jax.ShapeDtypeStruct((ls, bs, ds), BF16),
        grid=(1,),
        in_specs=[
            pl.BlockSpec((ls, bs, ds), c3), pl.BlockSpec((ls, bs, ds), c3),
            pl.BlockSpec((bs, SLAB_ROWS * ds), c2),
            pl.BlockSpec((ls, bs, ds), lambda i: (0, 0, (2 * dl) // ds)),
            pl.BlockSpec((1, ds), c2),
        ],
        out_specs=pl.BlockSpec((ls, bs, ds), c3),
        compiler_params=pltpu.CompilerParams(
            dimension_semantics=("arbitrary",),
            vmem_limit_bytes=_vmem_limit(pipelined, [])),
        name="ssd_slab_post",
    )(ypart, eace, yoff, proj3, g)


def _head_expansion(n_heads, head_dim):
    rows = lax.broadcasted_iota(jnp.int32, (HEAD_PAD, n_heads * head_dim), 0)
    cols = lax.broadcasted_iota(jnp.int32, (HEAD_PAD, n_heads * head_dim), 1)
    e = (cols // head_dim == rows).astype(BF16)
    return jnp.concatenate([e, e, e], axis=0)


def _pad_lanes(v, width):
    return jnp.pad(v, ((0, 0), (0, width - v.shape[1])))


def _tail_block(tail):
    return jnp.pad(tail, ((HALO - tail.shape[0], 0), (0, 0)))


def _mlp(x1, g_mlp, w_up, w_down, g_final):
    m = x1.shape[0]
    tm = _pick_tile(m, (1024, 512, 256, 128))
    hid = _mlp_up(x1, g_mlp, w_up, tm=tm, tn=_pick_tile(w_up.shape[1], (1024, 512, 256, 128)))
    tmd = _pick_tile(m, (1024, 512, 256, 128))
    return _mlp_down_final(hid, w_down, x1, g_final, tm=tmd, tk=_pick_tile(w_down.shape[0], (1024, 512, 256, 128)))


def kernel(x_prompt, x_sample, state_lru_h, state_lru_conv, state_ssd, state_ssd_conv, meta_tokens, g_mix, w_in, conv_lru_w, conv_lru_b, lru_wa, lru_ba, lru_wx, lru_bx, lru_lambda, g_lru_out, conv_ssd_w, conv_ssd_b, dt_bias, a_log, d_skip, g_ssd_out, w_out, g_mlp, w_up, w_down, g_final):
    depth = w_in.shape[0]
    assert depth == 1, "single-layer step"
    l = 0
    bp, lp, d = x_prompt.shape
    bs, ls, _ = x_sample.shape
    n_meta = meta_tokens.shape[0]
    dl = state_lru_h.shape[-1]
    n_heads, p, n = state_ssd.shape[-3:]
    ds = n_heads * p
    dc = state_ssd_conv.shape[-1]
    gn = (dc - ds) // 2
    n_groups = gn // n
    nw = 2 * dl + ds + dc
    ntail = CONV_TAPS - 1
    q = SSD_CHUNK
    meta_pad = (-n_meta) % q
    assert n_heads <= HEAD_PAD and (bs * ls) % q == 0 and lp % q == 0 and q % bs == 0

    row = lambda v: v.reshape(1, -1).astype(F32)
    w_in_b = w_in.astype(BF16)
    w_dt = _pad_lanes(w_in[l][:, nw:], HEAD_PAD).astype(BF16)
    w_out_b = w_out.astype(BF16)
    w_up_b = w_up[l].astype(BF16)
    w_down_b = w_down[l].astype(BF16)
    lru_params = (conv_lru_w[l], row(conv_lru_b[l]), (0.5 * lru_wa[l]).astype(BF16), (0.5 * lru_wx[l]).astype(BF16),
                  row(0.5 * lru_ba[l]), row(0.5 * lru_bx[l]), row(lru_lambda[l]), row(g_lru_out[l]))
    ssd_params = (conv_ssd_w[l], row(conv_ssd_b[l]), _pad_lanes(row(dt_bias[l]), HEAD_PAD),
                  _pad_lanes(row(a_log[l]), HEAD_PAD), row(jnp.repeat(d_skip[l], p)), row(g_ssd_out[l]),
                  _head_expansion(n_heads, p))
    g_mix_r, g_mlp_r, g_final_r = row(g_mix[l]), row(g_mlp[l]), row(g_final)

    xs_tm = x_sample.transpose(1, 0, 2).reshape(ls * bs, d)
    x_side = jnp.concatenate([xs_tm, jnp.zeros((meta_pad, d), F32), meta_tokens.astype(F32)], axis=0)
    xp_rows = x_prompt.reshape(bp * lp, d)

    tn_in = _pick_tile(nw, (1024, 512, 256, 128))
    proj_side, dt_side = _in_proj(x_side, g_mix_r, w_in_b, l, w_dt, nw=nw, tm=x_side.shape[0], tn=tn_in)
    proj_p, dt_p = _in_proj(xp_rows, g_mix_r, w_in_b, l, w_dt, nw=nw,
                            tm=_pick_tile(bp * lp, (1024, 512, 256, 128)), tn=tn_in)

    meta_row0 = ls * bs
    _, m_h, m_ltail = _lru_seq(proj_side, lru_params, jnp.zeros((1, dl), F32), jnp.zeros((HALO, dl), F32),
                               n_seq=1, seq_len=q, t=q, row0=meta_row0, pad=meta_pad, reset_first=True)
    _, m_s, m_stail = _ssd_seq(proj_side, dt_side, ssd_params, jnp.zeros((n, ds), F32), jnp.zeros((HALO, dc), F32),
                               n_seq=1, seq_len=q, row0=meta_row0, n_heads=n_heads, n_groups=n_groups, dl=dl,
                               pad=meta_pad)

    lru_p, p_h, p_ltail = _lru_seq(proj_p, lru_params, m_h[0], _tail_block(m_ltail[0]),
                                   n_seq=bp, seq_len=lp, t=_pick_tile(lp, (256, 128)), row0=0)
    ssd_p, p_s, p_stail = _ssd_seq(proj_p, dt_p, ssd_params, m_s[0], _tail_block(m_stail[0]),
                                   n_seq=bp, seq_len=lp, row0=0, n_heads=n_heads, n_groups=n_groups, dl=dl)
    x1_p = _out_proj(lru_p, ssd_p, w_out_b, l, xp_rows,
                     tm=_pick_tile(bp * lp, (512, 256, 128)), tn=_pick_tile(d, (1024, 512, 256, 128)))
    y_p = _mlp(x1_p, g_mlp_r, w_up_b, w_down_b, g_final_r)

    proj_s3 = proj_side.reshape(-1, bs, nw)
    dt_s3 = dt_side.reshape(-1, bs, HEAD_PAD)
    lru_s, s_h, s_ltail = _lru_slab(proj_s3, lru_params, state_lru_h[l],
                                    state_lru_conv[l].reshape(bs, ntail * dl), ls=ls)
    ypart, eace, c_rows, b_rows, xw_rows, cdec, s_stail = _ssd_slab_pre(
        proj_s3, dt_s3, state_ssd_conv[l].reshape(bs, ntail * dc), ssd_params,
        ls=ls, n_heads=n_heads, n_groups=n_groups, dl=dl)
    s_new, yoff = _ssd_state(cdec, state_ssd[l].reshape(bs, ds, n),
                             c_rows.reshape(bs, SLAB_ROWS, gn), b_rows.reshape(bs, SLAB_ROWS, gn),
                             xw_rows.reshape(bs, SLAB_ROWS, ds), n_heads=n_heads, n_groups=n_groups)
    ssd_s = _ssd_slab_post(ypart, eace, yoff.reshape(bs, SLAB_ROWS * ds), proj_s3, ssd_params[5],
                           n_groups=n_groups, dl=dl)
    x1_s = _out_proj(lru_s.reshape(ls * bs, dl), ssd_s.reshape(ls * bs, ds), w_out_b, l, xs_tm,
                     tm=_pick_tile(ls * bs, (512, 256, 128)), tn=_pick_tile(d, (1024, 512, 256, 128)))
    y_s = _mlp(x1_s, g_mlp_r, w_up_b, w_down_b, g_final_r)

    y_prompt = y_p.reshape(bp, lp, d)
    y_sample = y_s.reshape(ls, bs, d).transpose(1, 0, 2)
    p_lru_h = p_h.reshape(1, bp, dl)
    p_lru_conv = p_ltail.reshape(1, bp, ntail, dl)
    p_ssd = p_s.transpose(0, 2, 1).reshape(1, bp, n_heads, p, n)
    p_ssd_conv = p_stail.reshape(1, bp, ntail, dc)
    s_lru_h = s_h.reshape(1, bs, dl)
    s_lru_conv = s_ltail.reshape(1, bs, ntail, dl)
    s_ssd = s_new.reshape(1, bs, n_heads, p, n)
    s_ssd_conv = s_stail.reshape(1, bs, ntail, dc)
    return (y_prompt, y_sample, p_lru_h, p_lru_conv, p_ssd, p_ssd_conv, s_lru_h, s_lru_conv, s_ssd, s_ssd_conv)
```

```python
import functools

import jax
import jax.numpy as jnp
from jax import lax
from jax.experimental import pallas as pl
from jax.experimental.pallas import tpu as pltpu

F32 = jnp.float32
BF16 = jnp.bfloat16

EPS = 1e-6
LRU_C = 8.0
CONV_TAPS = 4

LANES = 128
SUBLANES = 8
VMEM_BYTES_V7X = 64 * 1024 * 1024
VMEM_TEMP_BYTES = 10 * 1024 * 1024
VMEM_CEILING_BYTES = VMEM_BYTES_V7X - 6 * 1024 * 1024

HALO = SUBLANES
SSD_CHUNK = 128
HEAD_PAD = LANES
SLAB_ROWS = SUBLANES
STATE_SEQS_PER_STEP = 4


def _nbytes(shape, dtype):
    n = 1
    for s in shape:
        n *= s
    return n * jnp.dtype(dtype).itemsize


def _vmem_limit(pipelined, resident):
    est = 2 * sum(_nbytes(s, d) for s, d in pipelined) + sum(_nbytes(s, d) for s, d in resident)
    return int(min(est + VMEM_TEMP_BYTES, VMEM_CEILING_BYTES))


def _pick_tile(m, prefs):
    for t in prefs:
        if m % t == 0:
            return t
    return m


def _silu(x):
    h = 0.5 * x
    return h * jnp.tanh(h) + h


def _split3(x):
    hi = x.astype(BF16)
    r1 = x - hi.astype(F32)
    mid = r1.astype(BF16)
    lo = (r1 - mid.astype(F32)).astype(BF16)
    return jnp.concatenate([hi, mid, lo], axis=1)


def _expand_heads(x, e3_ref):
    return jnp.dot(_split3(x), e3_ref[...], preferred_element_type=F32)


def _rmsnorm_rows(x, g):
    ms = jnp.mean(x * x, axis=-1, keepdims=True)
    return x * lax.rsqrt(ms + EPS) * g


def _conv_taps(ext, s, cw_ref, cb_ref, sl):
    v = cb_ref[:, sl] + ext[s] * cw_ref[0:1, sl]
    for k in range(1, CONV_TAPS):
        v = v + ext[s + k] * cw_ref[k:k + 1, sl]
    return v


def _weight_tile(w_ref, wb_ref):
    if wb_ref is None:
        return w_ref[...]
    w = w_ref[...].astype(BF16)
    wb_ref[...] = w
    return w


def _weight_specs(w, layer, blk, idx):
    if layer is None:
        return pl.BlockSpec(blk, idx), None, F32
    in_spec = pl.BlockSpec((None,) + blk, lambda i, j: (layer,) + idx(i, j))
    return in_spec, pl.BlockSpec(blk, idx), w.dtype


def _in_proj_kernel(x_ref, g_ref, w_ref, wdt_ref, o_ref, dt_ref, *rest, emit_w):
    wb_ref, xn_ref = rest if emit_w else (None, rest[0])

    @pl.when(pl.program_id(1) == 0)
    def _():
        xn = _rmsnorm_rows(x_ref[...], g_ref[...]).astype(BF16)
        xn_ref[...] = xn
        dt_ref[...] = jnp.dot(xn, wdt_ref[...], preferred_element_type=F32)

    o_ref[...] = jnp.dot(xn_ref[...], _weight_tile(w_ref, wb_ref), preferred_element_type=F32)


def _in_proj(x, g, w, wdt, *, nw, tm, tn, layer=None):
    m, k = x.shape
    emit_w = layer is not None
    assert m % tm == 0 and nw % tn == 0 and (not emit_w or m == tm)
    w_spec, wb_spec, w_dtype = _weight_specs(w, layer, (k, tn), lambda i, j: (0, j))
    out_shape = [jax.ShapeDtypeStruct((m, nw), F32), jax.ShapeDtypeStruct((m, HEAD_PAD), F32)]
    out_specs = [pl.BlockSpec((tm, tn), lambda i, j: (i, j)), pl.BlockSpec((tm, HEAD_PAD), lambda i, j: (i, 0))]
    pipelined = [((tm, k), F32), ((k, tn), w_dtype if emit_w else BF16), ((k, HEAD_PAD), BF16), ((tm, tn), F32),
                 ((tm, HEAD_PAD), F32)]
    if emit_w:
        out_shape.append(jax.ShapeDtypeStruct((k, nw), BF16))
        out_specs.append(wb_spec)
        pipelined.append(((k, tn), BF16))
    return pl.pallas_call(
        functools.partial(_in_proj_kernel, emit_w=emit_w),
        out_shape=tuple(out_shape),
        grid=(m // tm, nw // tn),
        in_specs=[
            pl.BlockSpec((tm, k), lambda i, j: (i, 0)),
            pl.BlockSpec((1, k), lambda i, j: (0, 0)),
            w_spec,
            pl.BlockSpec((k, HEAD_PAD), lambda i, j: (0, 0)),
        ],
        out_specs=tuple(out_specs),
        scratch_shapes=[pltpu.VMEM((tm, k), BF16)],
        compiler_params=pltpu.CompilerParams(
            dimension_semantics=("parallel", "arbitrary"),
            vmem_limit_bytes=_vmem_limit(pipelined, [((tm, k), BF16), ((k, tn), BF16)])),
        name="in_proj",
    )(x, g, w, wdt)


def _out_proj_kernel(a1_ref, a2_ref, w1_ref, w2_ref, res_ref, o_ref, wb1_ref, wb2_ref):
    acc = jnp.dot(a1_ref[...], _weight_tile(w1_ref, wb1_ref), preferred_element_type=F32)
    acc = acc + jnp.dot(a2_ref[...], _weight_tile(w2_ref, wb2_ref), preferred_element_type=F32)
    o_ref[...] = res_ref[...] + acc


def _out_proj(a1, a2, w, layer, res, *, tn):
    m, k1 = a1.shape
    k2 = a2.shape[1]
    n = w.shape[2]
    assert n % tn == 0 and k1 == k2 and w.shape[1] == k1 + k2
    pipelined = [((m, k1), BF16), ((m, k2), BF16), ((k1, tn), w.dtype), ((k2, tn), w.dtype), ((m, tn), F32),
                 ((m, tn), F32), ((k1, tn), BF16), ((k2, tn), BF16)]
    col = lambda j: (0, j)
    return pl.pallas_call(
        _out_proj_kernel,
        out_shape=(jax.ShapeDtypeStruct((m, n), F32), jax.ShapeDtypeStruct((k1, n), BF16),
                   jax.ShapeDtypeStruct((k2, n), BF16)),
        grid=(n // tn,),
        in_specs=[
            pl.BlockSpec((m, k1), lambda j: (0, 0)),
            pl.BlockSpec((m, k2), lambda j: (0, 0)),
            pl.BlockSpec((None, k1, tn), lambda j: (layer, 0, j)),
            pl.BlockSpec((None, k2, tn), lambda j: (layer, 1, j)),
            pl.BlockSpec((m, tn), col),
        ],
        out_specs=(pl.BlockSpec((m, tn), col), pl.BlockSpec((k1, tn), col), pl.BlockSpec((k2, tn), col)),
        compiler_params=pltpu.CompilerParams(
            dimension_semantics=("parallel",),
            vmem_limit_bytes=_vmem_limit(pipelined, [])),
        name="out_proj",
    )(a1, a2, w, w, res)


def _mlp_up_kernel(x_ref, g_ref, w_ref, o_ref, *rest, emit_w):
    wb_ref, xn_ref = rest if emit_w else (None, rest[0])

    @pl.when(pl.program_id(1) == 0)
    def _():
        xn_ref[...] = _rmsnorm_rows(x_ref[...], g_ref[...]).astype(BF16)

    acc = jnp.dot(xn_ref[...], _weight_tile(w_ref, wb_ref), preferred_element_type=F32)
    o_ref[...] = jnp.square(jnp.maximum(acc, 0.0)).astype(o_ref.dtype)


def _mlp_up(x, g, w, *, tm, tn, layer=None):
    m, k = x.shape
    n = w.shape[-1]
    emit_w = layer is not None
    assert m % tm == 0 and n % tn == 0 and (not emit_w or m == tm)
    w_spec, wb_spec, w_dtype = _weight_specs(w, layer, (k, tn), lambda i, j: (0, j))
    out_shape = [jax.ShapeDtypeStruct((m, n), BF16)]
    out_specs = [pl.BlockSpec((tm, tn), lambda i, j: (i, j))]
    pipelined = [((tm, k), F32), ((k, tn), w_dtype if emit_w else BF16), ((tm, tn), BF16)]
    if emit_w:
        out_shape.append(jax.ShapeDtypeStruct((k, n), BF16))
        out_specs.append(wb_spec)
        pipelined.append(((k, tn), BF16))
    return pl.pallas_call(
        functools.partial(_mlp_up_kernel, emit_w=emit_w),
        out_shape=tuple(out_shape),
        grid=(m // tm, n // tn),
        in_specs=[
            pl.BlockSpec((tm, k), lambda i, j: (i, 0)),
            pl.BlockSpec((1, k), lambda i, j: (0, 0)),
            w_spec,
        ],
        out_specs=tuple(out_specs),
        scratch_shapes=[pltpu.VMEM((tm, k), BF16)],
        compiler_params=pltpu.CompilerParams(
            dimension_semantics=("parallel", "arbitrary"),
            vmem_limit_bytes=_vmem_limit(pipelined, [((tm, k), BF16), ((k, tn), BF16)])),
        name="mlp_up",
    )(x, g, w)


def _mlp_down_kernel(h_ref, w_ref, res_ref, g_ref, o_ref, *rest, emit_w):
    wb_ref = rest[0] if emit_w else None
    kk = pl.program_id(1)

    @pl.when(kk == 0)
    def _():
        o_ref[...] = res_ref[...]

    o_ref[...] += jnp.dot(h_ref[...], _weight_tile(w_ref, wb_ref), preferred_element_type=F32)

    @pl.when(kk == pl.num_programs(1) - 1)
    def _():
        o_ref[...] = _rmsnorm_rows(o_ref[...], g_ref[...])


def _mlp_down_final(h, w, res, g, *, tm, tk, layer=None):
    m, k = h.shape
    n = w.shape[-1]
    emit_w = layer is not None
    assert m % tm == 0 and k % tk == 0 and (not emit_w or m == tm)
    w_spec, wb_spec, w_dtype = _weight_specs(w, layer, (tk, n), lambda i, j: (j, 0))
    out_shape = [jax.ShapeDtypeStruct((m, n), F32)]
    out_specs = [pl.BlockSpec((tm, n), lambda i, j: (i, 0))]
    pipelined = [((tm, tk), BF16), ((tk, n), w_dtype if emit_w else BF16), ((tm, n), F32), ((tm, n), F32)]
    if emit_w:
        out_shape.append(jax.ShapeDtypeStruct((k, n), BF16))
        out_specs.append(wb_spec)
        pipelined.append(((tk, n), BF16))
    return pl.pallas_call(
        functools.partial(_mlp_down_kernel, emit_w=emit_w),
        out_shape=tuple(out_shape),
        grid=(m // tm, k // tk),
        in_specs=[
            pl.BlockSpec((tm, tk), lambda i, j: (i, j)),
            w_spec,
            pl.BlockSpec((tm, n), lambda i, j: (i, 0)),
            pl.BlockSpec((1, n), lambda i, j: (0, 0)),
        ],
        out_specs=tuple(out_specs),
        compiler_params=pltpu.CompilerParams(
            dimension_semantics=("parallel", "arbitrary"),
            vmem_limit_bytes=_vmem_limit(pipelined, [((tk, n), BF16)] if emit_w else [])),
        name="mlp_down",
    )(h, w, res, g)


def _lru_gates(xh, wa_half, wx_half, ba_half, bx_half, hsp):
    xb = xh.astype(BF16)
    tr = jnp.tanh(jnp.dot(xb, wa_half, preferred_element_type=F32) + ba_half)
    ti = jnp.tanh(jnp.dot(xb, wx_half, preferred_element_type=F32) + bx_half)
    nla = tr * hsp + hsp
    a = jnp.exp(-nla)
    q = jnp.tanh(nla) * (1.0 + a * a)
    mult = jnp.where(q > 0.0, q * lax.rsqrt(q), 0.0)
    return a, mult, 0.5 * ti + 0.5


def _scan_rows(a, b, h_prev):
    t, hd = a.shape
    g = t // SUBLANES
    a3 = a.reshape(g, SUBLANES, hd)
    b3 = b.reshape(g, SUBLANES, hd)
    sub = lax.broadcasted_iota(jnp.int32, (g, SUBLANES, hd), 1)
    d = 1
    while d < SUBLANES:
        keep = sub >= d
        a_sh = jnp.where(keep, pltpu.roll(a3, d, axis=1), 1.0)
        b_sh = jnp.where(keep, pltpu.roll(b3, d, axis=1), 0.0)
        b3 = a3 * b_sh + b3
        a3 = a3 * a_sh
        d *= 2
    tiles = []
    h = h_prev
    for k in range(g):
        hk = a3[k] * h + b3[k]
        tiles.append(hk)
        h = hk[SUBLANES - 1:SUBLANES, :]
    return jnp.concatenate(tiles, axis=0), h


def _lru_seq_kernel(gate_ref, x_ref, cw_ref, cb_ref, wa_ref, wx_ref, ba_ref, bx_ref, lam_ref, g_ref,
                    h0_ref, tail0_ref, *rest, pad, reset_first, fuse_out):
    wo_ref = rest[0] if fuse_out else None
    o_ref, hfin_ref, tailfin_ref, xe_ref, hc_ref, y_ref = rest[1:] if fuse_out else rest
    c = pl.program_id(1)
    t, dl = x_ref.shape
    nh, hd = wa_ref.shape[0], wa_ref.shape[1]

    @pl.when(c == 0)
    def _():
        xe_ref[0:HALO, :] = tail0_ref[...]
        hc_ref[...] = h0_ref[...]

    xe_ref[HALO:HALO + t, :] = x_ref[...]
    grow = c * t + lax.broadcasted_iota(jnp.int32, (t, hd), 0)
    ssq = jnp.zeros((t, hd), F32)
    for h in range(nh):
        sl = slice(h * hd, (h + 1) * hd)
        ext = [xe_ref[HALO - 3 + k:HALO - 3 + k + t, sl] for k in range(CONV_TAPS)]
        xh = _conv_taps(ext, 0, cw_ref, cb_ref, sl)
        hsp = (0.5 * LRU_C) * jax.nn.softplus(-lam_ref[:, sl])
        a, mult, i = _lru_gates(xh, wa_ref[h], wx_ref[h], ba_ref[:, sl], bx_ref[:, sl], hsp)
        if reset_first:
            mult = jnp.where(grow == pad, 1.0, mult)
        b = mult * i * xh
        if pad:
            a = jnp.where(grow >= pad, a, 1.0)
            b = jnp.where(grow >= pad, b, 0.0)
        hs, h_last = _scan_rows(a, b, hc_ref[:, sl])
        hc_ref[:, sl] = h_last
        y = hs * jax.nn.gelu(gate_ref[:, sl])
        y_ref[:, sl] = y
        ssq = ssq + y * y
    scale = lax.rsqrt(jnp.sum(ssq, axis=-1, keepdims=True) / dl + EPS)
    yn = (y_ref[...] * scale * g_ref[...]).astype(BF16)
    if fuse_out:
        o_ref[...] = jnp.dot(yn, wo_ref[...], preferred_element_type=F32)
    else:
        o_ref[...] = yn
    xe_ref[0:HALO, :] = xe_ref[t:t + HALO, :]
    hfin_ref[0] = hc_ref[...]
    tailfin_ref[0] = xe_ref[HALO - 3:HALO, :]


def _lru_seq(proj, params, h0, tail0, *, n_seq, seq_len, t, row0, pad=0, reset_first=False, wo=None):
    cw, cb, wa, wx, ba, bx, lam, g = params
    dl = cw.shape[1]
    nh, hd = wa.shape[0], wa.shape[1]
    assert seq_len % t == 0 and row0 % t == 0 and t % SUBLANES == 0
    n_chunks = seq_len // t
    blk0 = row0 // t
    rows = lambda b, c: blk0 + b * n_chunks + c
    const2 = lambda b, c: (0, 0)
    fuse_out = wo is not None
    dout, out_dtype = (wo.shape[1], F32) if fuse_out else (dl, BF16)
    pipelined = [((t, dl), F32), ((t, dl), F32), ((t, dout), out_dtype)]
    resident = [((t + HALO, dl), F32), ((t, dl), F32), ((4 * nh, hd, hd), BF16)]
    extra_specs, extra_args = [], []
    if fuse_out:
        extra_specs.append(pl.BlockSpec((dl, dout), const2, pipeline_mode=pl.Buffered(1)))
        extra_args.append(wo)
        resident.append(((dl, dout), BF16))
    kern = functools.partial(_lru_seq_kernel, pad=pad, reset_first=reset_first, fuse_out=fuse_out)
    return pl.pallas_call(
        kern,
        out_shape=(jax.ShapeDtypeStruct((n_seq * seq_len, dout), out_dtype),
                   jax.ShapeDtypeStruct((n_seq, 1, dl), F32),
                   jax.ShapeDtypeStruct((n_seq, CONV_TAPS - 1, dl), F32)),
        grid=(n_seq, n_chunks),
        in_specs=[
            pl.BlockSpec((t, dl), lambda b, c: (rows(b, c), 0)),
            pl.BlockSpec((t, dl), lambda b, c: (rows(b, c), 1)),
            pl.BlockSpec((CONV_TAPS, dl), const2),
            pl.BlockSpec((1, dl), const2),
            pl.BlockSpec((nh, hd, hd), lambda b, c: (0, 0, 0)),
            pl.BlockSpec((nh, hd, hd), lambda b, c: (0, 0, 0)),
            pl.BlockSpec((1, dl), const2),
            pl.BlockSpec((1, dl), const2),
            pl.BlockSpec((1, dl), const2),
            pl.BlockSpec((1, dl), const2),
            pl.BlockSpec((1, dl), const2),
            pl.BlockSpec((HALO, dl), const2),
        ] + extra_specs,
        out_specs=(
            pl.BlockSpec((t, dout), lambda b, c: (b * n_chunks + c, 0)),
            pl.BlockSpec((1, 1, dl), lambda b, c: (b, 0, 0)),
            pl.BlockSpec((1, CONV_TAPS - 1, dl), lambda b, c: (b, 0, 0)),
        ),
        scratch_shapes=[pltpu.VMEM((t + HALO, dl), F32), pltpu.VMEM((1, dl), F32), pltpu.VMEM((t, dl), F32)],
        compiler_params=pltpu.CompilerParams(
            dimension_semantics=("parallel", "arbitrary"),
            vmem_limit_bytes=_vmem_limit(pipelined, resident)),
        name="lru_seq",
    )(proj, proj, cw, cb, wa, wx, ba, bx, lam, g, h0, tail0, *extra_args)


def _lru_slab_kernel(gate_ref, x_ref, cw_ref, cb_ref, wa_ref, wx_ref, ba_ref, bx_ref, lam_ref, g_ref,
                     h0_ref, tail_ref, o_ref, hfin_ref, tailfin_ref, y_ref):
    ls, bs, dl = x_ref.shape
    nh, hd = wa_ref.shape[0], wa_ref.shape[1]
    ntail = CONV_TAPS - 1
    for h in range(nh):
        sl = slice(h * hd, (h + 1) * hd)
        ext = [tail_ref[:, k * dl + h * hd:k * dl + (h + 1) * hd] for k in range(ntail)]
        ext += [x_ref[s, :, sl] for s in range(ls)]
        hsp = (0.5 * LRU_C) * jax.nn.softplus(-lam_ref[:, sl])
        hcur = h0_ref[:, sl]
        for s in range(ls):
            xh = _conv_taps(ext, s, cw_ref, cb_ref, sl)
            a, mult, i = _lru_gates(xh, wa_ref[h], wx_ref[h], ba_ref[:, sl], bx_ref[:, sl], hsp)
            hcur = a * hcur + mult * i * xh
            y_ref[s, :, sl] = hcur * jax.nn.gelu(gate_ref[s, :, sl])
        hfin_ref[:, sl] = hcur
        for k in range(ntail):
            tailfin_ref[:, k * dl + h * hd:k * dl + (h + 1) * hd] = ext[ls + k]
    for s in range(ls):
        y = y_ref[s]
        scale = lax.rsqrt(jnp.mean(y * y, axis=-1, keepdims=True) + EPS)
        o_ref[s] = (y * scale * g_ref[...]).astype(o_ref.dtype)


def _lru_slab(proj3, params, h0, tail, *, ls):
    cw, cb, wa, wx, ba, bx, lam, g = params
    bs = proj3.shape[1]
    dl = cw.shape[1]
    nh, hd = wa.shape[0], wa.shape[1]
    ntail = CONV_TAPS - 1
    c2 = lambda i: (0, 0)
    c3 = lambda i: (0, 0, 0)
    pipelined = [((ls, bs, dl), F32)] * 2 + [((ls, bs, dl), BF16)] + [((bs, (2 * ntail + 2) * dl), F32)]
    return pl.pallas_call(
        _lru_slab_kernel,
        out_shape=(jax.ShapeDtypeStruct((ls, bs, dl), BF16),
                   jax.ShapeDtypeStruct((bs, dl), F32),
                   jax.ShapeDtypeStruct((bs, ntail * dl), F32)),
        grid=(1,),
        in_specs=[
            pl.BlockSpec((ls, bs, dl), lambda i: (0, 0, 0)),
            pl.BlockSpec((ls, bs, dl), lambda i: (0, 0, 1)),
            pl.BlockSpec((CONV_TAPS, dl), c2), pl.BlockSpec((1, dl), c2),
            pl.BlockSpec((nh, hd, hd), c3), pl.BlockSpec((nh, hd, hd), c3),
            pl.BlockSpec((1, dl), c2), pl.BlockSpec((1, dl), c2), pl.BlockSpec((1, dl), c2), pl.BlockSpec((1, dl), c2),
            pl.BlockSpec((bs, dl), c2), pl.BlockSpec((bs, ntail * dl), c2),
        ],
        out_specs=(pl.BlockSpec((ls, bs, dl), c3), pl.BlockSpec((bs, dl), c2), pl.BlockSpec((bs, ntail * dl), c2)),
        scratch_shapes=[pltpu.VMEM((ls, bs, dl), F32)],
        compiler_params=pltpu.CompilerParams(
            dimension_semantics=("arbitrary",),
            vmem_limit_bytes=_vmem_limit(pipelined, [((ls, bs, dl), F32)])),
        name="lru_slab",
    )(proj3, proj3, cw, cb, wa, wx, ba, bx, lam, g, h0, tail)


def _head_lane_mask(n_heads):
    return lax.broadcasted_iota(jnp.int32, (1, HEAD_PAD), 1) < n_heads


def _gated_group_norm(y, z, g):
    yg = y * _silu(z)
    scale = lax.rsqrt(jnp.mean(yg * yg, axis=-1, keepdims=True) + EPS)
    return yg * scale * g


def _ssd_seq_kernel(z_ref, xbc_ref, dt_ref, cw_ref, cb_ref, dtb_ref, alog_ref, dskip_ref, g_ref, e3_ref,
                    s0_ref, tail0_ref, *rest, pad, n_heads, n_groups, fuse_out):
    if fuse_out:
        wo_ref, part_ref, res_ref = rest[:3]
        o_ref, sfin_ref, tailfin_ref, xe_ref, xc_ref, ex_ref, y_ref, s_ref, yn_ref = rest[3:]
    else:
        o_ref, sfin_ref, tailfin_ref, xe_ref, xc_ref, ex_ref, y_ref, s_ref = rest
        yn_ref = o_ref
    c = pl.program_id(1)
    q, ds = z_ref.shape
    dc = xbc_ref.shape[1]
    gn = (dc - ds) // 2
    n = gn // n_groups
    p = ds // n_heads
    r = n_heads // n_groups
    gw = ds // n_groups
    hpb = LANES // p

    @pl.when(c == 0)
    def _():
        xe_ref[0:HALO, :] = tail0_ref[...]
        s_ref[...] = s0_ref[...]

    xe_ref[HALO:HALO + q, :] = xbc_ref[...]
    valid = (c * q + lax.broadcasted_iota(jnp.int32, (q, 1), 0)) >= pad

    cblk = 512 if dc % 512 == 0 else LANES
    for j in range(dc // cblk):
        sl = slice(j * cblk, (j + 1) * cblk)
        ext = [xe_ref[HALO - 3 + k:HALO - 3 + k + q, sl] for k in range(CONV_TAPS)]
        v = _silu(_conv_taps(ext, 0, cw_ref, cb_ref, sl))
        if pad and (j + 1) * cblk <= ds:
            v = jnp.where(valid, v, 0.0)
        xc_ref[:, sl] = v

    dtv = jnp.where(_head_lane_mask(n_heads), jax.nn.softplus(dt_ref[...] + dtb_ref[...]), 0.0)
    if pad:
        dtv = jnp.where(valid, dtv, 0.0)
    da = dtv * (-jnp.exp(alog_ref[...]))
    ri = lax.broadcasted_iota(jnp.int32, (q, q), 0)
    ci = lax.broadcasted_iota(jnp.int32, (q, q), 1)
    causal = ci <= ri
    tri = jnp.where(causal, 1.0, 0.0).astype(BF16)
    ac3 = jnp.dot(tri, _split3(da), preferred_element_type=F32)
    acum = ac3[:, 0:HEAD_PAD] + ac3[:, HEAD_PAD:2 * HEAD_PAD] + ac3[:, 2 * HEAD_PAD:3 * HEAD_PAD]
    alast = acum[q - 1:q, :]
    eac = jnp.exp(acum)
    wend = jnp.exp(alast - acum) * dtv
    cdec = jnp.broadcast_to(jnp.exp(alast), (SUBLANES, HEAD_PAD))
    ex_ref[...] = _expand_heads(jnp.concatenate([eac, wend, cdec], axis=0), e3_ref)
    acum_t = acum.T
    dt_t = dtv.T
    lane = lax.broadcasted_iota(jnp.int32, (q, LANES), 1)

    for g in range(n_groups):
        gsl = slice(g * gw, (g + 1) * gw)
        bg = xc_ref[:, ds + g * n:ds + (g + 1) * n].astype(BF16)
        cg = xc_ref[:, ds + gn + g * n:ds + gn + (g + 1) * n].astype(BF16)
        cbm = lax.dot_general(cg, bg, (((1,), (1,)), ((), ())), preferred_element_type=F32)
        yoff = jnp.dot(cg, s_ref[:, gsl].astype(BF16), preferred_element_type=F32)
        for k in range(gw // LANES):
            lsl = slice(g * gw + k * LANES, g * gw + (k + 1) * LANES)
            xs = xc_ref[:, lsl]
            ms = []
            xparts = []
            for u in range(hpb):
                h = g * r + k * hpb + u
                seg = acum[:, h:h + 1] - acum_t[h:h + 1, :]
                lm = jnp.where(causal, jnp.exp(seg), 0.0) * dt_t[h:h + 1, :]
                ms.append((cbm * lm).astype(BF16))
                inhead = (lane >= u * p) & (lane < (u + 1) * p)
                xparts.append(jnp.where(inhead, xs, 0.0).astype(BF16))
            ydiag = jnp.dot(jnp.concatenate(ms, axis=1), jnp.concatenate(xparts, axis=0),
                            preferred_element_type=F32)
            y = ydiag + yoff[:, k * LANES:(k + 1) * LANES] * ex_ref[0:q, lsl]
            y_ref[:, lsl] = y + dskip_ref[:, lsl] * xs
        yn_ref[:, gsl] = _gated_group_norm(y_ref[:, gsl], z_ref[:, gsl], g_ref[:, gsl]).astype(BF16)
        xw = (xc_ref[:, gsl] * ex_ref[q:2 * q, gsl]).astype(BF16)
        upd = lax.dot_general(bg, xw, (((0,), (0,)), ((), ())), preferred_element_type=F32)
        s_ref[:, gsl] = ex_ref[2 * q:2 * q + 1, gsl] * s_ref[:, gsl] + upd

    if fuse_out:
        mix = part_ref[...] + jnp.dot(yn_ref[...], wo_ref[...], preferred_element_type=F32)
        o_ref[...] = res_ref[...] + mix
    xe_ref[0:HALO, :] = xe_ref[q:q + HALO, :]
    sfin_ref[0] = s_ref[...]
    tailfin_ref[0] = xe_ref[HALO - 3:HALO, :]


def _ssd_seq(proj, dt, params, s0, tail0, *, n_seq, seq_len, row0, n_heads, n_groups, dl, pad=0, fuse=None):
    cw, cb, dtb, alog, dskip, g, e3 = params
    dc = cw.shape[1]
    ds = dskip.shape[1]
    n = s0.shape[0]
    q = SSD_CHUNK
    assert seq_len % q == 0 and row0 % q == 0
    assert (2 * dl) % ds == 0 and (2 * dl + ds) % dc == 0 and LANES % (ds // n_heads) == 0
    n_chunks = seq_len // q
    blk0 = row0 // q
    rows = lambda b, c: blk0 + b * n_chunks + c
    const2 = lambda b, c: (0, 0)
    out_rows = lambda b, c: (b * n_chunks + c, 0)
    fuse_out = fuse is not None
    dout, out_dtype = (fuse[0].shape[1], F32) if fuse_out else (ds, BF16)
    pipelined = [((q, ds), F32), ((q, dc), F32), ((q, HEAD_PAD), F32), ((q, dout), out_dtype)]
    resident = [((q + HALO, dc), F32), ((q, dc), F32), ((2 * q + SUBLANES, ds), F32), ((q, ds), F32),
                ((3 * n, ds), F32), ((6 * HEAD_PAD, ds), BF16)]
    scratch = [pltpu.VMEM((q + HALO, dc), F32), pltpu.VMEM((q, dc), F32),
               pltpu.VMEM((2 * q + SUBLANES, ds), F32), pltpu.VMEM((q, ds), F32), pltpu.VMEM((n, ds), F32)]
    extra_specs, extra_args = [], []
    if fuse_out:
        extra_specs = [pl.BlockSpec((ds, dout), const2, pipeline_mode=pl.Buffered(1)),
                       pl.BlockSpec((q, dout), out_rows), pl.BlockSpec((q, dout), out_rows)]
        extra_args = list(fuse)
        pipelined += [((q, dout), F32)] * 2
        resident += [((ds, dout), BF16), ((q, ds), BF16)]
        scratch.append(pltpu.VMEM((q, ds), BF16))
    kern = functools.partial(_ssd_seq_kernel, pad=pad, n_heads=n_heads, n_groups=n_groups, fuse_out=fuse_out)
    return pl.pallas_call(
        kern,
        out_shape=(jax.ShapeDtypeStruct((n_seq * seq_len, dout), out_dtype),
                   jax.ShapeDtypeStruct((n_seq, n, ds), F32),
                   jax.ShapeDtypeStruct((n_seq, CONV_TAPS - 1, dc), F32)),
        grid=(n_seq, n_chunks),
        in_specs=[
            pl.BlockSpec((q, ds), lambda b, c: (rows(b, c), (2 * dl) // ds)),
            pl.BlockSpec((q, dc), lambda b, c: (rows(b, c), (2 * dl + ds) // dc)),
            pl.BlockSpec((q, HEAD_PAD), lambda b, c: (rows(b, c), 0)),
            pl.BlockSpec((CONV_TAPS, dc), const2),
            pl.BlockSpec((1, dc), const2),
            pl.BlockSpec((1, HEAD_PAD), const2),
            pl.BlockSpec((1, HEAD_PAD), const2),
            pl.BlockSpec((1, ds), const2),
            pl.BlockSpec((1, ds), const2),
            pl.BlockSpec((3 * HEAD_PAD, ds), const2),
            pl.BlockSpec((n, ds), const2),
            pl.BlockSpec((HALO, dc), const2),
        ] + extra_specs,
        out_specs=(
            pl.BlockSpec((q, dout), out_rows),
            pl.BlockSpec((1, n, ds), lambda b, c: (b, 0, 0)),
            pl.BlockSpec((1, CONV_TAPS - 1, dc), lambda b, c: (b, 0, 0)),
        ),
        scratch_shapes=scratch,
        compiler_params=pltpu.CompilerParams(
            dimension_semantics=("parallel", "arbitrary"),
            vmem_limit_bytes=_vmem_limit(pipelined, resident)),
        name="ssd_seq",
    )(proj, proj, dt, cw, cb, dtb, alog, dskip, g, e3, s0, tail0, *extra_args)


def _ssd_slab_pre_kernel(xbc_ref, dt_ref, tail_ref, cw_ref, cb_ref, dtb_ref, alog_ref, dskip_ref, e3_ref,
                         ypart_ref, eace_ref, c_ref, b_ref, xw_ref, cdec_ref, tailfin_ref, xc_ref,
                         *, n_heads, n_groups):
    ls, bs, dc = xbc_ref.shape
    ds = dskip_ref.shape[1]
    gn = (dc - ds) // 2
    n = gn // n_groups
    r = n_heads // n_groups
    ntail = CONV_TAPS - 1

    cblk = 512 if dc % 512 == 0 else LANES
    for j in range(dc // cblk):
        sl = slice(j * cblk, (j + 1) * cblk)
        ext = [tail_ref[:, k * dc + j * cblk:k * dc + (j + 1) * cblk] for k in range(ntail)]
        ext += [xbc_ref[s, :, sl] for s in range(ls)]
        for s in range(ls):
            xc_ref[s, :, sl] = _silu(_conv_taps(ext, s, cw_ref, cb_ref, sl))
        for k in range(ntail):
            tailfin_ref[:, k * dc + j * cblk:k * dc + (j + 1) * cblk] = ext[ls + k]

    hmask = _head_lane_mask(n_heads)
    a_neg = -jnp.exp(alog_ref[...])
    dtv, acum = [], []
    run = jnp.zeros((bs, HEAD_PAD), F32)
    for s in range(ls):
        d = jnp.where(hmask, jax.nn.softplus(dt_ref[s] + dtb_ref[...]), 0.0)
        run = run + d * a_neg
        dtv.append(d)
        acum.append(run)
    alast = acum[ls - 1]
    cdec_ref[...] = jnp.exp(alast)
    head_group = lax.broadcasted_iota(jnp.int32, (1, HEAD_PAD), 1) // r

    for s in range(ls):
        eace_ref[s] = _expand_heads(jnp.exp(acum[s]), e3_ref)
        wend_e = _expand_heads(jnp.exp(alast - acum[s]) * dtv[s], e3_ref)
        xw_ref[:, s * ds:(s + 1) * ds] = xc_ref[s, :, 0:ds] * wend_e
        b_ref[:, s * gn:(s + 1) * gn] = xc_ref[s, :, ds:ds + gn]
        c_ref[:, s * gn:(s + 1) * gn] = xc_ref[s, :, ds + gn:ds + 2 * gn]
        ypart = dskip_ref[...] * xc_ref[s, :, 0:ds]
        for j in range(s + 1):
            cbh = jnp.zeros((bs, HEAD_PAD), F32)
            for g in range(n_groups):
                cs = xc_ref[s, :, ds + gn + g * n:ds + gn + (g + 1) * n]
                bj = xc_ref[j, :, ds + g * n:ds + (g + 1) * n]
                cbg = jnp.sum(cs * bj, axis=-1, keepdims=True)
                cbh = cbh + jnp.where(head_group == g, cbg, 0.0)
            coef = cbh * (jnp.exp(acum[s] - acum[j]) * dtv[j])
            ypart = ypart + _expand_heads(coef, e3_ref) * xc_ref[j, :, 0:ds]
        ypart_ref[s] = ypart
    for s in range(ls, SLAB_ROWS):
        xw_ref[:, s * ds:(s + 1) * ds] = jnp.zeros((bs, ds), F32)
        b_ref[:, s * gn:(s + 1) * gn] = jnp.zeros((bs, gn), F32)
        c_ref[:, s * gn:(s + 1) * gn] = jnp.zeros((bs, gn), F32)


def _ssd_slab_pre(proj3, dt3, tail, params, *, ls, n_heads, n_groups, dl):
    cw, cb, dtb, alog, dskip, _, e3 = params
    bs = proj3.shape[1]
    dc = cw.shape[1]
    ds = dskip.shape[1]
    gn = (dc - ds) // 2
    ntail = CONV_TAPS - 1
    assert ls <= SLAB_ROWS and (2 * dl + ds) % dc == 0
    c2 = lambda i: (0, 0)
    c3 = lambda i: (0, 0, 0)
    pipelined = [((ls, bs, dc), F32), ((ls, bs, HEAD_PAD), F32), ((bs, 2 * ntail * dc), F32),
                 ((2 * ls, bs, ds), F32), ((bs, SLAB_ROWS * (2 * gn + ds)), F32), ((3 * HEAD_PAD, ds), BF16)]
    kern = functools.partial(_ssd_slab_pre_kernel, n_heads=n_heads, n_groups=n_groups)
    return pl.pallas_call(
        kern,
        out_shape=(jax.ShapeDtypeStruct((ls, bs, ds), F32),
                   jax.ShapeDtypeStruct((ls, bs, ds), F32),
                   jax.ShapeDtypeStruct((bs, SLAB_ROWS * gn), F32),
                   jax.ShapeDtypeStruct((bs, SLAB_ROWS * gn), F32),
                   jax.ShapeDtypeStruct((bs, SLAB_ROWS * ds), F32),
                   jax.ShapeDtypeStruct((bs, HEAD_PAD), F32),
                   jax.ShapeDtypeStruct((bs, ntail * dc), F32)),
        grid=(1,),
        in_specs=[
            pl.BlockSpec((ls, bs, dc), lambda i: (0, 0, (2 * dl + ds) // dc)),
            pl.BlockSpec((ls, bs, HEAD_PAD), c3),
            pl.BlockSpec((bs, ntail * dc), c2),
            pl.BlockSpec((CONV_TAPS, dc), c2), pl.BlockSpec((1, dc), c2),
            pl.BlockSpec((1, HEAD_PAD), c2), pl.BlockSpec((1, HEAD_PAD), c2),
            pl.BlockSpec((1, ds), c2), pl.BlockSpec((3 * HEAD_PAD, ds), c2),
        ],
        out_specs=(pl.BlockSpec((ls, bs, ds), c3), pl.BlockSpec((ls, bs, ds), c3),
                   pl.BlockSpec((bs, SLAB_ROWS * gn), c2), pl.BlockSpec((bs, SLAB_ROWS * gn), c2),
                   pl.BlockSpec((bs, SLAB_ROWS * ds), c2), pl.BlockSpec((bs, HEAD_PAD), c2),
                   pl.BlockSpec((bs, ntail * dc), c2)),
        scratch_shapes=[pltpu.VMEM((ls, bs, dc), F32)],
        compiler_params=pltpu.CompilerParams(
            dimension_semantics=("arbitrary",),
            vmem_limit_bytes=_vmem_limit(pipelined, [((ls, bs, dc), F32)])),
        name="ssd_slab_pre",
    )(proj3, dt3, tail, cw, cb, dtb, alog, dskip, e3)


def _ssd_state_kernel(cdec_ref, s_ref, c_ref, b_ref, xw_ref, snew_ref, yoff_ref, *, n_heads, n_groups):
    i = pl.program_id(0)
    sb, hp, n = s_ref.shape
    p = hp // n_heads
    r = n_heads // n_groups
    gw = hp // n_groups
    for q in range(sb):
        for g in range(n_groups):
            gsl = slice(g * gw, (g + 1) * gw)
            sg = s_ref[q, gsl, :]
            cg = c_ref[q, :, g * n:(g + 1) * n].astype(BF16)
            yoff_ref[q, :, gsl] = lax.dot_general(cg, sg.astype(BF16), (((1,), (1,)), ((), ())),
                                                  preferred_element_type=F32)
            upd = lax.dot_general(xw_ref[q, :, gsl].astype(BF16), b_ref[q, :, g * n:(g + 1) * n].astype(BF16),
                                  (((0,), (0,)), ((), ())), preferred_element_type=F32)
            for u in range(r):
                h = g * r + u
                rows = slice(h * p, (h + 1) * p)
                snew_ref[q, rows, :] = cdec_ref[i * sb + q, h] * s_ref[q, rows, :] + upd[u * p:(u + 1) * p, :]


def _ssd_state(cdec, state, c_rows, b_rows, xw_rows, *, n_heads, n_groups):
    bs, hp, n = state.shape
    gn = c_rows.shape[2]
    sb = STATE_SEQS_PER_STEP if bs % STATE_SEQS_PER_STEP == 0 else 1
    per_seq = lambda i: (i, 0, 0)
    pipelined = [((sb, hp, n), F32)] * 2 + [((sb, SLAB_ROWS, gn), F32)] * 2 + [((sb, SLAB_ROWS, hp), F32)] * 2
    kern = functools.partial(_ssd_state_kernel, n_heads=n_heads, n_groups=n_groups)
    return pl.pallas_call(
        kern,
        out_shape=(jax.ShapeDtypeStruct((bs, hp, n), F32), jax.ShapeDtypeStruct((bs, SLAB_ROWS, hp), F32)),
        grid=(bs // sb,),
        in_specs=[
            pl.BlockSpec(memory_space=pltpu.SMEM),
            pl.BlockSpec((sb, hp, n), per_seq),
            pl.BlockSpec((sb, SLAB_ROWS, gn), per_seq),
            pl.BlockSpec((sb, SLAB_ROWS, gn), per_seq),
            pl.BlockSpec((sb, SLAB_ROWS, hp), per_seq),
        ],
        out_specs=(pl.BlockSpec((sb, hp, n), per_seq), pl.BlockSpec((sb, SLAB_ROWS, hp), per_seq)),
        compiler_params=pltpu.CompilerParams(
            dimension_semantics=("parallel",),
            vmem_limit_bytes=_vmem_limit(pipelined, [])),
        name="ssd_state",
    )(cdec, state, c_rows, b_rows, xw_rows)


def _ssd_slab_post_kernel(ypart_ref, eace_ref, yoff_ref, z_ref, g_ref, o_ref, *, n_groups):
    ls, bs, ds = ypart_ref.shape
    gw = ds // n_groups
    for s in range(ls):
        for g in range(n_groups):
            gsl = slice(g * gw, (g + 1) * gw)
            y = ypart_ref[s, :, gsl] + eace_ref[s, :, gsl] * yoff_ref[:, s * ds + g * gw:s * ds + (g + 1) * gw]
            o_ref[s, :, gsl] = _gated_group_norm(y, z_ref[s, :, gsl], g_ref[:, gsl]).astype(o_ref.dtype)


def _ssd_slab_post(ypart, eace, yoff, proj3, g, *, n_groups, dl):
    ls, bs, ds = ypart.shape
    assert (2 * dl) % ds == 0
    c2 = lambda i: (0, 0)
    c3 = lambda i: (0, 0, 0)
    pipelined = [((ls, bs, ds), F32)] * 3 + [((bs, SLAB_ROWS * ds), F32), ((ls, bs, ds), BF16)]
    kern = functools.partial(_ssd_slab_post_kernel, n_groups=n_groups)
    return pl.pallas_call(
        kern,
        out_shape=jax.ShapeDtypeStruct((ls, bs, ds), BF16),
        grid=(1,),
        in_specs=[
            pl.BlockSpec((ls, bs, ds), c3), pl.BlockSpec((ls, bs, ds), c3),
            pl.BlockSpec((bs, SLAB_ROWS * ds), c2),
            pl.BlockSpec((ls, bs, ds), lambda i: (0, 0, (2 * dl) // ds)),
            pl.BlockSpec((1, ds), c2),
        ],
        out_specs=pl.BlockSpec((ls, bs, ds), c3),
        compiler_params=pltpu.CompilerParams(
            dimension_semantics=("arbitrary",),
            vmem_limit_bytes=_vmem_limit(pipelined, [])),
        name="ssd_slab_post",
    )(ypart, eace, yoff, proj3, g)


def _head_expansion(n_heads, head_dim):
    rows = lax.broadcasted_iota(jnp.int32, (HEAD_PAD, n_heads * head_dim), 0)
    cols = lax.broadcasted_iota(jnp.int32, (HEAD_PAD, n_heads * head_dim), 1)
    e = (cols // head_dim == rows).astype(BF16)
    return jnp.concatenate([e, e, e], axis=0)


def _pad_lanes(v, width):
    return jnp.pad(v, ((0, 0), (0, width - v.shape[1])))


def _tail_block(tail):
    return jnp.pad(tail, ((HALO - tail.shape[0], 0), (0, 0)))


def _mlp(x1, g_mlp, w_up, w_down, g_final, *, layer=None):
    m = x1.shape[0]
    tiles = (1024, 512, 256, 128)
    tm = m if layer is not None else _pick_tile(m, tiles)
    up = _mlp_up(x1, g_mlp, w_up, tm=tm, tn=_pick_tile(w_up.shape[-1], tiles), layer=layer)
    hid = up[0]
    down = _mlp_down_final(hid, w_down, x1, g_final, tm=tm, tk=_pick_tile(w_down.shape[-2], tiles), layer=layer)
    if layer is None:
        return down[0]
    return down[0], up[1], down[1]


def kernel(x_prompt, x_sample, state_lru_h, state_lru_conv, state_ssd, state_ssd_conv, meta_tokens, g_mix, w_in, conv_lru_w, conv_lru_b, lru_wa, lru_ba, lru_wx, lru_bx, lru_lambda, g_lru_out, conv_ssd_w, conv_ssd_b, dt_bias, a_log, d_skip, g_ssd_out, w_out, g_mlp, w_up, w_down, g_final):
    depth = w_in.shape[0]
    assert depth == 1, "single-layer step"
    l = 0
    bp, lp, d = x_prompt.shape
    bs, ls, _ = x_sample.shape
    n_meta = meta_tokens.shape[0]
    dl = state_lru_h.shape[-1]
    n_heads, p, n = state_ssd.shape[-3:]
    ds = n_heads * p
    dc = state_ssd_conv.shape[-1]
    gn = (dc - ds) // 2
    n_groups = gn // n
    nw = 2 * dl + ds + dc
    ntail = CONV_TAPS - 1
    q = SSD_CHUNK
    meta_pad = (-n_meta) % q
    assert n_heads <= HEAD_PAD and (bs * ls) % q == 0 and lp % q == 0 and q % bs == 0

    row = lambda v: v.reshape(1, -1).astype(F32)
    w_dt = _pad_lanes(w_in[l][:, nw:], HEAD_PAD).astype(BF16)
    lru_params = (conv_lru_w[l], row(conv_lru_b[l]), (0.5 * lru_wa[l]).astype(BF16), (0.5 * lru_wx[l]).astype(BF16),
                  row(0.5 * lru_ba[l]), row(0.5 * lru_bx[l]), row(lru_lambda[l]), row(g_lru_out[l]))
    ssd_params = (conv_ssd_w[l], row(conv_ssd_b[l]), _pad_lanes(row(dt_bias[l]), HEAD_PAD),
                  _pad_lanes(row(a_log[l]), HEAD_PAD), row(jnp.repeat(d_skip[l], p)), row(g_ssd_out[l]),
                  _head_expansion(n_heads, p))
    g_mix_r, g_mlp_r, g_final_r = row(g_mix[l]), row(g_mlp[l]), row(g_final)

    xs_tm = x_sample.transpose(1, 0, 2).reshape(ls * bs, d)
    x_side = jnp.concatenate([xs_tm, jnp.zeros((meta_pad, d), F32), meta_tokens.astype(F32)], axis=0)
    xp_rows = x_prompt.reshape(bp * lp, d)

    tiles = (1024, 512, 256, 128)
    tn_in = _pick_tile(nw, tiles)
    proj_side, dt_side, w_in_b = _in_proj(x_side, g_mix_r, w_in, w_dt, nw=nw, tm=x_side.shape[0], tn=tn_in, layer=l)

    proj_s3 = proj_side.reshape(-1, bs, nw)
    dt_s3 = dt_side.reshape(-1, bs, HEAD_PAD)
    lru_s, s_h, s_ltail = _lru_slab(proj_s3, lru_params, state_lru_h[l],
                                    state_lru_conv[l].reshape(bs, ntail * dl), ls=ls)
    ypart, eace, c_rows, b_rows, xw_rows, cdec, s_stail = _ssd_slab_pre(
        proj_s3, dt_s3, state_ssd_conv[l].reshape(bs, ntail * dc), ssd_params,
        ls=ls, n_heads=n_heads, n_groups=n_groups, dl=dl)
    s_new, yoff = _ssd_state(cdec, state_ssd[l].reshape(bs, ds, n),
                             c_rows.reshape(bs, SLAB_ROWS, gn), b_rows.reshape(bs, SLAB_ROWS, gn),
                             xw_rows.reshape(bs, SLAB_ROWS, ds), n_heads=n_heads, n_groups=n_groups)
    ssd_s = _ssd_slab_post(ypart, eace, yoff.reshape(bs, SLAB_ROWS * ds), proj_s3, ssd_params[5],
                           n_groups=n_groups, dl=dl)
    x1_s, w_out_lru, w_out_ssd = _out_proj(lru_s.reshape(ls * bs, dl), ssd_s.reshape(ls * bs, ds), w_out, l, xs_tm,
                                           tn=_pick_tile(d, (512, 256, 128)))
    y_s, w_up_b, w_down_b = _mlp(x1_s, g_mlp_r, w_up, w_down, g_final_r, layer=l)

    meta_row0 = ls * bs
    _, m_h, m_ltail = _lru_seq(proj_side, lru_params, jnp.zeros((1, dl), F32), jnp.zeros((HALO, dl), F32),
                               n_seq=1, seq_len=q, t=q, row0=meta_row0, pad=meta_pad, reset_first=True)
    _, m_s, m_stail = _ssd_seq(proj_side, dt_side, ssd_params, jnp.zeros((n, ds), F32), jnp.zeros((HALO, dc), F32),
                               n_seq=1, seq_len=q, row0=meta_row0, n_heads=n_heads, n_groups=n_groups, dl=dl,
                               pad=meta_pad)

    proj_p, dt_p = _in_proj(xp_rows, g_mix_r, w_in_b, w_dt, nw=nw, tm=_pick_tile(bp * lp, tiles), tn=tn_in)
    part_p, p_h, p_ltail = _lru_seq(proj_p, lru_params, m_h[0], _tail_block(m_ltail[0]),
                                    n_seq=bp, seq_len=lp, t=_pick_tile(lp, (256, 128)), row0=0, wo=w_out_lru)
    x1_p, p_s, p_stail = _ssd_seq(proj_p, dt_p, ssd_params, m_s[0], _tail_block(m_stail[0]),
                                  n_seq=bp, seq_len=lp, row0=0, n_heads=n_heads, n_groups=n_groups, dl=dl,
                                  fuse=(w_out_ssd, part_p, xp_rows))
    y_p = _mlp(x1_p, g_mlp_r, w_up_b, w_down_b, g_final_r)

    y_prompt = y_p.reshape(bp, lp, d)
    y_sample = y_s.reshape(ls, bs, d).transpose(1, 0, 2)
    p_lru_h = p_h.reshape(1, bp, dl)
    p_lru_conv = p_ltail.reshape(1, bp, ntail, dl)
    p_ssd = p_s.transpose(0, 2, 1).reshape(1, bp, n_heads, p, n)
    p_ssd_conv = p_stail.reshape(1, bp, ntail, dc)
    s_lru_h = s_h.reshape(1, bs, dl)
    s_lru_conv = s_ltail.reshape(1, bs, ntail, dl)
    s_ssd = s_new.reshape(1, bs, n_heads, p, n)
    s_ssd_conv = s_stail.reshape(1, bs, ntail, dc)
    return (y_prompt, y_sample, p_lru_h, p_lru_conv, p_ssd, p_ssd_conv, s_lru_h, s_lru_conv, s_ssd, s_ssd_conv)
```

```python
import functools

import jax
import jax.numpy as jnp
from jax import lax
from jax.experimental import pallas as pl
from jax.experimental.pallas import tpu as pltpu

F32 = jnp.float32
BF16 = jnp.bfloat16

EPS = 1e-6
LRU_C = 8.0
CONV_TAPS = 4

LANES = 128
SUBLANES = 8
VMEM_BYTES_V7X = 64 * 1024 * 1024
VMEM_TEMP_BYTES = 10 * 1024 * 1024
VMEM_CEILING_BYTES = VMEM_BYTES_V7X - 6 * 1024 * 1024

HALO = SUBLANES
SSD_CHUNK = 128
HEAD_PAD = LANES
SLAB_ROWS = SUBLANES
STATE_SEQS_PER_STEP = 4


def _nbytes(shape, dtype):
    n = 1
    for s in shape:
        n *= s
    return n * jnp.dtype(dtype).itemsize


def _vmem_limit(pipelined, resident):
    est = 2 * sum(_nbytes(s, d) for s, d in pipelined) + sum(_nbytes(s, d) for s, d in resident)
    return int(min(est + VMEM_TEMP_BYTES, VMEM_CEILING_BYTES))


def _pick_tile(m, prefs):
    for t in prefs:
        if m % t == 0:
            return t
    return m


def _silu(x):
    h = 0.5 * x
    return h * jnp.tanh(h) + h


def _split3(x):
    hi = x.astype(BF16)
    r1 = x - hi.astype(F32)
    mid = r1.astype(BF16)
    lo = (r1 - mid.astype(F32)).astype(BF16)
    return jnp.concatenate([hi, mid, lo], axis=1)


def _expand_heads(x, e3_ref):
    return jnp.dot(_split3(x), e3_ref[...], preferred_element_type=F32)


def _rmsnorm_rows(x, g):
    ms = jnp.mean(x * x, axis=-1, keepdims=True)
    return x * lax.rsqrt(ms + EPS) * g


def _conv_taps(ext, s, cw_ref, cb_ref, sl):
    v = cb_ref[:, sl] + ext[s] * cw_ref[0:1, sl]
    for k in range(1, CONV_TAPS):
        v = v + ext[s + k] * cw_ref[k:k + 1, sl]
    return v


def _weight_tile(w_ref, wb_ref):
    if wb_ref is None:
        return w_ref[...]
    w = w_ref[...].astype(BF16)
    wb_ref[...] = w
    return w


def _weight_specs(w, layer, blk, idx):
    if layer is None:
        return pl.BlockSpec(blk, idx), None, F32
    in_spec = pl.BlockSpec((None,) + blk, lambda i, j: (layer,) + idx(i, j))
    return in_spec, pl.BlockSpec(blk, idx), w.dtype


def _in_proj_kernel(x_ref, g_ref, w_ref, wdt_ref, o_ref, dt_ref, *rest, emit_w):
    wb_ref, xn_ref = rest if emit_w else (None, rest[0])

    @pl.when(pl.program_id(1) == 0)
    def _():
        xn = _rmsnorm_rows(x_ref[...], g_ref[...]).astype(BF16)
        xn_ref[...] = xn
        dt_ref[...] = jnp.dot(xn, wdt_ref[...].astype(BF16), preferred_element_type=F32)

    o_ref[...] = jnp.dot(xn_ref[...], _weight_tile(w_ref, wb_ref), preferred_element_type=F32)


def _in_proj(x, g, w, wdt, *, nw, tm, tn, layer=None):
    m, k = x.shape
    emit_w = layer is not None
    assert m % tm == 0 and nw % tn == 0 and (not emit_w or m == tm)
    w_spec, wb_spec, w_dtype = _weight_specs(w, layer, (k, tn), lambda i, j: (0, j))
    out_shape = [jax.ShapeDtypeStruct((m, nw), F32), jax.ShapeDtypeStruct((m, HEAD_PAD), F32)]
    out_specs = [pl.BlockSpec((tm, tn), lambda i, j: (i, j)), pl.BlockSpec((tm, HEAD_PAD), lambda i, j: (i, 0))]
    pipelined = [((tm, k), F32), ((k, tn), w_dtype if emit_w else BF16), ((k, HEAD_PAD), F32), ((tm, tn), F32),
                 ((tm, HEAD_PAD), F32)]
    if emit_w:
        out_shape.append(jax.ShapeDtypeStruct((k, nw), BF16))
        out_specs.append(wb_spec)
        pipelined.append(((k, tn), BF16))
    return pl.pallas_call(
        functools.partial(_in_proj_kernel, emit_w=emit_w),
        out_shape=tuple(out_shape),
        grid=(m // tm, nw // tn),
        in_specs=[
            pl.BlockSpec((tm, k), lambda i, j: (i, 0)),
            pl.BlockSpec((1, k), lambda i, j: (0, 0)),
            w_spec,
            pl.BlockSpec((k, HEAD_PAD), lambda i, j: (0, 0)),
        ],
        out_specs=tuple(out_specs),
        scratch_shapes=[pltpu.VMEM((tm, k), BF16)],
        compiler_params=pltpu.CompilerParams(
            dimension_semantics=("parallel", "arbitrary"),
            vmem_limit_bytes=_vmem_limit(pipelined, [((tm, k), BF16), ((k, tn), BF16)])),
        name="in_proj",
    )(x, g, w, wdt)


def _out_proj_kernel(a1_ref, a2_ref, w1_ref, w2_ref, res_ref, o_ref, wb1_ref, wb2_ref):
    acc = jnp.dot(a1_ref[...], _weight_tile(w1_ref, wb1_ref), preferred_element_type=F32)
    acc = acc + jnp.dot(a2_ref[...], _weight_tile(w2_ref, wb2_ref), preferred_element_type=F32)
    o_ref[...] = res_ref[...] + acc


def _out_proj(a1, a2, w, layer, res, *, tn):
    m, k1 = a1.shape
    k2 = a2.shape[1]
    n = w.shape[2]
    assert n % tn == 0 and k1 == k2 and w.shape[1] == k1 + k2
    pipelined = [((m, k1), BF16), ((m, k2), BF16), ((k1, tn), w.dtype), ((k2, tn), w.dtype), ((m, tn), F32),
                 ((m, tn), F32), ((k1, tn), BF16), ((k2, tn), BF16)]
    col = lambda j: (0, j)
    return pl.pallas_call(
        _out_proj_kernel,
        out_shape=(jax.ShapeDtypeStruct((m, n), F32), jax.ShapeDtypeStruct((k1, n), BF16),
                   jax.ShapeDtypeStruct((k2, n), BF16)),
        grid=(n // tn,),
        in_specs=[
            pl.BlockSpec((m, k1), lambda j: (0, 0)),
            pl.BlockSpec((m, k2), lambda j: (0, 0)),
            pl.BlockSpec((None, k1, tn), lambda j: (layer, 0, j)),
            pl.BlockSpec((None, k2, tn), lambda j: (layer, 1, j)),
            pl.BlockSpec((m, tn), col),
        ],
        out_specs=(pl.BlockSpec((m, tn), col), pl.BlockSpec((k1, tn), col), pl.BlockSpec((k2, tn), col)),
        compiler_params=pltpu.CompilerParams(
            dimension_semantics=("parallel",),
            vmem_limit_bytes=_vmem_limit(pipelined, [])),
        name="out_proj",
    )(a1, a2, w, w, res)


def _mlp_up_kernel(x_ref, g_ref, w_ref, o_ref, *rest, emit_w):
    wb_ref, xn_ref = rest if emit_w else (None, rest[0])

    @pl.when(pl.program_id(1) == 0)
    def _():
        xn_ref[...] = _rmsnorm_rows(x_ref[...], g_ref[...]).astype(BF16)

    acc = jnp.dot(xn_ref[...], _weight_tile(w_ref, wb_ref), preferred_element_type=F32)
    o_ref[...] = jnp.square(jnp.maximum(acc, 0.0)).astype(o_ref.dtype)


def _mlp_up(x, g, w, *, tm, tn, layer=None):
    m, k = x.shape
    n = w.shape[-1]
    emit_w = layer is not None
    assert m % tm == 0 and n % tn == 0 and (not emit_w or m == tm)
    w_spec, wb_spec, w_dtype = _weight_specs(w, layer, (k, tn), lambda i, j: (0, j))
    out_shape = [jax.ShapeDtypeStruct((m, n), BF16)]
    out_specs = [pl.BlockSpec((tm, tn), lambda i, j: (i, j))]
    pipelined = [((tm, k), F32), ((k, tn), w_dtype if emit_w else BF16), ((tm, tn), BF16)]
    if emit_w:
        out_shape.append(jax.ShapeDtypeStruct((k, n), BF16))
        out_specs.append(wb_spec)
        pipelined.append(((k, tn), BF16))
    return pl.pallas_call(
        functools.partial(_mlp_up_kernel, emit_w=emit_w),
        out_shape=tuple(out_shape),
        grid=(m // tm, n // tn),
        in_specs=[
            pl.BlockSpec((tm, k), lambda i, j: (i, 0)),
            pl.BlockSpec((1, k), lambda i, j: (0, 0)),
            w_spec,
        ],
        out_specs=tuple(out_specs),
        scratch_shapes=[pltpu.VMEM((tm, k), BF16)],
        compiler_params=pltpu.CompilerParams(
            dimension_semantics=("parallel", "arbitrary"),
            vmem_limit_bytes=_vmem_limit(pipelined, [((tm, k), BF16), ((k, tn), BF16)])),
        name="mlp_up",
    )(x, g, w)


def _mlp_down_kernel(h_ref, w_ref, res_ref, g_ref, o_ref, *rest, emit_w):
    wb_ref = rest[0] if emit_w else None
    kk = pl.program_id(1)

    @pl.when(kk == 0)
    def _():
        o_ref[...] = res_ref[...]

    o_ref[...] += jnp.dot(h_ref[...], _weight_tile(w_ref, wb_ref), preferred_element_type=F32)

    @pl.when(kk == pl.num_programs(1) - 1)
    def _():
        o_ref[...] = _rmsnorm_rows(o_ref[...], g_ref[...])


def _mlp_down_final(h, w, res, g, *, tm, tk, layer=None):
    m, k = h.shape
    n = w.shape[-1]
    emit_w = layer is not None
    assert m % tm == 0 and k % tk == 0 and (not emit_w or m == tm)
    w_spec, wb_spec, w_dtype = _weight_specs(w, layer, (tk, n), lambda i, j: (j, 0))
    out_shape = [jax.ShapeDtypeStruct((m, n), F32)]
    out_specs = [pl.BlockSpec((tm, n), lambda i, j: (i, 0))]
    pipelined = [((tm, tk), BF16), ((tk, n), w_dtype if emit_w else BF16), ((tm, n), F32), ((tm, n), F32)]
    if emit_w:
        out_shape.append(jax.ShapeDtypeStruct((k, n), BF16))
        out_specs.append(wb_spec)
        pipelined.append(((tk, n), BF16))
    return pl.pallas_call(
        functools.partial(_mlp_down_kernel, emit_w=emit_w),
        out_shape=tuple(out_shape),
        grid=(m // tm, k // tk),
        in_specs=[
            pl.BlockSpec((tm, tk), lambda i, j: (i, j)),
            w_spec,
            pl.BlockSpec((tm, n), lambda i, j: (i, 0)),
            pl.BlockSpec((1, n), lambda i, j: (0, 0)),
        ],
        out_specs=tuple(out_specs),
        compiler_params=pltpu.CompilerParams(
            dimension_semantics=("parallel", "arbitrary"),
            vmem_limit_bytes=_vmem_limit(pipelined, [((tk, n), BF16)] if emit_w else [])),
        name="mlp_down",
    )(h, w, res, g)


def _lru_gates(xh, wa_half, wx_half, ba_half, bx_half, hsp):
    xb = xh.astype(BF16)
    tr = jnp.tanh(jnp.dot(xb, wa_half, preferred_element_type=F32) + ba_half)
    ti = jnp.tanh(jnp.dot(xb, wx_half, preferred_element_type=F32) + bx_half)
    nla = tr * hsp + hsp
    a = jnp.exp(-nla)
    q = jnp.tanh(nla) * (1.0 + a * a)
    mult = jnp.where(q > 0.0, q * lax.rsqrt(q), 0.0)
    return a, mult, 0.5 * ti + 0.5


def _scan_rows(a, b, h_prev):
    t, hd = a.shape
    g = t // SUBLANES
    a3 = a.reshape(g, SUBLANES, hd)
    b3 = b.reshape(g, SUBLANES, hd)
    sub = lax.broadcasted_iota(jnp.int32, (g, SUBLANES, hd), 1)
    d = 1
    while d < SUBLANES:
        keep = sub >= d
        a_sh = jnp.where(keep, pltpu.roll(a3, d, axis=1), 1.0)
        b_sh = jnp.where(keep, pltpu.roll(b3, d, axis=1), 0.0)
        b3 = a3 * b_sh + b3
        a3 = a3 * a_sh
        d *= 2
    tiles = []
    h = h_prev
    for k in range(g):
        hk = a3[k] * h + b3[k]
        tiles.append(hk)
        h = hk[SUBLANES - 1:SUBLANES, :]
    return jnp.concatenate(tiles, axis=0), h


def _lru_seq_kernel(gate_ref, x_ref, cw_ref, cb_ref, wa_ref, wx_ref, ba_ref, bx_ref, lam_ref, g_ref,
                    h0_ref, tail0_ref, *rest, pad, reset_first, n_chunks, n_live, fuse_out):
    if fuse_out:
        wo_ref, o_ref, hfin_ref, tailfin_ref, xe_ref, hc_ref, y_ref, yn_ref = rest
    else:
        o_ref, hfin_ref, tailfin_ref, xe_ref, hc_ref, y_ref = rest
    step = pl.program_id(0)
    live = step < n_live
    c = lax.rem(jnp.minimum(step, n_live - 1), n_chunks)
    t, dl = x_ref.shape
    nh, hd = wa_ref.shape[0], wa_ref.shape[1]
    keep = (lambda new, old: jnp.where(live, new, old)) if fuse_out else (lambda new, old: new)

    @pl.when(jnp.logical_and(c == 0, live))
    def _():
        xe_ref[0:HALO, :] = tail0_ref[...]
        hc_ref[...] = h0_ref[...]

    if fuse_out:
        @pl.when(step == 0)
        def _():
            yn_ref[...] = jnp.zeros_like(yn_ref)

        o_ref[...] = jnp.dot(yn_ref[...], wo_ref[...], preferred_element_type=F32)

    xe_ref[HALO:HALO + t, :] = x_ref[...]
    grow = c * t + lax.broadcasted_iota(jnp.int32, (t, hd), 0)
    ssq = jnp.zeros((t, hd), F32)
    for h in range(nh):
        sl = slice(h * hd, (h + 1) * hd)
        ext = [xe_ref[HALO - 3 + k:HALO - 3 + k + t, sl] for k in range(CONV_TAPS)]
        xh = _conv_taps(ext, 0, cw_ref, cb_ref, sl)
        hsp = (0.5 * LRU_C) * jax.nn.softplus(-lam_ref[:, sl])
        a, mult, i = _lru_gates(xh, wa_ref[h], wx_ref[h], ba_ref[:, sl], bx_ref[:, sl], hsp)
        if reset_first:
            mult = jnp.where(grow == pad, 1.0, mult)
        b = mult * i * xh
        if pad:
            a = jnp.where(grow >= pad, a, 1.0)
            b = jnp.where(grow >= pad, b, 0.0)
        h_prev = hc_ref[:, sl]
        hs, h_last = _scan_rows(a, b, h_prev)
        hc_ref[:, sl] = keep(h_last, h_prev)
        y = hs * jax.nn.gelu(gate_ref[:, sl])
        y_ref[:, sl] = y
        ssq = ssq + y * y
    scale = lax.rsqrt(jnp.sum(ssq, axis=-1, keepdims=True) / dl + EPS)
    yn = (y_ref[...] * scale * g_ref[...]).astype(BF16)
    if fuse_out:
        yn_ref[...] = yn
    else:
        o_ref[...] = yn
    xe_ref[0:HALO, :] = keep(xe_ref[t:t + HALO, :], xe_ref[0:HALO, :])
    hfin_ref[0] = hc_ref[...]
    tailfin_ref[0] = xe_ref[HALO - 3:HALO, :]


def _lru_seq(proj, params, h0, tail0, *, n_seq, seq_len, t, row0, pad=0, reset_first=False, wo=None):
    cw, cb, wa, wx, ba, bx, lam, g = params
    dl = cw.shape[1]
    nh, hd = wa.shape[0], wa.shape[1]
    assert seq_len % t == 0 and row0 % t == 0 and t % SUBLANES == 0
    n_chunks = seq_len // t
    n_live = n_seq * n_chunks
    blk0 = row0 // t
    fuse_out = wo is not None
    chunk = (lambda s: jnp.minimum(s, n_live - 1)) if fuse_out else (lambda s: s)
    out_chunk = (lambda s: jnp.maximum(s - 1, 0)) if fuse_out else (lambda s: s)
    const2 = lambda s: (0, 0)
    const3 = lambda s: (0, 0, 0)
    per_seq = lambda s: (chunk(s) // n_chunks, 0, 0)
    dout, out_dtype = (wo.shape[1], F32) if fuse_out else (dl, BF16)
    pipelined = [((t, dl), F32), ((t, dl), F32), ((t, dout), out_dtype)]
    resident = [((t + HALO, dl), F32), ((t, dl), F32), ((4 * nh, hd, hd), BF16)]
    scratch = [pltpu.VMEM((t + HALO, dl), F32), pltpu.VMEM((1, dl), F32), pltpu.VMEM((t, dl), F32)]
    extra_specs, extra_args = [], []
    if fuse_out:
        extra_specs.append(pl.BlockSpec((dl, dout), const2, pipeline_mode=pl.Buffered(1)))
        extra_args.append(wo)
        resident += [((dl, dout), BF16), ((t, dl), BF16)]
        scratch.append(pltpu.VMEM((t, dl), BF16))
    kern = functools.partial(_lru_seq_kernel, pad=pad, reset_first=reset_first, n_chunks=n_chunks, n_live=n_live,
                             fuse_out=fuse_out)
    return pl.pallas_call(
        kern,
        out_shape=(jax.ShapeDtypeStruct((n_seq * seq_len, dout), out_dtype),
                   jax.ShapeDtypeStruct((n_seq, 1, dl), F32),
                   jax.ShapeDtypeStruct((n_seq, CONV_TAPS - 1, dl), F32)),
        grid=(n_live + 1 if fuse_out else n_live,),
        in_specs=[
            pl.BlockSpec((t, dl), lambda s: (blk0 + chunk(s), 0)),
            pl.BlockSpec((t, dl), lambda s: (blk0 + chunk(s), 1)),
            pl.BlockSpec((CONV_TAPS, dl), const2),
            pl.BlockSpec((1, dl), const2),
            pl.BlockSpec((nh, hd, hd), const3),
            pl.BlockSpec((nh, hd, hd), const3),
            pl.BlockSpec((1, dl), const2),
            pl.BlockSpec((1, dl), const2),
            pl.BlockSpec((1, dl), const2),
            pl.BlockSpec((1, dl), const2),
            pl.BlockSpec((1, dl), const2),
            pl.BlockSpec((HALO, dl), const2),
        ] + extra_specs,
        out_specs=(
            pl.BlockSpec((t, dout), lambda s: (out_chunk(s), 0)),
            pl.BlockSpec((1, 1, dl), per_seq),
            pl.BlockSpec((1, CONV_TAPS - 1, dl), per_seq),
        ),
        scratch_shapes=scratch,
        compiler_params=pltpu.CompilerParams(
            dimension_semantics=("arbitrary",),
            vmem_limit_bytes=_vmem_limit(pipelined, resident)),
        name="lru_seq",
    )(proj, proj, cw, cb, wa, wx, ba, bx, lam, g, h0, tail0, *extra_args)


def _lru_slab_kernel(gate_ref, x_ref, cw_ref, cb_ref, wa_ref, wx_ref, ba_ref, bx_ref, lam_ref, g_ref,
                     h0_ref, tail_ref, o_ref, hfin_ref, tailfin_ref, y_ref):
    ls, bs, dl = x_ref.shape
    nh, hd = wa_ref.shape[0], wa_ref.shape[1]
    ntail = CONV_TAPS - 1
    for h in range(nh):
        sl = slice(h * hd, (h + 1) * hd)
        ext = [tail_ref[:, k * dl + h * hd:k * dl + (h + 1) * hd] for k in range(ntail)]
        ext += [x_ref[s, :, sl] for s in range(ls)]
        hsp = (0.5 * LRU_C) * jax.nn.softplus(-lam_ref[:, sl])
        hcur = h0_ref[:, sl]
        for s in range(ls):
            xh = _conv_taps(ext, s, cw_ref, cb_ref, sl)
            a, mult, i = _lru_gates(xh, wa_ref[h], wx_ref[h], ba_ref[:, sl], bx_ref[:, sl], hsp)
            hcur = a * hcur + mult * i * xh
            y_ref[s, :, sl] = hcur * jax.nn.gelu(gate_ref[s, :, sl])
        hfin_ref[:, sl] = hcur
        for k in range(ntail):
            tailfin_ref[:, k * dl + h * hd:k * dl + (h + 1) * hd] = ext[ls + k]
    for s in range(ls):
        y = y_ref[s]
        scale = lax.rsqrt(jnp.mean(y * y, axis=-1, keepdims=True) + EPS)
        o_ref[s] = (y * scale * g_ref[...]).astype(o_ref.dtype)


def _lru_slab(proj3, params, h0, tail, *, ls):
    cw, cb, wa, wx, ba, bx, lam, g = params
    bs = proj3.shape[1]
    dl = cw.shape[1]
    nh, hd = wa.shape[0], wa.shape[1]
    ntail = CONV_TAPS - 1
    c2 = lambda i: (0, 0)
    c3 = lambda i: (0, 0, 0)
    pipelined = [((ls, bs, dl), F32)] * 2 + [((ls, bs, dl), BF16)] + [((bs, (2 * ntail + 2) * dl), F32)]
    return pl.pallas_call(
        _lru_slab_kernel,
        out_shape=(jax.ShapeDtypeStruct((ls, bs, dl), BF16),
                   jax.ShapeDtypeStruct((bs, dl), F32),
                   jax.ShapeDtypeStruct((bs, ntail * dl), F32)),
        grid=(1,),
        in_specs=[
            pl.BlockSpec((ls, bs, dl), lambda i: (0, 0, 0)),
            pl.BlockSpec((ls, bs, dl), lambda i: (0, 0, 1)),
            pl.BlockSpec((CONV_TAPS, dl), c2), pl.BlockSpec((1, dl), c2),
            pl.BlockSpec((nh, hd, hd), c3), pl.BlockSpec((nh, hd, hd), c3),
            pl.BlockSpec((1, dl), c2), pl.BlockSpec((1, dl), c2), pl.BlockSpec((1, dl), c2), pl.BlockSpec((1, dl), c2),
            pl.BlockSpec((bs, dl), c2), pl.BlockSpec((bs, ntail * dl), c2),
        ],
        out_specs=(pl.BlockSpec((ls, bs, dl), c3), pl.BlockSpec((bs, dl), c2), pl.BlockSpec((bs, ntail * dl), c2)),
        scratch_shapes=[pltpu.VMEM((ls, bs, dl), F32)],
        compiler_params=pltpu.CompilerParams(
            dimension_semantics=("arbitrary",),
            vmem_limit_bytes=_vmem_limit(pipelined, [((ls, bs, dl), F32)])),
        name="lru_slab",
    )(proj3, proj3, cw, cb, wa, wx, ba, bx, lam, g, h0, tail)


def _head_lane_mask(n_heads):
    return lax.broadcasted_iota(jnp.int32, (1, HEAD_PAD), 1) < n_heads


def _gated_group_norm(y, z, g):
    yg = y * _silu(z)
    scale = lax.rsqrt(jnp.mean(yg * yg, axis=-1, keepdims=True) + EPS)
    return yg * scale * g


def _ssd_seq_kernel(z_ref, xbc_ref, dt_ref, cw_ref, cb_ref, dtb_ref, alog_ref, dskip_ref, g_ref, e3_ref,
                    s0_ref, tail0_ref, *rest, pad, n_heads, n_groups, n_chunks, n_live, fuse_out):
    if fuse_out:
        wo_ref, part_ref, res_ref = rest[:3]
        o_ref, sfin_ref, tailfin_ref, xe_ref, xc_ref, ex_ref, y_ref, s_ref, yn_ref, ynp_ref = rest[3:]
    else:
        o_ref, sfin_ref, tailfin_ref, xe_ref, xc_ref, ex_ref, y_ref, s_ref = rest
        yn_ref = o_ref
    step = pl.program_id(0)
    live = step < n_live
    c = lax.rem(jnp.minimum(step, n_live - 1), n_chunks)
    keep = (lambda new, old: jnp.where(live, new, old)) if fuse_out else (lambda new, old: new)
    q, ds = z_ref.shape
    dc = xbc_ref.shape[1]
    gn = (dc - ds) // 2
    n = gn // n_groups
    p = ds // n_heads
    r = n_heads // n_groups
    gw = ds // n_groups
    hpb = LANES // p

    @pl.when(jnp.logical_and(c == 0, live))
    def _():
        xe_ref[0:HALO, :] = tail0_ref[...]
        s_ref[...] = s0_ref[...]

    if fuse_out:
        @pl.when(step == 0)
        def _():
            ynp_ref[...] = jnp.zeros_like(ynp_ref)

        mix = part_ref[...] + jnp.dot(ynp_ref[...], wo_ref[...], preferred_element_type=F32)
        o_ref[...] = res_ref[...] + mix

    xe_ref[HALO:HALO + q, :] = xbc_ref[...]
    valid = (c * q + lax.broadcasted_iota(jnp.int32, (q, 1), 0)) >= pad

    cblk = 512 if dc % 512 == 0 else LANES
    for j in range(dc // cblk):
        sl = slice(j * cblk, (j + 1) * cblk)
        ext = [xe_ref[HALO - 3 + k:HALO - 3 + k + q, sl] for k in range(CONV_TAPS)]
        v = _silu(_conv_taps(ext, 0, cw_ref, cb_ref, sl))
        if pad and (j + 1) * cblk <= ds:
            v = jnp.where(valid, v, 0.0)
        xc_ref[:, sl] = v

    dtv = jnp.where(_head_lane_mask(n_heads), jax.nn.softplus(dt_ref[...] + dtb_ref[...]), 0.0)
    if pad:
        dtv = jnp.where(valid, dtv, 0.0)
    da = dtv * (-jnp.exp(alog_ref[...]))
    ri = lax.broadcasted_iota(jnp.int32, (q, q), 0)
    ci = lax.broadcasted_iota(jnp.int32, (q, q), 1)
    causal = ci <= ri
    tri = jnp.where(causal, 1.0, 0.0).astype(BF16)
    ac3 = jnp.dot(tri, _split3(da), preferred_element_type=F32)
    acum = ac3[:, 0:HEAD_PAD] + ac3[:, HEAD_PAD:2 * HEAD_PAD] + ac3[:, 2 * HEAD_PAD:3 * HEAD_PAD]
    alast = acum[q - 1:q, :]
    eac = jnp.exp(acum)
    wend = jnp.exp(alast - acum) * dtv
    cdec = jnp.broadcast_to(jnp.exp(alast), (SUBLANES, HEAD_PAD))
    ex_ref[...] = _expand_heads(jnp.concatenate([eac, wend, cdec], axis=0), e3_ref)
    acum_t = acum.T
    dt_t = dtv.T
    lane = lax.broadcasted_iota(jnp.int32, (q, LANES), 1)

    for g in range(n_groups):
        gsl = slice(g * gw, (g + 1) * gw)
        bg = xc_ref[:, ds + g * n:ds + (g + 1) * n].astype(BF16)
        cg = xc_ref[:, ds + gn + g * n:ds + gn + (g + 1) * n].astype(BF16)
        cbm = lax.dot_general(cg, bg, (((1,), (1,)), ((), ())), preferred_element_type=F32)
        yoff = jnp.dot(cg, s_ref[:, gsl].astype(BF16), preferred_element_type=F32)
        for k in range(gw // LANES):
            lsl = slice(g * gw + k * LANES, g * gw + (k + 1) * LANES)
            xs = xc_ref[:, lsl]
            ms = []
            xparts = []
            for u in range(hpb):
                h = g * r + k * hpb + u
                seg = acum[:, h:h + 1] - acum_t[h:h + 1, :]
                lm = jnp.where(causal, jnp.exp(seg), 0.0) * dt_t[h:h + 1, :]
                ms.append((cbm * lm).astype(BF16))
                inhead = (lane >= u * p) & (lane < (u + 1) * p)
                xparts.append(jnp.where(inhead, xs, 0.0).astype(BF16))
            ydiag = jnp.dot(jnp.concatenate(ms, axis=1), jnp.concatenate(xparts, axis=0),
                            preferred_element_type=F32)
            y = ydiag + yoff[:, k * LANES:(k + 1) * LANES] * ex_ref[0:q, lsl]
            y_ref[:, lsl] = y + dskip_ref[:, lsl] * xs
        yn_ref[:, gsl] = _gated_group_norm(y_ref[:, gsl], z_ref[:, gsl], g_ref[:, gsl]).astype(BF16)
        xw = (xc_ref[:, gsl] * ex_ref[q:2 * q, gsl]).astype(BF16)
        upd = lax.dot_general(bg, xw, (((0,), (0,)), ((), ())), preferred_element_type=F32)
        s_old = s_ref[:, gsl]
        s_ref[:, gsl] = keep(ex_ref[2 * q:2 * q + 1, gsl] * s_old + upd, s_old)

    if fuse_out:
        ynp_ref[...] = yn_ref[...]
    xe_ref[0:HALO, :] = keep(xe_ref[q:q + HALO, :], xe_ref[0:HALO, :])
    sfin_ref[0] = s_ref[...]
    tailfin_ref[0] = xe_ref[HALO - 3:HALO, :]


def _ssd_seq(proj, dt, params, s0, tail0, *, n_seq, seq_len, row0, n_heads, n_groups, dl, pad=0, fuse=None):
    cw, cb, dtb, alog, dskip, g, e3 = params
    dc = cw.shape[1]
    ds = dskip.shape[1]
    n = s0.shape[0]
    q = SSD_CHUNK
    assert seq_len % q == 0 and row0 % q == 0
    assert (2 * dl) % ds == 0 and (2 * dl + ds) % dc == 0 and LANES % (ds // n_heads) == 0
    n_chunks = seq_len // q
    n_live = n_seq * n_chunks
    blk0 = row0 // q
    fuse_out = fuse is not None
    chunk = (lambda s: jnp.minimum(s, n_live - 1)) if fuse_out else (lambda s: s)
    rows = lambda s: blk0 + chunk(s)
    const2 = lambda s: (0, 0)
    out_rows = (lambda s: (jnp.maximum(s - 1, 0), 0)) if fuse_out else (lambda s: (s, 0))
    per_seq = lambda s: (chunk(s) // n_chunks, 0, 0)
    dout, out_dtype = (fuse[0].shape[1], F32) if fuse_out else (ds, BF16)
    pipelined = [((q, ds), F32), ((q, dc), F32), ((q, HEAD_PAD), F32), ((q, dout), out_dtype)]
    resident = [((q + HALO, dc), F32), ((q, dc), F32), ((2 * q + SUBLANES, ds), F32), ((q, ds), F32),
                ((3 * n, ds), F32), ((6 * HEAD_PAD, ds), BF16)]
    scratch = [pltpu.VMEM((q + HALO, dc), F32), pltpu.VMEM((q, dc), F32),
               pltpu.VMEM((2 * q + SUBLANES, ds), F32), pltpu.VMEM((q, ds), F32), pltpu.VMEM((n, ds), F32)]
    extra_specs, extra_args = [], []
    if fuse_out:
        extra_specs = [pl.BlockSpec((ds, dout), const2, pipeline_mode=pl.Buffered(1)),
                       pl.BlockSpec((q, dout), out_rows), pl.BlockSpec((q, dout), out_rows)]
        extra_args = list(fuse)
        pipelined += [((q, dout), F32)] * 2
        resident += [((ds, dout), BF16), ((2 * q, ds), BF16)]
        scratch += [pltpu.VMEM((q, ds), BF16), pltpu.VMEM((q, ds), BF16)]
    kern = functools.partial(_ssd_seq_kernel, pad=pad, n_heads=n_heads, n_groups=n_groups, n_chunks=n_chunks,
                             n_live=n_live, fuse_out=fuse_out)
    return pl.pallas_call(
        kern,
        out_shape=(jax.ShapeDtypeStruct((n_seq * seq_len, dout), out_dtype),
                   jax.ShapeDtypeStruct((n_seq, n, ds), F32),
                   jax.ShapeDtypeStruct((n_seq, CONV_TAPS - 1, dc), F32)),
        grid=(n_live + 1 if fuse_out else n_live,),
        in_specs=[
            pl.BlockSpec((q, ds), lambda s: (rows(s), (2 * dl) // ds)),
            pl.BlockSpec((q, dc), lambda s: (rows(s), (2 * dl + ds) // dc)),
            pl.BlockSpec((q, HEAD_PAD), lambda s: (rows(s), 0)),
            pl.BlockSpec((CONV_TAPS, dc), const2),
            pl.BlockSpec((1, dc), const2),
            pl.BlockSpec((1, HEAD_PAD), const2),
            pl.BlockSpec((1, HEAD_PAD), const2),
            pl.BlockSpec((1, ds), const2),
            pl.BlockSpec((1, ds), const2),
            pl.BlockSpec((3 * HEAD_PAD, ds), const2),
            pl.BlockSpec((n, ds), const2),
            pl.BlockSpec((HALO, dc), const2),
        ] + extra_specs,
        out_specs=(
            pl.BlockSpec((q, dout), out_rows),
            pl.BlockSpec((1, n, ds), per_seq),
            pl.BlockSpec((1, CONV_TAPS - 1, dc), per_seq),
        ),
        scratch_shapes=scratch,
        compiler_params=pltpu.CompilerParams(
            dimension_semantics=("arbitrary",),
            vmem_limit_bytes=_vmem_limit(pipelined, resident)),
        name="ssd_seq",
    )(proj, proj, dt, cw, cb, dtb, alog, dskip, g, e3, s0, tail0, *extra_args)


def _ssd_slab_pre_kernel(xbc_ref, dt_ref, tail_ref, cw_ref, cb_ref, dtb_ref, alog_ref, dskip_ref, e3_ref,
                         ypart_ref, eace_ref, c_ref, b_ref, xw_ref, cdec_ref, tailfin_ref, xc_ref,
                         *, n_heads, n_groups):
    ls, bs, dc = xbc_ref.shape
    ds = dskip_ref.shape[1]
    gn = (dc - ds) // 2
    n = gn // n_groups
    r = n_heads // n_groups
    ntail = CONV_TAPS - 1

    cblk = 512 if dc % 512 == 0 else LANES
    for j in range(dc // cblk):
        sl = slice(j * cblk, (j + 1) * cblk)
        ext = [tail_ref[:, k * dc + j * cblk:k * dc + (j + 1) * cblk] for k in range(ntail)]
        ext += [xbc_ref[s, :, sl] for s in range(ls)]
        for s in range(ls):
            xc_ref[s, :, sl] = _silu(_conv_taps(ext, s, cw_ref, cb_ref, sl))
        for k in range(ntail):
            tailfin_ref[:, k * dc + j * cblk:k * dc + (j + 1) * cblk] = ext[ls + k]

    hmask = _head_lane_mask(n_heads)
    a_neg = -jnp.exp(alog_ref[...])
    dtv, acum = [], []
    run = jnp.zeros((bs, HEAD_PAD), F32)
    for s in range(ls):
        d = jnp.where(hmask, jax.nn.softplus(dt_ref[s] + dtb_ref[...]), 0.0)
        run = run + d * a_neg
        dtv.append(d)
        acum.append(run)
    alast = acum[ls - 1]
    cdec_ref[...] = jnp.exp(alast)
    head_group = lax.broadcasted_iota(jnp.int32, (1, HEAD_PAD), 1) // r

    for s in range(ls):
        eace_ref[s] = _expand_heads(jnp.exp(acum[s]), e3_ref)
        wend_e = _expand_heads(jnp.exp(alast - acum[s]) * dtv[s], e3_ref)
        xw_ref[:, s * ds:(s + 1) * ds] = xc_ref[s, :, 0:ds] * wend_e
        b_ref[:, s * gn:(s + 1) * gn] = xc_ref[s, :, ds:ds + gn]
        c_ref[:, s * gn:(s + 1) * gn] = xc_ref[s, :, ds + gn:ds + 2 * gn]
        ypart = dskip_ref[...] * xc_ref[s, :, 0:ds]
        for j in range(s + 1):
            cbh = jnp.zeros((bs, HEAD_PAD), F32)
            for g in range(n_groups):
                cs = xc_ref[s, :, ds + gn + g * n:ds + gn + (g + 1) * n]
                bj = xc_ref[j, :, ds + g * n:ds + (g + 1) * n]
                cbg = jnp.sum(cs * bj, axis=-1, keepdims=True)
                cbh = cbh + jnp.where(head_group == g, cbg, 0.0)
            coef = cbh * (jnp.exp(acum[s] - acum[j]) * dtv[j])
            ypart = ypart + _expand_heads(coef, e3_ref) * xc_ref[j, :, 0:ds]
        ypart_ref[s] = ypart
    for s in range(ls, SLAB_ROWS):
        xw_ref[:, s * ds:(s + 1) * ds] = jnp.zeros((bs, ds), F32)
        b_ref[:, s * gn:(s + 1) * gn] = jnp.zeros((bs, gn), F32)
        c_ref[:, s * gn:(s + 1) * gn] = jnp.zeros((bs, gn), F32)


def _ssd_slab_pre(proj3, dt3, tail, params, *, ls, n_heads, n_groups, dl):
    cw, cb, dtb, alog, dskip, _, e3 = params
    bs = proj3.shape[1]
    dc = cw.shape[1]
    ds = dskip.shape[1]
    gn = (dc - ds) // 2
    ntail = CONV_TAPS - 1
    assert ls <= SLAB_ROWS and (2 * dl + ds) % dc == 0
    c2 = lambda i: (0, 0)
    c3 = lambda i: (0, 0, 0)
    pipelined = [((ls, bs, dc), F32), ((ls, bs, HEAD_PAD), F32), ((bs, 2 * ntail * dc), F32),
                 ((2 * ls, bs, ds), F32), ((bs, SLAB_ROWS * (2 * gn + ds)), F32), ((3 * HEAD_PAD, ds), BF16)]
    kern = functools.partial(_ssd_slab_pre_kernel, n_heads=n_heads, n_groups=n_groups)
    return pl.pallas_call(
        kern,
        out_shape=(jax.ShapeDtypeStruct((ls, bs, ds), F32),
                   jax.ShapeDtypeStruct((ls, bs, ds), F32),
                   jax.ShapeDtypeStruct((bs, SLAB_ROWS * gn), F32),
                   jax.ShapeDtypeStruct((bs, SLAB_ROWS * gn), F32),
                   jax.ShapeDtypeStruct((bs, SLAB_ROWS * ds), F32),
                   jax.ShapeDtypeStruct((bs, HEAD_PAD), F32),
                   jax.ShapeDtypeStruct((bs, ntail * dc), F32)),
        grid=(1,),
        in_specs=[
            pl.BlockSpec((ls, bs, dc), lambda i: (0, 0, (2 * dl + ds) // dc)),
            pl.BlockSpec((ls, bs, HEAD_PAD), c3),
            pl.BlockSpec((bs, ntail * dc), c2),
            pl.BlockSpec((CONV_TAPS, dc), c2), pl.BlockSpec((1, dc), c2),
            pl.BlockSpec((1, HEAD_PAD), c2), pl.BlockSpec((1, HEAD_PAD), c2),
            pl.BlockSpec((1, ds), c2), pl.BlockSpec((3 * HEAD_PAD, ds), c2),
        ],
        out_specs=(pl.BlockSpec((ls, bs, ds), c3), pl.BlockSpec((ls, bs, ds), c3),
                   pl.BlockSpec((bs, SLAB_ROWS * gn), c2), pl.BlockSpec((bs, SLAB_ROWS * gn), c2),
                   pl.BlockSpec((bs, SLAB_ROWS * ds), c2), pl.BlockSpec((bs, HEAD_PAD), c2),
                   pl.BlockSpec((bs, ntail * dc), c2)),
        scratch_shapes=[pltpu.VMEM((ls, bs, dc), F32)],
        compiler_params=pltpu.CompilerParams(
            dimension_semantics=("arbitrary",),
            vmem_limit_bytes=_vmem_limit(pipelined, [((ls, bs, dc), F32)])),
        name="ssd_slab_pre",
    )(proj3, dt3, tail, cw, cb, dtb, alog, dskip, e3)


def _ssd_state_kernel(cdec_ref, s_ref, c_ref, b_ref, xw_ref, snew_ref, yoff_ref, *, n_heads, n_groups):
    i = pl.program_id(0)
    sb, hp, n = s_ref.shape
    p = hp // n_heads
    r = n_heads // n_groups
    gw = hp // n_groups
    for q in range(sb):
        for g in range(n_groups):
            gsl = slice(g * gw, (g + 1) * gw)
            sg = s_ref[q, gsl, :]
            cg = c_ref[q, :, g * n:(g + 1) * n].astype(BF16)
            yoff_ref[q, :, gsl] = lax.dot_general(cg, sg.astype(BF16), (((1,), (1,)), ((), ())),
                                                  preferred_element_type=F32)
            upd = lax.dot_general(xw_ref[q, :, gsl].astype(BF16), b_ref[q, :, g * n:(g + 1) * n].astype(BF16),
                                  (((0,), (0,)), ((), ())), preferred_element_type=F32)
            for u in range(r):
                h = g * r + u
                rows = slice(h * p, (h + 1) * p)
                snew_ref[q, rows, :] = cdec_ref[i * sb + q, h] * s_ref[q, rows, :] + upd[u * p:(u + 1) * p, :]


def _ssd_state(cdec, state, c_rows, b_rows, xw_rows, *, n_heads, n_groups):
    bs, hp, n = state.shape
    gn = c_rows.shape[2]
    sb = STATE_SEQS_PER_STEP if bs % STATE_SEQS_PER_STEP == 0 else 1
    per_seq = lambda i: (i, 0, 0)
    pipelined = [((sb, hp, n), F32)] * 2 + [((sb, SLAB_ROWS, gn), F32)] * 2 + [((sb, SLAB_ROWS, hp), F32)] * 2
    kern = functools.partial(_ssd_state_kernel, n_heads=n_heads, n_groups=n_groups)
    return pl.pallas_call(
        kern,
        out_shape=(jax.ShapeDtypeStruct((bs, hp, n), F32), jax.ShapeDtypeStruct((bs, SLAB_ROWS, hp), F32)),
        grid=(bs // sb,),
        in_specs=[
            pl.BlockSpec(memory_space=pltpu.SMEM),
            pl.BlockSpec((sb, hp, n), per_seq),
            pl.BlockSpec((sb, SLAB_ROWS, gn), per_seq),
            pl.BlockSpec((sb, SLAB_ROWS, gn), per_seq),
            pl.BlockSpec((sb, SLAB_ROWS, hp), per_seq),
        ],
        out_specs=(pl.BlockSpec((sb, hp, n), per_seq), pl.BlockSpec((sb, SLAB_ROWS, hp), per_seq)),
        compiler_params=pltpu.CompilerParams(
            dimension_semantics=("parallel",),
            vmem_limit_bytes=_vmem_limit(pipelined, [])),
        name="ssd_state",
    )(cdec, state, c_rows, b_rows, xw_rows)


def _ssd_slab_post_kernel(ypart_ref, eace_ref, yoff_ref, z_ref, g_ref, o_ref, *, n_groups):
    ls, bs, ds = ypart_ref.shape
    gw = ds // n_groups
    for s in range(ls):
        for g in range(n_groups):
            gsl = slice(g * gw, (g + 1) * gw)
            y = ypart_ref[s, :, gsl] + eace_ref[s, :, gsl] * yoff_ref[:, s * ds + g * gw:s * ds + (g + 1) * gw]
            o_ref[s, :, gsl] = _gated_group_norm(y, z_ref[s, :, gsl], g_ref[:, gsl]).astype(o_ref.dtype)


def _ssd_slab_post(ypart, eace, yoff, proj3, g, *, n_groups, dl):
    ls, bs, ds = ypart.shape
    assert (2 * dl) % ds == 0
    c2 = lambda i: (0, 0)
    c3 = lambda i: (0, 0, 0)
    pipelined = [((ls, bs, ds), F32)] * 3 + [((bs, SLAB_ROWS * ds), F32), ((ls, bs, ds), BF16)]
    kern = functools.partial(_ssd_slab_post_kernel, n_groups=n_groups)
    return pl.pallas_call(
        kern,
        out_shape=jax.ShapeDtypeStruct((ls, bs, ds), BF16),
        grid=(1,),
        in_specs=[
            pl.BlockSpec((ls, bs, ds), c3), pl.BlockSpec((ls, bs, ds), c3),
            pl.BlockSpec((bs, SLAB_ROWS * ds), c2),
            pl.BlockSpec((ls, bs, ds), lambda i: (0, 0, (2 * dl) // ds)),
            pl.BlockSpec((1, ds), c2),
        ],
        out_specs=pl.BlockSpec((ls, bs, ds), c3),
        compiler_params=pltpu.CompilerParams(
            dimension_semantics=("arbitrary",),
            vmem_limit_bytes=_vmem_limit(pipelined, [])),
        name="ssd_slab_post",
    )(ypart, eace, yoff, proj3, g)


def _head_expansion(n_heads, head_dim):
    rows = lax.broadcasted_iota(jnp.int32, (HEAD_PAD, n_heads * head_dim), 0)
    cols = lax.broadcasted_iota(jnp.int32, (HEAD_PAD, n_heads * head_dim), 1)
    e = (cols // head_dim == rows).astype(BF16)
    return jnp.concatenate([e, e, e], axis=0)


def _pad_lanes(v, width):
    return jnp.pad(v, ((0, 0), (0, width - v.shape[1])))


def _tail_block(tail):
    return jnp.pad(tail, ((HALO - tail.shape[0], 0), (0, 0)))


def _mlp(x1, g_mlp, w_up, w_down, g_final, *, layer=None):
    m = x1.shape[0]
    tiles = (1024, 512, 256, 128)
    tm = m if layer is not None else _pick_tile(m, tiles)
    up = _mlp_up(x1, g_mlp, w_up, tm=tm, tn=_pick_tile(w_up.shape[-1], tiles), layer=layer)
    hid = up[0]
    down = _mlp_down_final(hid, w_down, x1, g_final, tm=tm, tk=_pick_tile(w_down.shape[-2], tiles), layer=layer)
    if layer is None:
        return down[0]
    return down[0], up[1], down[1]


def kernel(x_prompt, x_sample, state_lru_h, state_lru_conv, state_ssd, state_ssd_conv, meta_tokens, g_mix, w_in, conv_lru_w, conv_lru_b, lru_wa, lru_ba, lru_wx, lru_bx, lru_lambda, g_lru_out, conv_ssd_w, conv_ssd_b, dt_bias, a_log, d_skip, g_ssd_out, w_out, g_mlp, w_up, w_down, g_final):
    depth = w_in.shape[0]
    assert depth == 1, "single-layer step"
    l = 0
    bp, lp, d = x_prompt.shape
    bs, ls, _ = x_sample.shape
    n_meta = meta_tokens.shape[0]
    dl = state_lru_h.shape[-1]
    n_heads, p, n = state_ssd.shape[-3:]
    ds = n_heads * p
    dc = state_ssd_conv.shape[-1]
    gn = (dc - ds) // 2
    n_groups = gn // n
    nw = 2 * dl + ds + dc
    ntail = CONV_TAPS - 1
    q = SSD_CHUNK
    meta_pad = (-n_meta) % q
    assert n_heads <= HEAD_PAD and (bs * ls) % q == 0 and lp % q == 0 and q % bs == 0

    row = lambda v: v.reshape(1, -1).astype(F32)
    w_in_b = w_in[l].astype(BF16)
    w_dt = _pad_lanes(w_in_b[:, nw:], HEAD_PAD)
    lru_params = (conv_lru_w[l], row(conv_lru_b[l]), (0.5 * lru_wa[l]).astype(BF16), (0.5 * lru_wx[l]).astype(BF16),
                  row(0.5 * lru_ba[l]), row(0.5 * lru_bx[l]), row(lru_lambda[l]), row(g_lru_out[l]))
    ssd_params = (conv_ssd_w[l], row(conv_ssd_b[l]), _pad_lanes(row(dt_bias[l]), HEAD_PAD),
                  _pad_lanes(row(a_log[l]), HEAD_PAD), row(jnp.repeat(d_skip[l], p)), row(g_ssd_out[l]),
                  _head_expansion(n_heads, p))
    g_mix_r, g_mlp_r, g_final_r = row(g_mix[l]), row(g_mlp[l]), row(g_final)

    xs_tm = x_sample.transpose(1, 0, 2).reshape(ls * bs, d)
    x_side = jnp.concatenate([xs_tm, jnp.zeros((meta_pad, d), F32), meta_tokens.astype(F32)], axis=0)
    xp_rows = x_prompt.reshape(bp * lp, d)

    tiles = (1024, 512, 256, 128)
    tn_in = _pick_tile(nw, tiles)
    proj_side, dt_side = _in_proj(x_side, g_mix_r, w_in_b, w_dt, nw=nw, tm=x_side.shape[0], tn=tn_in)

    proj_s3 = proj_side.reshape(-1, bs, nw)
    dt_s3 = dt_side.reshape(-1, bs, HEAD_PAD)
    lru_s, s_h, s_ltail = _lru_slab(proj_s3, lru_params, state_lru_h[l],
                                    state_lru_conv[l].reshape(bs, ntail * dl), ls=ls)
    ypart, eace, c_rows, b_rows, xw_rows, cdec, s_stail = _ssd_slab_pre(
        proj_s3, dt_s3, state_ssd_conv[l].reshape(bs, ntail * dc), ssd_params,
        ls=ls, n_heads=n_heads, n_groups=n_groups, dl=dl)
    s_new, yoff = _ssd_state(cdec, state_ssd[l].reshape(bs, ds, n),
                             c_rows.reshape(bs, SLAB_ROWS, gn), b_rows.reshape(bs, SLAB_ROWS, gn),
                             xw_rows.reshape(bs, SLAB_ROWS, ds), n_heads=n_heads, n_groups=n_groups)
    ssd_s = _ssd_slab_post(ypart, eace, yoff.reshape(bs, SLAB_ROWS * ds), proj_s3, ssd_params[5],
                           n_groups=n_groups, dl=dl)
    x1_s, w_out_lru, w_out_ssd = _out_proj(lru_s.reshape(ls * bs, dl), ssd_s.reshape(ls * bs, ds), w_out, l, xs_tm,
                                           tn=_pick_tile(d, (512, 256, 128)))
    y_s, w_up_b, w_down_b = _mlp(x1_s, g_mlp_r, w_up, w_down, g_final_r, layer=l)

    meta_row0 = ls * bs
    _, m_h, m_ltail = _lru_seq(proj_side, lru_params, jnp.zeros((1, dl), F32), jnp.zeros((HALO, dl), F32),
                               n_seq=1, seq_len=q, t=q, row0=meta_row0, pad=meta_pad, reset_first=True)
    _, m_s, m_stail = _ssd_seq(proj_side, dt_side, ssd_params, jnp.zeros((n, ds), F32), jnp.zeros((HALO, dc), F32),
                               n_seq=1, seq_len=q, row0=meta_row0, n_heads=n_heads, n_groups=n_groups, dl=dl,
                               pad=meta_pad)

    proj_p, dt_p = _in_proj(xp_rows, g_mix_r, w_in_b, w_dt, nw=nw, tm=_pick_tile(bp * lp, tiles), tn=tn_in)
    part_p, p_h, p_ltail = _lru_seq(proj_p, lru_params, m_h[0], _tail_block(m_ltail[0]),
                                    n_seq=bp, seq_len=lp, t=_pick_tile(lp, (256, 128)), row0=0, wo=w_out_lru)
    x1_p, p_s, p_stail = _ssd_seq(proj_p, dt_p, ssd_params, m_s[0], _tail_block(m_stail[0]),
                                  n_seq=bp, seq_len=lp, row0=0, n_heads=n_heads, n_groups=n_groups, dl=dl,
                                  fuse=(w_out_ssd, part_p, xp_rows))
    y_p = _mlp(x1_p, g_mlp_r, w_up_b, w_down_b, g_final_r)

    y_prompt = y_p.reshape(bp, lp, d)
    y_sample = y_s.reshape(ls, bs, d).transpose(1, 0, 2)
    p_lru_h = p_h.reshape(1, bp, dl)
    p_lru_conv = p_ltail.reshape(1, bp, ntail, dl)
    p_ssd = p_s.transpose(0, 2, 1).reshape(1, bp, n_heads, p, n)
    p_ssd_conv = p_stail.reshape(1, bp, ntail, dc)
    s_lru_h = s_h.reshape(1, bs, dl)
    s_lru_conv = s_ltail.reshape(1, bs, ntail, dl)
    s_ssd = s_new.reshape(1, bs, n_heads, p, n)
    s_ssd_conv = s_stail.reshape(1, bs, ntail, dc)
    return (y_prompt, y_sample, p_lru_h, p_lru_conv, p_ssd, p_ssd_conv, s_lru_h, s_lru_conv, s_ssd, s_ssd_conv)
```

```python
import functools

import jax
import jax.numpy as jnp
from jax import lax
from jax.experimental import pallas as pl
from jax.experimental.pallas import tpu as pltpu

F32 = jnp.float32
BF16 = jnp.bfloat16

EPS = 1e-6
LRU_C = 8.0
CONV_TAPS = 4

LANES = 128
SUBLANES = 8
MXU_COLS = 256
VMEM_BYTES_V7X = 64 * 1024 * 1024
VMEM_TEMP_BYTES = 10 * 1024 * 1024
VMEM_CEILING_BYTES = VMEM_BYTES_V7X - 6 * 1024 * 1024

HALO = SUBLANES
SSD_CHUNK = 128
HEAD_PAD = LANES
SLAB_ROWS = SUBLANES
STATE_SEQS_PER_STEP = 4
MIXER_SCHED_FLAGS = None


def _nbytes(shape, dtype):
    n = 1
    for s in shape:
        n *= s
    return n * jnp.dtype(dtype).itemsize


def _vmem_limit(pipelined, resident):
    est = 2 * sum(_nbytes(s, d) for s, d in pipelined) + sum(_nbytes(s, d) for s, d in resident)
    return int(min(est + VMEM_TEMP_BYTES, VMEM_CEILING_BYTES))


def _pick_tile(m, prefs):
    for t in prefs:
        if m % t == 0:
            return t
    return m


def _silu(x):
    h = 0.5 * x
    return h * jnp.tanh(h) + h


def _split3(x):
    hi = x.astype(BF16)
    r1 = x - hi.astype(F32)
    mid = r1.astype(BF16)
    lo = (r1 - mid.astype(F32)).astype(BF16)
    return jnp.concatenate([hi, mid, lo], axis=1)


def _expand_heads(x, e3_ref):
    return jnp.dot(_split3(x), e3_ref[...], preferred_element_type=F32)


def _rmsnorm_rows(x, g):
    ms = jnp.mean(x * x, axis=-1, keepdims=True)
    return x * lax.rsqrt(ms + EPS) * g


def _conv_taps(ext, s, cw_ref, cb_ref, sl):
    v = cb_ref[:, sl] + ext[s] * cw_ref[0:1, sl]
    for k in range(1, CONV_TAPS):
        v = v + ext[s + k] * cw_ref[k:k + 1, sl]
    return v


def _weight_tile(w_ref, wb_ref):
    if wb_ref is None:
        return w_ref[...]
    w = w_ref[...].astype(BF16)
    wb_ref[...] = w
    return w


def _weight_specs(w, layer, blk, idx):
    if layer is None:
        return pl.BlockSpec(blk, idx), None, F32
    in_spec = pl.BlockSpec((None,) + blk, lambda i, j: (layer,) + idx(i, j))
    return in_spec, pl.BlockSpec(blk, idx), w.dtype


def _in_proj_kernel(x_ref, g_ref, w_ref, wdt_ref, o_ref, dt_ref, *rest, emit_w):
    wb_ref, xn_ref = rest if emit_w else (None, rest[0])

    @pl.when(pl.program_id(1) == 0)
    def _():
        xn = _rmsnorm_rows(x_ref[...], g_ref[...]).astype(BF16)
        xn_ref[...] = xn
        dt_ref[...] = jnp.dot(xn, wdt_ref[...].astype(BF16), preferred_element_type=F32)

    o_ref[...] = jnp.dot(xn_ref[...], _weight_tile(w_ref, wb_ref), preferred_element_type=F32)


def _in_proj(x, g, w, wdt, *, nw, tm, tn, layer=None):
    m, k = x.shape
    emit_w = layer is not None
    assert m % tm == 0 and nw % tn == 0 and (not emit_w or m == tm)
    w_spec, wb_spec, w_dtype = _weight_specs(w, layer, (k, tn), lambda i, j: (0, j))
    out_shape = [jax.ShapeDtypeStruct((m, nw), F32), jax.ShapeDtypeStruct((m, HEAD_PAD), F32)]
    out_specs = [pl.BlockSpec((tm, tn), lambda i, j: (i, j)), pl.BlockSpec((tm, HEAD_PAD), lambda i, j: (i, 0))]
    pipelined = [((tm, k), F32), ((k, tn), w_dtype if emit_w else BF16), ((k, HEAD_PAD), F32), ((tm, tn), F32),
                 ((tm, HEAD_PAD), F32)]
    if emit_w:
        out_shape.append(jax.ShapeDtypeStruct((k, nw), BF16))
        out_specs.append(wb_spec)
        pipelined.append(((k, tn), BF16))
    return pl.pallas_call(
        functools.partial(_in_proj_kernel, emit_w=emit_w),
        out_shape=tuple(out_shape),
        grid=(m // tm, nw // tn),
        in_specs=[
            pl.BlockSpec((tm, k), lambda i, j: (i, 0)),
            pl.BlockSpec((1, k), lambda i, j: (0, 0)),
            w_spec,
            pl.BlockSpec((k, HEAD_PAD), lambda i, j: (0, 0)),
        ],
        out_specs=tuple(out_specs),
        scratch_shapes=[pltpu.VMEM((tm, k), BF16)],
        compiler_params=pltpu.CompilerParams(
            dimension_semantics=("parallel", "arbitrary"),
            vmem_limit_bytes=_vmem_limit(pipelined, [((tm, k), BF16), ((k, tn), BF16)])),
        name="in_proj",
    )(x, g, w, wdt)


def _out_proj_kernel(a1_ref, a2_ref, w1_ref, w2_ref, res_ref, o_ref, wb1_ref, wb2_ref):
    acc = jnp.dot(a1_ref[...], _weight_tile(w1_ref, wb1_ref), preferred_element_type=F32)
    acc = acc + jnp.dot(a2_ref[...], _weight_tile(w2_ref, wb2_ref), preferred_element_type=F32)
    o_ref[...] = res_ref[...] + acc


def _out_proj(a1, a2, w, layer, res, *, tn):
    m, k1 = a1.shape
    k2 = a2.shape[1]
    n = w.shape[2]
    assert n % tn == 0 and k1 == k2 and w.shape[1] == k1 + k2
    pipelined = [((m, k1), BF16), ((m, k2), BF16), ((k1, tn), w.dtype), ((k2, tn), w.dtype), ((m, tn), F32),
                 ((m, tn), F32), ((k1, tn), BF16), ((k2, tn), BF16)]
    col = lambda j: (0, j)
    return pl.pallas_call(
        _out_proj_kernel,
        out_shape=(jax.ShapeDtypeStruct((m, n), F32), jax.ShapeDtypeStruct((k1, n), BF16),
                   jax.ShapeDtypeStruct((k2, n), BF16)),
        grid=(n // tn,),
        in_specs=[
            pl.BlockSpec((m, k1), lambda j: (0, 0)),
            pl.BlockSpec((m, k2), lambda j: (0, 0)),
            pl.BlockSpec((None, k1, tn), lambda j: (layer, 0, j)),
            pl.BlockSpec((None, k2, tn), lambda j: (layer, 1, j)),
            pl.BlockSpec((m, tn), col),
        ],
        out_specs=(pl.BlockSpec((m, tn), col), pl.BlockSpec((k1, tn), col), pl.BlockSpec((k2, tn), col)),
        compiler_params=pltpu.CompilerParams(
            dimension_semantics=("parallel",),
            vmem_limit_bytes=_vmem_limit(pipelined, [])),
        name="out_proj",
    )(a1, a2, w, w, res)


def _mlp_up_kernel(x_ref, g_ref, w_ref, o_ref, *rest, emit_w):
    wb_ref, xn_ref = rest if emit_w else (None, rest[0])

    @pl.when(pl.program_id(1) == 0)
    def _():
        xn_ref[...] = _rmsnorm_rows(x_ref[...], g_ref[...]).astype(BF16)

    acc = jnp.dot(xn_ref[...], _weight_tile(w_ref, wb_ref), preferred_element_type=F32)
    o_ref[...] = jnp.square(jnp.maximum(acc, 0.0)).astype(o_ref.dtype)


def _mlp_up(x, g, w, *, tm, tn, layer=None):
    m, k = x.shape
    n = w.shape[-1]
    emit_w = layer is not None
    assert m % tm == 0 and n % tn == 0 and (not emit_w or m == tm)
    w_spec, wb_spec, w_dtype = _weight_specs(w, layer, (k, tn), lambda i, j: (0, j))
    out_shape = [jax.ShapeDtypeStruct((m, n), BF16)]
    out_specs = [pl.BlockSpec((tm, tn), lambda i, j: (i, j))]
    pipelined = [((tm, k), F32), ((k, tn), w_dtype if emit_w else BF16), ((tm, tn), BF16)]
    if emit_w:
        out_shape.append(jax.ShapeDtypeStruct((k, n), BF16))
        out_specs.append(wb_spec)
        pipelined.append(((k, tn), BF16))
    return pl.pallas_call(
        functools.partial(_mlp_up_kernel, emit_w=emit_w),
        out_shape=tuple(out_shape),
        grid=(m // tm, n // tn),
        in_specs=[
            pl.BlockSpec((tm, k), lambda i, j: (i, 0)),
            pl.BlockSpec((1, k), lambda i, j: (0, 0)),
            w_spec,
        ],
        out_specs=tuple(out_specs),
        scratch_shapes=[pltpu.VMEM((tm, k), BF16)],
        compiler_params=pltpu.CompilerParams(
            dimension_semantics=("parallel", "arbitrary"),
            vmem_limit_bytes=_vmem_limit(pipelined, [((tm, k), BF16), ((k, tn), BF16)])),
        name="mlp_up",
    )(x, g, w)


def _mlp_down_kernel(h_ref, w_ref, res_ref, g_ref, o_ref, *rest, emit_w):
    wb_ref = rest[0] if emit_w else None
    kk = pl.program_id(1)

    @pl.when(kk == 0)
    def _():
        o_ref[...] = res_ref[...]

    o_ref[...] += jnp.dot(h_ref[...], _weight_tile(w_ref, wb_ref), preferred_element_type=F32)

    @pl.when(kk == pl.num_programs(1) - 1)
    def _():
        o_ref[...] = _rmsnorm_rows(o_ref[...], g_ref[...])


def _mlp_down_final(h, w, res, g, *, tm, tk, layer=None):
    m, k = h.shape
    n = w.shape[-1]
    emit_w = layer is not None
    assert m % tm == 0 and k % tk == 0 and (not emit_w or m == tm)
    w_spec, wb_spec, w_dtype = _weight_specs(w, layer, (tk, n), lambda i, j: (j, 0))
    out_shape = [jax.ShapeDtypeStruct((m, n), F32)]
    out_specs = [pl.BlockSpec((tm, n), lambda i, j: (i, 0))]
    pipelined = [((tm, tk), BF16), ((tk, n), w_dtype if emit_w else BF16), ((tm, n), F32), ((tm, n), F32)]
    if emit_w:
        out_shape.append(jax.ShapeDtypeStruct((k, n), BF16))
        out_specs.append(wb_spec)
        pipelined.append(((tk, n), BF16))
    return pl.pallas_call(
        functools.partial(_mlp_down_kernel, emit_w=emit_w),
        out_shape=tuple(out_shape),
        grid=(m // tm, k // tk),
        in_specs=[
            pl.BlockSpec((tm, tk), lambda i, j: (i, j)),
            w_spec,
            pl.BlockSpec((tm, n), lambda i, j: (i, 0)),
            pl.BlockSpec((1, n), lambda i, j: (0, 0)),
        ],
        out_specs=tuple(out_specs),
        compiler_params=pltpu.CompilerParams(
            dimension_semantics=("parallel", "arbitrary"),
            vmem_limit_bytes=_vmem_limit(pipelined, [((tk, n), BF16)] if emit_w else [])),
        name="mlp_down",
    )(h, w, res, g)


def _lru_gates(xh, wa_half, wx_half, ba_half, bx_half, hsp):
    xb = xh.astype(BF16)
    tr = jnp.tanh(jnp.dot(xb, wa_half, preferred_element_type=F32) + ba_half)
    ti = jnp.tanh(jnp.dot(xb, wx_half, preferred_element_type=F32) + bx_half)
    nla = tr * hsp + hsp
    a = jnp.exp(-nla)
    q = jnp.tanh(nla) * (1.0 + a * a)
    mult = jnp.where(q > 0.0, q * lax.rsqrt(q), 0.0)
    return a, mult, 0.5 * ti + 0.5


def _scan_rows(a, b, h_prev):
    t, hd = a.shape
    g = t // SUBLANES
    a3 = a.reshape(g, SUBLANES, hd)
    b3 = b.reshape(g, SUBLANES, hd)
    sub = lax.broadcasted_iota(jnp.int32, (g, SUBLANES, hd), 1)
    d = 1
    while d < SUBLANES:
        keep = sub >= d
        a_sh = jnp.where(keep, pltpu.roll(a3, d, axis=1), 1.0)
        b_sh = jnp.where(keep, pltpu.roll(b3, d, axis=1), 0.0)
        b3 = a3 * b_sh + b3
        a3 = a3 * a_sh
        d *= 2
    tiles = []
    h = h_prev
    for k in range(g):
        hk = a3[k] * h + b3[k]
        tiles.append(hk)
        h = hk[SUBLANES - 1:SUBLANES, :]
    return jnp.concatenate(tiles, axis=0), h


def _lru_seq_kernel(gate_ref, x_ref, cw_ref, cb_ref, wa_ref, wx_ref, ba_ref, bx_ref, lam_ref, g_ref,
                    h0_ref, tail0_ref, *rest, pad, reset_first, n_chunks, n_live, fuse_out):
    if fuse_out:
        wo_ref, o_ref, hfin_ref, tailfin_ref, xe_ref, hc_ref, y_ref, yn_ref = rest
    else:
        o_ref, hfin_ref, tailfin_ref, xe_ref, hc_ref, y_ref = rest
    step = pl.program_id(0)
    live = step < n_live
    c = lax.rem(jnp.minimum(step, n_live - 1), n_chunks)
    t, dl = x_ref.shape
    nh, hd = wa_ref.shape[0], wa_ref.shape[1]
    keep = (lambda new, old: jnp.where(live, new, old)) if fuse_out else (lambda new, old: new)

    @pl.when(jnp.logical_and(c == 0, live))
    def _():
        xe_ref[0:HALO, :] = tail0_ref[...]
        hc_ref[...] = h0_ref[...]

    if fuse_out:
        @pl.when(step == 0)
        def _():
            yn_ref[...] = jnp.zeros_like(yn_ref)

    xe_ref[HALO:HALO + t, :] = x_ref[...]
    grow = c * t + lax.broadcasted_iota(jnp.int32, (t, hd), 0)
    ssq = jnp.zeros((t, hd), F32)
    if fuse_out:
        n_pieces = max(1, min(nh, o_ref.shape[1] // MXU_COLS))
        while nh % n_pieces or o_ref.shape[1] % n_pieces:
            n_pieces -= 1
        heads_per_piece, piece = nh // n_pieces, o_ref.shape[1] // n_pieces
    for h in range(nh):
        sl = slice(h * hd, (h + 1) * hd)
        ext = [xe_ref[HALO - 3 + k:HALO - 3 + k + t, sl] for k in range(CONV_TAPS)]
        xh = _conv_taps(ext, 0, cw_ref, cb_ref, sl)
        hsp = (0.5 * LRU_C) * jax.nn.softplus(-lam_ref[:, sl])
        a, mult, i = _lru_gates(xh, wa_ref[h], wx_ref[h], ba_ref[:, sl], bx_ref[:, sl], hsp)
        if reset_first:
            mult = jnp.where(grow == pad, 1.0, mult)
        b = mult * i * xh
        if pad:
            a = jnp.where(grow >= pad, a, 1.0)
            b = jnp.where(grow >= pad, b, 0.0)
        h_prev = hc_ref[:, sl]
        hs, h_last = _scan_rows(a, b, h_prev)
        hc_ref[:, sl] = keep(h_last, h_prev)
        y = hs * jax.nn.gelu(gate_ref[:, sl])
        y_ref[:, sl] = y
        ssq = ssq + y * y
        if fuse_out and (h + 1) % heads_per_piece == 0:
            k = (h + 1) // heads_per_piece - 1
            psl = slice(k * piece, (k + 1) * piece)
            o_ref[:, psl] = jnp.dot(yn_ref[...], wo_ref[:, psl], preferred_element_type=F32)
    scale = lax.rsqrt(jnp.sum(ssq, axis=-1, keepdims=True) / dl + EPS)
    yn = (y_ref[...] * scale * g_ref[...]).astype(BF16)
    if fuse_out:
        yn_ref[...] = yn
    else:
        o_ref[...] = yn
    xe_ref[0:HALO, :] = keep(xe_ref[t:t + HALO, :], xe_ref[0:HALO, :])
    hfin_ref[0] = hc_ref[...]
    tailfin_ref[0] = xe_ref[HALO - 3:HALO, :]


def _lru_seq(proj, params, h0, tail0, *, n_seq, seq_len, t, row0, pad=0, reset_first=False, wo=None):
    cw, cb, wa, wx, ba, bx, lam, g = params
    dl = cw.shape[1]
    nh, hd = wa.shape[0], wa.shape[1]
    assert seq_len % t == 0 and row0 % t == 0 and t % SUBLANES == 0
    n_chunks = seq_len // t
    n_live = n_seq * n_chunks
    blk0 = row0 // t
    fuse_out = wo is not None
    chunk = (lambda s: jnp.minimum(s, n_live - 1)) if fuse_out else (lambda s: s)
    out_chunk = (lambda s: jnp.maximum(s - 1, 0)) if fuse_out else (lambda s: s)
    const2 = lambda s: (0, 0)
    const3 = lambda s: (0, 0, 0)
    per_seq = lambda s: (chunk(s) // n_chunks, 0, 0)
    dout, out_dtype = (wo.shape[1], F32) if fuse_out else (dl, BF16)
    pipelined = [((t, dl), F32), ((t, dl), F32), ((t, dout), out_dtype)]
    resident = [((t + HALO, dl), F32), ((t, dl), F32), ((4 * nh, hd, hd), BF16)]
    scratch = [pltpu.VMEM((t + HALO, dl), F32), pltpu.VMEM((1, dl), F32), pltpu.VMEM((t, dl), F32)]
    extra_specs, extra_args = [], []
    if fuse_out:
        extra_specs.append(pl.BlockSpec((dl, dout), const2, pipeline_mode=pl.Buffered(1)))
        extra_args.append(wo)
        resident += [((dl, dout), BF16), ((t, dl), BF16)]
        scratch.append(pltpu.VMEM((t, dl), BF16))
    kern = functools.partial(_lru_seq_kernel, pad=pad, reset_first=reset_first, n_chunks=n_chunks, n_live=n_live,
                             fuse_out=fuse_out)
    return pl.pallas_call(
        kern,
        out_shape=(jax.ShapeDtypeStruct((n_seq * seq_len, dout), out_dtype),
                   jax.ShapeDtypeStruct((n_seq, 1, dl), F32),
                   jax.ShapeDtypeStruct((n_seq, CONV_TAPS - 1, dl), F32)),
        grid=(n_live + 1 if fuse_out else n_live,),
        in_specs=[
            pl.BlockSpec((t, dl), lambda s: (blk0 + chunk(s), 0)),
            pl.BlockSpec((t, dl), lambda s: (blk0 + chunk(s), 1)),
            pl.BlockSpec((CONV_TAPS, dl), const2),
            pl.BlockSpec((1, dl), const2),
            pl.BlockSpec((nh, hd, hd), const3),
            pl.BlockSpec((nh, hd, hd), const3),
            pl.BlockSpec((1, dl), const2),
            pl.BlockSpec((1, dl), const2),
            pl.BlockSpec((1, dl), const2),
            pl.BlockSpec((1, dl), const2),
            pl.BlockSpec((1, dl), const2),
            pl.BlockSpec((HALO, dl), const2),
        ] + extra_specs,
        out_specs=(
            pl.BlockSpec((t, dout), lambda s: (out_chunk(s), 0)),
            pl.BlockSpec((1, 1, dl), per_seq),
            pl.BlockSpec((1, CONV_TAPS - 1, dl), per_seq),
        ),
        scratch_shapes=scratch,
        compiler_params=pltpu.CompilerParams(
            dimension_semantics=("arbitrary",),
            vmem_limit_bytes=_vmem_limit(pipelined, resident),
            flags=MIXER_SCHED_FLAGS if fuse_out else None),
        name="lru_seq",
    )(proj, proj, cw, cb, wa, wx, ba, bx, lam, g, h0, tail0, *extra_args)


def _lru_slab_kernel(gate_ref, x_ref, cw_ref, cb_ref, wa_ref, wx_ref, ba_ref, bx_ref, lam_ref, g_ref,
                     h0_ref, tail_ref, o_ref, hfin_ref, tailfin_ref, y_ref):
    ls, bs, dl = x_ref.shape
    nh, hd = wa_ref.shape[0], wa_ref.shape[1]
    ntail = CONV_TAPS - 1
    for h in range(nh):
        sl = slice(h * hd, (h + 1) * hd)
        ext = [tail_ref[:, k * dl + h * hd:k * dl + (h + 1) * hd] for k in range(ntail)]
        ext += [x_ref[s, :, sl] for s in range(ls)]
        hsp = (0.5 * LRU_C) * jax.nn.softplus(-lam_ref[:, sl])
        hcur = h0_ref[:, sl]
        for s in range(ls):
            xh = _conv_taps(ext, s, cw_ref, cb_ref, sl)
            a, mult, i = _lru_gates(xh, wa_ref[h], wx_ref[h], ba_ref[:, sl], bx_ref[:, sl], hsp)
            hcur = a * hcur + mult * i * xh
            y_ref[s, :, sl] = hcur * jax.nn.gelu(gate_ref[s, :, sl])
        hfin_ref[:, sl] = hcur
        for k in range(ntail):
            tailfin_ref[:, k * dl + h * hd:k * dl + (h + 1) * hd] = ext[ls + k]
    for s in range(ls):
        y = y_ref[s]
        scale = lax.rsqrt(jnp.mean(y * y, axis=-1, keepdims=True) + EPS)
        o_ref[s] = (y * scale * g_ref[...]).astype(o_ref.dtype)


def _lru_slab(proj3, params, h0, tail, *, ls):
    cw, cb, wa, wx, ba, bx, lam, g = params
    bs = proj3.shape[1]
    dl = cw.shape[1]
    nh, hd = wa.shape[0], wa.shape[1]
    ntail = CONV_TAPS - 1
    c2 = lambda i: (0, 0)
    c3 = lambda i: (0, 0, 0)
    pipelined = [((ls, bs, dl), F32)] * 2 + [((ls, bs, dl), BF16)] + [((bs, (2 * ntail + 2) * dl), F32)]
    return pl.pallas_call(
        _lru_slab_kernel,
        out_shape=(jax.ShapeDtypeStruct((ls, bs, dl), BF16),
                   jax.ShapeDtypeStruct((bs, dl), F32),
                   jax.ShapeDtypeStruct((bs, ntail * dl), F32)),
        grid=(1,),
        in_specs=[
            pl.BlockSpec((ls, bs, dl), lambda i: (0, 0, 0)),
            pl.BlockSpec((ls, bs, dl), lambda i: (0, 0, 1)),
            pl.BlockSpec((CONV_TAPS, dl), c2), pl.BlockSpec((1, dl), c2),
            pl.BlockSpec((nh, hd, hd), c3), pl.BlockSpec((nh, hd, hd), c3),
            pl.BlockSpec((1, dl), c2), pl.BlockSpec((1, dl), c2), pl.BlockSpec((1, dl), c2), pl.BlockSpec((1, dl), c2),
            pl.BlockSpec((bs, dl), c2), pl.BlockSpec((bs, ntail * dl), c2),
        ],
        out_specs=(pl.BlockSpec((ls, bs, dl), c3), pl.BlockSpec((bs, dl), c2), pl.BlockSpec((bs, ntail * dl), c2)),
        scratch_shapes=[pltpu.VMEM((ls, bs, dl), F32)],
        compiler_params=pltpu.CompilerParams(
            dimension_semantics=("arbitrary",),
            vmem_limit_bytes=_vmem_limit(pipelined, [((ls, bs, dl), F32)])),
        name="lru_slab",
    )(proj3, proj3, cw, cb, wa, wx, ba, bx, lam, g, h0, tail)


def _head_lane_mask(n_heads):
    return lax.broadcasted_iota(jnp.int32, (1, HEAD_PAD), 1) < n_heads


def _gated_group_norm(y, z, g):
    yg = y * _silu(z)
    scale = lax.rsqrt(jnp.mean(yg * yg, axis=-1, keepdims=True) + EPS)
    return yg * scale * g


def _ssd_seq_kernel(z_ref, xbc_ref, dt_ref, cw_ref, cb_ref, dtb_ref, alog_ref, dskip_ref, g_ref, e3_ref,
                    s0_ref, tail0_ref, *rest, pad, n_heads, n_groups, n_chunks, n_live, fuse_out):
    if fuse_out:
        wo_ref, part_ref, res_ref = rest[:3]
        o_ref, sfin_ref, tailfin_ref, xe_ref, xc_ref, ex_ref, y_ref, s_ref, yn_ref, ynp_ref = rest[3:]
    else:
        o_ref, sfin_ref, tailfin_ref, xe_ref, xc_ref, ex_ref, y_ref, s_ref = rest
        yn_ref = o_ref
    step = pl.program_id(0)
    live = step < n_live
    c = lax.rem(jnp.minimum(step, n_live - 1), n_chunks)
    keep = (lambda new, old: jnp.where(live, new, old)) if fuse_out else (lambda new, old: new)
    q, ds = z_ref.shape
    dc = xbc_ref.shape[1]
    gn = (dc - ds) // 2
    n = gn // n_groups
    p = ds // n_heads
    r = n_heads // n_groups
    gw = ds // n_groups
    hpb = LANES // p

    @pl.when(jnp.logical_and(c == 0, live))
    def _():
        xe_ref[0:HALO, :] = tail0_ref[...]
        s_ref[...] = s0_ref[...]

    if fuse_out:
        @pl.when(step == 0)
        def _():
            ynp_ref[...] = jnp.zeros_like(ynp_ref)

    xe_ref[HALO:HALO + q, :] = xbc_ref[...]
    valid = (c * q + lax.broadcasted_iota(jnp.int32, (q, 1), 0)) >= pad

    cblk = 512 if dc % 512 == 0 else LANES
    for j in range(dc // cblk):
        sl = slice(j * cblk, (j + 1) * cblk)
        ext = [xe_ref[HALO - 3 + k:HALO - 3 + k + q, sl] for k in range(CONV_TAPS)]
        v = _silu(_conv_taps(ext, 0, cw_ref, cb_ref, sl))
        if pad and (j + 1) * cblk <= ds:
            v = jnp.where(valid, v, 0.0)
        xc_ref[:, sl] = v

    dtv = jnp.where(_head_lane_mask(n_heads), jax.nn.softplus(dt_ref[...] + dtb_ref[...]), 0.0)
    if pad:
        dtv = jnp.where(valid, dtv, 0.0)
    da = dtv * (-jnp.exp(alog_ref[...]))
    ri = lax.broadcasted_iota(jnp.int32, (q, q), 0)
    ci = lax.broadcasted_iota(jnp.int32, (q, q), 1)
    causal = ci <= ri
    tri = jnp.where(causal, 1.0, 0.0).astype(BF16)
    ac3 = jnp.dot(tri, _split3(da), preferred_element_type=F32)
    acum = ac3[:, 0:HEAD_PAD] + ac3[:, HEAD_PAD:2 * HEAD_PAD] + ac3[:, 2 * HEAD_PAD:3 * HEAD_PAD]
    alast = acum[q - 1:q, :]
    eac = jnp.exp(acum)
    wend = jnp.exp(alast - acum) * dtv
    cdec = jnp.broadcast_to(jnp.exp(alast), (SUBLANES, HEAD_PAD))
    ex_ref[...] = _expand_heads(jnp.concatenate([eac, wend, cdec], axis=0), e3_ref)
    acum_t = acum.T
    dt_t = dtv.T
    lane = lax.broadcasted_iota(jnp.int32, (q, LANES), 1)

    for g in range(n_groups):
        gsl = slice(g * gw, (g + 1) * gw)
        bg = xc_ref[:, ds + g * n:ds + (g + 1) * n].astype(BF16)
        cg = xc_ref[:, ds + gn + g * n:ds + gn + (g + 1) * n].astype(BF16)
        cbm = lax.dot_general(cg, bg, (((1,), (1,)), ((), ())), preferred_element_type=F32)
        yoff = jnp.dot(cg, s_ref[:, gsl].astype(BF16), preferred_element_type=F32)
        for k in range(gw // LANES):
            lsl = slice(g * gw + k * LANES, g * gw + (k + 1) * LANES)
            xs = xc_ref[:, lsl]
            ms = []
            xparts = []
            for u in range(hpb):
                h = g * r + k * hpb + u
                seg = acum[:, h:h + 1] - acum_t[h:h + 1, :]
                lm = jnp.where(causal, jnp.exp(seg), 0.0) * dt_t[h:h + 1, :]
                ms.append((cbm * lm).astype(BF16))
                inhead = (lane >= u * p) & (lane < (u + 1) * p)
                xparts.append(jnp.where(inhead, xs, 0.0).astype(BF16))
            ydiag = jnp.dot(jnp.concatenate(ms, axis=1), jnp.concatenate(xparts, axis=0),
                            preferred_element_type=F32)
            y = ydiag + yoff[:, k * LANES:(k + 1) * LANES] * ex_ref[0:q, lsl]
            y_ref[:, lsl] = y + dskip_ref[:, lsl] * xs
        yn_ref[:, gsl] = _gated_group_norm(y_ref[:, gsl], z_ref[:, gsl], g_ref[:, gsl]).astype(BF16)
        xw = (xc_ref[:, gsl] * ex_ref[q:2 * q, gsl]).astype(BF16)
        upd = lax.dot_general(bg, xw, (((0,), (0,)), ((), ())), preferred_element_type=F32)
        s_old = s_ref[:, gsl]
        s_ref[:, gsl] = keep(ex_ref[2 * q:2 * q + 1, gsl] * s_old + upd, s_old)
        if fuse_out:
            piece = o_ref.shape[1] // n_groups
            psl = slice(g * piece, (g + 1) * piece)
            mix = part_ref[:, psl] + jnp.dot(ynp_ref[...], wo_ref[:, psl], preferred_element_type=F32)
            o_ref[:, psl] = res_ref[:, psl] + mix

    if fuse_out:
        ynp_ref[...] = yn_ref[...]
    xe_ref[0:HALO, :] = keep(xe_ref[q:q + HALO, :], xe_ref[0:HALO, :])
    sfin_ref[0] = s_ref[...]
    tailfin_ref[0] = xe_ref[HALO - 3:HALO, :]


def _ssd_seq(proj, dt, params, s0, tail0, *, n_seq, seq_len, row0, n_heads, n_groups, dl, pad=0, fuse=None):
    cw, cb, dtb, alog, dskip, g, e3 = params
    dc = cw.shape[1]
    ds = dskip.shape[1]
    n = s0.shape[0]
    q = SSD_CHUNK
    assert seq_len % q == 0 and row0 % q == 0
    assert (2 * dl) % ds == 0 and (2 * dl + ds) % dc == 0 and LANES % (ds // n_heads) == 0
    n_chunks = seq_len // q
    n_live = n_seq * n_chunks
    blk0 = row0 // q
    fuse_out = fuse is not None
    chunk = (lambda s: jnp.minimum(s, n_live - 1)) if fuse_out else (lambda s: s)
    rows = lambda s: blk0 + chunk(s)
    const2 = lambda s: (0, 0)
    out_rows = (lambda s: (jnp.maximum(s - 1, 0), 0)) if fuse_out else (lambda s: (s, 0))
    per_seq = lambda s: (chunk(s) // n_chunks, 0, 0)
    dout, out_dtype = (fuse[0].shape[1], F32) if fuse_out else (ds, BF16)
    pipelined = [((q, ds), F32), ((q, dc), F32), ((q, HEAD_PAD), F32), ((q, dout), out_dtype)]
    resident = [((q + HALO, dc), F32), ((q, dc), F32), ((2 * q + SUBLANES, ds), F32), ((q, ds), F32),
                ((3 * n, ds), F32), ((6 * HEAD_PAD, ds), BF16)]
    scratch = [pltpu.VMEM((q + HALO, dc), F32), pltpu.VMEM((q, dc), F32),
               pltpu.VMEM((2 * q + SUBLANES, ds), F32), pltpu.VMEM((q, ds), F32), pltpu.VMEM((n, ds), F32)]
    extra_specs, extra_args = [], []
    if fuse_out:
        extra_specs = [pl.BlockSpec((ds, dout), const2, pipeline_mode=pl.Buffered(1)),
                       pl.BlockSpec((q, dout), out_rows), pl.BlockSpec((q, dout), out_rows)]
        extra_args = list(fuse)
        pipelined += [((q, dout), F32)] * 2
        resident += [((ds, dout), BF16), ((2 * q, ds), BF16)]
        scratch += [pltpu.VMEM((q, ds), BF16), pltpu.VMEM((q, ds), BF16)]
    kern = functools.partial(_ssd_seq_kernel, pad=pad, n_heads=n_heads, n_groups=n_groups, n_chunks=n_chunks,
                             n_live=n_live, fuse_out=fuse_out)
    return pl.pallas_call(
        kern,
        out_shape=(jax.ShapeDtypeStruct((n_seq * seq_len, dout), out_dtype),
                   jax.ShapeDtypeStruct((n_seq, n, ds), F32),
                   jax.ShapeDtypeStruct((n_seq, CONV_TAPS - 1, dc), F32)),
        grid=(n_live + 1 if fuse_out else n_live,),
        in_specs=[
            pl.BlockSpec((q, ds), lambda s: (rows(s), (2 * dl) // ds)),
            pl.BlockSpec((q, dc), lambda s: (rows(s), (2 * dl + ds) // dc)),
            pl.BlockSpec((q, HEAD_PAD), lambda s: (rows(s), 0)),
            pl.BlockSpec((CONV_TAPS, dc), const2),
            pl.BlockSpec((1, dc), const2),
            pl.BlockSpec((1, HEAD_PAD), const2),
            pl.BlockSpec((1, HEAD_PAD), const2),
            pl.BlockSpec((1, ds), const2),
            pl.BlockSpec((1, ds), const2),
            pl.BlockSpec((3 * HEAD_PAD, ds), const2),
            pl.BlockSpec((n, ds), const2),
            pl.BlockSpec((HALO, dc), const2),
        ] + extra_specs,
        out_specs=(
            pl.BlockSpec((q, dout), out_rows),
            pl.BlockSpec((1, n, ds), per_seq),
            pl.BlockSpec((1, CONV_TAPS - 1, dc), per_seq),
        ),
        scratch_shapes=scratch,
        compiler_params=pltpu.CompilerParams(
            dimension_semantics=("arbitrary",),
            vmem_limit_bytes=_vmem_limit(pipelined, resident),
            flags=MIXER_SCHED_FLAGS if fuse_out else None),
        name="ssd_seq",
    )(proj, proj, dt, cw, cb, dtb, alog, dskip, g, e3, s0, tail0, *extra_args)


def _ssd_slab_pre_kernel(xbc_ref, dt_ref, tail_ref, cw_ref, cb_ref, dtb_ref, alog_ref, dskip_ref, e3_ref,
                         ypart_ref, eace_ref, c_ref, b_ref, xw_ref, cdec_ref, tailfin_ref, xc_ref,
                         *, n_heads, n_groups):
    ls, bs, dc = xbc_ref.shape
    ds = dskip_ref.shape[1]
    gn = (dc - ds) // 2
    n = gn // n_groups
    r = n_heads // n_groups
    ntail = CONV_TAPS - 1

    cblk = 512 if dc % 512 == 0 else LANES
    for j in range(dc // cblk):
        sl = slice(j * cblk, (j + 1) * cblk)
        ext = [tail_ref[:, k * dc + j * cblk:k * dc + (j + 1) * cblk] for k in range(ntail)]
        ext += [xbc_ref[s, :, sl] for s in range(ls)]
        for s in range(ls):
            xc_ref[s, :, sl] = _silu(_conv_taps(ext, s, cw_ref, cb_ref, sl))
        for k in range(ntail):
            tailfin_ref[:, k * dc + j * cblk:k * dc + (j + 1) * cblk] = ext[ls + k]

    hmask = _head_lane_mask(n_heads)
    a_neg = -jnp.exp(alog_ref[...])
    dtv, acum = [], []
    run = jnp.zeros((bs, HEAD_PAD), F32)
    for s in range(ls):
        d = jnp.where(hmask, jax.nn.softplus(dt_ref[s] + dtb_ref[...]), 0.0)
        run = run + d * a_neg
        dtv.append(d)
        acum.append(run)
    alast = acum[ls - 1]
    cdec_ref[...] = jnp.exp(alast)
    head_group = lax.broadcasted_iota(jnp.int32, (1, HEAD_PAD), 1) // r

    for s in range(ls):
        eace_ref[s] = _expand_heads(jnp.exp(acum[s]), e3_ref)
        wend_e = _expand_heads(jnp.exp(alast - acum[s]) * dtv[s], e3_ref)
        xw_ref[:, s * ds:(s + 1) * ds] = xc_ref[s, :, 0:ds] * wend_e
        b_ref[:, s * gn:(s + 1) * gn] = xc_ref[s, :, ds:ds + gn]
        c_ref[:, s * gn:(s + 1) * gn] = xc_ref[s, :, ds + gn:ds + 2 * gn]
        ypart = dskip_ref[...] * xc_ref[s, :, 0:ds]
        for j in range(s + 1):
            cbh = jnp.zeros((bs, HEAD_PAD), F32)
            for g in range(n_groups):
                cs = xc_ref[s, :, ds + gn + g * n:ds + gn + (g + 1) * n]
                bj = xc_ref[j, :, ds + g * n:ds + (g + 1) * n]
                cbg = jnp.sum(cs * bj, axis=-1, keepdims=True)
                cbh = cbh + jnp.where(head_group == g, cbg, 0.0)
            coef = cbh * (jnp.exp(acum[s] - acum[j]) * dtv[j])
            ypart = ypart + _expand_heads(coef, e3_ref) * xc_ref[j, :, 0:ds]
        ypart_ref[s] = ypart
    for s in range(ls, SLAB_ROWS):
        xw_ref[:, s * ds:(s + 1) * ds] = jnp.zeros((bs, ds), F32)
        b_ref[:, s * gn:(s + 1) * gn] = jnp.zeros((bs, gn), F32)
        c_ref[:, s * gn:(s + 1) * gn] = jnp.zeros((bs, gn), F32)


def _ssd_slab_pre(proj3, dt3, tail, params, *, ls, n_heads, n_groups, dl):
    cw, cb, dtb, alog, dskip, _, e3 = params
    bs = proj3.shape[1]
    dc = cw.shape[1]
    ds = dskip.shape[1]
    gn = (dc - ds) // 2
    ntail = CONV_TAPS - 1
    assert ls <= SLAB_ROWS and (2 * dl + ds) % dc == 0
    c2 = lambda i: (0, 0)
    c3 = lambda i: (0, 0, 0)
    pipelined = [((ls, bs, dc), F32), ((ls, bs, HEAD_PAD), F32), ((bs, 2 * ntail * dc), F32),
                 ((2 * ls, bs, ds), F32), ((bs, SLAB_ROWS * (2 * gn + ds)), F32), ((3 * HEAD_PAD, ds), BF16)]
    kern = functools.partial(_ssd_slab_pre_kernel, n_heads=n_heads, n_groups=n_groups)
    return pl.pallas_call(
        kern,
        out_shape=(jax.ShapeDtypeStruct((ls, bs, ds), F32),
                   jax.ShapeDtypeStruct((ls, bs, ds), F32),
                   jax.ShapeDtypeStruct((bs, SLAB_ROWS * gn), F32),
                   jax.ShapeDtypeStruct((bs, SLAB_ROWS * gn), F32),
                   jax.ShapeDtypeStruct((bs, SLAB_ROWS * ds), F32),
                   jax.ShapeDtypeStruct((bs, HEAD_PAD), F32),
                   jax.ShapeDtypeStruct((bs, ntail * dc), F32)),
        grid=(1,),
        in_specs=[
            pl.BlockSpec((ls, bs, dc), lambda i: (0, 0, (2 * dl + ds) // dc)),
            pl.BlockSpec((ls, bs, HEAD_PAD), c3),
            pl.BlockSpec((bs, ntail * dc), c2),
            pl.BlockSpec((CONV_TAPS, dc), c2), pl.BlockSpec((1, dc), c2),
            pl.BlockSpec((1, HEAD_PAD), c2), pl.BlockSpec((1, HEAD_PAD), c2),
            pl.BlockSpec((1, ds), c2), pl.BlockSpec((3 * HEAD_PAD, ds), c2),
        ],
        out_specs=(pl.BlockSpec((ls, bs, ds), c3), pl.BlockSpec((ls, bs, ds), c3),
                   pl.BlockSpec((bs, SLAB_ROWS * gn), c2), pl.BlockSpec((bs, SLAB_ROWS * gn), c2),
                   pl.BlockSpec((bs, SLAB_ROWS * ds), c2), pl.BlockSpec((bs, HEAD_PAD), c2),
                   pl.BlockSpec((bs, ntail * dc), c2)),
        scratch_shapes=[pltpu.VMEM((ls, bs, dc), F32)],
        compiler_params=pltpu.CompilerParams(
            dimension_semantics=("arbitrary",),
            vmem_limit_bytes=_vmem_limit(pipelined, [((ls, bs, dc), F32)])),
        name="ssd_slab_pre",
    )(proj3, dt3, tail, cw, cb, dtb, alog, dskip, e3)


def _ssd_state_kernel(cdec_ref, s_ref, c_ref, b_ref, xw_ref, snew_ref, yoff_ref, *, n_heads, n_groups):
    i = pl.program_id(0)
    sb, hp, n = s_ref.shape
    p = hp // n_heads
    r = n_heads // n_groups
    gw = hp // n_groups
    for q in range(sb):
        for g in range(n_groups):
            gsl = slice(g * gw, (g + 1) * gw)
            sg = s_ref[q, gsl, :]
            cg = c_ref[q, :, g * n:(g + 1) * n].astype(BF16)
            yoff_ref[q, :, gsl] = lax.dot_general(cg, sg.astype(BF16), (((1,), (1,)), ((), ())),
                                                  preferred_element_type=F32)
            upd = lax.dot_general(xw_ref[q, :, gsl].astype(BF16), b_ref[q, :, g * n:(g + 1) * n].astype(BF16),
                                  (((0,), (0,)), ((), ())), preferred_element_type=F32)
            for u in range(r):
                h = g * r + u
                rows = slice(h * p, (h + 1) * p)
                snew_ref[q, rows, :] = cdec_ref[i * sb + q, h] * s_ref[q, rows, :] + upd[u * p:(u + 1) * p, :]


def _ssd_state(cdec, state, c_rows, b_rows, xw_rows, *, n_heads, n_groups):
    bs, hp, n = state.shape
    gn = c_rows.shape[2]
    sb = STATE_SEQS_PER_STEP if bs % STATE_SEQS_PER_STEP == 0 else 1
    per_seq = lambda i: (i, 0, 0)
    pipelined = [((sb, hp, n), F32)] * 2 + [((sb, SLAB_ROWS, gn), F32)] * 2 + [((sb, SLAB_ROWS, hp), F32)] * 2
    kern = functools.partial(_ssd_state_kernel, n_heads=n_heads, n_groups=n_groups)
    return pl.pallas_call(
        kern,
        out_shape=(jax.ShapeDtypeStruct((bs, hp, n), F32), jax.ShapeDtypeStruct((bs, SLAB_ROWS, hp), F32)),
        grid=(bs // sb,),
        in_specs=[
            pl.BlockSpec(memory_space=pltpu.SMEM),
            pl.BlockSpec((sb, hp, n), per_seq),
            pl.BlockSpec((sb, SLAB_ROWS, gn), per_seq),
            pl.BlockSpec((sb, SLAB_ROWS, gn), per_seq),
            pl.BlockSpec((sb, SLAB_ROWS, hp), per_seq),
        ],
        out_specs=(pl.BlockSpec((sb, hp, n), per_seq), pl.BlockSpec((sb, SLAB_ROWS, hp), per_seq)),
        compiler_params=pltpu.CompilerParams(
            dimension_semantics=("parallel",),
            vmem_limit_bytes=_vmem_limit(pipelined, [])),
        name="ssd_state",
    )(cdec, state, c_rows, b_rows, xw_rows)


def _ssd_slab_post_kernel(ypart_ref, eace_ref, yoff_ref, z_ref, g_ref, o_ref, *, n_groups):
    ls, bs, ds = ypart_ref.shape
    gw = ds // n_groups
    for s in range(ls):
        for g in range(n_groups):
            gsl = slice(g * gw, (g + 1) * gw)
            y = ypart_ref[s, :, gsl] + eace_ref[s, :, gsl] * yoff_ref[:, s * ds + g * gw:s * ds + (g + 1) * gw]
            o_ref[s, :, gsl] = _gated_group_norm(y, z_ref[s, :, gsl], g_ref[:, gsl]).astype(o_ref.dtype)


def _ssd_slab_post(ypart, eace, yoff, proj3, g, *, n_groups, dl):
    ls, bs, ds = ypart.shape
    assert (2 * dl) % ds == 0
    c2 = lambda i: (0, 0)
    c3 = lambda i: (0, 0, 0)
    pipelined = [((ls, bs, ds), F32)] * 3 + [((bs, SLAB_ROWS * ds), F32), ((ls, bs, ds), BF16)]
    kern = functools.partial(_ssd_slab_post_kernel, n_groups=n_groups)
    return pl.pallas_call(
        kern,
        out_shape=jax.ShapeDtypeStruct((ls, bs, ds), BF16),
        grid=(1,),
        in_specs=[
            pl.BlockSpec((ls, bs, ds), c3), pl.BlockSpec((ls, bs, ds), c3),
            pl.BlockSpec((bs, SLAB_ROWS * ds), c2),
            pl.BlockSpec((ls, bs, ds), lambda i: (0, 0, (2 * dl) // ds)),
            pl.BlockSpec((1, ds), c2),
        ],
        out_specs=pl.BlockSpec((ls, bs, ds), c3),
        compiler_params=pltpu.CompilerParams(
            dimension_semantics=("arbitrary",),
            vmem_limit_bytes=_vmem_limit(pipelined, [])),
        name="ssd_slab_post",
    )(ypart, eace, yoff, proj3, g)


def _head_expansion(n_heads, head_dim):
    rows = lax.broadcasted_iota(jnp.int32, (HEAD_PAD, n_heads * head_dim), 0)
    cols = lax.broadcasted_iota(jnp.int32, (HEAD_PAD, n_heads * head_dim), 1)
    e = (cols // head_dim == rows).astype(BF16)
    return jnp.concatenate([e, e, e], axis=0)


def _pad_lanes(v, width):
    return jnp.pad(v, ((0, 0), (0, width - v.shape[1])))


def _tail_block(tail):
    return jnp.pad(tail, ((HALO - tail.shape[0], 0), (0, 0)))


def _mlp(x1, g_mlp, w_up, w_down, g_final, *, layer=None):
    m = x1.shape[0]
    tiles = (1024, 512, 256, 128)
    tm = m if layer is not None else _pick_tile(m, tiles)
    up = _mlp_up(x1, g_mlp, w_up, tm=tm, tn=_pick_tile(w_up.shape[-1], tiles), layer=layer)
    hid = up[0]
    down = _mlp_down_final(hid, w_down, x1, g_final, tm=tm, tk=_pick_tile(w_down.shape[-2], tiles), layer=layer)
    if layer is None:
        return down[0]
    return down[0], up[1], down[1]


def kernel(x_prompt, x_sample, state_lru_h, state_lru_conv, state_ssd, state_ssd_conv, meta_tokens, g_mix, w_in, conv_lru_w, conv_lru_b, lru_wa, lru_ba, lru_wx, lru_bx, lru_lambda, g_lru_out, conv_ssd_w, conv_ssd_b, dt_bias, a_log, d_skip, g_ssd_out, w_out, g_mlp, w_up, w_down, g_final):
    depth = w_in.shape[0]
    assert depth == 1, "single-layer step"
    l = 0
    bp, lp, d = x_prompt.shape
    bs, ls, _ = x_sample.shape
    n_meta = meta_tokens.shape[0]
    dl = state_lru_h.shape[-1]
    n_heads, p, n = state_ssd.shape[-3:]
    ds = n_heads * p
    dc = state_ssd_conv.shape[-1]
    gn = (dc - ds) // 2
    n_groups = gn // n
    nw = 2 * dl + ds + dc
    ntail = CONV_TAPS - 1
    q = SSD_CHUNK
    meta_pad = (-n_meta) % q
    assert n_heads <= HEAD_PAD and (bs * ls) % q == 0 and lp % q == 0 and q % bs == 0

    row = lambda v: v.reshape(1, -1).astype(F32)
    w_in_b = w_in[l].astype(BF16)
    w_dt = _pad_lanes(w_in_b[:, nw:], HEAD_PAD)
    lru_params = (conv_lru_w[l], row(conv_lru_b[l]), (0.5 * lru_wa[l]).astype(BF16), (0.5 * lru_wx[l]).astype(BF16),
                  row(0.5 * lru_ba[l]), row(0.5 * lru_bx[l]), row(lru_lambda[l]), row(g_lru_out[l]))
    ssd_params = (conv_ssd_w[l], row(conv_ssd_b[l]), _pad_lanes(row(dt_bias[l]), HEAD_PAD),
                  _pad_lanes(row(a_log[l]), HEAD_PAD), row(jnp.repeat(d_skip[l], p)), row(g_ssd_out[l]),
                  _head_expansion(n_heads, p))
    g_mix_r, g_mlp_r, g_final_r = row(g_mix[l]), row(g_mlp[l]), row(g_final)

    xs_tm = x_sample.transpose(1, 0, 2).reshape(ls * bs, d)
    x_side = jnp.concatenate([xs_tm, jnp.zeros((meta_pad, d), F32), meta_tokens.astype(F32)], axis=0)
    xp_rows = x_prompt.reshape(bp * lp, d)

    tiles = (1024, 512, 256, 128)
    tn_in = _pick_tile(nw, tiles)
    proj_side, dt_side = _in_proj(x_side, g_mix_r, w_in_b, w_dt, nw=nw, tm=x_side.shape[0], tn=tn_in)

    proj_s3 = proj_side.reshape(-1, bs, nw)
    dt_s3 = dt_side.reshape(-1, bs, HEAD_PAD)
    lru_s, s_h, s_ltail = _lru_slab(proj_s3, lru_params, state_lru_h[l],
                                    state_lru_conv[l].reshape(bs, ntail * dl), ls=ls)
    ypart, eace, c_rows, b_rows, xw_rows, cdec, s_stail = _ssd_slab_pre(
        proj_s3, dt_s3, state_ssd_conv[l].reshape(bs, ntail * dc), ssd_params,
        ls=ls, n_heads=n_heads, n_groups=n_groups, dl=dl)
    s_new, yoff = _ssd_state(cdec, state_ssd[l].reshape(bs, ds, n),
                             c_rows.reshape(bs, SLAB_ROWS, gn), b_rows.reshape(bs, SLAB_ROWS, gn),
                             xw_rows.reshape(bs, SLAB_ROWS, ds), n_heads=n_heads, n_groups=n_groups)
    ssd_s = _ssd_slab_post(ypart, eace, yoff.reshape(bs, SLAB_ROWS * ds), proj_s3, ssd_params[5],
                           n_groups=n_groups, dl=dl)
    x1_s, w_out_lru, w_out_ssd = _out_proj(lru_s.reshape(ls * bs, dl), ssd_s.reshape(ls * bs, ds), w_out, l, xs_tm,
                                           tn=_pick_tile(d, (512, 256, 128)))
    y_s, w_up_b, w_down_b = _mlp(x1_s, g_mlp_r, w_up, w_down, g_final_r, layer=l)

    meta_row0 = ls * bs
    _, m_h, m_ltail = _lru_seq(proj_side, lru_params, jnp.zeros((1, dl), F32), jnp.zeros((HALO, dl), F32),
                               n_seq=1, seq_len=q, t=q, row0=meta_row0, pad=meta_pad, reset_first=True)
    _, m_s, m_stail = _ssd_seq(proj_side, dt_side, ssd_params, jnp.zeros((n, ds), F32), jnp.zeros((HALO, dc), F32),
                               n_seq=1, seq_len=q, row0=meta_row0, n_heads=n_heads, n_groups=n_groups, dl=dl,
                               pad=meta_pad)

    proj_p, dt_p = _in_proj(xp_rows, g_mix_r, w_in_b, w_dt, nw=nw, tm=_pick_tile(bp * lp, tiles), tn=tn_in)
    part_p, p_h, p_ltail = _lru_seq(proj_p, lru_params, m_h[0], _tail_block(m_ltail[0]),
                                    n_seq=bp, seq_len=lp, t=_pick_tile(lp, (256, 128)), row0=0, wo=w_out_lru)
    x1_p, p_s, p_stail = _ssd_seq(proj_p, dt_p, ssd_params, m_s[0], _tail_block(m_stail[0]),
                                  n_seq=bp, seq_len=lp, row0=0, n_heads=n_heads, n_groups=n_groups, dl=dl,
                                  fuse=(w_out_ssd, part_p, xp_rows))
    y_p = _mlp(x1_p, g_mlp_r, w_up_b, w_down_b, g_final_r)

    y_prompt = y_p.reshape(bp, lp, d)
    y_sample = y_s.reshape(ls, bs, d).transpose(1, 0, 2)
    p_lru_h = p_h.reshape(1, bp, dl)
    p_lru_conv = p_ltail.reshape(1, bp, ntail, dl)
    p_ssd = p_s.transpose(0, 2, 1).reshape(1, bp, n_heads, p, n)
    p_ssd_conv = p_stail.reshape(1, bp, ntail, dc)
    s_lru_h = s_h.reshape(1, bs, dl)
    s_lru_conv = s_ltail.reshape(1, bs, ntail, dl)
    s_ssd = s_new.reshape(1, bs, n_heads, p, n)
    s_ssd_conv = s_stail.reshape(1, bs, ntail, dc)
    return (y_prompt, y_sample, p_lru_h, p_lru_conv, p_ssd, p_ssd_conv, s_lru_h, s_lru_conv, s_ssd, s_ssd_conv)
```

```python
import functools

import jax
import jax.numpy as jnp
from jax import lax
from jax.experimental import pallas as pl
from jax.experimental.pallas import tpu as pltpu

F32 = jnp.float32
BF16 = jnp.bfloat16

EPS = 1e-6
LRU_C = 8.0
CONV_TAPS = 4

LANES = 128
SUBLANES = 8
MXU_COLS = 256
VMEM_BYTES_V7X = 64 * 1024 * 1024
VMEM_TEMP_BYTES = 10 * 1024 * 1024
VMEM_CEILING_BYTES = VMEM_BYTES_V7X - 6 * 1024 * 1024

HALO = SUBLANES
SSD_CHUNK = 128
HEAD_PAD = LANES
SLAB_ROWS = SUBLANES
STATE_SEQS_PER_STEP = 4
MIXER_SCHED_FLAGS = None


def _nbytes(shape, dtype):
    n = 1
    for s in shape:
        n *= s
    return n * jnp.dtype(dtype).itemsize


def _vmem_limit(pipelined, resident):
    est = 2 * sum(_nbytes(s, d) for s, d in pipelined) + sum(_nbytes(s, d) for s, d in resident)
    return int(min(est + VMEM_TEMP_BYTES, VMEM_CEILING_BYTES))


def _pick_tile(m, prefs):
    for t in prefs:
        if m % t == 0:
            return t
    return m


def _silu(x):
    h = 0.5 * x
    return h * jnp.tanh(h) + h


def _split3(x):
    hi = x.astype(BF16)
    r1 = x - hi.astype(F32)
    mid = r1.astype(BF16)
    lo = (r1 - mid.astype(F32)).astype(BF16)
    return jnp.concatenate([hi, mid, lo], axis=1)


def _expand_heads(x, e3_ref):
    return jnp.dot(_split3(x), e3_ref[...], preferred_element_type=F32)


def _rmsnorm_rows(x, g):
    ms = jnp.mean(x * x, axis=-1, keepdims=True)
    return x * lax.rsqrt(ms + EPS) * g


def _conv_taps(ext, s, cw_ref, cb_ref, sl):
    v = cb_ref[:, sl] + ext[s] * cw_ref[0:1, sl]
    for k in range(1, CONV_TAPS):
        v = v + ext[s + k] * cw_ref[k:k + 1, sl]
    return v


def _weight_tile(w_ref, wb_ref):
    if wb_ref is None:
        return w_ref[...]
    w = w_ref[...].astype(BF16)
    wb_ref[...] = w
    return w


def _weight_specs(w, layer, blk, idx):
    if layer is None:
        return pl.BlockSpec(blk, idx), None, F32
    in_spec = pl.BlockSpec((None,) + blk, lambda i, j: (layer,) + idx(i, j))
    return in_spec, pl.BlockSpec(blk, idx), w.dtype


def _in_proj_kernel(x_ref, g_ref, w_ref, wdt_ref, o_ref, dt_ref, xn_ref, *, gelu_tiles, silu_tiles):
    j = pl.program_id(1)

    @pl.when(j == 0)
    def _():
        xn = _rmsnorm_rows(x_ref[...], g_ref[...]).astype(BF16)
        xn_ref[...] = xn
        dt_ref[...] = jnp.dot(xn, wdt_ref[...], preferred_element_type=F32)

    def tile():
        return jnp.dot(xn_ref[...], w_ref[...], preferred_element_type=F32)

    in_range = lambda r: jnp.logical_and(j >= r[0], j < r[1])
    is_gelu, is_silu = in_range(gelu_tiles), in_range(silu_tiles)

    @pl.when(is_gelu)
    def _():
        o_ref[...] = jax.nn.gelu(tile())

    @pl.when(is_silu)
    def _():
        o_ref[...] = _silu(tile())

    @pl.when(jnp.logical_not(jnp.logical_or(is_gelu, is_silu)))
    def _():
        o_ref[...] = tile()


def _in_proj(x, g, w, wdt, *, nw, tm, tn, gelu_cols, silu_cols):
    m, k = x.shape
    assert m % tm == 0 and nw % tn == 0 and all(c % tn == 0 for c in gelu_cols + silu_cols)
    pipelined = [((tm, k), F32), ((k, tn), BF16), ((k, HEAD_PAD), BF16), ((tm, tn), F32), ((tm, HEAD_PAD), F32)]
    kern = functools.partial(_in_proj_kernel, gelu_tiles=tuple(c // tn for c in gelu_cols),
                             silu_tiles=tuple(c // tn for c in silu_cols))
    return pl.pallas_call(
        kern,
        out_shape=(jax.ShapeDtypeStruct((m, nw), F32), jax.ShapeDtypeStruct((m, HEAD_PAD), F32)),
        grid=(m // tm, nw // tn),
        in_specs=[
            pl.BlockSpec((tm, k), lambda i, j: (i, 0)),
            pl.BlockSpec((1, k), lambda i, j: (0, 0)),
            pl.BlockSpec((k, tn), lambda i, j: (0, j)),
            pl.BlockSpec((k, HEAD_PAD), lambda i, j: (0, 0)),
        ],
        out_specs=(pl.BlockSpec((tm, tn), lambda i, j: (i, j)), pl.BlockSpec((tm, HEAD_PAD), lambda i, j: (i, 0))),
        scratch_shapes=[pltpu.VMEM((tm, k), BF16)],
        compiler_params=pltpu.CompilerParams(
            dimension_semantics=("parallel", "arbitrary"),
            vmem_limit_bytes=_vmem_limit(pipelined, [((tm, k), BF16)])),
        name="in_proj",
    )(x, g, w, wdt)


def _out_proj_kernel(a1_ref, a2_ref, w1_ref, w2_ref, res_ref, o_ref, wb1_ref, wb2_ref):
    acc = jnp.dot(a1_ref[...], _weight_tile(w1_ref, wb1_ref), preferred_element_type=F32)
    acc = acc + jnp.dot(a2_ref[...], _weight_tile(w2_ref, wb2_ref), preferred_element_type=F32)
    o_ref[...] = res_ref[...] + acc


def _out_proj(a1, a2, w, layer, res, *, tn):
    m, k1 = a1.shape
    k2 = a2.shape[1]
    n = w.shape[2]
    assert n % tn == 0 and k1 == k2 and w.shape[1] == k1 + k2
    pipelined = [((m, k1), BF16), ((m, k2), BF16), ((k1, tn), w.dtype), ((k2, tn), w.dtype), ((m, tn), F32),
                 ((m, tn), F32), ((k1, tn), BF16), ((k2, tn), BF16)]
    col = lambda j: (0, j)
    return pl.pallas_call(
        _out_proj_kernel,
        out_shape=(jax.ShapeDtypeStruct((m, n), F32), jax.ShapeDtypeStruct((k1, n), BF16),
                   jax.ShapeDtypeStruct((k2, n), BF16)),
        grid=(n // tn,),
        in_specs=[
            pl.BlockSpec((m, k1), lambda j: (0, 0)),
            pl.BlockSpec((m, k2), lambda j: (0, 0)),
            pl.BlockSpec((None, k1, tn), lambda j: (layer, 0, j)),
            pl.BlockSpec((None, k2, tn), lambda j: (layer, 1, j)),
            pl.BlockSpec((m, tn), col),
        ],
        out_specs=(pl.BlockSpec((m, tn), col), pl.BlockSpec((k1, tn), col), pl.BlockSpec((k2, tn), col)),
        compiler_params=pltpu.CompilerParams(
            dimension_semantics=("parallel",),
            vmem_limit_bytes=_vmem_limit(pipelined, [])),
        name="out_proj",
    )(a1, a2, w, w, res)


def _mlp_up_kernel(x_ref, g_ref, w_ref, o_ref, *rest, emit_w):
    wb_ref, xn_ref = rest if emit_w else (None, rest[0])

    @pl.when(pl.program_id(1) == 0)
    def _():
        xn_ref[...] = _rmsnorm_rows(x_ref[...], g_ref[...]).astype(BF16)

    acc = jnp.dot(xn_ref[...], _weight_tile(w_ref, wb_ref), preferred_element_type=F32)
    o_ref[...] = jnp.square(jnp.maximum(acc, 0.0)).astype(o_ref.dtype)


def _mlp_up(x, g, w, *, tm, tn, layer=None):
    m, k = x.shape
    n = w.shape[-1]
    emit_w = layer is not None
    assert m % tm == 0 and n % tn == 0 and (not emit_w or m == tm)
    w_spec, wb_spec, w_dtype = _weight_specs(w, layer, (k, tn), lambda i, j: (0, j))
    out_shape = [jax.ShapeDtypeStruct((m, n), BF16)]
    out_specs = [pl.BlockSpec((tm, tn), lambda i, j: (i, j))]
    pipelined = [((tm, k), F32), ((k, tn), w_dtype if emit_w else BF16), ((tm, tn), BF16)]
    if emit_w:
        out_shape.append(jax.ShapeDtypeStruct((k, n), BF16))
        out_specs.append(wb_spec)
        pipelined.append(((k, tn), BF16))
    return pl.pallas_call(
        functools.partial(_mlp_up_kernel, emit_w=emit_w),
        out_shape=tuple(out_shape),
        grid=(m // tm, n // tn),
        in_specs=[
            pl.BlockSpec((tm, k), lambda i, j: (i, 0)),
            pl.BlockSpec((1, k), lambda i, j: (0, 0)),
            w_spec,
        ],
        out_specs=tuple(out_specs),
        scratch_shapes=[pltpu.VMEM((tm, k), BF16)],
        compiler_params=pltpu.CompilerParams(
            dimension_semantics=("parallel", "arbitrary"),
            vmem_limit_bytes=_vmem_limit(pipelined, [((tm, k), BF16), ((k, tn), BF16)])),
        name="mlp_up",
    )(x, g, w)


def _mlp_down_kernel(h_ref, w_ref, res_ref, g_ref, o_ref, *rest, emit_w):
    wb_ref = rest[0] if emit_w else None
    kk = pl.program_id(1)

    @pl.when(kk == 0)
    def _():
        o_ref[...] = res_ref[...]

    o_ref[...] += jnp.dot(h_ref[...], _weight_tile(w_ref, wb_ref), preferred_element_type=F32)

    @pl.when(kk == pl.num_programs(1) - 1)
    def _():
        o_ref[...] = _rmsnorm_rows(o_ref[...], g_ref[...])


def _mlp_down_final(h, w, res, g, *, tm, tk, layer=None):
    m, k = h.shape
    n = w.shape[-1]
    emit_w = layer is not None
    assert m % tm == 0 and k % tk == 0 and (not emit_w or m == tm)
    w_spec, wb_spec, w_dtype = _weight_specs(w, layer, (tk, n), lambda i, j: (j, 0))
    out_shape = [jax.ShapeDtypeStruct((m, n), F32)]
    out_specs = [pl.BlockSpec((tm, n), lambda i, j: (i, 0))]
    pipelined = [((tm, tk), BF16), ((tk, n), w_dtype if emit_w else BF16), ((tm, n), F32), ((tm, n), F32)]
    if emit_w:
        out_shape.append(jax.ShapeDtypeStruct((k, n), BF16))
        out_specs.append(wb_spec)
        pipelined.append(((tk, n), BF16))
    return pl.pallas_call(
        functools.partial(_mlp_down_kernel, emit_w=emit_w),
        out_shape=tuple(out_shape),
        grid=(m // tm, k // tk),
        in_specs=[
            pl.BlockSpec((tm, tk), lambda i, j: (i, j)),
            w_spec,
            pl.BlockSpec((tm, n), lambda i, j: (i, 0)),
            pl.BlockSpec((1, n), lambda i, j: (0, 0)),
        ],
        out_specs=tuple(out_specs),
        compiler_params=pltpu.CompilerParams(
            dimension_semantics=("parallel", "arbitrary"),
            vmem_limit_bytes=_vmem_limit(pipelined, [((tk, n), BF16)] if emit_w else [])),
        name="mlp_down",
    )(h, w, res, g)


def _lru_gates(xh, wa_half, wx_half, ba_half, bx_half, hsp):
    xb = xh.astype(BF16)
    tr = jnp.tanh(jnp.dot(xb, wa_half, preferred_element_type=F32) + ba_half)
    ti = jnp.tanh(jnp.dot(xb, wx_half, preferred_element_type=F32) + bx_half)
    nla = tr * hsp + hsp
    a = jnp.exp(-nla)
    q = jnp.tanh(nla) * (1.0 + a * a)
    mult = jnp.where(q > 0.0, q * lax.rsqrt(q), 0.0)
    return a, mult, 0.5 * ti + 0.5


def _scan_rows(a, b, h_prev):
    t, hd = a.shape
    g = t // SUBLANES
    a3 = a.reshape(g, SUBLANES, hd)
    b3 = b.reshape(g, SUBLANES, hd)
    sub = lax.broadcasted_iota(jnp.int32, (g, SUBLANES, hd), 1)
    d = 1
    while d < SUBLANES:
        keep = sub >= d
        a_sh = jnp.where(keep, pltpu.roll(a3, d, axis=1), 1.0)
        b_sh = jnp.where(keep, pltpu.roll(b3, d, axis=1), 0.0)
        b3 = a3 * b_sh + b3
        a3 = a3 * a_sh
        d *= 2
    tiles = []
    h = h_prev
    for k in range(g):
        hk = a3[k] * h + b3[k]
        tiles.append(hk)
        h = hk[SUBLANES - 1:SUBLANES, :]
    return jnp.concatenate(tiles, axis=0), h


def _lru_seq_kernel(gate_ref, x_ref, cw_ref, cb_ref, wa_ref, wx_ref, ba_ref, bx_ref, lam_ref, g_ref,
                    h0_ref, tail0_ref, *rest, pad, reset_first, n_chunks, n_live, fuse_out):
    if fuse_out:
        wo_ref, o_ref, hfin_ref, tailfin_ref, xe_ref, hc_ref, y_ref, yn_ref = rest
    else:
        o_ref, hfin_ref, tailfin_ref, xe_ref, hc_ref, y_ref = rest
    step = pl.program_id(0)
    live = step < n_live
    c = lax.rem(jnp.minimum(step, n_live - 1), n_chunks)
    t, dl = x_ref.shape
    nh, hd = wa_ref.shape[0], wa_ref.shape[1]
    keep = (lambda new, old: jnp.where(live, new, old)) if fuse_out else (lambda new, old: new)

    @pl.when(jnp.logical_and(c == 0, live))
    def _():
        xe_ref[0:HALO, :] = tail0_ref[...]
        hc_ref[...] = h0_ref[...]

    if fuse_out:
        @pl.when(step == 0)
        def _():
            yn_ref[...] = jnp.zeros_like(yn_ref)

    xe_ref[HALO:HALO + t, :] = x_ref[...]
    grow = c * t + lax.broadcasted_iota(jnp.int32, (t, hd), 0)
    ssq = jnp.zeros((t, hd), F32)
    if fuse_out:
        n_pieces = max(1, min(nh, o_ref.shape[1] // MXU_COLS))
        while nh % n_pieces or o_ref.shape[1] % n_pieces:
            n_pieces -= 1
        heads_per_piece, piece = nh // n_pieces, o_ref.shape[1] // n_pieces
    for h in range(nh):
        sl = slice(h * hd, (h + 1) * hd)
        ext = [xe_ref[HALO - 3 + k:HALO - 3 + k + t, sl] for k in range(CONV_TAPS)]
        xh = _conv_taps(ext, 0, cw_ref, cb_ref, sl)
        hsp = (0.5 * LRU_C) * jax.nn.softplus(-lam_ref[:, sl])
        a, mult, i = _lru_gates(xh, wa_ref[h], wx_ref[h], ba_ref[:, sl], bx_ref[:, sl], hsp)
        if reset_first:
            mult = jnp.where(grow == pad, 1.0, mult)
        b = mult * i * xh
        if pad:
            a = jnp.where(grow >= pad, a, 1.0)
            b = jnp.where(grow >= pad, b, 0.0)
        h_prev = hc_ref[:, sl]
        hs, h_last = _scan_rows(a, b, h_prev)
        hc_ref[:, sl] = keep(h_last, h_prev)
        y = hs * gate_ref[:, sl]
        y_ref[:, sl] = y
        ssq = ssq + y * y
        if fuse_out and (h + 1) % heads_per_piece == 0:
            k = (h + 1) // heads_per_piece - 1
            psl = slice(k * piece, (k + 1) * piece)
            o_ref[:, psl] = jnp.dot(yn_ref[...], wo_ref[:, psl], preferred_element_type=F32)
    scale = lax.rsqrt(jnp.sum(ssq, axis=-1, keepdims=True) / dl + EPS)
    yn = (y_ref[...] * scale * g_ref[...]).astype(BF16)
    if fuse_out:
        yn_ref[...] = yn
    else:
        o_ref[...] = yn
    xe_ref[0:HALO, :] = keep(xe_ref[t:t + HALO, :], xe_ref[0:HALO, :])
    hfin_ref[0] = hc_ref[...]
    tailfin_ref[0] = xe_ref[HALO - 3:HALO, :]


def _lru_seq(proj, params, h0, tail0, *, n_seq, seq_len, t, row0, pad=0, reset_first=False, wo=None):
    cw, cb, wa, wx, ba, bx, lam, g = params
    dl = cw.shape[1]
    nh, hd = wa.shape[0], wa.shape[1]
    assert seq_len % t == 0 and row0 % t == 0 and t % SUBLANES == 0
    n_chunks = seq_len // t
    n_live = n_seq * n_chunks
    blk0 = row0 // t
    fuse_out = wo is not None
    chunk = (lambda s: jnp.minimum(s, n_live - 1)) if fuse_out else (lambda s: s)
    out_chunk = (lambda s: jnp.maximum(s - 1, 0)) if fuse_out else (lambda s: s)
    const2 = lambda s: (0, 0)
    const3 = lambda s: (0, 0, 0)
    per_seq = lambda s: (chunk(s) // n_chunks, 0, 0)
    dout, out_dtype = (wo.shape[1], F32) if fuse_out else (dl, BF16)
    pipelined = [((t, dl), F32), ((t, dl), F32), ((t, dout), out_dtype)]
    resident = [((t + HALO, dl), F32), ((t, dl), F32), ((4 * nh, hd, hd), BF16)]
    scratch = [pltpu.VMEM((t + HALO, dl), F32), pltpu.VMEM((1, dl), F32), pltpu.VMEM((t, dl), F32)]
    extra_specs, extra_args = [], []
    if fuse_out:
        extra_specs.append(pl.BlockSpec((dl, dout), const2, pipeline_mode=pl.Buffered(1)))
        extra_args.append(wo)
        resident += [((dl, dout), BF16), ((t, dl), BF16)]
        scratch.append(pltpu.VMEM((t, dl), BF16))
    kern = functools.partial(_lru_seq_kernel, pad=pad, reset_first=reset_first, n_chunks=n_chunks, n_live=n_live,
                             fuse_out=fuse_out)
    return pl.pallas_call(
        kern,
        out_shape=(jax.ShapeDtypeStruct((n_seq * seq_len, dout), out_dtype),
                   jax.ShapeDtypeStruct((n_seq, 1, dl), F32),
                   jax.ShapeDtypeStruct((n_seq, CONV_TAPS - 1, dl), F32)),
        grid=(n_live + 1 if fuse_out else n_live,),
        in_specs=[
            pl.BlockSpec((t, dl), lambda s: (blk0 + chunk(s), 0)),
            pl.BlockSpec((t, dl), lambda s: (blk0 + chunk(s), 1)),
            pl.BlockSpec((CONV_TAPS, dl), const2),
            pl.BlockSpec((1, dl), const2),
            pl.BlockSpec((nh, hd, hd), const3),
            pl.BlockSpec((nh, hd, hd), const3),
            pl.BlockSpec((1, dl), const2),
            pl.BlockSpec((1, dl), const2),
            pl.BlockSpec((1, dl), const2),
            pl.BlockSpec((1, dl), const2),
            pl.BlockSpec((1, dl), const2),
            pl.BlockSpec((HALO, dl), const2),
        ] + extra_specs,
        out_specs=(
            pl.BlockSpec((t, dout), lambda s: (out_chunk(s), 0)),
            pl.BlockSpec((1, 1, dl), per_seq),
            pl.BlockSpec((1, CONV_TAPS - 1, dl), per_seq),
        ),
        scratch_shapes=scratch,
        compiler_params=pltpu.CompilerParams(
            dimension_semantics=("arbitrary",),
            vmem_limit_bytes=_vmem_limit(pipelined, resident),
            flags=MIXER_SCHED_FLAGS if fuse_out else None),
        name="lru_seq",
    )(proj, proj, cw, cb, wa, wx, ba, bx, lam, g, h0, tail0, *extra_args)


def _lru_slab_kernel(gate_ref, x_ref, cw_ref, cb_ref, wa_ref, wx_ref, ba_ref, bx_ref, lam_ref, g_ref,
                     h0_ref, tail_ref, o_ref, hfin_ref, tailfin_ref, y_ref):
    ls, bs, dl = x_ref.shape
    nh, hd = wa_ref.shape[0], wa_ref.shape[1]
    ntail = CONV_TAPS - 1
    for h in range(nh):
        sl = slice(h * hd, (h + 1) * hd)
        ext = [tail_ref[:, k * dl + h * hd:k * dl + (h + 1) * hd] for k in range(ntail)]
        ext += [x_ref[s, :, sl] for s in range(ls)]
        hsp = (0.5 * LRU_C) * jax.nn.softplus(-lam_ref[:, sl])
        hcur = h0_ref[:, sl]
        for s in range(ls):
            xh = _conv_taps(ext, s, cw_ref, cb_ref, sl)
            a, mult, i = _lru_gates(xh, wa_ref[h], wx_ref[h], ba_ref[:, sl], bx_ref[:, sl], hsp)
            hcur = a * hcur + mult * i * xh
            y_ref[s, :, sl] = hcur * gate_ref[s, :, sl]
        hfin_ref[:, sl] = hcur
        for k in range(ntail):
            tailfin_ref[:, k * dl + h * hd:k * dl + (h + 1) * hd] = ext[ls + k]
    for s in range(ls):
        y = y_ref[s]
        scale = lax.rsqrt(jnp.mean(y * y, axis=-1, keepdims=True) + EPS)
        o_ref[s] = (y * scale * g_ref[...]).astype(o_ref.dtype)


def _lru_slab(proj3, params, h0, tail, *, ls):
    cw, cb, wa, wx, ba, bx, lam, g = params
    bs = proj3.shape[1]
    dl = cw.shape[1]
    nh, hd = wa.shape[0], wa.shape[1]
    ntail = CONV_TAPS - 1
    c2 = lambda i: (0, 0)
    c3 = lambda i: (0, 0, 0)
    pipelined = [((ls, bs, dl), F32)] * 2 + [((ls, bs, dl), BF16)] + [((bs, (2 * ntail + 2) * dl), F32)]
    return pl.pallas_call(
        _lru_slab_kernel,
        out_shape=(jax.ShapeDtypeStruct((ls, bs, dl), BF16),
                   jax.ShapeDtypeStruct((bs, dl), F32),
                   jax.ShapeDtypeStruct((bs, ntail * dl), F32)),
        grid=(1,),
        in_specs=[
            pl.BlockSpec((ls, bs, dl), lambda i: (0, 0, 0)),
            pl.BlockSpec((ls, bs, dl), lambda i: (0, 0, 1)),
            pl.BlockSpec((CONV_TAPS, dl), c2), pl.BlockSpec((1, dl), c2),
            pl.BlockSpec((nh, hd, hd), c3), pl.BlockSpec((nh, hd, hd), c3),
            pl.BlockSpec((1, dl), c2), pl.BlockSpec((1, dl), c2), pl.BlockSpec((1, dl), c2), pl.BlockSpec((1, dl), c2),
            pl.BlockSpec((bs, dl), c2), pl.BlockSpec((bs, ntail * dl), c2),
        ],
        out_specs=(pl.BlockSpec((ls, bs, dl), c3), pl.BlockSpec((bs, dl), c2), pl.BlockSpec((bs, ntail * dl), c2)),
        scratch_shapes=[pltpu.VMEM((ls, bs, dl), F32)],
        compiler_params=pltpu.CompilerParams(
            dimension_semantics=("arbitrary",),
            vmem_limit_bytes=_vmem_limit(pipelined, [((ls, bs, dl), F32)])),
        name="lru_slab",
    )(proj3, proj3, cw, cb, wa, wx, ba, bx, lam, g, h0, tail)


def _head_lane_mask(n_heads):
    return lax.broadcasted_iota(jnp.int32, (1, HEAD_PAD), 1) < n_heads


def _gated_group_norm(y, z_act, g):
    yg = y * z_act
    scale = lax.rsqrt(jnp.mean(yg * yg, axis=-1, keepdims=True) + EPS)
    return yg * scale * g


def _ssd_seq_kernel(z_ref, xbc_ref, dt_ref, cw_ref, cb_ref, dtb_ref, alog_ref, dskip_ref, g_ref, e3_ref,
                    s0_ref, tail0_ref, *rest, pad, n_heads, n_groups, n_chunks, n_live, fuse_out):
    if fuse_out:
        wo_ref, part_ref, res_ref = rest[:3]
        o_ref, sfin_ref, tailfin_ref, xe_ref, xc_ref, ex_ref, y_ref, s_ref, yn_ref, ynp_ref = rest[3:]
    else:
        o_ref, sfin_ref, tailfin_ref, xe_ref, xc_ref, ex_ref, y_ref, s_ref = rest
        yn_ref = o_ref
    step = pl.program_id(0)
    live = step < n_live
    c = lax.rem(jnp.minimum(step, n_live - 1), n_chunks)
    keep = (lambda new, old: jnp.where(live, new, old)) if fuse_out else (lambda new, old: new)
    q, ds = z_ref.shape
    dc = xbc_ref.shape[1]
    gn = (dc - ds) // 2
    n = gn // n_groups
    p = ds // n_heads
    r = n_heads // n_groups
    gw = ds // n_groups
    hpb = LANES // p

    @pl.when(jnp.logical_and(c == 0, live))
    def _():
        xe_ref[0:HALO, :] = tail0_ref[...]
        s_ref[...] = s0_ref[...]

    if fuse_out:
        @pl.when(step == 0)
        def _():
            ynp_ref[...] = jnp.zeros_like(ynp_ref)

    xe_ref[HALO:HALO + q, :] = xbc_ref[...]
    valid = (c * q + lax.broadcasted_iota(jnp.int32, (q, 1), 0)) >= pad

    cblk = 512 if dc % 512 == 0 else LANES
    for j in range(dc // cblk):
        sl = slice(j * cblk, (j + 1) * cblk)
        ext = [xe_ref[HALO - 3 + k:HALO - 3 + k + q, sl] for k in range(CONV_TAPS)]
        v = _silu(_conv_taps(ext, 0, cw_ref, cb_ref, sl))
        if pad and (j + 1) * cblk <= ds:
            v = jnp.where(valid, v, 0.0)
        xc_ref[:, sl] = v

    dtv = jnp.where(_head_lane_mask(n_heads), jax.nn.softplus(dt_ref[...] + dtb_ref[...]), 0.0)
    if pad:
        dtv = jnp.where(valid, dtv, 0.0)
    da = dtv * (-jnp.exp(alog_ref[...]))
    ri = lax.broadcasted_iota(jnp.int32, (q, q), 0)
    ci = lax.broadcasted_iota(jnp.int32, (q, q), 1)
    causal = ci <= ri
    tri = jnp.where(causal, 1.0, 0.0).astype(BF16)
    ac3 = jnp.dot(tri, _split3(da), preferred_element_type=F32)
    acum = ac3[:, 0:HEAD_PAD] + ac3[:, HEAD_PAD:2 * HEAD_PAD] + ac3[:, 2 * HEAD_PAD:3 * HEAD_PAD]
    alast = acum[q - 1:q, :]
    eac = jnp.exp(acum)
    wend = jnp.exp(alast - acum) * dtv
    cdec = jnp.broadcast_to(jnp.exp(alast), (SUBLANES, HEAD_PAD))
    ex_ref[...] = _expand_heads(jnp.concatenate([eac, wend, cdec], axis=0), e3_ref)
    acum_t = acum.T
    dt_t = dtv.T
    lane = lax.broadcasted_iota(jnp.int32, (q, LANES), 1)

    for g in range(n_groups):
        gsl = slice(g * gw, (g + 1) * gw)
        bg = xc_ref[:, ds + g * n:ds + (g + 1) * n].astype(BF16)
        cg = xc_ref[:, ds + gn + g * n:ds + gn + (g + 1) * n].astype(BF16)
        cbm = lax.dot_general(cg, bg, (((1,), (1,)), ((), ())), preferred_element_type=F32)
        yoff = jnp.dot(cg, s_ref[:, gsl].astype(BF16), preferred_element_type=F32)
        for k in range(gw // LANES):
            lsl = slice(g * gw + k * LANES, g * gw + (k + 1) * LANES)
            xs = xc_ref[:, lsl]
            ms = []
            xparts = []
            for u in range(hpb):
                h = g * r + k * hpb + u
                seg = acum[:, h:h + 1] - acum_t[h:h + 1, :]
                lm = jnp.where(causal, jnp.exp(seg), 0.0) * dt_t[h:h + 1, :]
                ms.append((cbm * lm).astype(BF16))
                inhead = (lane >= u * p) & (lane < (u + 1) * p)
                xparts.append(jnp.where(inhead, xs, 0.0).astype(BF16))
            ydiag = jnp.dot(jnp.concatenate(ms, axis=1), jnp.concatenate(xparts, axis=0),
                            preferred_element_type=F32)
            y = ydiag + yoff[:, k * LANES:(k + 1) * LANES] * ex_ref[0:q, lsl]
            y_ref[:, lsl] = y + dskip_ref[:, lsl] * xs
        yn_ref[:, gsl] = _gated_group_norm(y_ref[:, gsl], z_ref[:, gsl], g_ref[:, gsl]).astype(BF16)
        xw = (xc_ref[:, gsl] * ex_ref[q:2 * q, gsl]).astype(BF16)
        upd = lax.dot_general(bg, xw, (((0,), (0,)), ((), ())), preferred_element_type=F32)
        s_old = s_ref[:, gsl]
        s_ref[:, gsl] = keep(ex_ref[2 * q:2 * q + 1, gsl] * s_old + upd, s_old)
        if fuse_out:
            piece = o_ref.shape[1] // n_groups
            psl = slice(g * piece, (g + 1) * piece)
            mix = part_ref[:, psl] + jnp.dot(ynp_ref[...], wo_ref[:, psl], preferred_element_type=F32)
            o_ref[:, psl] = res_ref[:, psl] + mix

    if fuse_out:
        ynp_ref[...] = yn_ref[...]
    xe_ref[0:HALO, :] = keep(xe_ref[q:q + HALO, :], xe_ref[0:HALO, :])
    sfin_ref[0] = s_ref[...]
    tailfin_ref[0] = xe_ref[HALO - 3:HALO, :]


def _ssd_seq(proj, dt, params, s0, tail0, *, n_seq, seq_len, row0, n_heads, n_groups, dl, pad=0, fuse=None):
    cw, cb, dtb, alog, dskip, g, e3 = params
    dc = cw.shape[1]
    ds = dskip.shape[1]
    n = s0.shape[0]
    q = SSD_CHUNK
    assert seq_len % q == 0 and row0 % q == 0
    assert (2 * dl) % ds == 0 and (2 * dl + ds) % dc == 0 and LANES % (ds // n_heads) == 0
    n_chunks = seq_len // q
    n_live = n_seq * n_chunks
    blk0 = row0 // q
    fuse_out = fuse is not None
    chunk = (lambda s: jnp.minimum(s, n_live - 1)) if fuse_out else (lambda s: s)
    rows = lambda s: blk0 + chunk(s)
    const2 = lambda s: (0, 0)
    out_rows = (lambda s: (jnp.maximum(s - 1, 0), 0)) if fuse_out else (lambda s: (s, 0))
    per_seq = lambda s: (chunk(s) // n_chunks, 0, 0)
    dout, out_dtype = (fuse[0].shape[1], F32) if fuse_out else (ds, BF16)
    pipelined = [((q, ds), F32), ((q, dc), F32), ((q, HEAD_PAD), F32), ((q, dout), out_dtype)]
    resident = [((q + HALO, dc), F32), ((q, dc), F32), ((2 * q + SUBLANES, ds), F32), ((q, ds), F32),
                ((3 * n, ds), F32), ((6 * HEAD_PAD, ds), BF16)]
    scratch = [pltpu.VMEM((q + HALO, dc), F32), pltpu.VMEM((q, dc), F32),
               pltpu.VMEM((2 * q + SUBLANES, ds), F32), pltpu.VMEM((q, ds), F32), pltpu.VMEM((n, ds), F32)]
    extra_specs, extra_args = [], []
    if fuse_out:
        extra_specs = [pl.BlockSpec((ds, dout), const2, pipeline_mode=pl.Buffered(1)),
                       pl.BlockSpec((q, dout), out_rows), pl.BlockSpec((q, dout), out_rows)]
        extra_args = list(fuse)
        pipelined += [((q, dout), F32)] * 2
        resident += [((ds, dout), BF16), ((2 * q, ds), BF16)]
        scratch += [pltpu.VMEM((q, ds), BF16), pltpu.VMEM((q, ds), BF16)]
    kern = functools.partial(_ssd_seq_kernel, pad=pad, n_heads=n_heads, n_groups=n_groups, n_chunks=n_chunks,
                             n_live=n_live, fuse_out=fuse_out)
    return pl.pallas_call(
        kern,
        out_shape=(jax.ShapeDtypeStruct((n_seq * seq_len, dout), out_dtype),
                   jax.ShapeDtypeStruct((n_seq, n, ds), F32),
                   jax.ShapeDtypeStruct((n_seq, CONV_TAPS - 1, dc), F32)),
        grid=(n_live + 1 if fuse_out else n_live,),
        in_specs=[
            pl.BlockSpec((q, ds), lambda s: (rows(s), (2 * dl) // ds)),
            pl.BlockSpec((q, dc), lambda s: (rows(s), (2 * dl + ds) // dc)),
            pl.BlockSpec((q, HEAD_PAD), lambda s: (rows(s), 0)),
            pl.BlockSpec((CONV_TAPS, dc), const2),
            pl.BlockSpec((1, dc), const2),
            pl.BlockSpec((1, HEAD_PAD), const2),
            pl.BlockSpec((1, HEAD_PAD), const2),
            pl.BlockSpec((1, ds), const2),
            pl.BlockSpec((1, ds), const2),
            pl.BlockSpec((3 * HEAD_PAD, ds), const2),
            pl.BlockSpec((n, ds), const2),
            pl.BlockSpec((HALO, dc), const2),
        ] + extra_specs,
        out_specs=(
            pl.BlockSpec((q, dout), out_rows),
            pl.BlockSpec((1, n, ds), per_seq),
            pl.BlockSpec((1, CONV_TAPS - 1, dc), per_seq),
        ),
        scratch_shapes=scratch,
        compiler_params=pltpu.CompilerParams(
            dimension_semantics=("arbitrary",),
            vmem_limit_bytes=_vmem_limit(pipelined, resident),
            flags=MIXER_SCHED_FLAGS if fuse_out else None),
        name="ssd_seq",
    )(proj, proj, dt, cw, cb, dtb, alog, dskip, g, e3, s0, tail0, *extra_args)


def _ssd_slab_pre_kernel(xbc_ref, dt_ref, tail_ref, cw_ref, cb_ref, dtb_ref, alog_ref, dskip_ref, e3_ref,
                         ypart_ref, eace_ref, c_ref, b_ref, xw_ref, cdec_ref, tailfin_ref, xc_ref,
                         *, n_heads, n_groups):
    ls, bs, dc = xbc_ref.shape
    ds = dskip_ref.shape[1]
    gn = (dc - ds) // 2
    n = gn // n_groups
    r = n_heads // n_groups
    ntail = CONV_TAPS - 1

    cblk = 512 if dc % 512 == 0 else LANES
    for j in range(dc // cblk):
        sl = slice(j * cblk, (j + 1) * cblk)
        ext = [tail_ref[:, k * dc + j * cblk:k * dc + (j + 1) * cblk] for k in range(ntail)]
        ext += [xbc_ref[s, :, sl] for s in range(ls)]
        for s in range(ls):
            xc_ref[s, :, sl] = _silu(_conv_taps(ext, s, cw_ref, cb_ref, sl))
        for k in range(ntail):
            tailfin_ref[:, k * dc + j * cblk:k * dc + (j + 1) * cblk] = ext[ls + k]

    hmask = _head_lane_mask(n_heads)
    a_neg = -jnp.exp(alog_ref[...])
    dtv, acum = [], []
    run = jnp.zeros((bs, HEAD_PAD), F32)
    for s in range(ls):
        d = jnp.where(hmask, jax.nn.softplus(dt_ref[s] + dtb_ref[...]), 0.0)
        run = run + d * a_neg
        dtv.append(d)
        acum.append(run)
    alast = acum[ls - 1]
    cdec_ref[...] = jnp.exp(alast)
    head_group = lax.broadcasted_iota(jnp.int32, (1, HEAD_PAD), 1) // r

    for s in range(ls):
        eace_ref[s] = _expand_heads(jnp.exp(acum[s]), e3_ref)
        wend_e = _expand_heads(jnp.exp(alast - acum[s]) * dtv[s], e3_ref)
        xw_ref[:, s * ds:(s + 1) * ds] = xc_ref[s, :, 0:ds] * wend_e
        b_ref[:, s * gn:(s + 1) * gn] = xc_ref[s, :, ds:ds + gn]
        c_ref[:, s * gn:(s + 1) * gn] = xc_ref[s, :, ds + gn:ds + 2 * gn]
        ypart = dskip_ref[...] * xc_ref[s, :, 0:ds]
        for j in range(s + 1):
            cbh = jnp.zeros((bs, HEAD_PAD), F32)
            for g in range(n_groups):
                cs = xc_ref[s, :, ds + gn + g * n:ds + gn + (g + 1) * n]
                bj = xc_ref[j, :, ds + g * n:ds + (g + 1) * n]
                cbg = jnp.sum(cs * bj, axis=-1, keepdims=True)
                cbh = cbh + jnp.where(head_group == g, cbg, 0.0)
            coef = cbh * (jnp.exp(acum[s] - acum[j]) * dtv[j])
            ypart = ypart + _expand_heads(coef, e3_ref) * xc_ref[j, :, 0:ds]
        ypart_ref[s] = ypart
    for s in range(ls, SLAB_ROWS):
        xw_ref[:, s * ds:(s + 1) * ds] = jnp.zeros((bs, ds), F32)
        b_ref[:, s * gn:(s + 1) * gn] = jnp.zeros((bs, gn), F32)
        c_ref[:, s * gn:(s + 1) * gn] = jnp.zeros((bs, gn), F32)


def _ssd_slab_pre(proj3, dt3, tail, params, *, ls, n_heads, n_groups, dl):
    cw, cb, dtb, alog, dskip, _, e3 = params
    bs = proj3.shape[1]
    dc = cw.shape[1]
    ds = dskip.shape[1]
    gn = (dc - ds) // 2
    ntail = CONV_TAPS - 1
    assert ls <= SLAB_ROWS and (2 * dl + ds) % dc == 0
    c2 = lambda i: (0, 0)
    c3 = lambda i: (0, 0, 0)
    pipelined = [((ls, bs, dc), F32), ((ls, bs, HEAD_PAD), F32), ((bs, 2 * ntail * dc), F32),
                 ((2 * ls, bs, ds), F32), ((bs, SLAB_ROWS * (2 * gn + ds)), F32), ((3 * HEAD_PAD, ds), BF16)]
    kern = functools.partial(_ssd_slab_pre_kernel, n_heads=n_heads, n_groups=n_groups)
    return pl.pallas_call(
        kern,
        out_shape=(jax.ShapeDtypeStruct((ls, bs, ds), F32),
                   jax.ShapeDtypeStruct((ls, bs, ds), F32),
                   jax.ShapeDtypeStruct((bs, SLAB_ROWS * gn), F32),
                   jax.ShapeDtypeStruct((bs, SLAB_ROWS * gn), F32),
                   jax.ShapeDtypeStruct((bs, SLAB_ROWS * ds), F32),
                   jax.ShapeDtypeStruct((bs, HEAD_PAD), F32),
                   jax.ShapeDtypeStruct((bs, ntail * dc), F32)),
        grid=(1,),
        in_specs=[
            pl.BlockSpec((ls, bs, dc), lambda i: (0, 0, (2 * dl + ds) // dc)),
            pl.BlockSpec((ls, bs, HEAD_PAD), c3),
            pl.BlockSpec((bs, ntail * dc), c2),
            pl.BlockSpec((CONV_TAPS, dc), c2), pl.BlockSpec((1, dc), c2),
            pl.BlockSpec((1, HEAD_PAD), c2), pl.BlockSpec((1, HEAD_PAD), c2),
            pl.BlockSpec((1, ds), c2), pl.BlockSpec((3 * HEAD_PAD, ds), c2),
        ],
        out_specs=(pl.BlockSpec((ls, bs, ds), c3), pl.BlockSpec((ls, bs, ds), c3),
                   pl.BlockSpec((bs, SLAB_ROWS * gn), c2), pl.BlockSpec((bs, SLAB_ROWS * gn), c2),
                   pl.BlockSpec((bs, SLAB_ROWS * ds), c2), pl.BlockSpec((bs, HEAD_PAD), c2),
                   pl.BlockSpec((bs, ntail * dc), c2)),
        scratch_shapes=[pltpu.VMEM((ls, bs, dc), F32)],
        compiler_params=pltpu.CompilerParams(
            dimension_semantics=("arbitrary",),
            vmem_limit_bytes=_vmem_limit(pipelined, [((ls, bs, dc), F32)])),
        name="ssd_slab_pre",
    )(proj3, dt3, tail, cw, cb, dtb, alog, dskip, e3)


def _ssd_state_kernel(cdec_ref, s_ref, c_ref, b_ref, xw_ref, snew_ref, yoff_ref, *, n_heads, n_groups):
    i = pl.program_id(0)
    sb, hp, n = s_ref.shape
    p = hp // n_heads
    r = n_heads // n_groups
    gw = hp // n_groups
    for q in range(sb):
        for g in range(n_groups):
            gsl = slice(g * gw, (g + 1) * gw)
            sg = s_ref[q, gsl, :]
            cg = c_ref[q, :, g * n:(g + 1) * n].astype(BF16)
            yoff_ref[q, :, gsl] = lax.dot_general(cg, sg.astype(BF16), (((1,), (1,)), ((), ())),
                                                  preferred_element_type=F32)
            upd = lax.dot_general(xw_ref[q, :, gsl].astype(BF16), b_ref[q, :, g * n:(g + 1) * n].astype(BF16),
                                  (((0,), (0,)), ((), ())), preferred_element_type=F32)
            for u in range(r):
                h = g * r + u
                rows = slice(h * p, (h + 1) * p)
                snew_ref[q, rows, :] = cdec_ref[i * sb + q, h] * s_ref[q, rows, :] + upd[u * p:(u + 1) * p, :]


def _ssd_state(cdec, state, c_rows, b_rows, xw_rows, *, n_heads, n_groups):
    bs, hp, n = state.shape
    gn = c_rows.shape[2]
    sb = STATE_SEQS_PER_STEP if bs % STATE_SEQS_PER_STEP == 0 else 1
    per_seq = lambda i: (i, 0, 0)
    pipelined = [((sb, hp, n), F32)] * 2 + [((sb, SLAB_ROWS, gn), F32)] * 2 + [((sb, SLAB_ROWS, hp), F32)] * 2
    kern = functools.partial(_ssd_state_kernel, n_heads=n_heads, n_groups=n_groups)
    return pl.pallas_call(
        kern,
        out_shape=(jax.ShapeDtypeStruct((bs, hp, n), F32), jax.ShapeDtypeStruct((bs, SLAB_ROWS, hp), F32)),
        grid=(bs // sb,),
        in_specs=[
            pl.BlockSpec(memory_space=pltpu.SMEM),
            pl.BlockSpec((sb, hp, n), per_seq),
            pl.BlockSpec((sb, SLAB_ROWS, gn), per_seq),
            pl.BlockSpec((sb, SLAB_ROWS, gn), per_seq),
            pl.BlockSpec((sb, SLAB_ROWS, hp), per_seq),
        ],
        out_specs=(pl.BlockSpec((sb, hp, n), per_seq), pl.BlockSpec((sb, SLAB_ROWS, hp), per_seq)),
        compiler_params=pltpu.CompilerParams(
            dimension_semantics=("parallel",),
            vmem_limit_bytes=_vmem_limit(pipelined, [])),
        name="ssd_state",
    )(cdec, state, c_rows, b_rows, xw_rows)


def _ssd_slab_post_kernel(ypart_ref, eace_ref, yoff_ref, z_ref, g_ref, o_ref, *, n_groups):
    ls, bs, ds = ypart_ref.shape
    gw = ds // n_groups
    for s in range(ls):
        for g in range(n_groups):
            gsl = slice(g * gw, (g + 1) * gw)
            y = ypart_ref[s, :, gsl] + eace_ref[s, :, gsl] * yoff_ref[:, s * ds + g * gw:s * ds + (g + 1) * gw]
            o_ref[s, :, gsl] = _gated_group_norm(y, z_ref[s, :, gsl], g_ref[:, gsl]).astype(o_ref.dtype)


def _ssd_slab_post(ypart, eace, yoff, proj3, g, *, n_groups, dl):
    ls, bs, ds = ypart.shape
    assert (2 * dl) % ds == 0
    c2 = lambda i: (0, 0)
    c3 = lambda i: (0, 0, 0)
    pipelined = [((ls, bs, ds), F32)] * 3 + [((bs, SLAB_ROWS * ds), F32), ((ls, bs, ds), BF16)]
    kern = functools.partial(_ssd_slab_post_kernel, n_groups=n_groups)
    return pl.pallas_call(
        kern,
        out_shape=jax.ShapeDtypeStruct((ls, bs, ds), BF16),
        grid=(1,),
        in_specs=[
            pl.BlockSpec((ls, bs, ds), c3), pl.BlockSpec((ls, bs, ds), c3),
            pl.BlockSpec((bs, SLAB_ROWS * ds), c2),
            pl.BlockSpec((ls, bs, ds), lambda i: (0, 0, (2 * dl) // ds)),
            pl.BlockSpec((1, ds), c2),
        ],
        out_specs=pl.BlockSpec((ls, bs, ds), c3),
        compiler_params=pltpu.CompilerParams(
            dimension_semantics=("arbitrary",),
            vmem_limit_bytes=_vmem_limit(pipelined, [])),
        name="ssd_slab_post",
    )(ypart, eace, yoff, proj3, g)


def _head_expansion(n_heads, head_dim):
    rows = lax.broadcasted_iota(jnp.int32, (HEAD_PAD, n_heads * head_dim), 0)
    cols = lax.broadcasted_iota(jnp.int32, (HEAD_PAD, n_heads * head_dim), 1)
    e = (cols // head_dim == rows).astype(BF16)
    return jnp.concatenate([e, e, e], axis=0)


def _pad_lanes(v, width):
    return jnp.pad(v, ((0, 0), (0, width - v.shape[1])))


def _tail_block(tail):
    return jnp.pad(tail, ((HALO - tail.shape[0], 0), (0, 0)))


def _mlp(x1, g_mlp, w_up, w_down, g_final, *, layer=None):
    m = x1.shape[0]
    tiles = (1024, 512, 256, 128)
    tm = m if layer is not None else _pick_tile(m, tiles)
    up = _mlp_up(x1, g_mlp, w_up, tm=tm, tn=_pick_tile(w_up.shape[-1], tiles), layer=layer)
    hid = up[0]
    down = _mlp_down_final(hid, w_down, x1, g_final, tm=tm, tk=_pick_tile(w_down.shape[-2], tiles), layer=layer)
    if layer is None:
        return down[0]
    return down[0], up[1], down[1]


def kernel(x_prompt, x_sample, state_lru_h, state_lru_conv, state_ssd, state_ssd_conv, meta_tokens, g_mix, w_in, conv_lru_w, conv_lru_b, lru_wa, lru_ba, lru_wx, lru_bx, lru_lambda, g_lru_out, conv_ssd_w, conv_ssd_b, dt_bias, a_log, d_skip, g_ssd_out, w_out, g_mlp, w_up, w_down, g_final):
    depth = w_in.shape[0]
    assert depth == 1, "single-layer step"
    l = 0
    bp, lp, d = x_prompt.shape
    bs, ls, _ = x_sample.shape
    n_meta = meta_tokens.shape[0]
    dl = state_lru_h.shape[-1]
    n_heads, p, n = state_ssd.shape[-3:]
    ds = n_heads * p
    dc = state_ssd_conv.shape[-1]
    gn = (dc - ds) // 2
    n_groups = gn // n
    nw = 2 * dl + ds + dc
    ntail = CONV_TAPS - 1
    q = SSD_CHUNK
    meta_pad = (-n_meta) % q
    assert n_heads <= HEAD_PAD and (bs * ls) % q == 0 and lp % q == 0 and q % bs == 0

    row = lambda v: v.reshape(1, -1).astype(F32)
    w_in_b = w_in[l].astype(BF16)
    w_dt = _pad_lanes(w_in_b[:, nw:], HEAD_PAD)
    lru_params = (conv_lru_w[l], row(conv_lru_b[l]), (0.5 * lru_wa[l]).astype(BF16), (0.5 * lru_wx[l]).astype(BF16),
                  row(0.5 * lru_ba[l]), row(0.5 * lru_bx[l]), row(lru_lambda[l]), row(g_lru_out[l]))
    ssd_params = (conv_ssd_w[l], row(conv_ssd_b[l]), _pad_lanes(row(dt_bias[l]), HEAD_PAD),
                  _pad_lanes(row(a_log[l]), HEAD_PAD), row(jnp.repeat(d_skip[l], p)), row(g_ssd_out[l]),
                  _head_expansion(n_heads, p))
    g_mix_r, g_mlp_r, g_final_r = row(g_mix[l]), row(g_mlp[l]), row(g_final)

    xs_tm = x_sample.transpose(1, 0, 2).reshape(ls * bs, d)
    x_side = jnp.concatenate([xs_tm, jnp.zeros((meta_pad, d), F32), meta_tokens.astype(F32)], axis=0)
    xp_rows = x_prompt.reshape(bp * lp, d)

    tiles = (1024, 512, 256, 128)
    tn_in = next(t for t in tiles if nw % t == 0 and dl % t == 0 and ds % t == 0)
    act_cols = dict(gelu_cols=(0, dl), silu_cols=(2 * dl, 2 * dl + ds))
    proj_side, dt_side = _in_proj(x_side, g_mix_r, w_in_b, w_dt, nw=nw, tm=x_side.shape[0], tn=tn_in, **act_cols)

    proj_s3 = proj_side.reshape(-1, bs, nw)
    dt_s3 = dt_side.reshape(-1, bs, HEAD_PAD)
    lru_s, s_h, s_ltail = _lru_slab(proj_s3, lru_params, state_lru_h[l],
                                    state_lru_conv[l].reshape(bs, ntail * dl), ls=ls)
    ypart, eace, c_rows, b_rows, xw_rows, cdec, s_stail = _ssd_slab_pre(
        proj_s3, dt_s3, state_ssd_conv[l].reshape(bs, ntail * dc), ssd_params,
        ls=ls, n_heads=n_heads, n_groups=n_groups, dl=dl)
    s_new, yoff = _ssd_state(cdec, state_ssd[l].reshape(bs, ds, n),
                             c_rows.reshape(bs, SLAB_ROWS, gn), b_rows.reshape(bs, SLAB_ROWS, gn),
                             xw_rows.reshape(bs, SLAB_ROWS, ds), n_heads=n_heads, n_groups=n_groups)
    ssd_s = _ssd_slab_post(ypart, eace, yoff.reshape(bs, SLAB_ROWS * ds), proj_s3, ssd_params[5],
                           n_groups=n_groups, dl=dl)
    x1_s, w_out_lru, w_out_ssd = _out_proj(lru_s.reshape(ls * bs, dl), ssd_s.reshape(ls * bs, ds), w_out, l, xs_tm,
                                           tn=_pick_tile(d, (512, 256, 128)))
    y_s, w_up_b, w_down_b = _mlp(x1_s, g_mlp_r, w_up, w_down, g_final_r, layer=l)

    meta_row0 = ls * bs
    _, m_h, m_ltail = _lru_seq(proj_side, lru_params, jnp.zeros((1, dl), F32), jnp.zeros((HALO, dl), F32),
                               n_seq=1, seq_len=q, t=q, row0=meta_row0, pad=meta_pad, reset_first=True)
    _, m_s, m_stail = _ssd_seq(proj_side, dt_side, ssd_params, jnp.zeros((n, ds), F32), jnp.zeros((HALO, dc), F32),
                               n_seq=1, seq_len=q, row0=meta_row0, n_heads=n_heads, n_groups=n_groups, dl=dl,
                               pad=meta_pad)

    proj_p, dt_p = _in_proj(xp_rows, g_mix_r, w_in_b, w_dt, nw=nw, tm=_pick_tile(bp * lp, tiles), tn=tn_in, **act_cols)
    part_p, p_h, p_ltail = _lru_seq(proj_p, lru_params, m_h[0], _tail_block(m_ltail[0]),
                                    n_seq=bp, seq_len=lp, t=_pick_tile(lp, (256, 128)), row0=0, wo=w_out_lru)
    x1_p, p_s, p_stail = _ssd_seq(proj_p, dt_p, ssd_params, m_s[0], _tail_block(m_stail[0]),
                                  n_seq=bp, seq_len=lp, row0=0, n_heads=n_heads, n_groups=n_groups, dl=dl,
                                  fuse=(w_out_ssd, part_p, xp_rows))
    y_p = _mlp(x1_p, g_mlp_r, w_up_b, w_down_b, g_final_r)

    y_prompt = y_p.reshape(bp, lp, d)
    y_sample = y_s.reshape(ls, bs, d).transpose(1, 0, 2)
    p_lru_h = p_h.reshape(1, bp, dl)
    p_lru_conv = p_ltail.reshape(1, bp, ntail, dl)
    p_ssd = p_s.transpose(0, 2, 1).reshape(1, bp, n_heads, p, n)
    p_ssd_conv = p_stail.reshape(1, bp, ntail, dc)
    s_lru_h = s_h.reshape(1, bs, dl)
    s_lru_conv = s_ltail.reshape(1, bs, ntail, dl)
    s_ssd = s_new.reshape(1, bs, n_heads, p, n)
    s_ssd_conv = s_stail.reshape(1, bs, ntail, dc)
    return (y_prompt, y_sample, p_lru_h, p_lru_conv, p_ssd, p_ssd_conv, s_lru_h, s_lru_conv, s_ssd, s_ssd_conv)
```

```python
import functools

import jax
import jax.numpy as jnp
from jax import lax
from jax.experimental import pallas as pl
from jax.experimental.pallas import tpu as pltpu

F32 = jnp.float32
BF16 = jnp.bfloat16

EPS = 1e-6
LRU_C = 8.0
CONV_TAPS = 4

LANES = 128
SUBLANES = 8
MXU_COLS = 256
VMEM_BYTES_V7X = 64 * 1024 * 1024
VMEM_TEMP_BYTES = 10 * 1024 * 1024
VMEM_CEILING_BYTES = VMEM_BYTES_V7X - 6 * 1024 * 1024

HALO = SUBLANES
SSD_CHUNK = 128
HEAD_PAD = LANES
SLAB_ROWS = SUBLANES
STATE_SEQS_PER_STEP = 4
MIXER_SCHED_FLAGS = None


def _nbytes(shape, dtype):
    n = 1
    for s in shape:
        n *= s
    return n * jnp.dtype(dtype).itemsize


def _vmem_limit(pipelined, resident):
    est = 2 * sum(_nbytes(s, d) for s, d in pipelined) + sum(_nbytes(s, d) for s, d in resident)
    return int(min(est + VMEM_TEMP_BYTES, VMEM_CEILING_BYTES))


def _pick_tile(m, prefs):
    for t in prefs:
        if m % t == 0:
            return t
    return m


def _silu(x):
    h = 0.5 * x
    return h * jnp.tanh(h) + h


def _split3(x):
    hi = x.astype(BF16)
    r1 = x - hi.astype(F32)
    mid = r1.astype(BF16)
    lo = (r1 - mid.astype(F32)).astype(BF16)
    return jnp.concatenate([hi, mid, lo], axis=1)


def _expand_heads(x, e3_ref):
    return jnp.dot(_split3(x), e3_ref[...], preferred_element_type=F32)


def _rmsnorm_rows(x, g):
    ms = jnp.mean(x * x, axis=-1, keepdims=True)
    return x * lax.rsqrt(ms + EPS) * g


def _conv_taps(ext, s, cw_ref, cb_ref, sl):
    v = cb_ref[:, sl] + ext[s] * cw_ref[0:1, sl]
    for k in range(1, CONV_TAPS):
        v = v + ext[s + k] * cw_ref[k:k + 1, sl]
    return v


def _weight_tile(w_ref, wb_ref):
    if wb_ref is None:
        return w_ref[...]
    w = w_ref[...].astype(BF16)
    wb_ref[...] = w
    return w


def _weight_specs(w, layer, blk, idx):
    if layer is None:
        return pl.BlockSpec(blk, idx), None, F32
    in_spec = pl.BlockSpec((None,) + blk, lambda i, j: (layer,) + idx(i, j))
    return in_spec, pl.BlockSpec(blk, idx), w.dtype


_NT = (((1,), (1,)), ((), ()))


def _in_proj_kernel(x_ref, g_ref, w_ref, wdt_ref, o_ref, dt_ref, *rest, gelu_tiles, silu_tiles, emit_w):
    if emit_w:
        wb_ref, wdtb_ref, xn_ref = rest
    else:
        wb_ref, wdtb_ref, xn_ref = None, None, rest[0]
    j = pl.program_id(1)

    @pl.when(j == 0)
    def _():
        xn = _rmsnorm_rows(x_ref[...], g_ref[...]).astype(BF16)
        xn_ref[...] = xn
        wdt = _weight_tile(wdt_ref, wdtb_ref)
        wdt = jnp.concatenate([wdt, jnp.zeros((HEAD_PAD - wdt.shape[0], wdt.shape[1]), BF16)], axis=0)
        dt_ref[...] = lax.dot_general(xn, wdt, _NT, preferred_element_type=F32)

    def tile():
        return lax.dot_general(xn_ref[...], _weight_tile(w_ref, wb_ref), _NT, preferred_element_type=F32)

    in_range = lambda r: jnp.logical_and(j >= r[0], j < r[1])
    is_gelu, is_silu = in_range(gelu_tiles), in_range(silu_tiles)

    @pl.when(is_gelu)
    def _():
        o_ref[...] = jax.nn.gelu(tile())

    @pl.when(is_silu)
    def _():
        o_ref[...] = _silu(tile())

    @pl.when(jnp.logical_not(jnp.logical_or(is_gelu, is_silu)))
    def _():
        o_ref[...] = tile()


def _in_proj(x, g, w_t, wdt_t=None, *, nw, n_dt, tm, tn, gelu_cols, silu_cols, layer=None):
    m, k = x.shape
    emit_w = layer is not None
    assert m % tm == 0 and nw % tn == 0 and all(c % tn == 0 for c in gelu_cols + silu_cols)
    assert n_dt % SUBLANES == 0 and nw % n_dt == 0 and (not emit_w or m == tm)
    out_shape = [jax.ShapeDtypeStruct((m, nw), F32), jax.ShapeDtypeStruct((m, HEAD_PAD), F32)]
    out_specs = [pl.BlockSpec((tm, tn), lambda i, j: (i, j)), pl.BlockSpec((tm, HEAD_PAD), lambda i, j: (i, 0))]
    pipelined = [((tm, k), F32), ((tn, k), w_t.dtype), ((n_dt, k), w_t.dtype), ((tm, tn), F32), ((tm, HEAD_PAD), F32)]
    if emit_w:
        w_spec = pl.BlockSpec((None, tn, k), lambda i, j: (layer, j, 0))
        wdt_spec = pl.BlockSpec((None, n_dt, k), lambda i, j: (layer, nw // n_dt, 0))
        wdt_t = w_t
        out_shape += [jax.ShapeDtypeStruct((nw, k), BF16), jax.ShapeDtypeStruct((n_dt, k), BF16)]
        out_specs += [pl.BlockSpec((tn, k), lambda i, j: (j, 0)), pl.BlockSpec((n_dt, k), lambda i, j: (0, 0))]
        pipelined += [((tn, k), BF16), ((n_dt, k), BF16)]
    else:
        w_spec = pl.BlockSpec((tn, k), lambda i, j: (j, 0))
        wdt_spec = pl.BlockSpec((n_dt, k), lambda i, j: (0, 0))
    kern = functools.partial(_in_proj_kernel, gelu_tiles=tuple(c // tn for c in gelu_cols),
                             silu_tiles=tuple(c // tn for c in silu_cols), emit_w=emit_w)
    return pl.pallas_call(
        kern,
        out_shape=tuple(out_shape),
        grid=(m // tm, nw // tn),
        in_specs=[
            pl.BlockSpec((tm, k), lambda i, j: (i, 0)),
            pl.BlockSpec((1, k), lambda i, j: (0, 0)),
            w_spec,
            wdt_spec,
        ],
        out_specs=tuple(out_specs),
        scratch_shapes=[pltpu.VMEM((tm, k), BF16)],
        compiler_params=pltpu.CompilerParams(
            dimension_semantics=("parallel", "arbitrary"),
            vmem_limit_bytes=_vmem_limit(pipelined, [((tm, k), BF16), ((tn, k), BF16)])),
        name="in_proj",
    )(x, g, w_t, wdt_t)


def _out_proj_kernel(a1_ref, a2_ref, w1_ref, w2_ref, res_ref, o_ref, wb1_ref, wb2_ref):
    acc = jnp.dot(a1_ref[...], _weight_tile(w1_ref, wb1_ref), preferred_element_type=F32)
    acc = acc + jnp.dot(a2_ref[...], _weight_tile(w2_ref, wb2_ref), preferred_element_type=F32)
    o_ref[...] = res_ref[...] + acc


def _out_proj(a1, a2, w, layer, res, *, tn):
    m, k1 = a1.shape
    k2 = a2.shape[1]
    n = w.shape[2]
    assert n % tn == 0 and k1 == k2 and w.shape[1] == k1 + k2
    pipelined = [((m, k1), BF16), ((m, k2), BF16), ((k1, tn), w.dtype), ((k2, tn), w.dtype), ((m, tn), F32),
                 ((m, tn), F32), ((k1, tn), BF16), ((k2, tn), BF16)]
    col = lambda j: (0, j)
    return pl.pallas_call(
        _out_proj_kernel,
        out_shape=(jax.ShapeDtypeStruct((m, n), F32), jax.ShapeDtypeStruct((k1, n), BF16),
                   jax.ShapeDtypeStruct((k2, n), BF16)),
        grid=(n // tn,),
        in_specs=[
            pl.BlockSpec((m, k1), lambda j: (0, 0)),
            pl.BlockSpec((m, k2), lambda j: (0, 0)),
            pl.BlockSpec((None, k1, tn), lambda j: (layer, 0, j)),
            pl.BlockSpec((None, k2, tn), lambda j: (layer, 1, j)),
            pl.BlockSpec((m, tn), col),
        ],
        out_specs=(pl.BlockSpec((m, tn), col), pl.BlockSpec((k1, tn), col), pl.BlockSpec((k2, tn), col)),
        compiler_params=pltpu.CompilerParams(
            dimension_semantics=("parallel",),
            vmem_limit_bytes=_vmem_limit(pipelined, [])),
        name="out_proj",
    )(a1, a2, w, w, res)


def _mlp_up_kernel(x_ref, g_ref, w_ref, o_ref, *rest, emit_w):
    wb_ref, xn_ref = rest if emit_w else (None, rest[0])

    @pl.when(pl.program_id(1) == 0)
    def _():
        xn_ref[...] = _rmsnorm_rows(x_ref[...], g_ref[...]).astype(BF16)

    acc = jnp.dot(xn_ref[...], _weight_tile(w_ref, wb_ref), preferred_element_type=F32)
    o_ref[...] = jnp.square(jnp.maximum(acc, 0.0)).astype(o_ref.dtype)


def _mlp_up(x, g, w, *, tm, tn, layer=None):
    m, k = x.shape
    n = w.shape[-1]
    emit_w = layer is not None
    assert m % tm == 0 and n % tn == 0 and (not emit_w or m == tm)
    w_spec, wb_spec, w_dtype = _weight_specs(w, layer, (k, tn), lambda i, j: (0, j))
    out_shape = [jax.ShapeDtypeStruct((m, n), BF16)]
    out_specs = [pl.BlockSpec((tm, tn), lambda i, j: (i, j))]
    pipelined = [((tm, k), F32), ((k, tn), w_dtype if emit_w else BF16), ((tm, tn), BF16)]
    if emit_w:
        out_shape.append(jax.ShapeDtypeStruct((k, n), BF16))
        out_specs.append(wb_spec)
        pipelined.append(((k, tn), BF16))
    return pl.pallas_call(
        functools.partial(_mlp_up_kernel, emit_w=emit_w),
        out_shape=tuple(out_shape),
        grid=(m // tm, n // tn),
        in_specs=[
            pl.BlockSpec((tm, k), lambda i, j: (i, 0)),
            pl.BlockSpec((1, k), lambda i, j: (0, 0)),
            w_spec,
        ],
        out_specs=tuple(out_specs),
        scratch_shapes=[pltpu.VMEM((tm, k), BF16)],
        compiler_params=pltpu.CompilerParams(
            dimension_semantics=("parallel", "arbitrary"),
            vmem_limit_bytes=_vmem_limit(pipelined, [((tm, k), BF16), ((k, tn), BF16)])),
        name="mlp_up",
    )(x, g, w)


def _mlp_down_kernel(h_ref, w_ref, res_ref, g_ref, o_ref, *rest, emit_w):
    wb_ref = rest[0] if emit_w else None
    kk = pl.program_id(1)

    @pl.when(kk == 0)
    def _():
        o_ref[...] = res_ref[...]

    o_ref[...] += jnp.dot(h_ref[...], _weight_tile(w_ref, wb_ref), preferred_element_type=F32)

    @pl.when(kk == pl.num_programs(1) - 1)
    def _():
        o_ref[...] = _rmsnorm_rows(o_ref[...], g_ref[...])


def _mlp_down_final(h, w, res, g, *, tm, tk, layer=None):
    m, k = h.shape
    n = w.shape[-1]
    emit_w = layer is not None
    assert m % tm == 0 and k % tk == 0 and (not emit_w or m == tm)
    w_spec, wb_spec, w_dtype = _weight_specs(w, layer, (tk, n), lambda i, j: (j, 0))
    out_shape = [jax.ShapeDtypeStruct((m, n), F32)]
    out_specs = [pl.BlockSpec((tm, n), lambda i, j: (i, 0))]
    pipelined = [((tm, tk), BF16), ((tk, n), w_dtype if emit_w else BF16), ((tm, n), F32), ((tm, n), F32)]
    if emit_w:
        out_shape.append(jax.ShapeDtypeStruct((k, n), BF16))
        out_specs.append(wb_spec)
        pipelined.append(((tk, n), BF16))
    return pl.pallas_call(
        functools.partial(_mlp_down_kernel, emit_w=emit_w),
        out_shape=tuple(out_shape),
        grid=(m // tm, k // tk),
        in_specs=[
            pl.BlockSpec((tm, tk), lambda i, j: (i, j)),
            w_spec,
            pl.BlockSpec((tm, n), lambda i, j: (i, 0)),
            pl.BlockSpec((1, n), lambda i, j: (0, 0)),
        ],
        out_specs=tuple(out_specs),
        compiler_params=pltpu.CompilerParams(
            dimension_semantics=("parallel", "arbitrary"),
            vmem_limit_bytes=_vmem_limit(pipelined, [((tk, n), BF16)] if emit_w else [])),
        name="mlp_down",
    )(h, w, res, g)


def _lru_gates(xh, wa_half, wx_half, ba_half, bx_half, hsp):
    xb = xh.astype(BF16)
    tr = jnp.tanh(jnp.dot(xb, wa_half, preferred_element_type=F32) + ba_half)
    ti = jnp.tanh(jnp.dot(xb, wx_half, preferred_element_type=F32) + bx_half)
    nla = tr * hsp + hsp
    a = jnp.exp(-nla)
    q = jnp.tanh(nla) * (1.0 + a * a)
    mult = jnp.where(q > 0.0, q * lax.rsqrt(q), 0.0)
    return a, mult, 0.5 * ti + 0.5


def _scan_rows(a, b, h_prev):
    t, hd = a.shape
    g = t // SUBLANES
    a3 = a.reshape(g, SUBLANES, hd)
    b3 = b.reshape(g, SUBLANES, hd)
    sub = lax.broadcasted_iota(jnp.int32, (g, SUBLANES, hd), 1)
    d = 1
    while d < SUBLANES:
        keep = sub >= d
        a_sh = jnp.where(keep, pltpu.roll(a3, d, axis=1), 1.0)
        b_sh = jnp.where(keep, pltpu.roll(b3, d, axis=1), 0.0)
        b3 = a3 * b_sh + b3
        a3 = a3 * a_sh
        d *= 2
    tiles = []
    h = h_prev
    for k in range(g):
        hk = a3[k] * h + b3[k]
        tiles.append(hk)
        h = hk[SUBLANES - 1:SUBLANES, :]
    return jnp.concatenate(tiles, axis=0), h


def _lru_seq_kernel(gate_ref, x_ref, cw_ref, cb_ref, wa_ref, wx_ref, ba_ref, bx_ref, lam_ref, g_ref,
                    h0_ref, tail0_ref, *rest, pad, reset_first, n_chunks, n_live, fuse_out):
    if fuse_out:
        wo_ref, o_ref, hfin_ref, tailfin_ref, xe_ref, hc_ref, y_ref, yn_ref = rest
    else:
        o_ref, hfin_ref, tailfin_ref, xe_ref, hc_ref, y_ref = rest
    step = pl.program_id(0)
    live = step < n_live
    c = lax.rem(jnp.minimum(step, n_live - 1), n_chunks)
    t, dl = x_ref.shape
    nh, hd = wa_ref.shape[0], wa_ref.shape[1]
    keep = (lambda new, old: jnp.where(live, new, old)) if fuse_out else (lambda new, old: new)

    @pl.when(jnp.logical_and(c == 0, live))
    def _():
        xe_ref[0:HALO, :] = tail0_ref[...]
        hc_ref[...] = h0_ref[...]

    if fuse_out:
        @pl.when(step == 0)
        def _():
            yn_ref[...] = jnp.zeros_like(yn_ref)

    xe_ref[HALO:HALO + t, :] = x_ref[...]
    grow = c * t + lax.broadcasted_iota(jnp.int32, (t, hd), 0)
    ssq = jnp.zeros((t, hd), F32)
    if fuse_out:
        n_pieces = max(1, min(nh, o_ref.shape[1] // MXU_COLS))
        while nh % n_pieces or o_ref.shape[1] % n_pieces:
            n_pieces -= 1
        heads_per_piece, piece = nh // n_pieces, o_ref.shape[1] // n_pieces
    for h in range(nh):
        sl = slice(h * hd, (h + 1) * hd)
        ext = [xe_ref[HALO - 3 + k:HALO - 3 + k + t, sl] for k in range(CONV_TAPS)]
        xh = _conv_taps(ext, 0, cw_ref, cb_ref, sl)
        hsp = (0.5 * LRU_C) * jax.nn.softplus(-lam_ref[:, sl])
        a, mult, i = _lru_gates(xh, wa_ref[h], wx_ref[h], ba_ref[:, sl], bx_ref[:, sl], hsp)
        if reset_first:
            mult = jnp.where(grow == pad, 1.0, mult)
        b = mult * i * xh
        if pad:
            a = jnp.where(grow >= pad, a, 1.0)
            b = jnp.where(grow >= pad, b, 0.0)
        h_prev = hc_ref[:, sl]
        hs, h_last = _scan_rows(a, b, h_prev)
        hc_ref[:, sl] = keep(h_last, h_prev)
        y = hs * gate_ref[:, sl]
        y_ref[:, sl] = y
        ssq = ssq + y * y
        if fuse_out and (h + 1) % heads_per_piece == 0:
            k = (h + 1) // heads_per_piece - 1
            psl = slice(k * piece, (k + 1) * piece)
            o_ref[:, psl] = jnp.dot(yn_ref[...], wo_ref[:, psl], preferred_element_type=F32)
    scale = lax.rsqrt(jnp.sum(ssq, axis=-1, keepdims=True) / dl + EPS)
    yn = (y_ref[...] * scale * g_ref[...]).astype(BF16)
    if fuse_out:
        yn_ref[...] = yn
    else:
        o_ref[...] = yn
    xe_ref[0:HALO, :] = keep(xe_ref[t:t + HALO, :], xe_ref[0:HALO, :])
    hfin_ref[0] = hc_ref[...]
    tailfin_ref[0] = xe_ref[HALO - 3:HALO, :]


def _lru_seq(proj, params, h0, tail0, *, n_seq, seq_len, t, row0, pad=0, reset_first=False, wo=None):
    cw, cb, wa, wx, ba, bx, lam, g = params
    dl = cw.shape[1]
    nh, hd = wa.shape[0], wa.shape[1]
    assert seq_len % t == 0 and row0 % t == 0 and t % SUBLANES == 0
    n_chunks = seq_len // t
    n_live = n_seq * n_chunks
    blk0 = row0 // t
    fuse_out = wo is not None
    chunk = (lambda s: jnp.minimum(s, n_live - 1)) if fuse_out else (lambda s: s)
    out_chunk = (lambda s: jnp.maximum(s - 1, 0)) if fuse_out else (lambda s: s)
    const2 = lambda s: (0, 0)
    const3 = lambda s: (0, 0, 0)
    per_seq = lambda s: (chunk(s) // n_chunks, 0, 0)
    dout, out_dtype = (wo.shape[1], F32) if fuse_out else (dl, BF16)
    pipelined = [((t, dl), F32), ((t, dl), F32), ((t, dout), out_dtype)]
    resident = [((t + HALO, dl), F32), ((t, dl), F32), ((4 * nh, hd, hd), BF16)]
    scratch = [pltpu.VMEM((t + HALO, dl), F32), pltpu.VMEM((1, dl), F32), pltpu.VMEM((t, dl), F32)]
    extra_specs, extra_args = [], []
    if fuse_out:
        extra_specs.append(pl.BlockSpec((dl, dout), const2, pipeline_mode=pl.Buffered(1)))
        extra_args.append(wo)
        resident += [((dl, dout), BF16), ((t, dl), BF16)]
        scratch.append(pltpu.VMEM((t, dl), BF16))
    kern = functools.partial(_lru_seq_kernel, pad=pad, reset_first=reset_first, n_chunks=n_chunks, n_live=n_live,
                             fuse_out=fuse_out)
    return pl.pallas_call(
        kern,
        out_shape=(jax.ShapeDtypeStruct((n_seq * seq_len, dout), out_dtype),
                   jax.ShapeDtypeStruct((n_seq, 1, dl), F32),
                   jax.ShapeDtypeStruct((n_seq, CONV_TAPS - 1, dl), F32)),
        grid=(n_live + 1 if fuse_out else n_live,),
        in_specs=[
            pl.BlockSpec((t, dl), lambda s: (blk0 + chunk(s), 0)),
            pl.BlockSpec((t, dl), lambda s: (blk0 + chunk(s), 1)),
            pl.BlockSpec((CONV_TAPS, dl), const2),
            pl.BlockSpec((1, dl), const2),
            pl.BlockSpec((nh, hd, hd), const3),
            pl.BlockSpec((nh, hd, hd), const3),
            pl.BlockSpec((1, dl), const2),
            pl.BlockSpec((1, dl), const2),
            pl.BlockSpec((1, dl), const2),
            pl.BlockSpec((1, dl), const2),
            pl.BlockSpec((1, dl), const2),
            pl.BlockSpec((HALO, dl), const2),
        ] + extra_specs,
        out_specs=(
            pl.BlockSpec((t, dout), lambda s: (out_chunk(s), 0)),
            pl.BlockSpec((1, 1, dl), per_seq),
            pl.BlockSpec((1, CONV_TAPS - 1, dl), per_seq),
        ),
        scratch_shapes=scratch,
        compiler_params=pltpu.CompilerParams(
            dimension_semantics=("arbitrary",),
            vmem_limit_bytes=_vmem_limit(pipelined, resident),
            flags=MIXER_SCHED_FLAGS if fuse_out else None),
        name="lru_seq",
    )(proj, proj, cw, cb, wa, wx, ba, bx, lam, g, h0, tail0, *extra_args)


def _lru_slab_kernel(gate_ref, x_ref, cw_ref, cb_ref, wa_ref, wx_ref, ba_ref, bx_ref, lam_ref, g_ref,
                     h0_ref, tail_ref, o_ref, hfin_ref, tailfin_ref, y_ref):
    ls, bs, dl = x_ref.shape
    nh, hd = wa_ref.shape[0], wa_ref.shape[1]
    ntail = CONV_TAPS - 1
    for h in range(nh):
        sl = slice(h * hd, (h + 1) * hd)
        ext = [tail_ref[:, k * dl + h * hd:k * dl + (h + 1) * hd] for k in range(ntail)]
        ext += [x_ref[s, :, sl] for s in range(ls)]
        hsp = (0.5 * LRU_C) * jax.nn.softplus(-lam_ref[:, sl])
        hcur = h0_ref[:, sl]
        for s in range(ls):
            xh = _conv_taps(ext, s, cw_ref, cb_ref, sl)
            a, mult, i = _lru_gates(xh, wa_ref[h], wx_ref[h], ba_ref[:, sl], bx_ref[:, sl], hsp)
            hcur = a * hcur + mult * i * xh
            y_ref[s, :, sl] = hcur * gate_ref[s, :, sl]
        hfin_ref[:, sl] = hcur
        for k in range(ntail):
            tailfin_ref[:, k * dl + h * hd:k * dl + (h + 1) * hd] = ext[ls + k]
    for s in range(ls):
        y = y_ref[s]
        scale = lax.rsqrt(jnp.mean(y * y, axis=-1, keepdims=True) + EPS)
        o_ref[s] = (y * scale * g_ref[...]).astype(o_ref.dtype)


def _lru_slab(proj3, params, h0, tail, *, ls):
    cw, cb, wa, wx, ba, bx, lam, g = params
    bs = proj3.shape[1]
    dl = cw.shape[1]
    nh, hd = wa.shape[0], wa.shape[1]
    ntail = CONV_TAPS - 1
    c2 = lambda i: (0, 0)
    c3 = lambda i: (0, 0, 0)
    pipelined = [((ls, bs, dl), F32)] * 2 + [((ls, bs, dl), BF16)] + [((bs, (2 * ntail + 2) * dl), F32)]
    return pl.pallas_call(
        _lru_slab_kernel,
        out_shape=(jax.ShapeDtypeStruct((ls, bs, dl), BF16),
                   jax.ShapeDtypeStruct((bs, dl), F32),
                   jax.ShapeDtypeStruct((bs, ntail * dl), F32)),
        grid=(1,),
        in_specs=[
            pl.BlockSpec((ls, bs, dl), lambda i: (0, 0, 0)),
            pl.BlockSpec((ls, bs, dl), lambda i: (0, 0, 1)),
            pl.BlockSpec((CONV_TAPS, dl), c2), pl.BlockSpec((1, dl), c2),
            pl.BlockSpec((nh, hd, hd), c3), pl.BlockSpec((nh, hd, hd), c3),
            pl.BlockSpec((1, dl), c2), pl.BlockSpec((1, dl), c2), pl.BlockSpec((1, dl), c2), pl.BlockSpec((1, dl), c2),
            pl.BlockSpec((bs, dl), c2), pl.BlockSpec((bs, ntail * dl), c2),
        ],
        out_specs=(pl.BlockSpec((ls, bs, dl), c3), pl.BlockSpec((bs, dl), c2), pl.BlockSpec((bs, ntail * dl), c2)),
        scratch_shapes=[pltpu.VMEM((ls, bs, dl), F32)],
        compiler_params=pltpu.CompilerParams(
            dimension_semantics=("arbitrary",),
            vmem_limit_bytes=_vmem_limit(pipelined, [((ls, bs, dl), F32)])),
        name="lru_slab",
    )(proj3, proj3, cw, cb, wa, wx, ba, bx, lam, g, h0, tail)


def _head_lane_mask(n_heads):
    return lax.broadcasted_iota(jnp.int32, (1, HEAD_PAD), 1) < n_heads


def _gated_group_norm(y, z_act, g):
    yg = y * z_act
    scale = lax.rsqrt(jnp.mean(yg * yg, axis=-1, keepdims=True) + EPS)
    return yg * scale * g


def _ssd_seq_kernel(z_ref, xbc_ref, dt_ref, cw_ref, cb_ref, dtb_ref, alog_ref, dskip_ref, g_ref, e3_ref,
                    s0_ref, tail0_ref, *rest, pad, n_heads, n_groups, n_chunks, n_live, fuse_out):
    if fuse_out:
        wo_ref, part_ref, res_ref = rest[:3]
        o_ref, sfin_ref, tailfin_ref, xe_ref, xc_ref, ex_ref, y_ref, s_ref, yn_ref, ynp_ref = rest[3:]
    else:
        o_ref, sfin_ref, tailfin_ref, xe_ref, xc_ref, ex_ref, y_ref, s_ref = rest
        yn_ref = o_ref
    step = pl.program_id(0)
    live = step < n_live
    c = lax.rem(jnp.minimum(step, n_live - 1), n_chunks)
    keep = (lambda new, old: jnp.where(live, new, old)) if fuse_out else (lambda new, old: new)
    q, ds = z_ref.shape
    dc = xbc_ref.shape[1]
    gn = (dc - ds) // 2
    n = gn // n_groups
    p = ds // n_heads
    r = n_heads // n_groups
    gw = ds // n_groups
    hpb = LANES // p

    @pl.when(jnp.logical_and(c == 0, live))
    def _():
        xe_ref[0:HALO, :] = tail0_ref[...]
        s_ref[...] = s0_ref[...]

    if fuse_out:
        @pl.when(step == 0)
        def _():
            ynp_ref[...] = jnp.zeros_like(ynp_ref)

    xe_ref[HALO:HALO + q, :] = xbc_ref[...]
    valid = (c * q + lax.broadcasted_iota(jnp.int32, (q, 1), 0)) >= pad

    cblk = 512 if dc % 512 == 0 else LANES
    for j in range(dc // cblk):
        sl = slice(j * cblk, (j + 1) * cblk)
        ext = [xe_ref[HALO - 3 + k:HALO - 3 + k + q, sl] for k in range(CONV_TAPS)]
        v = _silu(_conv_taps(ext, 0, cw_ref, cb_ref, sl))
        if pad and (j + 1) * cblk <= ds:
            v = jnp.where(valid, v, 0.0)
        xc_ref[:, sl] = v

    dtv = jnp.where(_head_lane_mask(n_heads), jax.nn.softplus(dt_ref[...] + dtb_ref[...]), 0.0)
    if pad:
        dtv = jnp.where(valid, dtv, 0.0)
    da = dtv * (-jnp.exp(alog_ref[...]))
    ri = lax.broadcasted_iota(jnp.int32, (q, q), 0)
    ci = lax.broadcasted_iota(jnp.int32, (q, q), 1)
    causal = ci <= ri
    tri = jnp.where(causal, 1.0, 0.0).astype(BF16)
    ac3 = jnp.dot(tri, _split3(da), preferred_element_type=F32)
    acum = ac3[:, 0:HEAD_PAD] + ac3[:, HEAD_PAD:2 * HEAD_PAD] + ac3[:, 2 * HEAD_PAD:3 * HEAD_PAD]
    alast = acum[q - 1:q, :]
    eac = jnp.exp(acum)
    wend = jnp.exp(alast - acum) * dtv
    cdec = jnp.broadcast_to(jnp.exp(alast), (SUBLANES, HEAD_PAD))
    ex_ref[...] = _expand_heads(jnp.concatenate([eac, wend, cdec], axis=0), e3_ref)
    acum_t = acum.T
    dt_t = dtv.T
    lane = lax.broadcasted_iota(jnp.int32, (q, LANES), 1)

    for g in range(n_groups):
        gsl = slice(g * gw, (g + 1) * gw)
        bg = xc_ref[:, ds + g * n:ds + (g + 1) * n].astype(BF16)
        cg = xc_ref[:, ds + gn + g * n:ds + gn + (g + 1) * n].astype(BF16)
        cbm = lax.dot_general(cg, bg, (((1,), (1,)), ((), ())), preferred_element_type=F32)
        yoff = jnp.dot(cg, s_ref[:, gsl].astype(BF16), preferred_element_type=F32)
        for k in range(gw // LANES):
            lsl = slice(g * gw + k * LANES, g * gw + (k + 1) * LANES)
            xs = xc_ref[:, lsl]
            ms = []
            xparts = []
            for u in range(hpb):
                h = g * r + k * hpb + u
                seg = acum[:, h:h + 1] - acum_t[h:h + 1, :]
                lm = jnp.where(causal, jnp.exp(seg), 0.0) * dt_t[h:h + 1, :]
                ms.append((cbm * lm).astype(BF16))
                inhead = (lane >= u * p) & (lane < (u + 1) * p)
                xparts.append(jnp.where(inhead, xs, 0.0).astype(BF16))
            ydiag = jnp.dot(jnp.concatenate(ms, axis=1), jnp.concatenate(xparts, axis=0),
                            preferred_element_type=F32)
            y = ydiag + yoff[:, k * LANES:(k + 1) * LANES] * ex_ref[0:q, lsl]
            y_ref[:, lsl] = y + dskip_ref[:, lsl] * xs
        yn_ref[:, gsl] = _gated_group_norm(y_ref[:, gsl], z_ref[:, gsl], g_ref[:, gsl]).astype(BF16)
        xw = (xc_ref[:, gsl] * ex_ref[q:2 * q, gsl]).astype(BF16)
        upd = lax.dot_general(bg, xw, (((0,), (0,)), ((), ())), preferred_element_type=F32)
        s_old = s_ref[:, gsl]
        s_ref[:, gsl] = keep(ex_ref[2 * q:2 * q + 1, gsl] * s_old + upd, s_old)
        if fuse_out:
            piece = o_ref.shape[1] // n_groups
            psl = slice(g * piece, (g + 1) * piece)
            mix = part_ref[:, psl] + jnp.dot(ynp_ref[...], wo_ref[:, psl], preferred_element_type=F32)
            o_ref[:, psl] = res_ref[:, psl] + mix

    if fuse_out:
        ynp_ref[...] = yn_ref[...]
    xe_ref[0:HALO, :] = keep(xe_ref[q:q + HALO, :], xe_ref[0:HALO, :])
    sfin_ref[0] = s_ref[...]
    tailfin_ref[0] = xe_ref[HALO - 3:HALO, :]


def _ssd_seq(proj, dt, params, s0, tail0, *, n_seq, seq_len, row0, n_heads, n_groups, dl, pad=0, fuse=None):
    cw, cb, dtb, alog, dskip, g, e3 = params
    dc = cw.shape[1]
    ds = dskip.shape[1]
    n = s0.shape[0]
    q = SSD_CHUNK
    assert seq_len % q == 0 and row0 % q == 0
    assert (2 * dl) % ds == 0 and (2 * dl + ds) % dc == 0 and LANES % (ds // n_heads) == 0
    n_chunks = seq_len // q
    n_live = n_seq * n_chunks
    blk0 = row0 // q
    fuse_out = fuse is not None
    chunk = (lambda s: jnp.minimum(s, n_live - 1)) if fuse_out else (lambda s: s)
    rows = lambda s: blk0 + chunk(s)
    const2 = lambda s: (0, 0)
    out_rows = (lambda s: (jnp.maximum(s - 1, 0), 0)) if fuse_out else (lambda s: (s, 0))
    per_seq = lambda s: (chunk(s) // n_chunks, 0, 0)
    dout, out_dtype = (fuse[0].shape[1], F32) if fuse_out else (ds, BF16)
    pipelined = [((q, ds), F32), ((q, dc), F32), ((q, HEAD_PAD), F32), ((q, dout), out_dtype)]
    resident = [((q + HALO, dc), F32), ((q, dc), F32), ((2 * q + SUBLANES, ds), F32), ((q, ds), F32),
                ((3 * n, ds), F32), ((6 * HEAD_PAD, ds), BF16)]
    scratch = [pltpu.VMEM((q + HALO, dc), F32), pltpu.VMEM((q, dc), F32),
               pltpu.VMEM((2 * q + SUBLANES, ds), F32), pltpu.VMEM((q, ds), F32), pltpu.VMEM((n, ds), F32)]
    extra_specs, extra_args = [], []
    if fuse_out:
        extra_specs = [pl.BlockSpec((ds, dout), const2, pipeline_mode=pl.Buffered(1)),
                       pl.BlockSpec((q, dout), out_rows), pl.BlockSpec((q, dout), out_rows)]
        extra_args = list(fuse)
        pipelined += [((q, dout), F32)] * 2
        resident += [((ds, dout), BF16), ((2 * q, ds), BF16)]
        scratch += [pltpu.VMEM((q, ds), BF16), pltpu.VMEM((q, ds), BF16)]
    kern = functools.partial(_ssd_seq_kernel, pad=pad, n_heads=n_heads, n_groups=n_groups, n_chunks=n_chunks,
                             n_live=n_live, fuse_out=fuse_out)
    return pl.pallas_call(
        kern,
        out_shape=(jax.ShapeDtypeStruct((n_seq * seq_len, dout), out_dtype),
                   jax.ShapeDtypeStruct((n_seq, n, ds), F32),
                   jax.ShapeDtypeStruct((n_seq, CONV_TAPS - 1, dc), F32)),
        grid=(n_live + 1 if fuse_out else n_live,),
        in_specs=[
            pl.BlockSpec((q, ds), lambda s: (rows(s), (2 * dl) // ds)),
            pl.BlockSpec((q, dc), lambda s: (rows(s), (2 * dl + ds) // dc)),
            pl.BlockSpec((q, HEAD_PAD), lambda s: (rows(s), 0)),
            pl.BlockSpec((CONV_TAPS, dc), const2),
            pl.BlockSpec((1, dc), const2),
            pl.BlockSpec((1, HEAD_PAD), const2),
            pl.BlockSpec((1, HEAD_PAD), const2),
            pl.BlockSpec((1, ds), const2),
            pl.BlockSpec((1, ds), const2),
            pl.BlockSpec((3 * HEAD_PAD, ds), const2),
            pl.BlockSpec((n, ds), const2),
            pl.BlockSpec((HALO, dc), const2),
        ] + extra_specs,
        out_specs=(
            pl.BlockSpec((q, dout), out_rows),
            pl.BlockSpec((1, n, ds), per_seq),
            pl.BlockSpec((1, CONV_TAPS - 1, dc), per_seq),
        ),
        scratch_shapes=scratch,
        compiler_params=pltpu.CompilerParams(
            dimension_semantics=("arbitrary",),
            vmem_limit_bytes=_vmem_limit(pipelined, resident),
            flags=MIXER_SCHED_FLAGS if fuse_out else None),
        name="ssd_seq",
    )(proj, proj, dt, cw, cb, dtb, alog, dskip, g, e3, s0, tail0, *extra_args)


def _ssd_slab_pre_kernel(xbc_ref, dt_ref, tail_ref, cw_ref, cb_ref, dtb_ref, alog_ref, dskip_ref, e3_ref,
                         ypart_ref, eace_ref, c_ref, b_ref, xw_ref, cdec_ref, tailfin_ref, xc_ref,
                         *, n_heads, n_groups):
    ls, bs, dc = xbc_ref.shape
    ds = dskip_ref.shape[1]
    gn = (dc - ds) // 2
    n = gn // n_groups
    r = n_heads // n_groups
    ntail = CONV_TAPS - 1

    cblk = 512 if dc % 512 == 0 else LANES
    for j in range(dc // cblk):
        sl = slice(j * cblk, (j + 1) * cblk)
        ext = [tail_ref[:, k * dc + j * cblk:k * dc + (j + 1) * cblk] for k in range(ntail)]
        ext += [xbc_ref[s, :, sl] for s in range(ls)]
        for s in range(ls):
            xc_ref[s, :, sl] = _silu(_conv_taps(ext, s, cw_ref, cb_ref, sl))
        for k in range(ntail):
            tailfin_ref[:, k * dc + j * cblk:k * dc + (j + 1) * cblk] = ext[ls + k]

    hmask = _head_lane_mask(n_heads)
    a_neg = -jnp.exp(alog_ref[...])
    dtv, acum = [], []
    run = jnp.zeros((bs, HEAD_PAD), F32)
    for s in range(ls):
        d = jnp.where(hmask, jax.nn.softplus(dt_ref[s] + dtb_ref[...]), 0.0)
        run = run + d * a_neg
        dtv.append(d)
        acum.append(run)
    alast = acum[ls - 1]
    cdec_ref[...] = jnp.exp(alast)
    head_group = lax.broadcasted_iota(jnp.int32, (1, HEAD_PAD), 1) // r

    for s in range(ls):
        eace_ref[s] = _expand_heads(jnp.exp(acum[s]), e3_ref)
        wend_e = _expand_heads(jnp.exp(alast - acum[s]) * dtv[s], e3_ref)
        xw_ref[:, s * ds:(s + 1) * ds] = xc_ref[s, :, 0:ds] * wend_e
        b_ref[:, s * gn:(s + 1) * gn] = xc_ref[s, :, ds:ds + gn]
        c_ref[:, s * gn:(s + 1) * gn] = xc_ref[s, :, ds + gn:ds + 2 * gn]
        ypart = dskip_ref[...] * xc_ref[s, :, 0:ds]
        for j in range(s + 1):
            cbh = jnp.zeros((bs, HEAD_PAD), F32)
            for g in range(n_groups):
                cs = xc_ref[s, :, ds + gn + g * n:ds + gn + (g + 1) * n]
                bj = xc_ref[j, :, ds + g * n:ds + (g + 1) * n]
                cbg = jnp.sum(cs * bj, axis=-1, keepdims=True)
                cbh = cbh + jnp.where(head_group == g, cbg, 0.0)
            coef = cbh * (jnp.exp(acum[s] - acum[j]) * dtv[j])
            ypart = ypart + _expand_heads(coef, e3_ref) * xc_ref[j, :, 0:ds]
        ypart_ref[s] = ypart
    for s in range(ls, SLAB_ROWS):
        xw_ref[:, s * ds:(s + 1) * ds] = jnp.zeros((bs, ds), F32)
        b_ref[:, s * gn:(s + 1) * gn] = jnp.zeros((bs, gn), F32)
        c_ref[:, s * gn:(s + 1) * gn] = jnp.zeros((bs, gn), F32)


def _ssd_slab_pre(proj3, dt3, tail, params, *, ls, n_heads, n_groups, dl):
    cw, cb, dtb, alog, dskip, _, e3 = params
    bs = proj3.shape[1]
    dc = cw.shape[1]
    ds = dskip.shape[1]
    gn = (dc - ds) // 2
    ntail = CONV_TAPS - 1
    assert ls <= SLAB_ROWS and (2 * dl + ds) % dc == 0
    c2 = lambda i: (0, 0)
    c3 = lambda i: (0, 0, 0)
    pipelined = [((ls, bs, dc), F32), ((ls, bs, HEAD_PAD), F32), ((bs, 2 * ntail * dc), F32),
                 ((2 * ls, bs, ds), F32), ((bs, SLAB_ROWS * (2 * gn + ds)), F32), ((3 * HEAD_PAD, ds), BF16)]
    kern = functools.partial(_ssd_slab_pre_kernel, n_heads=n_heads, n_groups=n_groups)
    return pl.pallas_call(
        kern,
        out_shape=(jax.ShapeDtypeStruct((ls, bs, ds), F32),
                   jax.ShapeDtypeStruct((ls, bs, ds), F32),
                   jax.ShapeDtypeStruct((bs, SLAB_ROWS * gn), F32),
                   jax.ShapeDtypeStruct((bs, SLAB_ROWS * gn), F32),
                   jax.ShapeDtypeStruct((bs, SLAB_ROWS * ds), F32),
                   jax.ShapeDtypeStruct((bs, HEAD_PAD), F32),
                   jax.ShapeDtypeStruct((bs, ntail * dc), F32)),
        grid=(1,),
        in_specs=[
            pl.BlockSpec((ls, bs, dc), lambda i: (0, 0, (2 * dl + ds) // dc)),
            pl.BlockSpec((ls, bs, HEAD_PAD), c3),
            pl.BlockSpec((bs, ntail * dc), c2),
            pl.BlockSpec((CONV_TAPS, dc), c2), pl.BlockSpec((1, dc), c2),
            pl.BlockSpec((1, HEAD_PAD), c2), pl.BlockSpec((1, HEAD_PAD), c2),
            pl.BlockSpec((1, ds), c2), pl.BlockSpec((3 * HEAD_PAD, ds), c2),
        ],
        out_specs=(pl.BlockSpec((ls, bs, ds), c3), pl.BlockSpec((ls, bs, ds), c3),
                   pl.BlockSpec((bs, SLAB_ROWS * gn), c2), pl.BlockSpec((bs, SLAB_ROWS * gn), c2),
                   pl.BlockSpec((bs, SLAB_ROWS * ds), c2), pl.BlockSpec((bs, HEAD_PAD), c2),
                   pl.BlockSpec((bs, ntail * dc), c2)),
        scratch_shapes=[pltpu.VMEM((ls, bs, dc), F32)],
        compiler_params=pltpu.CompilerParams(
            dimension_semantics=("arbitrary",),
            vmem_limit_bytes=_vmem_limit(pipelined, [((ls, bs, dc), F32)])),
        name="ssd_slab_pre",
    )(proj3, dt3, tail, cw, cb, dtb, alog, dskip, e3)


def _ssd_state_kernel(cdec_ref, s_ref, c_ref, b_ref, xw_ref, snew_ref, yoff_ref, *, n_heads, n_groups):
    i = pl.program_id(0)
    sb, hp, n = s_ref.shape
    p = hp // n_heads
    r = n_heads // n_groups
    gw = hp // n_groups
    for q in range(sb):
        for g in range(n_groups):
            gsl = slice(g * gw, (g + 1) * gw)
            sg = s_ref[q, gsl, :]
            cg = c_ref[q, :, g * n:(g + 1) * n].astype(BF16)
            yoff_ref[q, :, gsl] = lax.dot_general(cg, sg.astype(BF16), (((1,), (1,)), ((), ())),
                                                  preferred_element_type=F32)
            upd = lax.dot_general(xw_ref[q, :, gsl].astype(BF16), b_ref[q, :, g * n:(g + 1) * n].astype(BF16),
                                  (((0,), (0,)), ((), ())), preferred_element_type=F32)
            for u in range(r):
                h = g * r + u
                rows = slice(h * p, (h + 1) * p)
                snew_ref[q, rows, :] = cdec_ref[i * sb + q, h] * s_ref[q, rows, :] + upd[u * p:(u + 1) * p, :]


def _ssd_state(cdec, state, c_rows, b_rows, xw_rows, *, n_heads, n_groups):
    bs, hp, n = state.shape
    gn = c_rows.shape[2]
    sb = STATE_SEQS_PER_STEP if bs % STATE_SEQS_PER_STEP == 0 else 1
    per_seq = lambda i: (i, 0, 0)
    pipelined = [((sb, hp, n), F32)] * 2 + [((sb, SLAB_ROWS, gn), F32)] * 2 + [((sb, SLAB_ROWS, hp), F32)] * 2
    kern = functools.partial(_ssd_state_kernel, n_heads=n_heads, n_groups=n_groups)
    return pl.pallas_call(
        kern,
        out_shape=(jax.ShapeDtypeStruct((bs, hp, n), F32), jax.ShapeDtypeStruct((bs, SLAB_ROWS, hp), F32)),
        grid=(bs // sb,),
        in_specs=[
            pl.BlockSpec(memory_space=pltpu.SMEM),
            pl.BlockSpec((sb, hp, n), per_seq),
            pl.BlockSpec((sb, SLAB_ROWS, gn), per_seq),
            pl.BlockSpec((sb, SLAB_ROWS, gn), per_seq),
            pl.BlockSpec((sb, SLAB_ROWS, hp), per_seq),
        ],
        out_specs=(pl.BlockSpec((sb, hp, n), per_seq), pl.BlockSpec((sb, SLAB_ROWS, hp), per_seq)),
        compiler_params=pltpu.CompilerParams(
            dimension_semantics=("parallel",),
            vmem_limit_bytes=_vmem_limit(pipelined, [])),
        name="ssd_state",
    )(cdec, state, c_rows, b_rows, xw_rows)


def _ssd_slab_post_kernel(ypart_ref, eace_ref, yoff_ref, z_ref, g_ref, o_ref, *, n_groups):
    ls, bs, ds = ypart_ref.shape
    gw = ds // n_groups
    for s in range(ls):
        for g in range(n_groups):
            gsl = slice(g * gw, (g + 1) * gw)
            y = ypart_ref[s, :, gsl] + eace_ref[s, :, gsl] * yoff_ref[:, s * ds + g * gw:s * ds + (g + 1) * gw]
            o_ref[s, :, gsl] = _gated_group_norm(y, z_ref[s, :, gsl], g_ref[:, gsl]).astype(o_ref.dtype)


def _ssd_slab_post(ypart, eace, yoff, proj3, g, *, n_groups, dl):
    ls, bs, ds = ypart.shape
    assert (2 * dl) % ds == 0
    c2 = lambda i: (0, 0)
    c3 = lambda i: (0, 0, 0)
    pipelined = [((ls, bs, ds), F32)] * 3 + [((bs, SLAB_ROWS * ds), F32), ((ls, bs, ds), BF16)]
    kern = functools.partial(_ssd_slab_post_kernel, n_groups=n_groups)
    return pl.pallas_call(
        kern,
        out_shape=jax.ShapeDtypeStruct((ls, bs, ds), BF16),
        grid=(1,),
        in_specs=[
            pl.BlockSpec((ls, bs, ds), c3), pl.BlockSpec((ls, bs, ds), c3),
            pl.BlockSpec((bs, SLAB_ROWS * ds), c2),
            pl.BlockSpec((ls, bs, ds), lambda i: (0, 0, (2 * dl) // ds)),
            pl.BlockSpec((1, ds), c2),
        ],
        out_specs=pl.BlockSpec((ls, bs, ds), c3),
        compiler_params=pltpu.CompilerParams(
            dimension_semantics=("arbitrary",),
            vmem_limit_bytes=_vmem_limit(pipelined, [])),
        name="ssd_slab_post",
    )(ypart, eace, yoff, proj3, g)


def _head_expansion(n_heads, head_dim):
    rows = lax.broadcasted_iota(jnp.int32, (HEAD_PAD, n_heads * head_dim), 0)
    cols = lax.broadcasted_iota(jnp.int32, (HEAD_PAD, n_heads * head_dim), 1)
    e = (cols // head_dim == rows).astype(BF16)
    return jnp.concatenate([e, e, e], axis=0)


def _pad_lanes(v, width):
    return jnp.pad(v, ((0, 0), (0, width - v.shape[1])))


def _tail_block(tail):
    return jnp.pad(tail, ((HALO - tail.shape[0], 0), (0, 0)))


def _mlp(x1, g_mlp, w_up, w_down, g_final, *, layer=None):
    m = x1.shape[0]
    tiles = (1024, 512, 256, 128)
    tm = m if layer is not None else _pick_tile(m, tiles)
    up = _mlp_up(x1, g_mlp, w_up, tm=tm, tn=_pick_tile(w_up.shape[-1], tiles), layer=layer)
    hid = up[0]
    down = _mlp_down_final(hid, w_down, x1, g_final, tm=tm, tk=_pick_tile(w_down.shape[-2], tiles), layer=layer)
    if layer is None:
        return down[0]
    return down[0], up[1], down[1]


def kernel(x_prompt, x_sample, state_lru_h, state_lru_conv, state_ssd, state_ssd_conv, meta_tokens, g_mix, w_in, conv_lru_w, conv_lru_b, lru_wa, lru_ba, lru_wx, lru_bx, lru_lambda, g_lru_out, conv_ssd_w, conv_ssd_b, dt_bias, a_log, d_skip, g_ssd_out, w_out, g_mlp, w_up, w_down, g_final):
    depth = w_in.shape[0]
    assert depth == 1, "single-layer step"
    l = 0
    bp, lp, d = x_prompt.shape
    bs, ls, _ = x_sample.shape
    n_meta = meta_tokens.shape[0]
    dl = state_lru_h.shape[-1]
    n_heads, p, n = state_ssd.shape[-3:]
    ds = n_heads * p
    dc = state_ssd_conv.shape[-1]
    gn = (dc - ds) // 2
    n_groups = gn // n
    nw = 2 * dl + ds + dc
    ntail = CONV_TAPS - 1
    q = SSD_CHUNK
    meta_pad = (-n_meta) % q
    assert n_heads <= HEAD_PAD and (bs * ls) % q == 0 and lp % q == 0 and q % bs == 0

    row = lambda v: v.reshape(1, -1).astype(F32)
    w_in_t = jnp.swapaxes(w_in, 1, 2)
    lru_params = (conv_lru_w[l], row(conv_lru_b[l]), (0.5 * lru_wa[l]).astype(BF16), (0.5 * lru_wx[l]).astype(BF16),
                  row(0.5 * lru_ba[l]), row(0.5 * lru_bx[l]), row(lru_lambda[l]), row(g_lru_out[l]))
    ssd_params = (conv_ssd_w[l], row(conv_ssd_b[l]), _pad_lanes(row(dt_bias[l]), HEAD_PAD),
                  _pad_lanes(row(a_log[l]), HEAD_PAD), row(jnp.repeat(d_skip[l], p)), row(g_ssd_out[l]),
                  _head_expansion(n_heads, p))
    g_mix_r, g_mlp_r, g_final_r = row(g_mix[l]), row(g_mlp[l]), row(g_final)

    xs_tm = x_sample.transpose(1, 0, 2).reshape(ls * bs, d)
    x_side = jnp.concatenate([xs_tm, jnp.zeros((meta_pad, d), F32), meta_tokens.astype(F32)], axis=0)
    xp_rows = x_prompt.reshape(bp * lp, d)

    tiles = (1024, 512, 256, 128)
    tn_in = next(t for t in tiles if nw % t == 0 and dl % t == 0 and ds % t == 0)
    act_cols = dict(gelu_cols=(0, dl), silu_cols=(2 * dl, 2 * dl + ds))
    proj_side, dt_side, w_in_b, w_dt_b = _in_proj(x_side, g_mix_r, w_in_t, nw=nw, n_dt=n_heads, tm=x_side.shape[0],
                                                  tn=tn_in, layer=l, **act_cols)

    proj_s3 = proj_side.reshape(-1, bs, nw)
    dt_s3 = dt_side.reshape(-1, bs, HEAD_PAD)
    lru_s, s_h, s_ltail = _lru_slab(proj_s3, lru_params, state_lru_h[l],
                                    state_lru_conv[l].reshape(bs, ntail * dl), ls=ls)
    ypart, eace, c_rows, b_rows, xw_rows, cdec, s_stail = _ssd_slab_pre(
        proj_s3, dt_s3, state_ssd_conv[l].reshape(bs, ntail * dc), ssd_params,
        ls=ls, n_heads=n_heads, n_groups=n_groups, dl=dl)
    s_new, yoff = _ssd_state(cdec, state_ssd[l].reshape(bs, ds, n),
                             c_rows.reshape(bs, SLAB_ROWS, gn), b_rows.reshape(bs, SLAB_ROWS, gn),
                             xw_rows.reshape(bs, SLAB_ROWS, ds), n_heads=n_heads, n_groups=n_groups)
    ssd_s = _ssd_slab_post(ypart, eace, yoff.reshape(bs, SLAB_ROWS * ds), proj_s3, ssd_params[5],
                           n_groups=n_groups, dl=dl)
    x1_s, w_out_lru, w_out_ssd = _out_proj(lru_s.reshape(ls * bs, dl), ssd_s.reshape(ls * bs, ds), w_out, l, xs_tm,
                                           tn=_pick_tile(d, (512, 256, 128)))
    y_s, w_up_b, w_down_b = _mlp(x1_s, g_mlp_r, w_up, w_down, g_final_r, layer=l)

    meta_row0 = ls * bs
    _, m_h, m_ltail = _lru_seq(proj_side, lru_params, jnp.zeros((1, dl), F32), jnp.zeros((HALO, dl), F32),
                               n_seq=1, seq_len=q, t=q, row0=meta_row0, pad=meta_pad, reset_first=True)
    _, m_s, m_stail = _ssd_seq(proj_side, dt_side, ssd_params, jnp.zeros((n, ds), F32), jnp.zeros((HALO, dc), F32),
                               n_seq=1, seq_len=q, row0=meta_row0, n_heads=n_heads, n_groups=n_groups, dl=dl,
                               pad=meta_pad)

    proj_p, dt_p = _in_proj(xp_rows, g_mix_r, w_in_b, w_dt_b, nw=nw, n_dt=n_heads, tm=_pick_tile(bp * lp, tiles),
                            tn=tn_in, **act_cols)
    part_p, p_h, p_ltail = _lru_seq(proj_p, lru_params, m_h[0], _tail_block(m_ltail[0]),
                                    n_seq=bp, seq_len=lp, t=_pick_tile(lp, (256, 128)), row0=0, wo=w_out_lru)
    x1_p, p_s, p_stail = _ssd_seq(proj_p, dt_p, ssd_params, m_s[0], _tail_block(m_stail[0]),
                                  n_seq=bp, seq_len=lp, row0=0, n_heads=n_heads, n_groups=n_groups, dl=dl,
                                  fuse=(w_out_ssd, part_p, xp_rows))
    y_p = _mlp(x1_p, g_mlp_r, w_up_b, w_down_b, g_final_r)

    y_prompt = y_p.reshape(bp, lp, d)
    y_sample = y_s.reshape(ls, bs, d).transpose(1, 0, 2)
    p_lru_h = p_h.reshape(1, bp, dl)
    p_lru_conv = p_ltail.reshape(1, bp, ntail, dl)
    p_ssd = p_s.transpose(0, 2, 1).reshape(1, bp, n_heads, p, n)
    p_ssd_conv = p_stail.reshape(1, bp, ntail, dc)
    s_lru_h = s_h.reshape(1, bs, dl)
    s_lru_conv = s_ltail.reshape(1, bs, ntail, dl)
    s_ssd = s_new.reshape(1, bs, n_heads, p, n)
    s_ssd_conv = s_stail.reshape(1, bs, ntail, dc)
    return (y_prompt, y_sample, p_lru_h, p_lru_conv, p_ssd, p_ssd_conv, s_lru_h, s_lru_conv, s_ssd, s_ssd_conv)
```

```python
import functools

import jax
import jax.numpy as jnp
from jax import lax
from jax.experimental import pallas as pl
from jax.experimental.pallas import tpu as pltpu

F32 = jnp.float32
BF16 = jnp.bfloat16

EPS = 1e-6
LRU_C = 8.0
CONV_TAPS = 4

LANES = 128
SUBLANES = 8
MXU_COLS = 256
VMEM_BYTES_V7X = 64 * 1024 * 1024
VMEM_TEMP_BYTES = 10 * 1024 * 1024
VMEM_CEILING_BYTES = VMEM_BYTES_V7X - 6 * 1024 * 1024

HALO = SUBLANES
SSD_CHUNK = 128
HEAD_PAD = LANES
SLAB_ROWS = SUBLANES
STATE_SEQS_PER_STEP = 4
MIXER_SCHED_FLAGS = None


def _nbytes(shape, dtype):
    n = 1
    for s in shape:
        n *= s
    return n * jnp.dtype(dtype).itemsize


def _vmem_limit(pipelined, resident):
    est = 2 * sum(_nbytes(s, d) for s, d in pipelined) + sum(_nbytes(s, d) for s, d in resident)
    return int(min(est + VMEM_TEMP_BYTES, VMEM_CEILING_BYTES))


def _pick_tile(m, prefs):
    for t in prefs:
        if m % t == 0:
            return t
    return m


def _silu(x):
    h = 0.5 * x
    return h * jnp.tanh(h) + h


def _split3(x):
    hi = x.astype(BF16)
    r1 = x - hi.astype(F32)
    mid = r1.astype(BF16)
    lo = (r1 - mid.astype(F32)).astype(BF16)
    return jnp.concatenate([hi, mid, lo], axis=1)


def _expand_heads(x, e3_ref):
    return jnp.dot(_split3(x), e3_ref[...], preferred_element_type=F32)


def _rmsnorm_rows(x, g):
    ms = jnp.mean(x * x, axis=-1, keepdims=True)
    return x * lax.rsqrt(ms + EPS) * g


def _conv_taps(ext, s, cw_ref, cb_ref, sl):
    v = cb_ref[:, sl] + ext[s] * cw_ref[0:1, sl]
    for k in range(1, CONV_TAPS):
        v = v + ext[s + k] * cw_ref[k:k + 1, sl]
    return v


def _weight_tile(w_ref, wb_ref):
    if wb_ref is None:
        return w_ref[...]
    w = w_ref[...].astype(BF16)
    wb_ref[...] = w
    return w


def _weight_specs(w, layer, blk, idx):
    if layer is None:
        return pl.BlockSpec(blk, idx), None, F32
    in_spec = pl.BlockSpec((None,) + blk, lambda i, j: (layer,) + idx(i, j))
    return in_spec, pl.BlockSpec(blk, idx), w.dtype


_NT = (((1,), (1,)), ((), ()))


def _in_proj_kernel(x_ref, g_ref, w_ref, wdt_ref, o_ref, dt_ref, *rest, gelu_tiles, silu_tiles, emit_w):
    if emit_w:
        wb_ref, wdtb_ref, xn_ref = rest
    else:
        wb_ref, wdtb_ref, xn_ref = None, None, rest[0]
    j = pl.program_id(1)

    @pl.when(j == 0)
    def _():
        xn = _rmsnorm_rows(x_ref[...], g_ref[...]).astype(BF16)
        xn_ref[...] = xn
        wdt = _weight_tile(wdt_ref, wdtb_ref)
        wdt = jnp.concatenate([wdt, jnp.zeros((HEAD_PAD - wdt.shape[0], wdt.shape[1]), BF16)], axis=0)
        dt_ref[...] = lax.dot_general(xn, wdt, _NT, preferred_element_type=F32)

    def tile():
        return lax.dot_general(xn_ref[...], _weight_tile(w_ref, wb_ref), _NT, preferred_element_type=F32)

    in_range = lambda r: jnp.logical_and(j >= r[0], j < r[1])
    is_gelu, is_silu = in_range(gelu_tiles), in_range(silu_tiles)

    @pl.when(is_gelu)
    def _():
        o_ref[...] = jax.nn.gelu(tile())

    @pl.when(is_silu)
    def _():
        o_ref[...] = _silu(tile())

    @pl.when(jnp.logical_not(jnp.logical_or(is_gelu, is_silu)))
    def _():
        o_ref[...] = tile()


def _in_proj(x, g, w_t, wdt_t=None, *, nw, n_dt, tm, tn, gelu_cols, silu_cols, layer=None):
    m, k = x.shape
    emit_w = layer is not None
    assert m % tm == 0 and nw % tn == 0 and all(c % tn == 0 for c in gelu_cols + silu_cols)
    assert n_dt % SUBLANES == 0 and nw % n_dt == 0 and (not emit_w or m == tm)
    out_shape = [jax.ShapeDtypeStruct((m, nw), F32), jax.ShapeDtypeStruct((m, HEAD_PAD), F32)]
    out_specs = [pl.BlockSpec((tm, tn), lambda i, j: (i, j)), pl.BlockSpec((tm, HEAD_PAD), lambda i, j: (i, 0))]
    pipelined = [((tm, k), F32), ((tn, k), w_t.dtype), ((n_dt, k), w_t.dtype), ((tm, tn), F32), ((tm, HEAD_PAD), F32)]
    if emit_w:
        w_spec = pl.BlockSpec((None, tn, k), lambda i, j: (layer, j, 0))
        wdt_spec = pl.BlockSpec((None, n_dt, k), lambda i, j: (layer, nw // n_dt, 0))
        wdt_t = w_t
        out_shape += [jax.ShapeDtypeStruct((nw, k), BF16), jax.ShapeDtypeStruct((n_dt, k), BF16)]
        out_specs += [pl.BlockSpec((tn, k), lambda i, j: (j, 0)), pl.BlockSpec((n_dt, k), lambda i, j: (0, 0))]
        pipelined += [((tn, k), BF16), ((n_dt, k), BF16)]
    else:
        w_spec = pl.BlockSpec((tn, k), lambda i, j: (j, 0))
        wdt_spec = pl.BlockSpec((n_dt, k), lambda i, j: (0, 0))
    kern = functools.partial(_in_proj_kernel, gelu_tiles=tuple(c // tn for c in gelu_cols),
                             silu_tiles=tuple(c // tn for c in silu_cols), emit_w=emit_w)
    return pl.pallas_call(
        kern,
        out_shape=tuple(out_shape),
        grid=(m // tm, nw // tn),
        in_specs=[
            pl.BlockSpec((tm, k), lambda i, j: (i, 0)),
            pl.BlockSpec((1, k), lambda i, j: (0, 0)),
            w_spec,
            wdt_spec,
        ],
        out_specs=tuple(out_specs),
        scratch_shapes=[pltpu.VMEM((tm, k), BF16)],
        compiler_params=pltpu.CompilerParams(
            dimension_semantics=("parallel", "arbitrary"),
            vmem_limit_bytes=_vmem_limit(pipelined, [((tm, k), BF16), ((tn, k), BF16)])),
        name="in_proj",
    )(x, g, w_t, wdt_t)


def _out_proj_kernel(a1_ref, a2_ref, w1_ref, w2_ref, res_ref, o_ref, wb1_ref, wb2_ref):
    acc = jnp.dot(a1_ref[...], _weight_tile(w1_ref, wb1_ref), preferred_element_type=F32)
    acc = acc + jnp.dot(a2_ref[...], _weight_tile(w2_ref, wb2_ref), preferred_element_type=F32)
    o_ref[...] = res_ref[...] + acc


def _out_proj(a1, a2, w, layer, res, *, tn):
    m, k1 = a1.shape
    k2 = a2.shape[1]
    n = w.shape[2]
    assert n % tn == 0 and k1 == k2 and w.shape[1] == k1 + k2
    pipelined = [((m, k1), BF16), ((m, k2), BF16), ((k1, tn), w.dtype), ((k2, tn), w.dtype), ((m, tn), F32),
                 ((m, tn), F32), ((k1, tn), BF16), ((k2, tn), BF16)]
    col = lambda j: (0, j)
    return pl.pallas_call(
        _out_proj_kernel,
        out_shape=(jax.ShapeDtypeStruct((m, n), F32), jax.ShapeDtypeStruct((k1, n), BF16),
                   jax.ShapeDtypeStruct((k2, n), BF16)),
        grid=(n // tn,),
        in_specs=[
            pl.BlockSpec((m, k1), lambda j: (0, 0)),
            pl.BlockSpec((m, k2), lambda j: (0, 0)),
            pl.BlockSpec((None, k1, tn), lambda j: (layer, 0, j)),
            pl.BlockSpec((None, k2, tn), lambda j: (layer, 1, j)),
            pl.BlockSpec((m, tn), col),
        ],
        out_specs=(pl.BlockSpec((m, tn), col), pl.BlockSpec((k1, tn), col), pl.BlockSpec((k2, tn), col)),
        compiler_params=pltpu.CompilerParams(
            dimension_semantics=("parallel",),
            vmem_limit_bytes=_vmem_limit(pipelined, [])),
        name="out_proj",
    )(a1, a2, w, w, res)


def _mlp_up_kernel(x_ref, g_ref, w_ref, o_ref, *rest, emit_w):
    wb_ref, xn_ref = rest if emit_w else (None, rest[0])

    @pl.when(pl.program_id(1) == 0)
    def _():
        xn_ref[...] = _rmsnorm_rows(x_ref[...], g_ref[...]).astype(BF16)

    acc = jnp.dot(xn_ref[...], _weight_tile(w_ref, wb_ref), preferred_element_type=F32)
    o_ref[...] = jnp.square(jnp.maximum(acc, 0.0)).astype(o_ref.dtype)


def _mlp_up(x, g, w, *, tm, tn, layer=None):
    m, k = x.shape
    n = w.shape[-1]
    emit_w = layer is not None
    assert m % tm == 0 and n % tn == 0 and (not emit_w or m == tm)
    w_spec, wb_spec, w_dtype = _weight_specs(w, layer, (k, tn), lambda i, j: (0, j))
    out_shape = [jax.ShapeDtypeStruct((m, n), BF16)]
    out_specs = [pl.BlockSpec((tm, tn), lambda i, j: (i, j))]
    pipelined = [((tm, k), F32), ((k, tn), w_dtype if emit_w else BF16), ((tm, tn), BF16)]
    if emit_w:
        out_shape.append(jax.ShapeDtypeStruct((k, n), BF16))
        out_specs.append(wb_spec)
        pipelined.append(((k, tn), BF16))
    return pl.pallas_call(
        functools.partial(_mlp_up_kernel, emit_w=emit_w),
        out_shape=tuple(out_shape),
        grid=(m // tm, n // tn),
        in_specs=[
            pl.BlockSpec((tm, k), lambda i, j: (i, 0)),
            pl.BlockSpec((1, k), lambda i, j: (0, 0)),
            w_spec,
        ],
        out_specs=tuple(out_specs),
        scratch_shapes=[pltpu.VMEM((tm, k), BF16)],
        compiler_params=pltpu.CompilerParams(
            dimension_semantics=("parallel", "arbitrary"),
            vmem_limit_bytes=_vmem_limit(pipelined, [((tm, k), BF16), ((k, tn), BF16)])),
        name="mlp_up",
    )(x, g, w)


def _mlp_down_kernel(h_ref, w_ref, res_ref, g_ref, o_ref, *rest, emit_w):
    wb_ref = rest[0] if emit_w else None
    kk = pl.program_id(1)

    @pl.when(kk == 0)
    def _():
        o_ref[...] = res_ref[...]

    o_ref[...] += jnp.dot(h_ref[...], _weight_tile(w_ref, wb_ref), preferred_element_type=F32)

    @pl.when(kk == pl.num_programs(1) - 1)
    def _():
        o_ref[...] = _rmsnorm_rows(o_ref[...], g_ref[...])


def _mlp_down_final(h, w, res, g, *, tm, tk, layer=None):
    m, k = h.shape
    n = w.shape[-1]
    emit_w = layer is not None
    assert m % tm == 0 and k % tk == 0 and (not emit_w or m == tm)
    w_spec, wb_spec, w_dtype = _weight_specs(w, layer, (tk, n), lambda i, j: (j, 0))
    out_shape = [jax.ShapeDtypeStruct((m, n), F32)]
    out_specs = [pl.BlockSpec((tm, n), lambda i, j: (i, 0))]
    pipelined = [((tm, tk), BF16), ((tk, n), w_dtype if emit_w else BF16), ((tm, n), F32), ((tm, n), F32)]
    if emit_w:
        out_shape.append(jax.ShapeDtypeStruct((k, n), BF16))
        out_specs.append(wb_spec)
        pipelined.append(((tk, n), BF16))
    return pl.pallas_call(
        functools.partial(_mlp_down_kernel, emit_w=emit_w),
        out_shape=tuple(out_shape),
        grid=(m // tm, k // tk),
        in_specs=[
            pl.BlockSpec((tm, tk), lambda i, j: (i, j)),
            w_spec,
            pl.BlockSpec((tm, n), lambda i, j: (i, 0)),
            pl.BlockSpec((1, n), lambda i, j: (0, 0)),
        ],
        out_specs=tuple(out_specs),
        compiler_params=pltpu.CompilerParams(
            dimension_semantics=("parallel", "arbitrary"),
            vmem_limit_bytes=_vmem_limit(pipelined, [((tk, n), BF16)] if emit_w else [])),
        name="mlp_down",
    )(h, w, res, g)


def _lru_gates(xh, wa_half, wx_half, ba_half, bx_half, hsp):
    xb = xh.astype(BF16)
    tr = jnp.tanh(jnp.dot(xb, wa_half, preferred_element_type=F32) + ba_half)
    ti = jnp.tanh(jnp.dot(xb, wx_half, preferred_element_type=F32) + bx_half)
    nla = tr * hsp + hsp
    a = jnp.exp(-nla)
    q = jnp.tanh(nla) * (1.0 + a * a)
    mult = jnp.where(q > 0.0, q * lax.rsqrt(q), 0.0)
    return a, mult, 0.5 * ti + 0.5


def _scan_rows(a, b, h_prev):
    t, hd = a.shape
    g = t // SUBLANES
    a3 = a.reshape(g, SUBLANES, hd)
    b3 = b.reshape(g, SUBLANES, hd)
    sub = lax.broadcasted_iota(jnp.int32, (g, SUBLANES, hd), 1)
    d = 1
    while d < SUBLANES:
        keep = sub >= d
        a_sh = jnp.where(keep, pltpu.roll(a3, d, axis=1), 1.0)
        b_sh = jnp.where(keep, pltpu.roll(b3, d, axis=1), 0.0)
        b3 = a3 * b_sh + b3
        a3 = a3 * a_sh
        d *= 2
    tiles = []
    h = h_prev
    for k in range(g):
        hk = a3[k] * h + b3[k]
        tiles.append(hk)
        h = hk[SUBLANES - 1:SUBLANES, :]
    return jnp.concatenate(tiles, axis=0), h


def _lru_seq_kernel(gate_ref, x_ref, cw_ref, cb_ref, wa_ref, wx_ref, ba_ref, bx_ref, lam_ref, g_ref,
                    h0_ref, tail0_ref, *rest, pad, reset_first, n_chunks, n_live, fuse_out):
    if fuse_out:
        wo_ref, o_ref, hfin_ref, tailfin_ref, xe_ref, hc_ref, y_ref, yn_ref = rest
    else:
        o_ref, hfin_ref, tailfin_ref, xe_ref, hc_ref, y_ref = rest
    step = pl.program_id(0)
    live = step < n_live
    c = lax.rem(jnp.minimum(step, n_live - 1), n_chunks)
    t, dl = x_ref.shape
    nh, hd = wa_ref.shape[0], wa_ref.shape[1]
    keep = (lambda new, old: jnp.where(live, new, old)) if fuse_out else (lambda new, old: new)

    @pl.when(jnp.logical_and(c == 0, live))
    def _():
        xe_ref[0:HALO, :] = tail0_ref[...]
        hc_ref[...] = h0_ref[...]

    if fuse_out:
        @pl.when(step == 0)
        def _():
            yn_ref[...] = jnp.zeros_like(yn_ref)

    xe_ref[HALO:HALO + t, :] = x_ref[...]
    grow = c * t + lax.broadcasted_iota(jnp.int32, (t, hd), 0)
    ssq = jnp.zeros((t, hd), F32)
    if fuse_out:
        n_pieces = max(1, min(nh, o_ref.shape[1] // MXU_COLS))
        while nh % n_pieces or o_ref.shape[1] % n_pieces:
            n_pieces -= 1
        heads_per_piece, piece = nh // n_pieces, o_ref.shape[1] // n_pieces
    for h in range(nh):
        sl = slice(h * hd, (h + 1) * hd)
        ext = [xe_ref[HALO - 3 + k:HALO - 3 + k + t, sl] for k in range(CONV_TAPS)]
        xh = _conv_taps(ext, 0, cw_ref, cb_ref, sl)
        hsp = (0.5 * LRU_C) * jax.nn.softplus(-lam_ref[:, sl])
        a, mult, i = _lru_gates(xh, wa_ref[h], wx_ref[h], ba_ref[:, sl], bx_ref[:, sl], hsp)
        if reset_first:
            mult = jnp.where(grow == pad, 1.0, mult)
        b = mult * i * xh
        if pad:
            a = jnp.where(grow >= pad, a, 1.0)
            b = jnp.where(grow >= pad, b, 0.0)
        h_prev = hc_ref[:, sl]
        hs, h_last = _scan_rows(a, b, h_prev)
        hc_ref[:, sl] = keep(h_last, h_prev)
        y = hs * gate_ref[:, sl]
        y_ref[:, sl] = y
        ssq = ssq + y * y
        if fuse_out and (h + 1) % heads_per_piece == 0:
            k = (h + 1) // heads_per_piece - 1
            psl = slice(k * piece, (k + 1) * piece)
            o_ref[:, psl] = jnp.dot(yn_ref[...], wo_ref[:, psl], preferred_element_type=F32)
    scale = lax.rsqrt(jnp.sum(ssq, axis=-1, keepdims=True) / dl + EPS)
    yn = (y_ref[...] * scale * g_ref[...]).astype(BF16)
    if fuse_out:
        yn_ref[...] = yn
    else:
        o_ref[...] = yn
    xe_ref[0:HALO, :] = keep(xe_ref[t:t + HALO, :], xe_ref[0:HALO, :])
    hfin_ref[0] = hc_ref[...]
    tailfin_ref[0] = xe_ref[HALO - 3:HALO, :]


def _lru_seq(proj, params, h0, tail0, *, n_seq, seq_len, t, row0, pad=0, reset_first=False, wo=None):
    cw, cb, wa, wx, ba, bx, lam, g = params
    dl = cw.shape[1]
    nh, hd = wa.shape[0], wa.shape[1]
    assert seq_len % t == 0 and row0 % t == 0 and t % SUBLANES == 0
    n_chunks = seq_len // t
    n_live = n_seq * n_chunks
    blk0 = row0 // t
    fuse_out = wo is not None
    chunk = (lambda s: jnp.minimum(s, n_live - 1)) if fuse_out else (lambda s: s)
    out_chunk = (lambda s: jnp.maximum(s - 1, 0)) if fuse_out else (lambda s: s)
    const2 = lambda s: (0, 0)
    const3 = lambda s: (0, 0, 0)
    per_seq = lambda s: (chunk(s) // n_chunks, 0, 0)
    dout, out_dtype = (wo.shape[1], F32) if fuse_out else (dl, BF16)
    pipelined = [((t, dl), F32), ((t, dl), F32), ((t, dout), out_dtype)]
    resident = [((t + HALO, dl), F32), ((t, dl), F32), ((4 * nh, hd, hd), BF16)]
    scratch = [pltpu.VMEM((t + HALO, dl), F32), pltpu.VMEM((1, dl), F32), pltpu.VMEM((t, dl), F32)]
    extra_specs, extra_args = [], []
    if fuse_out:
        extra_specs.append(pl.BlockSpec((dl, dout), const2, pipeline_mode=pl.Buffered(1)))
        extra_args.append(wo)
        resident += [((dl, dout), BF16), ((t, dl), BF16)]
        scratch.append(pltpu.VMEM((t, dl), BF16))
    kern = functools.partial(_lru_seq_kernel, pad=pad, reset_first=reset_first, n_chunks=n_chunks, n_live=n_live,
                             fuse_out=fuse_out)
    return pl.pallas_call(
        kern,
        out_shape=(jax.ShapeDtypeStruct((n_seq * seq_len, dout), out_dtype),
                   jax.ShapeDtypeStruct((n_seq, 1, dl), F32),
                   jax.ShapeDtypeStruct((n_seq, CONV_TAPS - 1, dl), F32)),
        grid=(n_live + 1 if fuse_out else n_live,),
        in_specs=[
            pl.BlockSpec((t, dl), lambda s: (blk0 + chunk(s), 0)),
            pl.BlockSpec((t, dl), lambda s: (blk0 + chunk(s), 1)),
            pl.BlockSpec((CONV_TAPS, dl), const2),
            pl.BlockSpec((1, dl), const2),
            pl.BlockSpec((nh, hd, hd), const3),
            pl.BlockSpec((nh, hd, hd), const3),
            pl.BlockSpec((1, dl), const2),
            pl.BlockSpec((1, dl), const2),
            pl.BlockSpec((1, dl), const2),
            pl.BlockSpec((1, dl), const2),
            pl.BlockSpec((1, dl), const2),
            pl.BlockSpec((HALO, dl), const2),
        ] + extra_specs,
        out_specs=(
            pl.BlockSpec((t, dout), lambda s: (out_chunk(s), 0)),
            pl.BlockSpec((1, 1, dl), per_seq),
            pl.BlockSpec((1, CONV_TAPS - 1, dl), per_seq),
        ),
        scratch_shapes=scratch,
        compiler_params=pltpu.CompilerParams(
            dimension_semantics=("arbitrary",),
            vmem_limit_bytes=_vmem_limit(pipelined, resident),
            flags=MIXER_SCHED_FLAGS if fuse_out else None),
        name="lru_seq",
    )(proj, proj, cw, cb, wa, wx, ba, bx, lam, g, h0, tail0, *extra_args)


def _lru_slab_kernel(gate_ref, x_ref, cw_ref, cb_ref, wa_ref, wx_ref, ba_ref, bx_ref, lam_ref, g_ref,
                     h0_ref, tail_ref, o_ref, hfin_ref, tailfin_ref, y_ref):
    ls, bs, dl = x_ref.shape
    nh, hd = wa_ref.shape[0], wa_ref.shape[1]
    ntail = CONV_TAPS - 1
    for h in range(nh):
        sl = slice(h * hd, (h + 1) * hd)
        ext = [tail_ref[:, k * dl + h * hd:k * dl + (h + 1) * hd] for k in range(ntail)]
        ext += [x_ref[s, :, sl] for s in range(ls)]
        hsp = (0.5 * LRU_C) * jax.nn.softplus(-lam_ref[:, sl])
        hcur = h0_ref[:, sl]
        for s in range(ls):
            xh = _conv_taps(ext, s, cw_ref, cb_ref, sl)
            a, mult, i = _lru_gates(xh, wa_ref[h], wx_ref[h], ba_ref[:, sl], bx_ref[:, sl], hsp)
            hcur = a * hcur + mult * i * xh
            y_ref[s, :, sl] = hcur * gate_ref[s, :, sl]
        hfin_ref[:, sl] = hcur
        for k in range(ntail):
            tailfin_ref[:, k * dl + h * hd:k * dl + (h + 1) * hd] = ext[ls + k]
    for s in range(ls):
        y = y_ref[s]
        scale = lax.rsqrt(jnp.mean(y * y, axis=-1, keepdims=True) + EPS)
        o_ref[s] = (y * scale * g_ref[...]).astype(o_ref.dtype)


def _lru_slab(proj3, params, h0, tail, *, ls):
    cw, cb, wa, wx, ba, bx, lam, g = params
    bs = proj3.shape[1]
    dl = cw.shape[1]
    nh, hd = wa.shape[0], wa.shape[1]
    ntail = CONV_TAPS - 1
    c2 = lambda i: (0, 0)
    c3 = lambda i: (0, 0, 0)
    pipelined = [((ls, bs, dl), F32)] * 2 + [((ls, bs, dl), BF16)] + [((bs, (2 * ntail + 2) * dl), F32)]
    return pl.pallas_call(
        _lru_slab_kernel,
        out_shape=(jax.ShapeDtypeStruct((ls, bs, dl), BF16),
                   jax.ShapeDtypeStruct((bs, dl), F32),
                   jax.ShapeDtypeStruct((bs, ntail * dl), F32)),
        grid=(1,),
        in_specs=[
            pl.BlockSpec((ls, bs, dl), lambda i: (0, 0, 0)),
            pl.BlockSpec((ls, bs, dl), lambda i: (0, 0, 1)),
            pl.BlockSpec((CONV_TAPS, dl), c2), pl.BlockSpec((1, dl), c2),
            pl.BlockSpec((nh, hd, hd), c3), pl.BlockSpec((nh, hd, hd), c3),
            pl.BlockSpec((1, dl), c2), pl.BlockSpec((1, dl), c2), pl.BlockSpec((1, dl), c2), pl.BlockSpec((1, dl), c2),
            pl.BlockSpec((bs, dl), c2), pl.BlockSpec((bs, ntail * dl), c2),
        ],
        out_specs=(pl.BlockSpec((ls, bs, dl), c3), pl.BlockSpec((bs, dl), c2), pl.BlockSpec((bs, ntail * dl), c2)),
        scratch_shapes=[pltpu.VMEM((ls, bs, dl), F32)],
        compiler_params=pltpu.CompilerParams(
            dimension_semantics=("arbitrary",),
            vmem_limit_bytes=_vmem_limit(pipelined, [((ls, bs, dl), F32)])),
        name="lru_slab",
    )(proj3, proj3, cw, cb, wa, wx, ba, bx, lam, g, h0, tail)


def _head_lane_mask(n_heads):
    return lax.broadcasted_iota(jnp.int32, (1, HEAD_PAD), 1) < n_heads


def _gated_group_norm(y, z_act, g):
    yg = y * z_act
    scale = lax.rsqrt(jnp.mean(yg * yg, axis=-1, keepdims=True) + EPS)
    return yg * scale * g


def _ssd_seq_kernel(z_ref, xbc_ref, dt_ref, cw_ref, cb_ref, dtb_ref, alog_ref, dskip_ref, g_ref, e3_ref,
                    s0_ref, tail0_ref, *rest, pad, n_heads, n_groups, n_chunks, n_live, fuse_out):
    if fuse_out:
        wo_ref, part_ref, res_ref = rest[:3]
        o_ref, sfin_ref, tailfin_ref, xe_ref, xc_ref, ex_ref, y_ref, s_ref, yn_ref, ynp_ref = rest[3:]
    else:
        o_ref, sfin_ref, tailfin_ref, xe_ref, xc_ref, ex_ref, y_ref, s_ref = rest
        yn_ref = o_ref
    step = pl.program_id(0)
    live = step < n_live
    c = lax.rem(jnp.minimum(step, n_live - 1), n_chunks)
    keep = (lambda new, old: jnp.where(live, new, old)) if fuse_out else (lambda new, old: new)
    q, ds = z_ref.shape
    dc = xbc_ref.shape[1]
    gn = (dc - ds) // 2
    n = gn // n_groups
    p = ds // n_heads
    r = n_heads // n_groups
    gw = ds // n_groups
    hpb = LANES // p

    @pl.when(jnp.logical_and(c == 0, live))
    def _():
        xe_ref[0:HALO, :] = tail0_ref[...]
        s_ref[...] = s0_ref[...]

    if fuse_out:
        @pl.when(step == 0)
        def _():
            ynp_ref[...] = jnp.zeros_like(ynp_ref)

    xe_ref[HALO:HALO + q, :] = xbc_ref[...]
    valid = (c * q + lax.broadcasted_iota(jnp.int32, (q, 1), 0)) >= pad

    cblk = 512 if dc % 512 == 0 else LANES
    for j in range(dc // cblk):
        sl = slice(j * cblk, (j + 1) * cblk)
        ext = [xe_ref[HALO - 3 + k:HALO - 3 + k + q, sl] for k in range(CONV_TAPS)]
        v = _silu(_conv_taps(ext, 0, cw_ref, cb_ref, sl))
        if pad and (j + 1) * cblk <= ds:
            v = jnp.where(valid, v, 0.0)
        xc_ref[:, sl] = v

    dtv = jnp.where(_head_lane_mask(n_heads), jax.nn.softplus(dt_ref[...] + dtb_ref[...]), 0.0)
    if pad:
        dtv = jnp.where(valid, dtv, 0.0)
    da = dtv * (-jnp.exp(alog_ref[...]))
    ri = lax.broadcasted_iota(jnp.int32, (q, q), 0)
    ci = lax.broadcasted_iota(jnp.int32, (q, q), 1)
    causal = ci <= ri
    tri = jnp.where(causal, 1.0, 0.0).astype(BF16)
    ac3 = jnp.dot(tri, _split3(da), preferred_element_type=F32)
    acum = ac3[:, 0:HEAD_PAD] + ac3[:, HEAD_PAD:2 * HEAD_PAD] + ac3[:, 2 * HEAD_PAD:3 * HEAD_PAD]
    alast = acum[q - 1:q, :]
    eac = jnp.exp(acum)
    wend = jnp.exp(alast - acum) * dtv
    cdec = jnp.broadcast_to(jnp.exp(alast), (SUBLANES, HEAD_PAD))
    ex_ref[...] = _expand_heads(jnp.concatenate([eac, wend, cdec], axis=0), e3_ref)
    acum_t = acum.T
    dt_t = dtv.T
    lane = lax.broadcasted_iota(jnp.int32, (q, LANES), 1)

    for g in range(n_groups):
        gsl = slice(g * gw, (g + 1) * gw)
        bg = xc_ref[:, ds + g * n:ds + (g + 1) * n].astype(BF16)
        cg = xc_ref[:, ds + gn + g * n:ds + gn + (g + 1) * n].astype(BF16)
        cbm = lax.dot_general(cg, bg, (((1,), (1,)), ((), ())), preferred_element_type=F32)
        yoff = jnp.dot(cg, s_ref[:, gsl].astype(BF16), preferred_element_type=F32)
        for k in range(gw // LANES):
            lsl = slice(g * gw + k * LANES, g * gw + (k + 1) * LANES)
            xs = xc_ref[:, lsl]
            ms = []
            xparts = []
            for u in range(hpb):
                h = g * r + k * hpb + u
                seg = acum[:, h:h + 1] - acum_t[h:h + 1, :]
                lm = jnp.where(causal, jnp.exp(seg), 0.0) * dt_t[h:h + 1, :]
                ms.append((cbm * lm).astype(BF16))
                inhead = (lane >= u * p) & (lane < (u + 1) * p)
                xparts.append(jnp.where(inhead, xs, 0.0).astype(BF16))
            ydiag = jnp.dot(jnp.concatenate(ms, axis=1), jnp.concatenate(xparts, axis=0),
                            preferred_element_type=F32)
            y = ydiag + yoff[:, k * LANES:(k + 1) * LANES] * ex_ref[0:q, lsl]
            y_ref[:, lsl] = y + dskip_ref[:, lsl] * xs
        yn_ref[:, gsl] = _gated_group_norm(y_ref[:, gsl], z_ref[:, gsl], g_ref[:, gsl]).astype(BF16)
        xw = (xc_ref[:, gsl] * ex_ref[q:2 * q, gsl]).astype(BF16)
        upd = lax.dot_general(bg, xw, (((0,), (0,)), ((), ())), preferred_element_type=F32)
        s_old = s_ref[:, gsl]
        s_ref[:, gsl] = keep(ex_ref[2 * q:2 * q + 1, gsl] * s_old + upd, s_old)
        if fuse_out:
            piece = o_ref.shape[1] // n_groups
            psl = slice(g * piece, (g + 1) * piece)
            mix = part_ref[:, psl] + jnp.dot(ynp_ref[...], wo_ref[:, psl], preferred_element_type=F32)
            o_ref[:, psl] = res_ref[:, psl] + mix

    if fuse_out:
        ynp_ref[...] = yn_ref[...]
    xe_ref[0:HALO, :] = keep(xe_ref[q:q + HALO, :], xe_ref[0:HALO, :])
    sfin_ref[0] = s_ref[...]
    tailfin_ref[0] = xe_ref[HALO - 3:HALO, :]


def _ssd_seq(proj, dt, params, s0, tail0, *, n_seq, seq_len, row0, n_heads, n_groups, dl, pad=0, fuse=None):
    cw, cb, dtb, alog, dskip, g, e3 = params
    dc = cw.shape[1]
    ds = dskip.shape[1]
    n = s0.shape[0]
    q = SSD_CHUNK
    assert seq_len % q == 0 and row0 % q == 0
    assert (2 * dl) % ds == 0 and (2 * dl + ds) % dc == 0 and LANES % (ds // n_heads) == 0
    n_chunks = seq_len // q
    n_live = n_seq * n_chunks
    blk0 = row0 // q
    fuse_out = fuse is not None
    chunk = (lambda s: jnp.minimum(s, n_live - 1)) if fuse_out else (lambda s: s)
    rows = lambda s: blk0 + chunk(s)
    const2 = lambda s: (0, 0)
    out_rows = (lambda s: (jnp.maximum(s - 1, 0), 0)) if fuse_out else (lambda s: (s, 0))
    per_seq = lambda s: (chunk(s) // n_chunks, 0, 0)
    dout, out_dtype = (fuse[0].shape[1], F32) if fuse_out else (ds, BF16)
    pipelined = [((q, ds), F32), ((q, dc), F32), ((q, HEAD_PAD), F32), ((q, dout), out_dtype)]
    resident = [((q + HALO, dc), F32), ((q, dc), F32), ((2 * q + SUBLANES, ds), F32), ((q, ds), F32),
                ((3 * n, ds), F32), ((6 * HEAD_PAD, ds), BF16)]
    scratch = [pltpu.VMEM((q + HALO, dc), F32), pltpu.VMEM((q, dc), F32),
               pltpu.VMEM((2 * q + SUBLANES, ds), F32), pltpu.VMEM((q, ds), F32), pltpu.VMEM((n, ds), F32)]
    extra_specs, extra_args = [], []
    if fuse_out:
        extra_specs = [pl.BlockSpec((ds, dout), const2, pipeline_mode=pl.Buffered(1)),
                       pl.BlockSpec((q, dout), out_rows), pl.BlockSpec((q, dout), out_rows)]
        extra_args = list(fuse)
        pipelined += [((q, dout), F32)] * 2
        resident += [((ds, dout), BF16), ((2 * q, ds), BF16)]
        scratch += [pltpu.VMEM((q, ds), BF16), pltpu.VMEM((q, ds), BF16)]
    kern = functools.partial(_ssd_seq_kernel, pad=pad, n_heads=n_heads, n_groups=n_groups, n_chunks=n_chunks,
                             n_live=n_live, fuse_out=fuse_out)
    return pl.pallas_call(
        kern,
        out_shape=(jax.ShapeDtypeStruct((n_seq * seq_len, dout), out_dtype),
                   jax.ShapeDtypeStruct((n_seq, n, ds), F32),
                   jax.ShapeDtypeStruct((n_seq, CONV_TAPS - 1, dc), F32)),
        grid=(n_live + 1 if fuse_out else n_live,),
        in_specs=[
            pl.BlockSpec((q, ds), lambda s: (rows(s), (2 * dl) // ds)),
            pl.BlockSpec((q, dc), lambda s: (rows(s), (2 * dl + ds) // dc)),
            pl.BlockSpec((q, HEAD_PAD), lambda s: (rows(s), 0)),
            pl.BlockSpec((CONV_TAPS, dc), const2),
            pl.BlockSpec((1, dc), const2),
            pl.BlockSpec((1, HEAD_PAD), const2),
            pl.BlockSpec((1, HEAD_PAD), const2),
            pl.BlockSpec((1, ds), const2),
            pl.BlockSpec((1, ds), const2),
            pl.BlockSpec((3 * HEAD_PAD, ds), const2),
            pl.BlockSpec((n, ds), const2),
            pl.BlockSpec((HALO, dc), const2),
        ] + extra_specs,
        out_specs=(
            pl.BlockSpec((q, dout), out_rows),
            pl.BlockSpec((1, n, ds), per_seq),
            pl.BlockSpec((1, CONV_TAPS - 1, dc), per_seq),
        ),
        scratch_shapes=scratch,
        compiler_params=pltpu.CompilerParams(
            dimension_semantics=("arbitrary",),
            vmem_limit_bytes=_vmem_limit(pipelined, resident),
            flags=MIXER_SCHED_FLAGS if fuse_out else None),
        name="ssd_seq",
    )(proj, proj, dt, cw, cb, dtb, alog, dskip, g, e3, s0, tail0, *extra_args)


def _ssd_slab_pre_kernel(xbc_ref, dt_ref, tail_ref, cw_ref, cb_ref, dtb_ref, alog_ref, dskip_ref, e3_ref,
                         ypart_ref, eace_ref, c_ref, b_ref, xw_ref, cdec_ref, tailfin_ref, xc_ref,
                         *, n_heads, n_groups):
    ls, bs, dc = xbc_ref.shape
    ds = dskip_ref.shape[1]
    gn = (dc - ds) // 2
    n = gn // n_groups
    r = n_heads // n_groups
    ntail = CONV_TAPS - 1

    cblk = 512 if dc % 512 == 0 else LANES
    for j in range(dc // cblk):
        sl = slice(j * cblk, (j + 1) * cblk)
        ext = [tail_ref[:, k * dc + j * cblk:k * dc + (j + 1) * cblk] for k in range(ntail)]
        ext += [xbc_ref[s, :, sl] for s in range(ls)]
        for s in range(ls):
            xc_ref[s, :, sl] = _silu(_conv_taps(ext, s, cw_ref, cb_ref, sl))
        for k in range(ntail):
            tailfin_ref[:, k * dc + j * cblk:k * dc + (j + 1) * cblk] = ext[ls + k]

    hmask = _head_lane_mask(n_heads)
    a_neg = -jnp.exp(alog_ref[...])
    dtv, acum = [], []
    run = jnp.zeros((bs, HEAD_PAD), F32)
    for s in range(ls):
        d = jnp.where(hmask, jax.nn.softplus(dt_ref[s] + dtb_ref[...]), 0.0)
        run = run + d * a_neg
        dtv.append(d)
        acum.append(run)
    alast = acum[ls - 1]
    cdec_ref[...] = jnp.exp(alast)
    head_group = lax.broadcasted_iota(jnp.int32, (1, HEAD_PAD), 1) // r

    for s in range(ls):
        eace_ref[s] = _expand_heads(jnp.exp(acum[s]), e3_ref)
        wend_e = _expand_heads(jnp.exp(alast - acum[s]) * dtv[s], e3_ref)
        xw_ref[:, s * ds:(s + 1) * ds] = xc_ref[s, :, 0:ds] * wend_e
        b_ref[:, s * gn:(s + 1) * gn] = xc_ref[s, :, ds:ds + gn]
        c_ref[:, s * gn:(s + 1) * gn] = xc_ref[s, :, ds + gn:ds + 2 * gn]
        ypart = dskip_ref[...] * xc_ref[s, :, 0:ds]
        for j in range(s + 1):
            cbh = jnp.zeros((bs, HEAD_PAD), F32)
            for g in range(n_groups):
                cs = xc_ref[s, :, ds + gn + g * n:ds + gn + (g + 1) * n]
                bj = xc_ref[j, :, ds + g * n:ds + (g + 1) * n]
                cbg = jnp.sum(cs * bj, axis=-1, keepdims=True)
                cbh = cbh + jnp.where(head_group == g, cbg, 0.0)
            coef = cbh * (jnp.exp(acum[s] - acum[j]) * dtv[j])
            ypart = ypart + _expand_heads(coef, e3_ref) * xc_ref[j, :, 0:ds]
        ypart_ref[s] = ypart
    for s in range(ls, SLAB_ROWS):
        xw_ref[:, s * ds:(s + 1) * ds] = jnp.zeros((bs, ds), F32)
        b_ref[:, s * gn:(s + 1) * gn] = jnp.zeros((bs, gn), F32)
        c_ref[:, s * gn:(s + 1) * gn] = jnp.zeros((bs, gn), F32)


def _ssd_slab_pre(proj3, dt3, tail, params, *, ls, n_heads, n_groups, dl):
    cw, cb, dtb, alog, dskip, _, e3 = params
    bs = proj3.shape[1]
    dc = cw.shape[1]
    ds = dskip.shape[1]
    gn = (dc - ds) // 2
    ntail = CONV_TAPS - 1
    assert ls <= SLAB_ROWS and (2 * dl + ds) % dc == 0
    c2 = lambda i: (0, 0)
    c3 = lambda i: (0, 0, 0)
    pipelined = [((ls, bs, dc), F32), ((ls, bs, HEAD_PAD), F32), ((bs, 2 * ntail * dc), F32),
                 ((2 * ls, bs, ds), F32), ((bs, SLAB_ROWS * (2 * gn + ds)), F32), ((3 * HEAD_PAD, ds), BF16)]
    kern = functools.partial(_ssd_slab_pre_kernel, n_heads=n_heads, n_groups=n_groups)
    return pl.pallas_call(
        kern,
        out_shape=(jax.ShapeDtypeStruct((ls, bs, ds), F32),
                   jax.ShapeDtypeStruct((ls, bs, ds), F32),
                   jax.ShapeDtypeStruct((bs, SLAB_ROWS * gn), F32),
                   jax.ShapeDtypeStruct((bs, SLAB_ROWS * gn), F32),
                   jax.ShapeDtypeStruct((bs, SLAB_ROWS * ds), F32),
                   jax.ShapeDtypeStruct((bs, HEAD_PAD), F32),
                   jax.ShapeDtypeStruct((bs, ntail * dc), F32)),
        grid=(1,),
        in_specs=[
            pl.BlockSpec((ls, bs, dc), lambda i: (0, 0, (2 * dl + ds) // dc)),
            pl.BlockSpec((ls, bs, HEAD_PAD), c3),
            pl.BlockSpec((bs, ntail * dc), c2),
            pl.BlockSpec((CONV_TAPS, dc), c2), pl.BlockSpec((1, dc), c2),
            pl.BlockSpec((1, HEAD_PAD), c2), pl.BlockSpec((1, HEAD_PAD), c2),
            pl.BlockSpec((1, ds), c2), pl.BlockSpec((3 * HEAD_PAD, ds), c2),
        ],
        out_specs=(pl.BlockSpec((ls, bs, ds), c3), pl.BlockSpec((ls, bs, ds), c3),
                   pl.BlockSpec((bs, SLAB_ROWS * gn), c2), pl.BlockSpec((bs, SLAB_ROWS * gn), c2),
                   pl.BlockSpec((bs, SLAB_ROWS * ds), c2), pl.BlockSpec((bs, HEAD_PAD), c2),
                   pl.BlockSpec((bs, ntail * dc), c2)),
        scratch_shapes=[pltpu.VMEM((ls, bs, dc), F32)],
        compiler_params=pltpu.CompilerParams(
            dimension_semantics=("arbitrary",),
            vmem_limit_bytes=_vmem_limit(pipelined, [((ls, bs, dc), F32)])),
        name="ssd_slab_pre",
    )(proj3, dt3, tail, cw, cb, dtb, alog, dskip, e3)


def _ssd_state_kernel(cdec_ref, s_ref, c_ref, b_ref, xw_ref, snew_ref, yoff_ref, *, n_heads, n_groups):
    i = pl.program_id(0)
    sb, hp, n = s_ref.shape
    p = hp // n_heads
    r = n_heads // n_groups
    gw = hp // n_groups
    for q in range(sb):
        for g in range(n_groups):
            gsl = slice(g * gw, (g + 1) * gw)
            sg = s_ref[q, gsl, :]
            cg = c_ref[q, :, g * n:(g + 1) * n].astype(BF16)
            yoff_ref[q, :, gsl] = lax.dot_general(cg, sg.astype(BF16), (((1,), (1,)), ((), ())),
                                                  preferred_element_type=F32)
            upd = lax.dot_general(xw_ref[q, :, gsl].astype(BF16), b_ref[q, :, g * n:(g + 1) * n].astype(BF16),
                                  (((0,), (0,)), ((), ())), preferred_element_type=F32)
            for u in range(r):
                h = g * r + u
                rows = slice(h * p, (h + 1) * p)
                snew_ref[q, rows, :] = cdec_ref[i * sb + q, h] * s_ref[q, rows, :] + upd[u * p:(u + 1) * p, :]


def _ssd_state(cdec, state, c_rows, b_rows, xw_rows, *, n_heads, n_groups):
    bs, hp, n = state.shape
    gn = c_rows.shape[2]
    sb = STATE_SEQS_PER_STEP if bs % STATE_SEQS_PER_STEP == 0 else 1
    per_seq = lambda i: (i, 0, 0)
    pipelined = [((sb, hp, n), F32)] * 2 + [((sb, SLAB_ROWS, gn), F32)] * 2 + [((sb, SLAB_ROWS, hp), F32)] * 2
    kern = functools.partial(_ssd_state_kernel, n_heads=n_heads, n_groups=n_groups)
    return pl.pallas_call(
        kern,
        out_shape=(jax.ShapeDtypeStruct((bs, hp, n), F32), jax.ShapeDtypeStruct((bs, SLAB_ROWS, hp), F32)),
        grid=(bs // sb,),
        in_specs=[
            pl.BlockSpec(memory_space=pltpu.SMEM),
            pl.BlockSpec((sb, hp, n), per_seq),
            pl.BlockSpec((sb, SLAB_ROWS, gn), per_seq),
            pl.BlockSpec((sb, SLAB_ROWS, gn), per_seq),
            pl.BlockSpec((sb, SLAB_ROWS, hp), per_seq),
        ],
        out_specs=(pl.BlockSpec((sb, hp, n), per_seq), pl.BlockSpec((sb, SLAB_ROWS, hp), per_seq)),
        compiler_params=pltpu.CompilerParams(
            dimension_semantics=("parallel",),
            vmem_limit_bytes=_vmem_limit(pipelined, [])),
        name="ssd_state",
    )(cdec, state, c_rows, b_rows, xw_rows)


def _ssd_slab_post_kernel(ypart_ref, eace_ref, yoff_ref, z_ref, g_ref, o_ref, *, n_groups):
    ls, bs, ds = ypart_ref.shape
    gw = ds // n_groups
    for s in range(ls):
        for g in range(n_groups):
            gsl = slice(g * gw, (g + 1) * gw)
            y = ypart_ref[s, :, gsl] + eace_ref[s, :, gsl] * yoff_ref[:, s * ds + g * gw:s * ds + (g + 1) * gw]
            o_ref[s, :, gsl] = _gated_group_norm(y, z_ref[s, :, gsl], g_ref[:, gsl]).astype(o_ref.dtype)


def _ssd_slab_post(ypart, eace, yoff, proj3, g, *, n_groups, dl):
    ls, bs, ds = ypart.shape
    assert (2 * dl) % ds == 0
    c2 = lambda i: (0, 0)
    c3 = lambda i: (0, 0, 0)
    pipelined = [((ls, bs, ds), F32)] * 3 + [((bs, SLAB_ROWS * ds), F32), ((ls, bs, ds), BF16)]
    kern = functools.partial(_ssd_slab_post_kernel, n_groups=n_groups)
    return pl.pallas_call(
        kern,
        out_shape=jax.ShapeDtypeStruct((ls, bs, ds), BF16),
        grid=(1,),
        in_specs=[
            pl.BlockSpec((ls, bs, ds), c3), pl.BlockSpec((ls, bs, ds), c3),
            pl.BlockSpec((bs, SLAB_ROWS * ds), c2),
            pl.BlockSpec((ls, bs, ds), lambda i: (0, 0, (2 * dl) // ds)),
            pl.BlockSpec((1, ds), c2),
        ],
        out_specs=pl.BlockSpec((ls, bs, ds), c3),
        compiler_params=pltpu.CompilerParams(
            dimension_semantics=("arbitrary",),
            vmem_limit_bytes=_vmem_limit(pipelined, [])),
        name="ssd_slab_post",
    )(ypart, eace, yoff, proj3, g)


def _head_expansion(n_heads, head_dim):
    rows = lax.broadcasted_iota(jnp.int32, (HEAD_PAD, n_heads * head_dim), 0)
    cols = lax.broadcasted_iota(jnp.int32, (HEAD_PAD, n_heads * head_dim), 1)
    e = (cols // head_dim == rows).astype(BF16)
    return jnp.concatenate([e, e, e], axis=0)


def _pad_lanes(v, width):
    return jnp.pad(v, ((0, 0), (0, width - v.shape[1])))


def _tail_block(tail):
    return jnp.pad(tail, ((HALO - tail.shape[0], 0), (0, 0)))


def _mlp(x1, g_mlp, w_up, w_down, g_final, *, layer=None):
    m = x1.shape[0]
    tiles = (1024, 512, 256, 128)
    tm = m if layer is not None else _pick_tile(m, tiles)
    up_tiles = tiles if layer is not None else (2048,) + tiles
    up = _mlp_up(x1, g_mlp, w_up, tm=tm, tn=_pick_tile(w_up.shape[-1], up_tiles), layer=layer)
    hid = up[0]
    down = _mlp_down_final(hid, w_down, x1, g_final, tm=tm, tk=_pick_tile(w_down.shape[-2], tiles), layer=layer)
    if layer is None:
        return down[0]
    return down[0], up[1], down[1]


def kernel(x_prompt, x_sample, state_lru_h, state_lru_conv, state_ssd, state_ssd_conv, meta_tokens, g_mix, w_in, conv_lru_w, conv_lru_b, lru_wa, lru_ba, lru_wx, lru_bx, lru_lambda, g_lru_out, conv_ssd_w, conv_ssd_b, dt_bias, a_log, d_skip, g_ssd_out, w_out, g_mlp, w_up, w_down, g_final):
    depth = w_in.shape[0]
    assert depth == 1, "single-layer step"
    l = 0
    bp, lp, d = x_prompt.shape
    bs, ls, _ = x_sample.shape
    n_meta = meta_tokens.shape[0]
    dl = state_lru_h.shape[-1]
    n_heads, p, n = state_ssd.shape[-3:]
    ds = n_heads * p
    dc = state_ssd_conv.shape[-1]
    gn = (dc - ds) // 2
    n_groups = gn // n
    nw = 2 * dl + ds + dc
    ntail = CONV_TAPS - 1
    q = SSD_CHUNK
    meta_pad = (-n_meta) % q
    assert n_heads <= HEAD_PAD and (bs * ls) % q == 0 and lp % q == 0 and q % bs == 0

    row = lambda v: v.reshape(1, -1).astype(F32)
    w_in_t = jnp.swapaxes(w_in, 1, 2)
    lru_params = (conv_lru_w[l], row(conv_lru_b[l]), (0.5 * lru_wa[l]).astype(BF16), (0.5 * lru_wx[l]).astype(BF16),
                  row(0.5 * lru_ba[l]), row(0.5 * lru_bx[l]), row(lru_lambda[l]), row(g_lru_out[l]))
    ssd_params = (conv_ssd_w[l], row(conv_ssd_b[l]), _pad_lanes(row(dt_bias[l]), HEAD_PAD),
                  _pad_lanes(row(a_log[l]), HEAD_PAD), row(jnp.repeat(d_skip[l], p)), row(g_ssd_out[l]),
                  _head_expansion(n_heads, p))
    g_mix_r, g_mlp_r, g_final_r = row(g_mix[l]), row(g_mlp[l]), row(g_final)

    xs_tm = x_sample.transpose(1, 0, 2).reshape(ls * bs, d)
    x_side = jnp.concatenate([xs_tm, jnp.zeros((meta_pad, d), F32), meta_tokens.astype(F32)], axis=0)
    xp_rows = x_prompt.reshape(bp * lp, d)

    tiles = (1024, 512, 256, 128)
    tn_in = next(t for t in tiles if nw % t == 0 and dl % t == 0 and ds % t == 0)
    act_cols = dict(gelu_cols=(0, dl), silu_cols=(2 * dl, 2 * dl + ds))
    proj_side, dt_side, w_in_b, w_dt_b = _in_proj(x_side, g_mix_r, w_in_t, nw=nw, n_dt=n_heads, tm=x_side.shape[0],
                                                  tn=tn_in, layer=l, **act_cols)

    proj_s3 = proj_side.reshape(-1, bs, nw)
    dt_s3 = dt_side.reshape(-1, bs, HEAD_PAD)
    lru_s, s_h, s_ltail = _lru_slab(proj_s3, lru_params, state_lru_h[l],
                                    state_lru_conv[l].reshape(bs, ntail * dl), ls=ls)
    ypart, eace, c_rows, b_rows, xw_rows, cdec, s_stail = _ssd_slab_pre(
        proj_s3, dt_s3, state_ssd_conv[l].reshape(bs, ntail * dc), ssd_params,
        ls=ls, n_heads=n_heads, n_groups=n_groups, dl=dl)
    s_new, yoff = _ssd_state(cdec, state_ssd[l].reshape(bs, ds, n),
                             c_rows.reshape(bs, SLAB_ROWS, gn), b_rows.reshape(bs, SLAB_ROWS, gn),
                             xw_rows.reshape(bs, SLAB_ROWS, ds), n_heads=n_heads, n_groups=n_groups)
    ssd_s = _ssd_slab_post(ypart, eace, yoff.reshape(bs, SLAB_ROWS * ds), proj_s3, ssd_params[5],
                           n_groups=n_groups, dl=dl)
    x1_s, w_out_lru, w_out_ssd = _out_proj(lru_s.reshape(ls * bs, dl), ssd_s.reshape(ls * bs, ds), w_out, l, xs_tm,
                                           tn=_pick_tile(d, (512, 256, 128)))
    y_s, w_up_b, w_down_b = _mlp(x1_s, g_mlp_r, w_up, w_down, g_final_r, layer=l)

    meta_row0 = ls * bs
    _, m_h, m_ltail = _lru_seq(proj_side, lru_params, jnp.zeros((1, dl), F32), jnp.zeros((HALO, dl), F32),
                               n_seq=1, seq_len=q, t=q, row0=meta_row0, pad=meta_pad, reset_first=True)
    _, m_s, m_stail = _ssd_seq(proj_side, dt_side, ssd_params, jnp.zeros((n, ds), F32), jnp.zeros((HALO, dc), F32),
                               n_seq=1, seq_len=q, row0=meta_row0, n_heads=n_heads, n_groups=n_groups, dl=dl,
                               pad=meta_pad)

    proj_p, dt_p = _in_proj(xp_rows, g_mix_r, w_in_b, w_dt_b, nw=nw, n_dt=n_heads, tm=_pick_tile(bp * lp, tiles),
                            tn=tn_in, **act_cols)
    part_p, p_h, p_ltail = _lru_seq(proj_p, lru_params, m_h[0], _tail_block(m_ltail[0]),
                                    n_seq=bp, seq_len=lp, t=_pick_tile(lp, (512, 256, 128)), row0=0, wo=w_out_lru)
    x1_p, p_s, p_stail = _ssd_seq(proj_p, dt_p, ssd_params, m_s[0], _tail_block(m_stail[0]),
                                  n_seq=bp, seq_len=lp, row0=0, n_heads=n_heads, n_groups=n_groups, dl=dl,
                                  fuse=(w_out_ssd, part_p, xp_rows))
    y_p = _mlp(x1_p, g_mlp_r, w_up_b, w_down_b, g_final_r)

    y_prompt = y_p.reshape(bp, lp, d)
    y_sample = y_s.reshape(ls, bs, d).transpose(1, 0, 2)
    p_lru_h = p_h.reshape(1, bp, dl)
    p_lru_conv = p_ltail.reshape(1, bp, ntail, dl)
    p_ssd = p_s.transpose(0, 2, 1).reshape(1, bp, n_heads, p, n)
    p_ssd_conv = p_stail.reshape(1, bp, ntail, dc)
    s_lru_h = s_h.reshape(1, bs, dl)
    s_lru_conv = s_ltail.reshape(1, bs, ntail, dl)
    s_ssd = s_new.reshape(1, bs, n_heads, p, n)
    s_ssd_conv = s_stail.reshape(1, bs, ntail, dc)
    return (y_prompt, y_sample, p_lru_h, p_lru_conv, p_ssd, p_ssd_conv, s_lru_h, s_lru_conv, s_ssd, s_ssd_conv)
```

```python
import functools

import jax
import jax.numpy as jnp
from jax import lax
from jax.experimental import pallas as pl
from jax.experimental.pallas import tpu as pltpu

F32 = jnp.float32
BF16 = jnp.bfloat16

EPS = 1e-6
LRU_C = 8.0
CONV_TAPS = 4

LANES = 128
SUBLANES = 8
MXU_COLS = 256
VMEM_BYTES_V7X = 64 * 1024 * 1024
VMEM_TEMP_BYTES = 10 * 1024 * 1024
VMEM_CEILING_BYTES = VMEM_BYTES_V7X - 6 * 1024 * 1024

HALO = SUBLANES
SSD_CHUNK = 128
SSD_CHUNKS_PER_STEP = 2
HEAD_PAD = LANES
SLAB_ROWS = SUBLANES
STATE_SEQS_PER_STEP = 4
MIXER_SCHED_FLAGS = None


def _nbytes(shape, dtype):
    n = 1
    for s in shape:
        n *= s
    return n * jnp.dtype(dtype).itemsize


def _vmem_limit(pipelined, resident):
    est = 2 * sum(_nbytes(s, d) for s, d in pipelined) + sum(_nbytes(s, d) for s, d in resident)
    return int(min(est + VMEM_TEMP_BYTES, VMEM_CEILING_BYTES))


def _pick_tile(m, prefs):
    for t in prefs:
        if m % t == 0:
            return t
    return m


def _silu(x):
    h = 0.5 * x
    return h * jnp.tanh(h) + h


def _split3(x):
    hi = x.astype(BF16)
    r1 = x - hi.astype(F32)
    mid = r1.astype(BF16)
    lo = (r1 - mid.astype(F32)).astype(BF16)
    return jnp.concatenate([hi, mid, lo], axis=1)


def _expand_heads(x, e3_ref):
    return jnp.dot(_split3(x), e3_ref[...], preferred_element_type=F32)


def _rmsnorm_rows(x, g):
    ms = jnp.mean(x * x, axis=-1, keepdims=True)
    return x * lax.rsqrt(ms + EPS) * g


def _conv_taps(ext, s, cw_ref, cb_ref, sl):
    v = cb_ref[:, sl] + ext[s] * cw_ref[0:1, sl]
    for k in range(1, CONV_TAPS):
        v = v + ext[s + k] * cw_ref[k:k + 1, sl]
    return v


def _weight_tile(w_ref, wb_ref):
    if wb_ref is None:
        return w_ref[...]
    w = w_ref[...].astype(BF16)
    wb_ref[...] = w
    return w


def _weight_specs(w, layer, blk, idx):
    if layer is None:
        return pl.BlockSpec(blk, idx), None, F32
    in_spec = pl.BlockSpec((None,) + blk, lambda i, j: (layer,) + idx(i, j))
    return in_spec, pl.BlockSpec(blk, idx), w.dtype


_NT = (((1,), (1,)), ((), ()))


def _in_proj_kernel(x_ref, g_ref, w_ref, wdt_ref, o_ref, dt_ref, *rest, gelu_tiles, silu_tiles, emit_w):
    if emit_w:
        wb_ref, wdtb_ref, xn_ref = rest
    else:
        wb_ref, wdtb_ref, xn_ref = None, None, rest[0]
    j = pl.program_id(1)

    @pl.when(j == 0)
    def _():
        xn = _rmsnorm_rows(x_ref[...], g_ref[...]).astype(BF16)
        xn_ref[...] = xn
        wdt = _weight_tile(wdt_ref, wdtb_ref)
        wdt = jnp.concatenate([wdt, jnp.zeros((HEAD_PAD - wdt.shape[0], wdt.shape[1]), BF16)], axis=0)
        dt_ref[...] = lax.dot_general(xn, wdt, _NT, preferred_element_type=F32)

    def tile():
        return lax.dot_general(xn_ref[...], _weight_tile(w_ref, wb_ref), _NT, preferred_element_type=F32)

    in_range = lambda r: jnp.logical_and(j >= r[0], j < r[1])
    is_gelu, is_silu = in_range(gelu_tiles), in_range(silu_tiles)

    @pl.when(is_gelu)
    def _():
        o_ref[...] = jax.nn.gelu(tile())

    @pl.when(is_silu)
    def _():
        o_ref[...] = _silu(tile())

    @pl.when(jnp.logical_not(jnp.logical_or(is_gelu, is_silu)))
    def _():
        o_ref[...] = tile()


def _in_proj(x, g, w_t, wdt_t=None, *, nw, n_dt, tm, tn, gelu_cols, silu_cols, layer=None):
    m, k = x.shape
    emit_w = layer is not None
    assert m % tm == 0 and nw % tn == 0 and all(c % tn == 0 for c in gelu_cols + silu_cols)
    assert n_dt % SUBLANES == 0 and nw % n_dt == 0 and (not emit_w or m == tm)
    out_shape = [jax.ShapeDtypeStruct((m, nw), F32), jax.ShapeDtypeStruct((m, HEAD_PAD), F32)]
    out_specs = [pl.BlockSpec((tm, tn), lambda i, j: (i, j)), pl.BlockSpec((tm, HEAD_PAD), lambda i, j: (i, 0))]
    pipelined = [((tm, k), F32), ((tn, k), w_t.dtype), ((n_dt, k), w_t.dtype), ((tm, tn), F32), ((tm, HEAD_PAD), F32)]
    if emit_w:
        w_spec = pl.BlockSpec((None, tn, k), lambda i, j: (layer, j, 0))
        wdt_spec = pl.BlockSpec((None, n_dt, k), lambda i, j: (layer, nw // n_dt, 0))
        wdt_t = w_t
        out_shape += [jax.ShapeDtypeStruct((nw, k), BF16), jax.ShapeDtypeStruct((n_dt, k), BF16)]
        out_specs += [pl.BlockSpec((tn, k), lambda i, j: (j, 0)), pl.BlockSpec((n_dt, k), lambda i, j: (0, 0))]
        pipelined += [((tn, k), BF16), ((n_dt, k), BF16)]
    else:
        w_spec = pl.BlockSpec((tn, k), lambda i, j: (j, 0))
        wdt_spec = pl.BlockSpec((n_dt, k), lambda i, j: (0, 0))
    kern = functools.partial(_in_proj_kernel, gelu_tiles=tuple(c // tn for c in gelu_cols),
                             silu_tiles=tuple(c // tn for c in silu_cols), emit_w=emit_w)
    return pl.pallas_call(
        kern,
        out_shape=tuple(out_shape),
        grid=(m // tm, nw // tn),
        in_specs=[
            pl.BlockSpec((tm, k), lambda i, j: (i, 0)),
            pl.BlockSpec((1, k), lambda i, j: (0, 0)),
            w_spec,
            wdt_spec,
        ],
        out_specs=tuple(out_specs),
        scratch_shapes=[pltpu.VMEM((tm, k), BF16)],
        compiler_params=pltpu.CompilerParams(
            dimension_semantics=("parallel", "arbitrary"),
            vmem_limit_bytes=_vmem_limit(pipelined, [((tm, k), BF16), ((tn, k), BF16)])),
        name="in_proj",
    )(x, g, w_t, wdt_t)


def _out_proj_kernel(a1_ref, a2_ref, w1_ref, w2_ref, res_ref, o_ref, wb1_ref, wb2_ref):
    acc = jnp.dot(a1_ref[...], _weight_tile(w1_ref, wb1_ref), preferred_element_type=F32)
    acc = acc + jnp.dot(a2_ref[...], _weight_tile(w2_ref, wb2_ref), preferred_element_type=F32)
    o_ref[...] = res_ref[...] + acc


def _out_proj(a1, a2, w, layer, res, *, tn):
    m, k1 = a1.shape
    k2 = a2.shape[1]
    n = w.shape[2]
    assert n % tn == 0 and k1 == k2 and w.shape[1] == k1 + k2
    pipelined = [((m, k1), BF16), ((m, k2), BF16), ((k1, tn), w.dtype), ((k2, tn), w.dtype), ((m, tn), F32),
                 ((m, tn), F32), ((k1, tn), BF16), ((k2, tn), BF16)]
    col = lambda j: (0, j)
    return pl.pallas_call(
        _out_proj_kernel,
        out_shape=(jax.ShapeDtypeStruct((m, n), F32), jax.ShapeDtypeStruct((k1, n), BF16),
                   jax.ShapeDtypeStruct((k2, n), BF16)),
        grid=(n // tn,),
        in_specs=[
            pl.BlockSpec((m, k1), lambda j: (0, 0)),
            pl.BlockSpec((m, k2), lambda j: (0, 0)),
            pl.BlockSpec((None, k1, tn), lambda j: (layer, 0, j)),
            pl.BlockSpec((None, k2, tn), lambda j: (layer, 1, j)),
            pl.BlockSpec((m, tn), col),
        ],
        out_specs=(pl.BlockSpec((m, tn), col), pl.BlockSpec((k1, tn), col), pl.BlockSpec((k2, tn), col)),
        compiler_params=pltpu.CompilerParams(
            dimension_semantics=("parallel",),
            vmem_limit_bytes=_vmem_limit(pipelined, [])),
        name="out_proj",
    )(a1, a2, w, w, res)


def _mlp_up_kernel(x_ref, g_ref, w_ref, o_ref, *rest, emit_w):
    wb_ref, xn_ref = rest if emit_w else (None, rest[0])

    @pl.when(pl.program_id(1) == 0)
    def _():
        xn_ref[...] = _rmsnorm_rows(x_ref[...], g_ref[...]).astype(BF16)

    acc = jnp.dot(xn_ref[...], _weight_tile(w_ref, wb_ref), preferred_element_type=F32)
    o_ref[...] = jnp.square(jnp.maximum(acc, 0.0)).astype(o_ref.dtype)


def _mlp_up(x, g, w, *, tm, tn, layer=None):
    m, k = x.shape
    n = w.shape[-1]
    emit_w = layer is not None
    assert m % tm == 0 and n % tn == 0 and (not emit_w or m == tm)
    w_spec, wb_spec, w_dtype = _weight_specs(w, layer, (k, tn), lambda i, j: (0, j))
    out_shape = [jax.ShapeDtypeStruct((m, n), BF16)]
    out_specs = [pl.BlockSpec((tm, tn), lambda i, j: (i, j))]
    pipelined = [((tm, k), F32), ((k, tn), w_dtype if emit_w else BF16), ((tm, tn), BF16)]
    if emit_w:
        out_shape.append(jax.ShapeDtypeStruct((k, n), BF16))
        out_specs.append(wb_spec)
        pipelined.append(((k, tn), BF16))
    return pl.pallas_call(
        functools.partial(_mlp_up_kernel, emit_w=emit_w),
        out_shape=tuple(out_shape),
        grid=(m // tm, n // tn),
        in_specs=[
            pl.BlockSpec((tm, k), lambda i, j: (i, 0)),
            pl.BlockSpec((1, k), lambda i, j: (0, 0)),
            w_spec,
        ],
        out_specs=tuple(out_specs),
        scratch_shapes=[pltpu.VMEM((tm, k), BF16)],
        compiler_params=pltpu.CompilerParams(
            dimension_semantics=("parallel", "arbitrary"),
            vmem_limit_bytes=_vmem_limit(pipelined, [((tm, k), BF16), ((k, tn), BF16)])),
        name="mlp_up",
    )(x, g, w)


def _mlp_down_kernel(h_ref, w_ref, res_ref, g_ref, o_ref, *rest, emit_w):
    wb_ref = rest[0] if emit_w else None
    kk = pl.program_id(1)

    @pl.when(kk == 0)
    def _():
        o_ref[...] = res_ref[...]

    o_ref[...] += jnp.dot(h_ref[...], _weight_tile(w_ref, wb_ref), preferred_element_type=F32)

    @pl.when(kk == pl.num_programs(1) - 1)
    def _():
        o_ref[...] = _rmsnorm_rows(o_ref[...], g_ref[...])


def _mlp_down_final(h, w, res, g, *, tm, tk, layer=None):
    m, k = h.shape
    n = w.shape[-1]
    emit_w = layer is not None
    assert m % tm == 0 and k % tk == 0 and (not emit_w or m == tm)
    w_spec, wb_spec, w_dtype = _weight_specs(w, layer, (tk, n), lambda i, j: (j, 0))
    out_shape = [jax.ShapeDtypeStruct((m, n), F32)]
    out_specs = [pl.BlockSpec((tm, n), lambda i, j: (i, 0))]
    pipelined = [((tm, tk), BF16), ((tk, n), w_dtype if emit_w else BF16), ((tm, n), F32), ((tm, n), F32)]
    if emit_w:
        out_shape.append(jax.ShapeDtypeStruct((k, n), BF16))
        out_specs.append(wb_spec)
        pipelined.append(((tk, n), BF16))
    return pl.pallas_call(
        functools.partial(_mlp_down_kernel, emit_w=emit_w),
        out_shape=tuple(out_shape),
        grid=(m // tm, k // tk),
        in_specs=[
            pl.BlockSpec((tm, tk), lambda i, j: (i, j)),
            w_spec,
            pl.BlockSpec((tm, n), lambda i, j: (i, 0)),
            pl.BlockSpec((1, n), lambda i, j: (0, 0)),
        ],
        out_specs=tuple(out_specs),
        compiler_params=pltpu.CompilerParams(
            dimension_semantics=("parallel", "arbitrary"),
            vmem_limit_bytes=_vmem_limit(pipelined, [((tk, n), BF16)] if emit_w else [])),
        name="mlp_down",
    )(h, w, res, g)


def _lru_gates(xh, wa_half, wx_half, ba_half, bx_half, hsp):
    xb = xh.astype(BF16)
    tr = jnp.tanh(jnp.dot(xb, wa_half, preferred_element_type=F32) + ba_half)
    ti = jnp.tanh(jnp.dot(xb, wx_half, preferred_element_type=F32) + bx_half)
    nla = tr * hsp + hsp
    a = jnp.exp(-nla)
    q = jnp.tanh(nla) * (1.0 + a * a)
    mult = jnp.where(q > 0.0, q * lax.rsqrt(q), 0.0)
    return a, mult, 0.5 * ti + 0.5


def _scan_rows(a, b, h_prev):
    t, hd = a.shape
    g = t // SUBLANES
    a3 = a.reshape(g, SUBLANES, hd)
    b3 = b.reshape(g, SUBLANES, hd)
    sub = lax.broadcasted_iota(jnp.int32, (g, SUBLANES, hd), 1)
    d = 1
    while d < SUBLANES:
        keep = sub >= d
        a_sh = jnp.where(keep, pltpu.roll(a3, d, axis=1), 1.0)
        b_sh = jnp.where(keep, pltpu.roll(b3, d, axis=1), 0.0)
        b3 = a3 * b_sh + b3
        a3 = a3 * a_sh
        d *= 2
    tiles = []
    h = h_prev
    for k in range(g):
        hk = a3[k] * h + b3[k]
        tiles.append(hk)
        h = hk[SUBLANES - 1:SUBLANES, :]
    return jnp.concatenate(tiles, axis=0), h


def _lru_seq_kernel(gate_ref, x_ref, cw_ref, cb_ref, wa_ref, wx_ref, ba_ref, bx_ref, lam_ref, g_ref,
                    h0_ref, tail0_ref, *rest, pad, reset_first, n_chunks, n_live, fuse_out):
    if fuse_out:
        wo_ref, o_ref, hfin_ref, tailfin_ref, xe_ref, hc_ref, y_ref, yn_ref = rest
    else:
        o_ref, hfin_ref, tailfin_ref, xe_ref, hc_ref, y_ref = rest
    step = pl.program_id(0)
    live = step < n_live
    c = lax.rem(jnp.minimum(step, n_live - 1), n_chunks)
    t, dl = x_ref.shape
    nh, hd = wa_ref.shape[0], wa_ref.shape[1]
    keep = (lambda new, old: jnp.where(live, new, old)) if fuse_out else (lambda new, old: new)

    @pl.when(jnp.logical_and(c == 0, live))
    def _():
        xe_ref[0:HALO, :] = tail0_ref[...]
        hc_ref[...] = h0_ref[...]

    if fuse_out:
        @pl.when(step == 0)
        def _():
            yn_ref[...] = jnp.zeros_like(yn_ref)

    xe_ref[HALO:HALO + t, :] = x_ref[...]
    grow = c * t + lax.broadcasted_iota(jnp.int32, (t, hd), 0)
    ssq = jnp.zeros((t, hd), F32)
    if fuse_out:
        n_pieces = max(1, min(nh, o_ref.shape[1] // MXU_COLS))
        while nh % n_pieces or o_ref.shape[1] % n_pieces:
            n_pieces -= 1
        heads_per_piece, piece = nh // n_pieces, o_ref.shape[1] // n_pieces
    for h in range(nh):
        sl = slice(h * hd, (h + 1) * hd)
        ext = [xe_ref[HALO - 3 + k:HALO - 3 + k + t, sl] for k in range(CONV_TAPS)]
        xh = _conv_taps(ext, 0, cw_ref, cb_ref, sl)
        hsp = (0.5 * LRU_C) * jax.nn.softplus(-lam_ref[:, sl])
        a, mult, i = _lru_gates(xh, wa_ref[h], wx_ref[h], ba_ref[:, sl], bx_ref[:, sl], hsp)
        if reset_first:
            mult = jnp.where(grow == pad, 1.0, mult)
        b = mult * i * xh
        if pad:
            a = jnp.where(grow >= pad, a, 1.0)
            b = jnp.where(grow >= pad, b, 0.0)
        h_prev = hc_ref[:, sl]
        hs, h_last = _scan_rows(a, b, h_prev)
        hc_ref[:, sl] = keep(h_last, h_prev)
        y = hs * gate_ref[:, sl]
        y_ref[:, sl] = y
        ssq = ssq + y * y
        if fuse_out and (h + 1) % heads_per_piece == 0:
            k = (h + 1) // heads_per_piece - 1
            psl = slice(k * piece, (k + 1) * piece)
            o_ref[:, psl] = jnp.dot(yn_ref[...], wo_ref[:, psl], preferred_element_type=F32)
    scale = lax.rsqrt(jnp.sum(ssq, axis=-1, keepdims=True) / dl + EPS)
    yn = (y_ref[...] * scale * g_ref[...]).astype(BF16)
    if fuse_out:
        yn_ref[...] = yn
    else:
        o_ref[...] = yn
    xe_ref[0:HALO, :] = keep(xe_ref[t:t + HALO, :], xe_ref[0:HALO, :])
    hfin_ref[0] = hc_ref[...]
    tailfin_ref[0] = xe_ref[HALO - 3:HALO, :]


def _lru_seq(proj, params, h0, tail0, *, n_seq, seq_len, t, row0, pad=0, reset_first=False, wo=None):
    cw, cb, wa, wx, ba, bx, lam, g = params
    dl = cw.shape[1]
    nh, hd = wa.shape[0], wa.shape[1]
    assert seq_len % t == 0 and row0 % t == 0 and t % SUBLANES == 0
    n_chunks = seq_len // t
    n_live = n_seq * n_chunks
    blk0 = row0 // t
    fuse_out = wo is not None
    chunk = (lambda s: jnp.minimum(s, n_live - 1)) if fuse_out else (lambda s: s)
    out_chunk = (lambda s: jnp.maximum(s - 1, 0)) if fuse_out else (lambda s: s)
    const2 = lambda s: (0, 0)
    const3 = lambda s: (0, 0, 0)
    per_seq = lambda s: (chunk(s) // n_chunks, 0, 0)
    dout, out_dtype = (wo.shape[1], F32) if fuse_out else (dl, BF16)
    pipelined = [((t, dl), F32), ((t, dl), F32), ((t, dout), out_dtype)]
    resident = [((t + HALO, dl), F32), ((t, dl), F32), ((4 * nh, hd, hd), BF16)]
    scratch = [pltpu.VMEM((t + HALO, dl), F32), pltpu.VMEM((1, dl), F32), pltpu.VMEM((t, dl), F32)]
    extra_specs, extra_args = [], []
    if fuse_out:
        extra_specs.append(pl.BlockSpec((dl, dout), const2, pipeline_mode=pl.Buffered(1)))
        extra_args.append(wo)
        resident += [((dl, dout), BF16), ((t, dl), BF16)]
        scratch.append(pltpu.VMEM((t, dl), BF16))
    kern = functools.partial(_lru_seq_kernel, pad=pad, reset_first=reset_first, n_chunks=n_chunks, n_live=n_live,
                             fuse_out=fuse_out)
    return pl.pallas_call(
        kern,
        out_shape=(jax.ShapeDtypeStruct((n_seq * seq_len, dout), out_dtype),
                   jax.ShapeDtypeStruct((n_seq, 1, dl), F32),
                   jax.ShapeDtypeStruct((n_seq, CONV_TAPS - 1, dl), F32)),
        grid=(n_live + 1 if fuse_out else n_live,),
        in_specs=[
            pl.BlockSpec((t, dl), lambda s: (blk0 + chunk(s), 0)),
            pl.BlockSpec((t, dl), lambda s: (blk0 + chunk(s), 1)),
            pl.BlockSpec((CONV_TAPS, dl), const2),
            pl.BlockSpec((1, dl), const2),
            pl.BlockSpec((nh, hd, hd), const3),
            pl.BlockSpec((nh, hd, hd), const3),
            pl.BlockSpec((1, dl), const2),
            pl.BlockSpec((1, dl), const2),
            pl.BlockSpec((1, dl), const2),
            pl.BlockSpec((1, dl), const2),
            pl.BlockSpec((1, dl), const2),
            pl.BlockSpec((HALO, dl), const2),
        ] + extra_specs,
        out_specs=(
            pl.BlockSpec((t, dout), lambda s: (out_chunk(s), 0)),
            pl.BlockSpec((1, 1, dl), per_seq),
            pl.BlockSpec((1, CONV_TAPS - 1, dl), per_seq),
        ),
        scratch_shapes=scratch,
        compiler_params=pltpu.CompilerParams(
            dimension_semantics=("arbitrary",),
            vmem_limit_bytes=_vmem_limit(pipelined, resident),
            flags=MIXER_SCHED_FLAGS if fuse_out else None),
        name="lru_seq",
    )(proj, proj, cw, cb, wa, wx, ba, bx, lam, g, h0, tail0, *extra_args)


def _lru_slab_kernel(gate_ref, x_ref, cw_ref, cb_ref, wa_ref, wx_ref, ba_ref, bx_ref, lam_ref, g_ref,
                     h0_ref, tail_ref, o_ref, hfin_ref, tailfin_ref, y_ref):
    ls, bs, dl = x_ref.shape
    nh, hd = wa_ref.shape[0], wa_ref.shape[1]
    ntail = CONV_TAPS - 1
    for h in range(nh):
        sl = slice(h * hd, (h + 1) * hd)
        ext = [tail_ref[:, k * dl + h * hd:k * dl + (h + 1) * hd] for k in range(ntail)]
        ext += [x_ref[s, :, sl] for s in range(ls)]
        hsp = (0.5 * LRU_C) * jax.nn.softplus(-lam_ref[:, sl])
        hcur = h0_ref[:, sl]
        for s in range(ls):
            xh = _conv_taps(ext, s, cw_ref, cb_ref, sl)
            a, mult, i = _lru_gates(xh, wa_ref[h], wx_ref[h], ba_ref[:, sl], bx_ref[:, sl], hsp)
            hcur = a * hcur + mult * i * xh
            y_ref[s, :, sl] = hcur * gate_ref[s, :, sl]
        hfin_ref[:, sl] = hcur
        for k in range(ntail):
            tailfin_ref[:, k * dl + h * hd:k * dl + (h + 1) * hd] = ext[ls + k]
    for s in range(ls):
        y = y_ref[s]
        scale = lax.rsqrt(jnp.mean(y * y, axis=-1, keepdims=True) + EPS)
        o_ref[s] = (y * scale * g_ref[...]).astype(o_ref.dtype)


def _lru_slab(proj3, params, h0, tail, *, ls):
    cw, cb, wa, wx, ba, bx, lam, g = params
    bs = proj3.shape[1]
    dl = cw.shape[1]
    nh, hd = wa.shape[0], wa.shape[1]
    ntail = CONV_TAPS - 1
    c2 = lambda i: (0, 0)
    c3 = lambda i: (0, 0, 0)
    pipelined = [((ls, bs, dl), F32)] * 2 + [((ls, bs, dl), BF16)] + [((bs, (2 * ntail + 2) * dl), F32)]
    return pl.pallas_call(
        _lru_slab_kernel,
        out_shape=(jax.ShapeDtypeStruct((ls, bs, dl), BF16),
                   jax.ShapeDtypeStruct((bs, dl), F32),
                   jax.ShapeDtypeStruct((bs, ntail * dl), F32)),
        grid=(1,),
        in_specs=[
            pl.BlockSpec((ls, bs, dl), lambda i: (0, 0, 0)),
            pl.BlockSpec((ls, bs, dl), lambda i: (0, 0, 1)),
            pl.BlockSpec((CONV_TAPS, dl), c2), pl.BlockSpec((1, dl), c2),
            pl.BlockSpec((nh, hd, hd), c3), pl.BlockSpec((nh, hd, hd), c3),
            pl.BlockSpec((1, dl), c2), pl.BlockSpec((1, dl), c2), pl.BlockSpec((1, dl), c2), pl.BlockSpec((1, dl), c2),
            pl.BlockSpec((bs, dl), c2), pl.BlockSpec((bs, ntail * dl), c2),
        ],
        out_specs=(pl.BlockSpec((ls, bs, dl), c3), pl.BlockSpec((bs, dl), c2), pl.BlockSpec((bs, ntail * dl), c2)),
        scratch_shapes=[pltpu.VMEM((ls, bs, dl), F32)],
        compiler_params=pltpu.CompilerParams(
            dimension_semantics=("arbitrary",),
            vmem_limit_bytes=_vmem_limit(pipelined, [((ls, bs, dl), F32)])),
        name="lru_slab",
    )(proj3, proj3, cw, cb, wa, wx, ba, bx, lam, g, h0, tail)


def _head_lane_mask(n_heads):
    return lax.broadcasted_iota(jnp.int32, (1, HEAD_PAD), 1) < n_heads


def _gated_group_norm(y, z_act, g):
    yg = y * z_act
    scale = lax.rsqrt(jnp.mean(yg * yg, axis=-1, keepdims=True) + EPS)
    return yg * scale * g


def _ssd_seq_kernel(z_ref, xbc_ref, dt_ref, cw_ref, cb_ref, dtb_ref, alog_ref, dskip_ref, g_ref, e3_ref,
                    s0_ref, tail0_ref, *rest, pad, q, n_heads, n_groups, n_chunks, n_live, fuse_out):
    if fuse_out:
        wo_ref, part_ref, res_ref = rest[:3]
        o_ref, sfin_ref, tailfin_ref, xe_ref, xc_ref, ex_ref, y_ref, s_ref, yn_ref, ynp_ref = rest[3:]
    else:
        o_ref, sfin_ref, tailfin_ref, xe_ref, xc_ref, ex_ref, y_ref, s_ref = rest
        yn_ref = o_ref
    step = pl.program_id(0)
    live = step < n_live
    c = lax.rem(jnp.minimum(step, n_live - 1), n_chunks)
    keep = (lambda new, old: jnp.where(live, new, old)) if fuse_out else (lambda new, old: new)
    tq, ds = z_ref.shape
    cps = tq // q
    dc = xbc_ref.shape[1]
    gn = (dc - ds) // 2
    n = gn // n_groups
    p = ds // n_heads
    r = n_heads // n_groups
    gw = ds // n_groups
    hpb = LANES // p

    @pl.when(jnp.logical_and(c == 0, live))
    def _():
        xe_ref[0:HALO, :] = tail0_ref[...]
        s_ref[...] = s0_ref[...]

    if fuse_out:
        @pl.when(step == 0)
        def _():
            ynp_ref[...] = jnp.zeros_like(ynp_ref)

    xe_ref[HALO:HALO + tq, :] = xbc_ref[...]
    valid_all = (c * tq + lax.broadcasted_iota(jnp.int32, (tq, 1), 0)) >= pad

    cblk = 512 if dc % 512 == 0 else LANES
    for j in range(dc // cblk):
        sl = slice(j * cblk, (j + 1) * cblk)
        ext = [xe_ref[HALO - 3 + k:HALO - 3 + k + tq, sl] for k in range(CONV_TAPS)]
        v = _silu(_conv_taps(ext, 0, cw_ref, cb_ref, sl))
        if pad and (j + 1) * cblk <= ds:
            v = jnp.where(valid_all, v, 0.0)
        xc_ref[:, sl] = v

    ri = lax.broadcasted_iota(jnp.int32, (q, q), 0)
    ci = lax.broadcasted_iota(jnp.int32, (q, q), 1)
    causal = ci <= ri
    tri = jnp.where(causal, 1.0, 0.0).astype(BF16)
    lane = lax.broadcasted_iota(jnp.int32, (q, LANES), 1)
    a_neg = -jnp.exp(alog_ref[...])
    n_pieces = cps * n_groups
    piece = o_ref.shape[1] // n_pieces if fuse_out else 0

    for u in range(cps):
        rows = slice(u * q, (u + 1) * q)
        dtv = jnp.where(_head_lane_mask(n_heads), jax.nn.softplus(dt_ref[rows, :] + dtb_ref[...]), 0.0)
        if pad:
            dtv = jnp.where(valid_all[u * q:(u + 1) * q], dtv, 0.0)
        ac3 = jnp.dot(tri, _split3(dtv * a_neg), preferred_element_type=F32)
        acum = ac3[:, 0:HEAD_PAD] + ac3[:, HEAD_PAD:2 * HEAD_PAD] + ac3[:, 2 * HEAD_PAD:3 * HEAD_PAD]
        alast = acum[q - 1:q, :]
        eac = jnp.exp(acum)
        wend = jnp.exp(alast - acum) * dtv
        cdec = jnp.broadcast_to(jnp.exp(alast), (SUBLANES, HEAD_PAD))
        ex_ref[...] = _expand_heads(jnp.concatenate([eac, wend, cdec], axis=0), e3_ref)
        acum_t = acum.T
        dt_t = dtv.T

        for g in range(n_groups):
            gsl = slice(g * gw, (g + 1) * gw)
            bg = xc_ref[rows, ds + g * n:ds + (g + 1) * n].astype(BF16)
            cg = xc_ref[rows, ds + gn + g * n:ds + gn + (g + 1) * n].astype(BF16)
            cbm = lax.dot_general(cg, bg, _NT, preferred_element_type=F32)
            yoff = jnp.dot(cg, s_ref[:, gsl].astype(BF16), preferred_element_type=F32)
            for k in range(gw // LANES):
                lsl = slice(g * gw + k * LANES, g * gw + (k + 1) * LANES)
                xs = xc_ref[rows, lsl]
                ms = []
                xparts = []
                for w in range(hpb):
                    h = g * r + k * hpb + w
                    seg = acum[:, h:h + 1] - acum_t[h:h + 1, :]
                    lm = jnp.where(causal, jnp.exp(seg), 0.0) * dt_t[h:h + 1, :]
                    ms.append((cbm * lm).astype(BF16))
                    inhead = (lane >= w * p) & (lane < (w + 1) * p)
                    xparts.append(jnp.where(inhead, xs, 0.0).astype(BF16))
                ydiag = jnp.dot(jnp.concatenate(ms, axis=1), jnp.concatenate(xparts, axis=0),
                                preferred_element_type=F32)
                y = ydiag + yoff[:, k * LANES:(k + 1) * LANES] * ex_ref[0:q, lsl]
                y_ref[rows, lsl] = y + dskip_ref[:, lsl] * xs
            yn_ref[rows, gsl] = _gated_group_norm(y_ref[rows, gsl], z_ref[rows, gsl], g_ref[:, gsl]).astype(BF16)
            xw = (xc_ref[rows, gsl] * ex_ref[q:2 * q, gsl]).astype(BF16)
            upd = lax.dot_general(bg, xw, (((0,), (0,)), ((), ())), preferred_element_type=F32)
            s_old = s_ref[:, gsl]
            s_ref[:, gsl] = keep(ex_ref[2 * q:2 * q + 1, gsl] * s_old + upd, s_old)
            if fuse_out:
                psl = slice((u * n_groups + g) * piece, (u * n_groups + g + 1) * piece)
                mix = part_ref[:, psl] + jnp.dot(ynp_ref[...], wo_ref[:, psl], preferred_element_type=F32)
                o_ref[:, psl] = res_ref[:, psl] + mix

    if fuse_out:
        ynp_ref[...] = yn_ref[...]
    xe_ref[0:HALO, :] = keep(xe_ref[tq:tq + HALO, :], xe_ref[0:HALO, :])
    sfin_ref[0] = s_ref[...]
    tailfin_ref[0] = xe_ref[HALO - 3:HALO, :]


def _ssd_seq(proj, dt, params, s0, tail0, *, n_seq, seq_len, row0, n_heads, n_groups, dl, pad=0, fuse=None):
    cw, cb, dtb, alog, dskip, g, e3 = params
    dc = cw.shape[1]
    ds = dskip.shape[1]
    n = s0.shape[0]
    q = SSD_CHUNK
    tq = q * (SSD_CHUNKS_PER_STEP if seq_len % (q * SSD_CHUNKS_PER_STEP) == 0 else 1)
    assert seq_len % tq == 0 and row0 % tq == 0
    assert (2 * dl) % ds == 0 and (2 * dl + ds) % dc == 0 and LANES % (ds // n_heads) == 0
    n_chunks = seq_len // tq
    n_live = n_seq * n_chunks
    blk0 = row0 // tq
    fuse_out = fuse is not None
    chunk = (lambda s: jnp.minimum(s, n_live - 1)) if fuse_out else (lambda s: s)
    rows = lambda s: blk0 + chunk(s)
    const2 = lambda s: (0, 0)
    out_rows = (lambda s: (jnp.maximum(s - 1, 0), 0)) if fuse_out else (lambda s: (s, 0))
    per_seq = lambda s: (chunk(s) // n_chunks, 0, 0)
    dout, out_dtype = (fuse[0].shape[1], F32) if fuse_out else (ds, BF16)
    pipelined = [((tq, ds), F32), ((tq, dc), F32), ((tq, HEAD_PAD), F32), ((tq, dout), out_dtype)]
    resident = [((tq + HALO, dc), F32), ((tq, dc), F32), ((2 * q + SUBLANES, ds), F32), ((tq, ds), F32),
                ((3 * n, ds), F32), ((6 * HEAD_PAD, ds), BF16)]
    scratch = [pltpu.VMEM((tq + HALO, dc), F32), pltpu.VMEM((tq, dc), F32),
               pltpu.VMEM((2 * q + SUBLANES, ds), F32), pltpu.VMEM((tq, ds), F32), pltpu.VMEM((n, ds), F32)]
    extra_specs, extra_args = [], []
    if fuse_out:
        extra_specs = [pl.BlockSpec((ds, dout), const2, pipeline_mode=pl.Buffered(1)),
                       pl.BlockSpec((tq, dout), out_rows), pl.BlockSpec((tq, dout), out_rows)]
        extra_args = list(fuse)
        pipelined += [((tq, dout), F32)] * 2
        resident += [((ds, dout), BF16), ((2 * tq, ds), BF16)]
        scratch += [pltpu.VMEM((tq, ds), BF16), pltpu.VMEM((tq, ds), BF16)]
    kern = functools.partial(_ssd_seq_kernel, pad=pad, q=q, n_heads=n_heads, n_groups=n_groups, n_chunks=n_chunks,
                             n_live=n_live, fuse_out=fuse_out)
    return pl.pallas_call(
        kern,
        out_shape=(jax.ShapeDtypeStruct((n_seq * seq_len, dout), out_dtype),
                   jax.ShapeDtypeStruct((n_seq, n, ds), F32),
                   jax.ShapeDtypeStruct((n_seq, CONV_TAPS - 1, dc), F32)),
        grid=(n_live + 1 if fuse_out else n_live,),
        in_specs=[
            pl.BlockSpec((tq, ds), lambda s: (rows(s), (2 * dl) // ds)),
            pl.BlockSpec((tq, dc), lambda s: (rows(s), (2 * dl + ds) // dc)),
            pl.BlockSpec((tq, HEAD_PAD), lambda s: (rows(s), 0)),
            pl.BlockSpec((CONV_TAPS, dc), const2),
            pl.BlockSpec((1, dc), const2),
            pl.BlockSpec((1, HEAD_PAD), const2),
            pl.BlockSpec((1, HEAD_PAD), const2),
            pl.BlockSpec((1, ds), const2),
            pl.BlockSpec((1, ds), const2),
            pl.BlockSpec((3 * HEAD_PAD, ds), const2),
            pl.BlockSpec((n, ds), const2),
            pl.BlockSpec((HALO, dc), const2),
        ] + extra_specs,
        out_specs=(
            pl.BlockSpec((tq, dout), out_rows),
            pl.BlockSpec((1, n, ds), per_seq),
            pl.BlockSpec((1, CONV_TAPS - 1, dc), per_seq),
        ),
        scratch_shapes=scratch,
        compiler_params=pltpu.CompilerParams(
            dimension_semantics=("arbitrary",),
            vmem_limit_bytes=_vmem_limit(pipelined, resident),
            flags=MIXER_SCHED_FLAGS if fuse_out else None),
        name="ssd_seq",
    )(proj, proj, dt, cw, cb, dtb, alog, dskip, g, e3, s0, tail0, *extra_args)


def _ssd_slab_pre_kernel(xbc_ref, dt_ref, tail_ref, cw_ref, cb_ref, dtb_ref, alog_ref, dskip_ref, e3_ref,
                         ypart_ref, eace_ref, c_ref, b_ref, xw_ref, cdec_ref, tailfin_ref, xc_ref,
                         *, n_heads, n_groups):
    ls, bs, dc = xbc_ref.shape
    ds = dskip_ref.shape[1]
    gn = (dc - ds) // 2
    n = gn // n_groups
    r = n_heads // n_groups
    ntail = CONV_TAPS - 1

    cblk = 512 if dc % 512 == 0 else LANES
    for j in range(dc // cblk):
        sl = slice(j * cblk, (j + 1) * cblk)
        ext = [tail_ref[:, k * dc + j * cblk:k * dc + (j + 1) * cblk] for k in range(ntail)]
        ext += [xbc_ref[s, :, sl] for s in range(ls)]
        for s in range(ls):
            xc_ref[s, :, sl] = _silu(_conv_taps(ext, s, cw_ref, cb_ref, sl))
        for k in range(ntail):
            tailfin_ref[:, k * dc + j * cblk:k * dc + (j + 1) * cblk] = ext[ls + k]

    hmask = _head_lane_mask(n_heads)
    a_neg = -jnp.exp(alog_ref[...])
    dtv, acum = [], []
    run = jnp.zeros((bs, HEAD_PAD), F32)
    for s in range(ls):
        d = jnp.where(hmask, jax.nn.softplus(dt_ref[s] + dtb_ref[...]), 0.0)
        run = run + d * a_neg
        dtv.append(d)
        acum.append(run)
    alast = acum[ls - 1]
    cdec_ref[...] = jnp.exp(alast)
    head_group = lax.broadcasted_iota(jnp.int32, (1, HEAD_PAD), 1) // r

    for s in range(ls):
        eace_ref[s] = _expand_heads(jnp.exp(acum[s]), e3_ref)
        wend_e = _expand_heads(jnp.exp(alast - acum[s]) * dtv[s], e3_ref)
        xw_ref[:, s * ds:(s + 1) * ds] = xc_ref[s, :, 0:ds] * wend_e
        b_ref[:, s * gn:(s + 1) * gn] = xc_ref[s, :, ds:ds + gn]
        c_ref[:, s * gn:(s + 1) * gn] = xc_ref[s, :, ds + gn:ds + 2 * gn]
        ypart = dskip_ref[...] * xc_ref[s, :, 0:ds]
        for j in range(s + 1):
            cbh = jnp.zeros((bs, HEAD_PAD), F32)
            for g in range(n_groups):
                cs = xc_ref[s, :, ds + gn + g * n:ds + gn + (g + 1) * n]
                bj = xc_ref[j, :, ds + g * n:ds + (g + 1) * n]
                cbg = jnp.sum(cs * bj, axis=-1, keepdims=True)
                cbh = cbh + jnp.where(head_group == g, cbg, 0.0)
            coef = cbh * (jnp.exp(acum[s] - acum[j]) * dtv[j])
            ypart = ypart + _expand_heads(coef, e3_ref) * xc_ref[j, :, 0:ds]
        ypart_ref[s] = ypart
    for s in range(ls, SLAB_ROWS):
        xw_ref[:, s * ds:(s + 1) * ds] = jnp.zeros((bs, ds), F32)
        b_ref[:, s * gn:(s + 1) * gn] = jnp.zeros((bs, gn), F32)
        c_ref[:, s * gn:(s + 1) * gn] = jnp.zeros((bs, gn), F32)


def _ssd_slab_pre(proj3, dt3, tail, params, *, ls, n_heads, n_groups, dl):
    cw, cb, dtb, alog, dskip, _, e3 = params
    bs = proj3.shape[1]
    dc = cw.shape[1]
    ds = dskip.shape[1]
    gn = (dc - ds) // 2
    ntail = CONV_TAPS - 1
    assert ls <= SLAB_ROWS and (2 * dl + ds) % dc == 0
    c2 = lambda i: (0, 0)
    c3 = lambda i: (0, 0, 0)
    pipelined = [((ls, bs, dc), F32), ((ls, bs, HEAD_PAD), F32), ((bs, 2 * ntail * dc), F32),
                 ((2 * ls, bs, ds), F32), ((bs, SLAB_ROWS * (2 * gn + ds)), F32), ((3 * HEAD_PAD, ds), BF16)]
    kern = functools.partial(_ssd_slab_pre_kernel, n_heads=n_heads, n_groups=n_groups)
    return pl.pallas_call(
        kern,
        out_shape=(jax.ShapeDtypeStruct((ls, bs, ds), F32),
                   jax.ShapeDtypeStruct((ls, bs, ds), F32),
                   jax.ShapeDtypeStruct((bs, SLAB_ROWS * gn), F32),
                   jax.ShapeDtypeStruct((bs, SLAB_ROWS * gn), F32),
                   jax.ShapeDtypeStruct((bs, SLAB_ROWS * ds), F32),
                   jax.ShapeDtypeStruct((bs, HEAD_PAD), F32),
                   jax.ShapeDtypeStruct((bs, ntail * dc), F32)),
        grid=(1,),
        in_specs=[
            pl.BlockSpec((ls, bs, dc), lambda i: (0, 0, (2 * dl + ds) // dc)),
            pl.BlockSpec((ls, bs, HEAD_PAD), c3),
            pl.BlockSpec((bs, ntail * dc), c2),
            pl.BlockSpec((CONV_TAPS, dc), c2), pl.BlockSpec((1, dc), c2),
            pl.BlockSpec((1, HEAD_PAD), c2), pl.BlockSpec((1, HEAD_PAD), c2),
            pl.BlockSpec((1, ds), c2), pl.BlockSpec((3 * HEAD_PAD, ds), c2),
        ],
        out_specs=(pl.BlockSpec((ls, bs, ds), c3), pl.BlockSpec((ls, bs, ds), c3),
                   pl.BlockSpec((bs, SLAB_ROWS * gn), c2), pl.BlockSpec((bs, SLAB_ROWS * gn), c2),
                   pl.BlockSpec((bs, SLAB_ROWS * ds), c2), pl.BlockSpec((bs, HEAD_PAD), c2),
                   pl.BlockSpec((bs, ntail * dc), c2)),
        scratch_shapes=[pltpu.VMEM((ls, bs, dc), F32)],
        compiler_params=pltpu.CompilerParams(
            dimension_semantics=("arbitrary",),
            vmem_limit_bytes=_vmem_limit(pipelined, [((ls, bs, dc), F32)])),
        name="ssd_slab_pre",
    )(proj3, dt3, tail, cw, cb, dtb, alog, dskip, e3)


def _ssd_state_kernel(cdec_ref, s_ref, c_ref, b_ref, xw_ref, snew_ref, yoff_ref, *, n_heads, n_groups):
    i = pl.program_id(0)
    sb, hp, n = s_ref.shape
    p = hp // n_heads
    r = n_heads // n_groups
    gw = hp // n_groups
    for q in range(sb):
        for g in range(n_groups):
            gsl = slice(g * gw, (g + 1) * gw)
            sg = s_ref[q, gsl, :]
            cg = c_ref[q, :, g * n:(g + 1) * n].astype(BF16)
            yoff_ref[q, :, gsl] = lax.dot_general(cg, sg.astype(BF16), (((1,), (1,)), ((), ())),
                                                  preferred_element_type=F32)
            upd = lax.dot_general(xw_ref[q, :, gsl].astype(BF16), b_ref[q, :, g * n:(g + 1) * n].astype(BF16),
                                  (((0,), (0,)), ((), ())), preferred_element_type=F32)
            for u in range(r):
                h = g * r + u
                rows = slice(h * p, (h + 1) * p)
                snew_ref[q, rows, :] = cdec_ref[i * sb + q, h] * s_ref[q, rows, :] + upd[u * p:(u + 1) * p, :]


def _ssd_state(cdec, state, c_rows, b_rows, xw_rows, *, n_heads, n_groups):
    bs, hp, n = state.shape
    gn = c_rows.shape[2]
    sb = STATE_SEQS_PER_STEP if bs % STATE_SEQS_PER_STEP == 0 else 1
    per_seq = lambda i: (i, 0, 0)
    pipelined = [((sb, hp, n), F32)] * 2 + [((sb, SLAB_ROWS, gn), F32)] * 2 + [((sb, SLAB_ROWS, hp), F32)] * 2
    kern = functools.partial(_ssd_state_kernel, n_heads=n_heads, n_groups=n_groups)
    return pl.pallas_call(
        kern,
        out_shape=(jax.ShapeDtypeStruct((bs, hp, n), F32), jax.ShapeDtypeStruct((bs, SLAB_ROWS, hp), F32)),
        grid=(bs // sb,),
        in_specs=[
            pl.BlockSpec(memory_space=pltpu.SMEM),
            pl.BlockSpec((sb, hp, n), per_seq),
            pl.BlockSpec((sb, SLAB_ROWS, gn), per_seq),
            pl.BlockSpec((sb, SLAB_ROWS, gn), per_seq),
            pl.BlockSpec((sb, SLAB_ROWS, hp), per_seq),
        ],
        out_specs=(pl.BlockSpec((sb, hp, n), per_seq), pl.BlockSpec((sb, SLAB_ROWS, hp), per_seq)),
        compiler_params=pltpu.CompilerParams(
            dimension_semantics=("parallel",),
            vmem_limit_bytes=_vmem_limit(pipelined, [])),
        name="ssd_state",
    )(cdec, state, c_rows, b_rows, xw_rows)


def _ssd_slab_post_kernel(ypart_ref, eace_ref, yoff_ref, z_ref, g_ref, o_ref, *, n_groups):
    ls, bs, ds = ypart_ref.shape
    gw = ds // n_groups
    for s in range(ls):
        for g in range(n_groups):
            gsl = slice(g * gw, (g + 1) * gw)
            y = ypart_ref[s, :, gsl] + eace_ref[s, :, gsl] * yoff_ref[:, s * ds + g * gw:s * ds + (g + 1) * gw]
            o_ref[s, :, gsl] = _gated_group_norm(y, z_ref[s, :, gsl], g_ref[:, gsl]).astype(o_ref.dtype)


def _ssd_slab_post(ypart, eace, yoff, proj3, g, *, n_groups, dl):
    ls, bs, ds = ypart.shape
    assert (2 * dl) % ds == 0
    c2 = lambda i: (0, 0)
    c3 = lambda i: (0, 0, 0)
    pipelined = [((ls, bs, ds), F32)] * 3 + [((bs, SLAB_ROWS * ds), F32), ((ls, bs, ds), BF16)]
    kern = functools.partial(_ssd_slab_post_kernel, n_groups=n_groups)
    return pl.pallas_call(
        kern,
        out_shape=jax.ShapeDtypeStruct((ls, bs, ds), BF16),
        grid=(1,),
        in_specs=[
            pl.BlockSpec((ls, bs, ds), c3), pl.BlockSpec((ls, bs, ds), c3),
            pl.BlockSpec((bs, SLAB_ROWS * ds), c2),
            pl.BlockSpec((ls, bs, ds), lambda i: (0, 0, (2 * dl) // ds)),
            pl.BlockSpec((1, ds), c2),
        ],
        out_specs=pl.BlockSpec((ls, bs, ds), c3),
        compiler_params=pltpu.CompilerParams(
            dimension_semantics=("arbitrary",),
            vmem_limit_bytes=_vmem_limit(pipelined, [])),
        name="ssd_slab_post",
    )(ypart, eace, yoff, proj3, g)


def _head_expansion(n_heads, head_dim):
    rows = lax.broadcasted_iota(jnp.int32, (HEAD_PAD, n_heads * head_dim), 0)
    cols = lax.broadcasted_iota(jnp.int32, (HEAD_PAD, n_heads * head_dim), 1)
    e = (cols // head_dim == rows).astype(BF16)
    return jnp.concatenate([e, e, e], axis=0)


def _pad_lanes(v, width):
    return jnp.pad(v, ((0, 0), (0, width - v.shape[1])))


def _tail_block(tail):
    return jnp.pad(tail, ((HALO - tail.shape[0], 0), (0, 0)))


def _mlp(x1, g_mlp, w_up, w_down, g_final, *, layer=None):
    m = x1.shape[0]
    tiles = (1024, 512, 256, 128)
    tm = m if layer is not None else _pick_tile(m, tiles)
    up_tiles = tiles if layer is not None else (2048,) + tiles
    up = _mlp_up(x1, g_mlp, w_up, tm=tm, tn=_pick_tile(w_up.shape[-1], up_tiles), layer=layer)
    hid = up[0]
    down = _mlp_down_final(hid, w_down, x1, g_final, tm=tm, tk=_pick_tile(w_down.shape[-2], tiles), layer=layer)
    if layer is None:
        return down[0]
    return down[0], up[1], down[1]


def kernel(x_prompt, x_sample, state_lru_h, state_lru_conv, state_ssd, state_ssd_conv, meta_tokens, g_mix, w_in, conv_lru_w, conv_lru_b, lru_wa, lru_ba, lru_wx, lru_bx, lru_lambda, g_lru_out, conv_ssd_w, conv_ssd_b, dt_bias, a_log, d_skip, g_ssd_out, w_out, g_mlp, w_up, w_down, g_final):
    depth = w_in.shape[0]
    assert depth == 1, "single-layer step"
    l = 0
    bp, lp, d = x_prompt.shape
    bs, ls, _ = x_sample.shape
    n_meta = meta_tokens.shape[0]
    dl = state_lru_h.shape[-1]
    n_heads, p, n = state_ssd.shape[-3:]
    ds = n_heads * p
    dc = state_ssd_conv.shape[-1]
    gn = (dc - ds) // 2
    n_groups = gn // n
    nw = 2 * dl + ds + dc
    ntail = CONV_TAPS - 1
    q = SSD_CHUNK
    meta_pad = (-n_meta) % q
    assert n_heads <= HEAD_PAD and (bs * ls) % q == 0 and lp % q == 0 and q % bs == 0

    row = lambda v: v.reshape(1, -1).astype(F32)
    w_in_t = jnp.swapaxes(w_in, 1, 2)
    lru_params = (conv_lru_w[l], row(conv_lru_b[l]), (0.5 * lru_wa[l]).astype(BF16), (0.5 * lru_wx[l]).astype(BF16),
                  row(0.5 * lru_ba[l]), row(0.5 * lru_bx[l]), row(lru_lambda[l]), row(g_lru_out[l]))
    ssd_params = (conv_ssd_w[l], row(conv_ssd_b[l]), _pad_lanes(row(dt_bias[l]), HEAD_PAD),
                  _pad_lanes(row(a_log[l]), HEAD_PAD), row(jnp.repeat(d_skip[l], p)), row(g_ssd_out[l]),
                  _head_expansion(n_heads, p))
    g_mix_r, g_mlp_r, g_final_r = row(g_mix[l]), row(g_mlp[l]), row(g_final)

    xs_tm = x_sample.transpose(1, 0, 2).reshape(ls * bs, d)
    x_side = jnp.concatenate([xs_tm, jnp.zeros((meta_pad, d), F32), meta_tokens.astype(F32)], axis=0)
    xp_rows = x_prompt.reshape(bp * lp, d)

    tiles = (1024, 512, 256, 128)
    tn_in = next(t for t in tiles if nw % t == 0 and dl % t == 0 and ds % t == 0)
    act_cols = dict(gelu_cols=(0, dl), silu_cols=(2 * dl, 2 * dl + ds))
    proj_side, dt_side, w_in_b, w_dt_b = _in_proj(x_side, g_mix_r, w_in_t, nw=nw, n_dt=n_heads, tm=x_side.shape[0],
                                                  tn=tn_in, layer=l, **act_cols)

    proj_s3 = proj_side.reshape(-1, bs, nw)
    dt_s3 = dt_side.reshape(-1, bs, HEAD_PAD)
    lru_s, s_h, s_ltail = _lru_slab(proj_s3, lru_params, state_lru_h[l],
                                    state_lru_conv[l].reshape(bs, ntail * dl), ls=ls)
    ypart, eace, c_rows, b_rows, xw_rows, cdec, s_stail = _ssd_slab_pre(
        proj_s3, dt_s3, state_ssd_conv[l].reshape(bs, ntail * dc), ssd_params,
        ls=ls, n_heads=n_heads, n_groups=n_groups, dl=dl)
    s_new, yoff = _ssd_state(cdec, state_ssd[l].reshape(bs, ds, n),
                             c_rows.reshape(bs, SLAB_ROWS, gn), b_rows.reshape(bs, SLAB_ROWS, gn),
                             xw_rows.reshape(bs, SLAB_ROWS, ds), n_heads=n_heads, n_groups=n_groups)
    ssd_s = _ssd_slab_post(ypart, eace, yoff.reshape(bs, SLAB_ROWS * ds), proj_s3, ssd_params[5],
                           n_groups=n_groups, dl=dl)
    x1_s, w_out_lru, w_out_ssd = _out_proj(lru_s.reshape(ls * bs, dl), ssd_s.reshape(ls * bs, ds), w_out, l, xs_tm,
                                           tn=_pick_tile(d, (512, 256, 128)))
    y_s, w_up_b, w_down_b = _mlp(x1_s, g_mlp_r, w_up, w_down, g_final_r, layer=l)

    meta_row0 = ls * bs
    _, m_h, m_ltail = _lru_seq(proj_side, lru_params, jnp.zeros((1, dl), F32), jnp.zeros((HALO, dl), F32),
                               n_seq=1, seq_len=q, t=q, row0=meta_row0, pad=meta_pad, reset_first=True)
    _, m_s, m_stail = _ssd_seq(proj_side, dt_side, ssd_params, jnp.zeros((n, ds), F32), jnp.zeros((HALO, dc), F32),
                               n_seq=1, seq_len=q, row0=meta_row0, n_heads=n_heads, n_groups=n_groups, dl=dl,
                               pad=meta_pad)

    proj_p, dt_p = _in_proj(xp_rows, g_mix_r, w_in_b, w_dt_b, nw=nw, n_dt=n_heads, tm=_pick_tile(bp * lp, tiles),
                            tn=tn_in, **act_cols)
    part_p, p_h, p_ltail = _lru_seq(proj_p, lru_params, m_h[0], _tail_block(m_ltail[0]),
                                    n_seq=bp, seq_len=lp, t=_pick_tile(lp, (256, 128)), row0=0, wo=w_out_lru)
    x1_p, p_s, p_stail = _ssd_seq(proj_p, dt_p, ssd_params, m_s[0], _tail_block(m_stail[0]),
                                  n_seq=bp, seq_len=lp, row0=0, n_heads=n_heads, n_groups=n_groups, dl=dl,
                                  fuse=(w_out_ssd, part_p, xp_rows))
    y_p = _mlp(x1_p, g_mlp_r, w_up_b, w_down_b, g_final_r)

    y_prompt = y_p.reshape(bp, lp, d)
    y_sample = y_s.reshape(ls, bs, d).transpose(1, 0, 2)
    p_lru_h = p_h.reshape(1, bp, dl)
    p_lru_conv = p_ltail.reshape(1, bp, ntail, dl)
    p_ssd = p_s.transpose(0, 2, 1).reshape(1, bp, n_heads, p, n)
    p_ssd_conv = p_stail.reshape(1, bp, ntail, dc)
    s_lru_h = s_h.reshape(1, bs, dl)
    s_lru_conv = s_ltail.reshape(1, bs, ntail, dl)
    s_ssd = s_new.reshape(1, bs, n_heads, p, n)
    s_ssd_conv = s_stail.reshape(1, bs, ntail, dc)
    return (y_prompt, y_sample, p_lru_h, p_lru_conv, p_ssd, p_ssd_conv, s_lru_h, s_lru_conv, s_ssd, s_ssd_conv)
```

```python
import functools

import jax
import jax.numpy as jnp
from jax import lax
from jax.experimental import pallas as pl
from jax.experimental.pallas import tpu as pltpu

F32 = jnp.float32
BF16 = jnp.bfloat16

EPS = 1e-6
LRU_C = 8.0
CONV_TAPS = 4

LANES = 128
SUBLANES = 8
MXU_COLS = 256
VMEM_BYTES_V7X = 64 * 1024 * 1024
VMEM_TEMP_BYTES = 10 * 1024 * 1024
VMEM_CEILING_BYTES = VMEM_BYTES_V7X - 6 * 1024 * 1024

HALO = 2 * SUBLANES
SSD_CHUNK = 128
SSD_CHUNKS_PER_STEP = 2
HEAD_PAD = LANES
SLAB_ROWS = SUBLANES
STATE_SEQS_PER_STEP = 8
CONV_LANE_BLOCK = 512


def _nbytes(shape, dtype):
    n = 1
    for s in shape:
        n *= s
    return n * jnp.dtype(dtype).itemsize


def _vmem_limit(pipelined, resident):
    est = 2 * sum(_nbytes(s, d) for s, d in pipelined) + sum(_nbytes(s, d) for s, d in resident)
    return int(min(est + VMEM_TEMP_BYTES, VMEM_CEILING_BYTES))


def _pick_tile(m, prefs):
    for t in prefs:
        if m % t == 0:
            return t
    return m


def _silu(x):
    h = 0.5 * x
    return h * jnp.tanh(h) + h


def _split3(x):
    hi = x.astype(BF16)
    r1 = x - hi.astype(F32)
    mid = r1.astype(BF16)
    lo = (r1 - mid.astype(F32)).astype(BF16)
    return jnp.concatenate([hi, mid, lo], axis=1)


def _expand_heads(x, e3_ref):
    return jnp.dot(_split3(x), e3_ref[...], preferred_element_type=F32)


def _rmsnorm_rows(x, g):
    ms = jnp.mean(x * x, axis=-1, keepdims=True)
    return x * lax.rsqrt(ms + EPS) * g


def _conv_rows(xe_ref, pair_ref, cw_ref, cb_ref, sl, t):
    lo = HALO - SUBLANES
    x0 = xe_ref[lo:HALO + t, sl]
    x1 = xe_ref[lo - 1:HALO + t - 1, sl]
    pair_ref[...] = x0 * cw_ref[1:2, sl] + x1 * cw_ref[0:1, sl]
    near = x0[SUBLANES:] * cw_ref[3:4, sl] + x1[SUBLANES:] * cw_ref[2:3, sl]
    return cb_ref[:, sl] + near + pair_ref[SUBLANES - 2:SUBLANES - 2 + t, :]


def _conv_taps(ext, s, cw_ref, cb_ref, sl):
    v = cb_ref[:, sl] + ext[s] * cw_ref[0:1, sl]
    for k in range(1, CONV_TAPS):
        v = v + ext[s + k] * cw_ref[k:k + 1, sl]
    return v


def _weight_tile(w_ref, wb_ref):
    if wb_ref is None:
        return w_ref[...]
    w = w_ref[...].astype(BF16)
    wb_ref[...] = w
    return w


def _weight_specs(layer, blk, idx):
    if layer is None:
        return pl.BlockSpec(blk, idx), None
    return pl.BlockSpec((None,) + blk, lambda i, j: (layer,) + idx(i, j)), pl.BlockSpec(blk, idx)


_NT = (((1,), (1,)), ((), ()))


def _in_proj_kernel(x_ref, g_ref, w_ref, wdt_ref, o_ref, dt_ref, *rest, gelu_tiles, silu_tiles, emit_w):
    if emit_w:
        wb_ref, wdtb_ref, xn_ref = rest
    else:
        wb_ref, wdtb_ref, xn_ref = None, None, rest[0]
    j = pl.program_id(1)

    @pl.when(j == 0)
    def _():
        xn = _rmsnorm_rows(x_ref[...], g_ref[...]).astype(BF16)
        xn_ref[...] = xn
        wdt = _weight_tile(wdt_ref, wdtb_ref)
        wdt = jnp.concatenate([wdt, jnp.zeros((HEAD_PAD - wdt.shape[0], wdt.shape[1]), BF16)], axis=0)
        dt_ref[...] = lax.dot_general(xn, wdt, _NT, preferred_element_type=F32)

    def tile():
        return lax.dot_general(xn_ref[...], _weight_tile(w_ref, wb_ref), _NT, preferred_element_type=F32)

    in_range = lambda r: jnp.logical_and(j >= r[0], j < r[1])
    is_gelu, is_silu = in_range(gelu_tiles), in_range(silu_tiles)

    @pl.when(is_gelu)
    def _():
        o_ref[...] = jax.nn.gelu(tile())

    @pl.when(is_silu)
    def _():
        o_ref[...] = _silu(tile())

    @pl.when(jnp.logical_not(jnp.logical_or(is_gelu, is_silu)))
    def _():
        o_ref[...] = tile()


def _in_proj(x, g, w_t, wdt_t=None, *, nw, n_dt, tm, tn, gelu_cols, silu_cols, layer=None):
    m, k = x.shape
    emit_w = layer is not None
    assert m % tm == 0 and nw % tn == 0 and all(c % tn == 0 for c in gelu_cols + silu_cols)
    assert n_dt % SUBLANES == 0 and nw % n_dt == 0 and (not emit_w or m == tm)
    out_shape = [jax.ShapeDtypeStruct((m, nw), F32), jax.ShapeDtypeStruct((m, HEAD_PAD), F32)]
    out_specs = [pl.BlockSpec((tm, tn), lambda i, j: (i, j)), pl.BlockSpec((tm, HEAD_PAD), lambda i, j: (i, 0))]
    pipelined = [((tm, k), F32), ((tn, k), w_t.dtype), ((n_dt, k), w_t.dtype), ((tm, tn), F32), ((tm, HEAD_PAD), F32)]
    if emit_w:
        w_spec = pl.BlockSpec((None, tn, k), lambda i, j: (layer, j, 0))
        wdt_spec = pl.BlockSpec((None, n_dt, k), lambda i, j: (layer, nw // n_dt, 0))
        wdt_t = w_t
        out_shape += [jax.ShapeDtypeStruct((nw, k), BF16), jax.ShapeDtypeStruct((n_dt, k), BF16)]
        out_specs += [pl.BlockSpec((tn, k), lambda i, j: (j, 0)), pl.BlockSpec((n_dt, k), lambda i, j: (0, 0))]
        pipelined += [((tn, k), BF16), ((n_dt, k), BF16)]
    else:
        w_spec = pl.BlockSpec((tn, k), lambda i, j: (j, 0))
        wdt_spec = pl.BlockSpec((n_dt, k), lambda i, j: (0, 0))
    kern = functools.partial(_in_proj_kernel, gelu_tiles=tuple(c // tn for c in gelu_cols),
                             silu_tiles=tuple(c // tn for c in silu_cols), emit_w=emit_w)
    return pl.pallas_call(
        kern,
        out_shape=tuple(out_shape),
        grid=(m // tm, nw // tn),
        in_specs=[
            pl.BlockSpec((tm, k), lambda i, j: (i, 0)),
            pl.BlockSpec((1, k), lambda i, j: (0, 0)),
            w_spec,
            wdt_spec,
        ],
        out_specs=tuple(out_specs),
        scratch_shapes=[pltpu.VMEM((tm, k), BF16)],
        compiler_params=pltpu.CompilerParams(
            dimension_semantics=("parallel", "arbitrary"),
            vmem_limit_bytes=_vmem_limit(pipelined, [((tm, k), BF16), ((tn, k), BF16)])),
        name="in_proj",
    )(x, g, w_t, wdt_t)


def _out_proj_kernel(a1_ref, a2_ref, w1_ref, w2_ref, res_ref, o_ref, wb1_ref, wb2_ref):
    acc = jnp.dot(a1_ref[...], _weight_tile(w1_ref, wb1_ref), preferred_element_type=F32)
    acc = acc + jnp.dot(a2_ref[...], _weight_tile(w2_ref, wb2_ref), preferred_element_type=F32)
    o_ref[...] = res_ref[...] + acc


def _out_proj(a1, a2, w, layer, res, *, tn):
    m, k1 = a1.shape
    k2 = a2.shape[1]
    n = w.shape[2]
    assert n % tn == 0 and k1 == k2 and w.shape[1] == k1 + k2
    pipelined = [((m, k1), BF16), ((m, k2), BF16), ((k1, tn), w.dtype), ((k2, tn), w.dtype), ((m, tn), F32),
                 ((m, tn), F32), ((k1, tn), BF16), ((k2, tn), BF16)]
    col = lambda j: (0, j)
    return pl.pallas_call(
        _out_proj_kernel,
        out_shape=(jax.ShapeDtypeStruct((m, n), F32), jax.ShapeDtypeStruct((k1, n), BF16),
                   jax.ShapeDtypeStruct((k2, n), BF16)),
        grid=(n // tn,),
        in_specs=[
            pl.BlockSpec((m, k1), lambda j: (0, 0)),
            pl.BlockSpec((m, k2), lambda j: (0, 0)),
            pl.BlockSpec((None, k1, tn), lambda j: (layer, 0, j)),
            pl.BlockSpec((None, k2, tn), lambda j: (layer, 1, j)),
            pl.BlockSpec((m, tn), col),
        ],
        out_specs=(pl.BlockSpec((m, tn), col), pl.BlockSpec((k1, tn), col), pl.BlockSpec((k2, tn), col)),
        compiler_params=pltpu.CompilerParams(
            dimension_semantics=("parallel",),
            vmem_limit_bytes=_vmem_limit(pipelined, [])),
        name="out_proj",
    )(a1, a2, w, w, res)


def _mlp_up_kernel(x_ref, g_ref, w_ref, o_ref, *rest, emit_w):
    wb_ref, xn_ref = rest if emit_w else (None, rest[0])

    @pl.when(pl.program_id(1) == 0)
    def _():
        xn_ref[...] = _rmsnorm_rows(x_ref[...], g_ref[...]).astype(BF16)

    acc = jnp.dot(xn_ref[...], _weight_tile(w_ref, wb_ref), preferred_element_type=F32)
    o_ref[...] = jnp.square(jnp.maximum(acc, 0.0)).astype(o_ref.dtype)


def _mlp_up(x, g, w, *, tm, tn, layer=None):
    m, k = x.shape
    n = w.shape[-1]
    emit_w = layer is not None
    assert m % tm == 0 and n % tn == 0 and (not emit_w or m == tm)
    w_spec, wb_spec = _weight_specs(layer, (k, tn), lambda i, j: (0, j))
    out_shape = [jax.ShapeDtypeStruct((m, n), BF16)]
    out_specs = [pl.BlockSpec((tm, tn), lambda i, j: (i, j))]
    pipelined = [((tm, k), F32), ((k, tn), w.dtype), ((tm, tn), BF16)]
    if emit_w:
        out_shape.append(jax.ShapeDtypeStruct((k, n), BF16))
        out_specs.append(wb_spec)
        pipelined.append(((k, tn), BF16))
    return pl.pallas_call(
        functools.partial(_mlp_up_kernel, emit_w=emit_w),
        out_shape=tuple(out_shape),
        grid=(m // tm, n // tn),
        in_specs=[
            pl.BlockSpec((tm, k), lambda i, j: (i, 0)),
            pl.BlockSpec((1, k), lambda i, j: (0, 0)),
            w_spec,
        ],
        out_specs=tuple(out_specs),
        scratch_shapes=[pltpu.VMEM((tm, k), BF16)],
        compiler_params=pltpu.CompilerParams(
            dimension_semantics=("parallel", "arbitrary"),
            vmem_limit_bytes=_vmem_limit(pipelined, [((tm, k), BF16), ((k, tn), BF16)])),
        name="mlp_up",
    )(x, g, w)


def _mlp_down_kernel(h_ref, w_ref, res_ref, g_ref, o_ref, *rest, emit_w):
    wb_ref = rest[0] if emit_w else None
    kk = pl.program_id(1)

    @pl.when(kk == 0)
    def _():
        o_ref[...] = res_ref[...]

    o_ref[...] += jnp.dot(h_ref[...], _weight_tile(w_ref, wb_ref), preferred_element_type=F32)

    @pl.when(kk == pl.num_programs(1) - 1)
    def _():
        o_ref[...] = _rmsnorm_rows(o_ref[...], g_ref[...])


def _mlp_down_final(h, w, res, g, *, tm, tk, layer=None):
    m, k = h.shape
    n = w.shape[-1]
    emit_w = layer is not None
    assert m % tm == 0 and k % tk == 0 and (not emit_w or m == tm)
    w_spec, wb_spec = _weight_specs(layer, (tk, n), lambda i, j: (j, 0))
    out_shape = [jax.ShapeDtypeStruct((m, n), F32)]
    out_specs = [pl.BlockSpec((tm, n), lambda i, j: (i, 0))]
    pipelined = [((tm, tk), BF16), ((tk, n), w.dtype), ((tm, n), F32), ((tm, n), F32)]
    if emit_w:
        out_shape.append(jax.ShapeDtypeStruct((k, n), BF16))
        out_specs.append(wb_spec)
        pipelined.append(((tk, n), BF16))
    return pl.pallas_call(
        functools.partial(_mlp_down_kernel, emit_w=emit_w),
        out_shape=tuple(out_shape),
        grid=(m // tm, k // tk),
        in_specs=[
            pl.BlockSpec((tm, tk), lambda i, j: (i, j)),
            w_spec,
            pl.BlockSpec((tm, n), lambda i, j: (i, 0)),
            pl.BlockSpec((1, n), lambda i, j: (0, 0)),
        ],
        out_specs=tuple(out_specs),
        compiler_params=pltpu.CompilerParams(
            dimension_semantics=("parallel", "arbitrary"),
            vmem_limit_bytes=_vmem_limit(pipelined, [((tk, n), BF16)] if emit_w else [])),
        name="mlp_down",
    )(h, w, res, g)


def _lru_gates(xh, wa_half, wx_half, ba_half, bx_half, hsp):
    xb = xh.astype(BF16)
    tr = jnp.tanh(jnp.dot(xb, wa_half, preferred_element_type=F32) + ba_half)
    ti = jnp.tanh(jnp.dot(xb, wx_half, preferred_element_type=F32) + bx_half)
    nla = tr * hsp + hsp
    a = jnp.exp(-nla)
    q = jnp.tanh(nla) * (1.0 + a * a)
    mult = jnp.where(q > 0.0, q * lax.rsqrt(q), 0.0)
    return a, mult, 0.5 * ti + 0.5


def _scan_rows(a, b, h_prev):
    t, hd = a.shape
    g = t // SUBLANES
    a3 = a.reshape(g, SUBLANES, hd)
    b3 = b.reshape(g, SUBLANES, hd)
    sub = lax.broadcasted_iota(jnp.int32, (g, SUBLANES, hd), 1)
    d = 1
    while d < SUBLANES:
        keep = sub >= d
        a_sh = jnp.where(keep, pltpu.roll(a3, d, axis=1), 1.0)
        b_sh = jnp.where(keep, pltpu.roll(b3, d, axis=1), 0.0)
        b3 = a3 * b_sh + b3
        a3 = a3 * a_sh
        d *= 2
    tiles = []
    h = h_prev
    for k in range(g):
        hk = a3[k] * h + b3[k]
        tiles.append(hk)
        h = hk[SUBLANES - 1:SUBLANES, :]
    return jnp.concatenate(tiles, axis=0), h


def _lru_seq_kernel(gate_ref, x_ref, cw_ref, cb_ref, wa_ref, wx_ref, ba_ref, bx_ref, lam_ref, g_ref,
                    h0_ref, tail0_ref, *rest, pad, reset_first, n_chunks, n_live, fuse_out):
    if fuse_out:
        wo_ref, o_ref, hfin_ref, tailfin_ref, xe_ref, hc_ref, y_ref, pair_ref, yn_ref = rest
    else:
        o_ref, hfin_ref, tailfin_ref, xe_ref, hc_ref, y_ref, pair_ref = rest
    step = pl.program_id(0)
    live = step < n_live
    c = lax.rem(jnp.minimum(step, n_live - 1), n_chunks)
    t, dl = x_ref.shape
    nh, hd = wa_ref.shape[0], wa_ref.shape[1]
    keep = (lambda new, old: jnp.where(live, new, old)) if fuse_out else (lambda new, old: new)

    @pl.when(jnp.logical_and(c == 0, live))
    def _():
        xe_ref[0:HALO, :] = tail0_ref[...]
        hc_ref[...] = h0_ref[...]

    if fuse_out:
        @pl.when(step == 0)
        def _():
            yn_ref[...] = jnp.zeros_like(yn_ref)

    xe_ref[HALO:HALO + t, :] = x_ref[...]
    grow = c * t + lax.broadcasted_iota(jnp.int32, (t, hd), 0)
    ssq = jnp.zeros((t, hd), F32)
    if fuse_out:
        n_pieces = max(1, min(nh, o_ref.shape[1] // MXU_COLS))
        while nh % n_pieces or o_ref.shape[1] % n_pieces:
            n_pieces -= 1
        heads_per_piece, piece = nh // n_pieces, o_ref.shape[1] // n_pieces
    for h in range(nh):
        sl = slice(h * hd, (h + 1) * hd)
        xh = _conv_rows(xe_ref, pair_ref.at[h % 2], cw_ref, cb_ref, sl, t)
        hsp = (0.5 * LRU_C) * jax.nn.softplus(-lam_ref[:, sl])
        a, mult, i = _lru_gates(xh, wa_ref[h], wx_ref[h], ba_ref[:, sl], bx_ref[:, sl], hsp)
        if reset_first:
            mult = jnp.where(grow == pad, 1.0, mult)
        b = mult * i * xh
        if pad:
            a = jnp.where(grow >= pad, a, 1.0)
            b = jnp.where(grow >= pad, b, 0.0)
        h_prev = hc_ref[:, sl]
        hs, h_last = _scan_rows(a, b, h_prev)
        hc_ref[:, sl] = keep(h_last, h_prev)
        y = hs * gate_ref[:, sl]
        y_ref[:, sl] = y
        ssq = ssq + y * y
        if fuse_out and (h + 1) % heads_per_piece == 0:
            k = (h + 1) // heads_per_piece - 1
            psl = slice(k * piece, (k + 1) * piece)
            o_ref[:, psl] = jnp.dot(yn_ref[...], wo_ref[:, psl], preferred_element_type=F32)
    scale = lax.rsqrt(jnp.sum(ssq, axis=-1, keepdims=True) / dl + EPS)
    yn = (y_ref[...] * scale * g_ref[...]).astype(BF16)
    if fuse_out:
        yn_ref[...] = yn
    else:
        o_ref[...] = yn
    xe_ref[0:HALO, :] = keep(xe_ref[t:t + HALO, :], xe_ref[0:HALO, :])
    hfin_ref[0] = hc_ref[...]
    tailfin_ref[0] = xe_ref[HALO - 3:HALO, :]


def _lru_seq(proj, params, h0, tail0, *, n_seq, seq_len, t, row0, pad=0, reset_first=False, wo=None):
    cw, cb, wa, wx, ba, bx, lam, g = params
    dl = cw.shape[1]
    nh, hd = wa.shape[0], wa.shape[1]
    assert seq_len % t == 0 and row0 % t == 0 and t % SUBLANES == 0
    n_chunks = seq_len // t
    n_live = n_seq * n_chunks
    blk0 = row0 // t
    fuse_out = wo is not None
    chunk = (lambda s: jnp.minimum(s, n_live - 1)) if fuse_out else (lambda s: s)
    out_chunk = (lambda s: jnp.maximum(s - 1, 0)) if fuse_out else (lambda s: s)
    const2 = lambda s: (0, 0)
    const3 = lambda s: (0, 0, 0)
    per_seq = lambda s: (chunk(s) // n_chunks, 0, 0)
    dout, out_dtype = (wo.shape[1], F32) if fuse_out else (dl, BF16)
    pipelined = [((t, dl), F32), ((t, dl), F32), ((t, dout), out_dtype)]
    resident = [((t + HALO, dl), F32), ((t, dl), F32), ((4 * nh, hd, hd), BF16), ((2, t + SUBLANES, hd), F32)]
    scratch = [pltpu.VMEM((t + HALO, dl), F32), pltpu.VMEM((1, dl), F32), pltpu.VMEM((t, dl), F32),
               pltpu.VMEM((2, t + SUBLANES, hd), F32)]
    extra_specs, extra_args = [], []
    if fuse_out:
        extra_specs.append(pl.BlockSpec((dl, dout), const2, pipeline_mode=pl.Buffered(1)))
        extra_args.append(wo)
        resident += [((dl, dout), BF16), ((t, dl), BF16)]
        scratch.append(pltpu.VMEM((t, dl), BF16))
    kern = functools.partial(_lru_seq_kernel, pad=pad, reset_first=reset_first, n_chunks=n_chunks, n_live=n_live,
                             fuse_out=fuse_out)
    return pl.pallas_call(
        kern,
        out_shape=(jax.ShapeDtypeStruct((n_seq * seq_len, dout), out_dtype),
                   jax.ShapeDtypeStruct((n_seq, 1, dl), F32),
                   jax.ShapeDtypeStruct((n_seq, CONV_TAPS - 1, dl), F32)),
        grid=(n_live + 1 if fuse_out else n_live,),
        in_specs=[
            pl.BlockSpec((t, dl), lambda s: (blk0 + chunk(s), 0)),
            pl.BlockSpec((t, dl), lambda s: (blk0 + chunk(s), 1)),
            pl.BlockSpec((CONV_TAPS, dl), const2),
            pl.BlockSpec((1, dl), const2),
            pl.BlockSpec((nh, hd, hd), const3),
            pl.BlockSpec((nh, hd, hd), const3),
            pl.BlockSpec((1, dl), const2),
            pl.BlockSpec((1, dl), const2),
            pl.BlockSpec((1, dl), const2),
            pl.BlockSpec((1, dl), const2),
            pl.BlockSpec((1, dl), const2),
            pl.BlockSpec((HALO, dl), const2),
        ] + extra_specs,
        out_specs=(
            pl.BlockSpec((t, dout), lambda s: (out_chunk(s), 0)),
            pl.BlockSpec((1, 1, dl), per_seq),
            pl.BlockSpec((1, CONV_TAPS - 1, dl), per_seq),
        ),
        scratch_shapes=scratch,
        compiler_params=pltpu.CompilerParams(
            dimension_semantics=("arbitrary",),
            vmem_limit_bytes=_vmem_limit(pipelined, resident)),
        name="lru_seq",
    )(proj, proj, cw, cb, wa, wx, ba, bx, lam, g, h0, tail0, *extra_args)


def _lru_slab_kernel(gate_ref, x_ref, cw_ref, cb_ref, wa_ref, wx_ref, ba_ref, bx_ref, lam_ref, g_ref,
                     h0_ref, tail_ref, o_ref, hfin_ref, tailfin_ref, y_ref):
    ls, bs, dl = x_ref.shape
    nh, hd = wa_ref.shape[0], wa_ref.shape[1]
    ntail = CONV_TAPS - 1
    for h in range(nh):
        sl = slice(h * hd, (h + 1) * hd)
        ext = [tail_ref[:, k * dl + h * hd:k * dl + (h + 1) * hd] for k in range(ntail)]
        ext += [x_ref[s, :, sl] for s in range(ls)]
        hsp = (0.5 * LRU_C) * jax.nn.softplus(-lam_ref[:, sl])
        hcur = h0_ref[:, sl]
        for s in range(ls):
            xh = _conv_taps(ext, s, cw_ref, cb_ref, sl)
            a, mult, i = _lru_gates(xh, wa_ref[h], wx_ref[h], ba_ref[:, sl], bx_ref[:, sl], hsp)
            hcur = a * hcur + mult * i * xh
            y_ref[s, :, sl] = hcur * gate_ref[s, :, sl]
        hfin_ref[:, sl] = hcur
        for k in range(ntail):
            tailfin_ref[:, k * dl + h * hd:k * dl + (h + 1) * hd] = ext[ls + k]
    for s in range(ls):
        y = y_ref[s]
        scale = lax.rsqrt(jnp.mean(y * y, axis=-1, keepdims=True) + EPS)
        o_ref[s] = (y * scale * g_ref[...]).astype(o_ref.dtype)


def _lru_slab(proj3, params, h0, tail, *, ls):
    cw, cb, wa, wx, ba, bx, lam, g = params
    bs = proj3.shape[1]
    dl = cw.shape[1]
    nh, hd = wa.shape[0], wa.shape[1]
    ntail = CONV_TAPS - 1
    c2 = lambda i: (0, 0)
    c3 = lambda i: (0, 0, 0)
    pipelined = [((ls, bs, dl), F32)] * 2 + [((ls, bs, dl), BF16)] + [((bs, (2 * ntail + 2) * dl), F32)]
    return pl.pallas_call(
        _lru_slab_kernel,
        out_shape=(jax.ShapeDtypeStruct((ls, bs, dl), BF16),
                   jax.ShapeDtypeStruct((bs, dl), F32),
                   jax.ShapeDtypeStruct((bs, ntail * dl), F32)),
        grid=(1,),
        in_specs=[
            pl.BlockSpec((ls, bs, dl), lambda i: (0, 0, 0)),
            pl.BlockSpec((ls, bs, dl), lambda i: (0, 0, 1)),
            pl.BlockSpec((CONV_TAPS, dl), c2), pl.BlockSpec((1, dl), c2),
            pl.BlockSpec((nh, hd, hd), c3), pl.BlockSpec((nh, hd, hd), c3),
            pl.BlockSpec((1, dl), c2), pl.BlockSpec((1, dl), c2), pl.BlockSpec((1, dl), c2), pl.BlockSpec((1, dl), c2),
            pl.BlockSpec((bs, dl), c2), pl.BlockSpec((bs, ntail * dl), c2),
        ],
        out_specs=(pl.BlockSpec((ls, bs, dl), c3), pl.BlockSpec((bs, dl), c2), pl.BlockSpec((bs, ntail * dl), c2)),
        scratch_shapes=[pltpu.VMEM((ls, bs, dl), F32)],
        compiler_params=pltpu.CompilerParams(
            dimension_semantics=("arbitrary",),
            vmem_limit_bytes=_vmem_limit(pipelined, [((ls, bs, dl), F32)])),
        name="lru_slab",
    )(proj3, proj3, cw, cb, wa, wx, ba, bx, lam, g, h0, tail)


def _head_lane_mask(n_heads):
    return lax.broadcasted_iota(jnp.int32, (1, HEAD_PAD), 1) < n_heads


def _gated_group_norm(y, z_act, g):
    yg = y * z_act
    scale = lax.rsqrt(jnp.mean(yg * yg, axis=-1, keepdims=True) + EPS)
    return yg * scale * g


def _ssd_seq_kernel(z_ref, xbc_ref, dt_ref, cw_ref, cb_ref, dtb_ref, alog_ref, dskip_ref, g_ref, e3_ref,
                    s0_ref, tail0_ref, *rest, pad, q, n_heads, n_groups, n_chunks, n_live, fuse_out):
    if fuse_out:
        wo_ref, part_ref, res_ref = rest[:3]
        o_ref, sfin_ref, tailfin_ref, xe_ref, xc_ref, ex_ref, y_ref, s_ref, pair_ref, yn_ref, ynp_ref = rest[3:]
    else:
        o_ref, sfin_ref, tailfin_ref, xe_ref, xc_ref, ex_ref, y_ref, s_ref, pair_ref = rest
        yn_ref = o_ref
    step = pl.program_id(0)
    live = step < n_live
    c = lax.rem(jnp.minimum(step, n_live - 1), n_chunks)
    keep = (lambda new, old: jnp.where(live, new, old)) if fuse_out else (lambda new, old: new)
    tq, ds = z_ref.shape
    cps = tq // q
    dc = xbc_ref.shape[1]
    gn = (dc - ds) // 2
    n = gn // n_groups
    p = ds // n_heads
    r = n_heads // n_groups
    gw = ds // n_groups
    hpb = LANES // p

    @pl.when(jnp.logical_and(c == 0, live))
    def _():
        xe_ref[0:HALO, :] = tail0_ref[...]
        s_ref[...] = s0_ref[...]

    if fuse_out:
        @pl.when(step == 0)
        def _():
            ynp_ref[...] = jnp.zeros_like(ynp_ref)

    xe_ref[HALO:HALO + tq, :] = xbc_ref[...]
    valid_all = (c * tq + lax.broadcasted_iota(jnp.int32, (tq, 1), 0)) >= pad

    cblk = CONV_LANE_BLOCK if dc % CONV_LANE_BLOCK == 0 else LANES
    for j in range(dc // cblk):
        sl = slice(j * cblk, (j + 1) * cblk)
        v = _silu(_conv_rows(xe_ref, pair_ref.at[j % 2], cw_ref, cb_ref, sl, tq))
        if pad and (j + 1) * cblk <= ds:
            v = jnp.where(valid_all, v, 0.0)
        xc_ref[:, sl] = v

    ri = lax.broadcasted_iota(jnp.int32, (q, q), 0)
    ci = lax.broadcasted_iota(jnp.int32, (q, q), 1)
    causal = ci <= ri
    tri = jnp.where(causal, 1.0, 0.0).astype(BF16)
    lane = lax.broadcasted_iota(jnp.int32, (q, LANES), 1)
    a_neg = -jnp.exp(alog_ref[...])
    n_pieces = cps * n_groups
    piece = o_ref.shape[1] // n_pieces if fuse_out else 0

    for u in range(cps):
        rows = slice(u * q, (u + 1) * q)
        dtv = jnp.where(_head_lane_mask(n_heads), jax.nn.softplus(dt_ref[rows, :] + dtb_ref[...]), 0.0)
        if pad:
            dtv = jnp.where(valid_all[u * q:(u + 1) * q], dtv, 0.0)
        ac3 = jnp.dot(tri, _split3(dtv * a_neg), preferred_element_type=F32)
        acum = ac3[:, 0:HEAD_PAD] + ac3[:, HEAD_PAD:2 * HEAD_PAD] + ac3[:, 2 * HEAD_PAD:3 * HEAD_PAD]
        alast = acum[q - 1:q, :]
        eac = jnp.exp(acum)
        wend = jnp.exp(alast - acum) * dtv
        cdec = jnp.broadcast_to(jnp.exp(alast), (SUBLANES, HEAD_PAD))
        ex_ref[...] = _expand_heads(jnp.concatenate([eac, wend, cdec], axis=0), e3_ref)
        acum_t = acum.T
        dt_t = dtv.T

        for g in range(n_groups):
            gsl = slice(g * gw, (g + 1) * gw)
            bg = xc_ref[rows, ds + g * n:ds + (g + 1) * n].astype(BF16)
            cg = xc_ref[rows, ds + gn + g * n:ds + gn + (g + 1) * n].astype(BF16)
            cbm = lax.dot_general(cg, bg, _NT, preferred_element_type=F32)
            yoff = jnp.dot(cg, s_ref[:, gsl].astype(BF16), preferred_element_type=F32)
            for k in range(gw // LANES):
                lsl = slice(g * gw + k * LANES, g * gw + (k + 1) * LANES)
                xs = xc_ref[rows, lsl]
                ms = []
                xparts = []
                for w in range(hpb):
                    h = g * r + k * hpb + w
                    seg = acum[:, h:h + 1] - acum_t[h:h + 1, :]
                    lm = jnp.where(causal, jnp.exp(seg), 0.0) * dt_t[h:h + 1, :]
                    ms.append((cbm * lm).astype(BF16))
                    inhead = (lane >= w * p) & (lane < (w + 1) * p)
                    xparts.append(jnp.where(inhead, xs, 0.0).astype(BF16))
                ydiag = jnp.dot(jnp.concatenate(ms, axis=1), jnp.concatenate(xparts, axis=0),
                                preferred_element_type=F32)
                y = ydiag + yoff[:, k * LANES:(k + 1) * LANES] * ex_ref[0:q, lsl]
                y_ref[rows, lsl] = y + dskip_ref[:, lsl] * xs
            yn_ref[rows, gsl] = _gated_group_norm(y_ref[rows, gsl], z_ref[rows, gsl], g_ref[:, gsl]).astype(BF16)
            xw = (xc_ref[rows, gsl] * ex_ref[q:2 * q, gsl]).astype(BF16)
            upd = lax.dot_general(bg, xw, (((0,), (0,)), ((), ())), preferred_element_type=F32)
            s_old = s_ref[:, gsl]
            s_ref[:, gsl] = keep(ex_ref[2 * q:2 * q + 1, gsl] * s_old + upd, s_old)
            if fuse_out:
                psl = slice((u * n_groups + g) * piece, (u * n_groups + g + 1) * piece)
                mix = part_ref[:, psl] + jnp.dot(ynp_ref[...], wo_ref[:, psl], preferred_element_type=F32)
                o_ref[:, psl] = res_ref[:, psl] + mix

    if fuse_out:
        ynp_ref[...] = yn_ref[...]
    xe_ref[0:HALO, :] = keep(xe_ref[tq:tq + HALO, :], xe_ref[0:HALO, :])
    sfin_ref[0] = s_ref[...]
    tailfin_ref[0] = xe_ref[HALO - 3:HALO, :]


def _ssd_seq(proj, dt, params, s0, tail0, *, n_seq, seq_len, row0, n_heads, n_groups, dl, pad=0, fuse=None):
    cw, cb, dtb, alog, dskip, g, e3 = params
    dc = cw.shape[1]
    ds = dskip.shape[1]
    n = s0.shape[0]
    q = SSD_CHUNK
    tq = q * (SSD_CHUNKS_PER_STEP if seq_len % (q * SSD_CHUNKS_PER_STEP) == 0 else 1)
    assert seq_len % tq == 0 and row0 % tq == 0
    assert (2 * dl) % ds == 0 and (2 * dl + ds) % dc == 0 and LANES % (ds // n_heads) == 0
    n_chunks = seq_len // tq
    n_live = n_seq * n_chunks
    blk0 = row0 // tq
    fuse_out = fuse is not None
    chunk = (lambda s: jnp.minimum(s, n_live - 1)) if fuse_out else (lambda s: s)
    rows = lambda s: blk0 + chunk(s)
    const2 = lambda s: (0, 0)
    out_rows = (lambda s: (jnp.maximum(s - 1, 0), 0)) if fuse_out else (lambda s: (s, 0))
    per_seq = lambda s: (chunk(s) // n_chunks, 0, 0)
    dout, out_dtype = (fuse[0].shape[1], F32) if fuse_out else (ds, BF16)
    pipelined = [((tq, ds), F32), ((tq, dc), F32), ((tq, HEAD_PAD), F32), ((tq, dout), out_dtype)]
    resident = [((tq + HALO, dc), F32), ((tq, dc), F32), ((2 * q + SUBLANES, ds), F32), ((tq, ds), F32),
                ((3 * n, ds), F32), ((6 * HEAD_PAD, ds), BF16), ((2, tq + SUBLANES, CONV_LANE_BLOCK), F32)]
    cblk = CONV_LANE_BLOCK if dc % CONV_LANE_BLOCK == 0 else LANES
    scratch = [pltpu.VMEM((tq + HALO, dc), F32), pltpu.VMEM((tq, dc), F32),
               pltpu.VMEM((2 * q + SUBLANES, ds), F32), pltpu.VMEM((tq, ds), F32), pltpu.VMEM((n, ds), F32),
               pltpu.VMEM((2, tq + SUBLANES, cblk), F32)]
    extra_specs, extra_args = [], []
    if fuse_out:
        extra_specs = [pl.BlockSpec((ds, dout), const2, pipeline_mode=pl.Buffered(1)),
                       pl.BlockSpec((tq, dout), out_rows), pl.BlockSpec((tq, dout), out_rows)]
        extra_args = list(fuse)
        pipelined += [((tq, dout), F32)] * 2
        resident += [((ds, dout), BF16), ((2 * tq, ds), BF16)]
        scratch += [pltpu.VMEM((tq, ds), BF16), pltpu.VMEM((tq, ds), BF16)]
    kern = functools.partial(_ssd_seq_kernel, pad=pad, q=q, n_heads=n_heads, n_groups=n_groups, n_chunks=n_chunks,
                             n_live=n_live, fuse_out=fuse_out)
    return pl.pallas_call(
        kern,
        out_shape=(jax.ShapeDtypeStruct((n_seq * seq_len, dout), out_dtype),
                   jax.ShapeDtypeStruct((n_seq, n, ds), F32),
                   jax.ShapeDtypeStruct((n_seq, CONV_TAPS - 1, dc), F32)),
        grid=(n_live + 1 if fuse_out else n_live,),
        in_specs=[
            pl.BlockSpec((tq, ds), lambda s: (rows(s), (2 * dl) // ds)),
            pl.BlockSpec((tq, dc), lambda s: (rows(s), (2 * dl + ds) // dc)),
            pl.BlockSpec((tq, HEAD_PAD), lambda s: (rows(s), 0)),
            pl.BlockSpec((CONV_TAPS, dc), const2),
            pl.BlockSpec((1, dc), const2),
            pl.BlockSpec((1, HEAD_PAD), const2),
            pl.BlockSpec((1, HEAD_PAD), const2),
            pl.BlockSpec((1, ds), const2),
            pl.BlockSpec((1, ds), const2),
            pl.BlockSpec((3 * HEAD_PAD, ds), const2),
            pl.BlockSpec((n, ds), const2),
            pl.BlockSpec((HALO, dc), const2),
        ] + extra_specs,
        out_specs=(
            pl.BlockSpec((tq, dout), out_rows),
            pl.BlockSpec((1, n, ds), per_seq),
            pl.BlockSpec((1, CONV_TAPS - 1, dc), per_seq),
        ),
        scratch_shapes=scratch,
        compiler_params=pltpu.CompilerParams(
            dimension_semantics=("arbitrary",),
            vmem_limit_bytes=_vmem_limit(pipelined, resident)),
        name="ssd_seq",
    )(proj, proj, dt, cw, cb, dtb, alog, dskip, g, e3, s0, tail0, *extra_args)


def _ssd_slab_pre_kernel(xbc_ref, dt_ref, tail_ref, cw_ref, cb_ref, dtb_ref, alog_ref, dskip_ref, e3_ref,
                         ypart_ref, eace_ref, c_ref, b_ref, xw_ref, cdec_ref, tailfin_ref, xc_ref,
                         *, n_heads, n_groups):
    ls, bs, dc = xbc_ref.shape
    ds = dskip_ref.shape[1]
    gn = (dc - ds) // 2
    n = gn // n_groups
    r = n_heads // n_groups
    ntail = CONV_TAPS - 1

    cblk = CONV_LANE_BLOCK if dc % CONV_LANE_BLOCK == 0 else LANES
    for j in range(dc // cblk):
        sl = slice(j * cblk, (j + 1) * cblk)
        ext = [tail_ref[:, k * dc + j * cblk:k * dc + (j + 1) * cblk] for k in range(ntail)]
        ext += [xbc_ref[s, :, sl] for s in range(ls)]
        for s in range(ls):
            xc_ref[s, :, sl] = _silu(_conv_taps(ext, s, cw_ref, cb_ref, sl))
        for k in range(ntail):
            tailfin_ref[:, k * dc + j * cblk:k * dc + (j + 1) * cblk] = ext[ls + k]

    hmask = _head_lane_mask(n_heads)
    a_neg = -jnp.exp(alog_ref[...])
    dtv, acum = [], []
    run = jnp.zeros((bs, HEAD_PAD), F32)
    for s in range(ls):
        d = jnp.where(hmask, jax.nn.softplus(dt_ref[s] + dtb_ref[...]), 0.0)
        run = run + d * a_neg
        dtv.append(d)
        acum.append(run)
    alast = acum[ls - 1]
    cdec_ref[...] = jnp.exp(alast)
    head_group = lax.broadcasted_iota(jnp.int32, (1, HEAD_PAD), 1) // r

    for s in range(ls):
        eace_ref[s] = _expand_heads(jnp.exp(acum[s]), e3_ref)
        wend_e = _expand_heads(jnp.exp(alast - acum[s]) * dtv[s], e3_ref)
        xw_ref[:, s * ds:(s + 1) * ds] = xc_ref[s, :, 0:ds] * wend_e
        b_ref[:, s * gn:(s + 1) * gn] = xc_ref[s, :, ds:ds + gn]
        c_ref[:, s * gn:(s + 1) * gn] = xc_ref[s, :, ds + gn:ds + 2 * gn]
        ypart = dskip_ref[...] * xc_ref[s, :, 0:ds]
        for j in range(s + 1):
            cbh = jnp.zeros((bs, HEAD_PAD), F32)
            for g in range(n_groups):
                cs = xc_ref[s, :, ds + gn + g * n:ds + gn + (g + 1) * n]
                bj = xc_ref[j, :, ds + g * n:ds + (g + 1) * n]
                cbg = jnp.sum(cs * bj, axis=-1, keepdims=True)
                cbh = cbh + jnp.where(head_group == g, cbg, 0.0)
            coef = cbh * (jnp.exp(acum[s] - acum[j]) * dtv[j])
            ypart = ypart + _expand_heads(coef, e3_ref) * xc_ref[j, :, 0:ds]
        ypart_ref[s] = ypart
    for s in range(ls, SLAB_ROWS):
        xw_ref[:, s * ds:(s + 1) * ds] = jnp.zeros((bs, ds), F32)
        b_ref[:, s * gn:(s + 1) * gn] = jnp.zeros((bs, gn), F32)
        c_ref[:, s * gn:(s + 1) * gn] = jnp.zeros((bs, gn), F32)


def _ssd_slab_pre(proj3, dt3, tail, params, *, ls, n_heads, n_groups, dl):
    cw, cb, dtb, alog, dskip, _, e3 = params
    bs = proj3.shape[1]
    dc = cw.shape[1]
    ds = dskip.shape[1]
    gn = (dc - ds) // 2
    ntail = CONV_TAPS - 1
    assert ls <= SLAB_ROWS and (2 * dl + ds) % dc == 0
    c2 = lambda i: (0, 0)
    c3 = lambda i: (0, 0, 0)
    pipelined = [((ls, bs, dc), F32), ((ls, bs, HEAD_PAD), F32), ((bs, 2 * ntail * dc), F32),
                 ((2 * ls, bs, ds), F32), ((bs, SLAB_ROWS * (2 * gn + ds)), F32), ((3 * HEAD_PAD, ds), BF16)]
    kern = functools.partial(_ssd_slab_pre_kernel, n_heads=n_heads, n_groups=n_groups)
    return pl.pallas_call(
        kern,
        out_shape=(jax.ShapeDtypeStruct((ls, bs, ds), F32),
                   jax.ShapeDtypeStruct((ls, bs, ds), F32),
                   jax.ShapeDtypeStruct((bs, SLAB_ROWS * gn), F32),
                   jax.ShapeDtypeStruct((bs, SLAB_ROWS * gn), F32),
                   jax.ShapeDtypeStruct((bs, SLAB_ROWS * ds), F32),
                   jax.ShapeDtypeStruct((bs, HEAD_PAD), F32),
                   jax.ShapeDtypeStruct((bs, ntail * dc), F32)),
        grid=(1,),
        in_specs=[
            pl.BlockSpec((ls, bs, dc), lambda i: (0, 0, (2 * dl + ds) // dc)),
            pl.BlockSpec((ls, bs, HEAD_PAD), c3),
            pl.BlockSpec((bs, ntail * dc), c2),
            pl.BlockSpec((CONV_TAPS, dc), c2), pl.BlockSpec((1, dc), c2),
            pl.BlockSpec((1, HEAD_PAD), c2), pl.BlockSpec((1, HEAD_PAD), c2),
            pl.BlockSpec((1, ds), c2), pl.BlockSpec((3 * HEAD_PAD, ds), c2),
        ],
        out_specs=(pl.BlockSpec((ls, bs, ds), c3), pl.BlockSpec((ls, bs, ds), c3),
                   pl.BlockSpec((bs, SLAB_ROWS * gn), c2), pl.BlockSpec((bs, SLAB_ROWS * gn), c2),
                   pl.BlockSpec((bs, SLAB_ROWS * ds), c2), pl.BlockSpec((bs, HEAD_PAD), c2),
                   pl.BlockSpec((bs, ntail * dc), c2)),
        scratch_shapes=[pltpu.VMEM((ls, bs, dc), F32)],
        compiler_params=pltpu.CompilerParams(
            dimension_semantics=("arbitrary",),
            vmem_limit_bytes=_vmem_limit(pipelined, [((ls, bs, dc), F32)])),
        name="ssd_slab_pre",
    )(proj3, dt3, tail, cw, cb, dtb, alog, dskip, e3)


def _ssd_state_kernel(cdec_ref, s_ref, c_ref, b_ref, xw_ref, snew_ref, yoff_ref, *, n_heads, n_groups):
    i = pl.program_id(0)
    sb, hp, n = s_ref.shape
    p = hp // n_heads
    r = n_heads // n_groups
    gw = hp // n_groups
    for q in range(sb):
        for g in range(n_groups):
            gsl = slice(g * gw, (g + 1) * gw)
            sg = s_ref[q, gsl, :]
            cg = c_ref[q, :, g * n:(g + 1) * n].astype(BF16)
            yoff_ref[q, :, gsl] = lax.dot_general(cg, sg.astype(BF16), (((1,), (1,)), ((), ())),
                                                  preferred_element_type=F32)
            upd = lax.dot_general(xw_ref[q, :, gsl].astype(BF16), b_ref[q, :, g * n:(g + 1) * n].astype(BF16),
                                  (((0,), (0,)), ((), ())), preferred_element_type=F32)
            for u in range(r):
                h = g * r + u
                rows = slice(h * p, (h + 1) * p)
                snew_ref[q, rows, :] = cdec_ref[i * sb + q, h] * s_ref[q, rows, :] + upd[u * p:(u + 1) * p, :]


def _ssd_state(cdec, state, c_rows, b_rows, xw_rows, *, n_heads, n_groups):
    bs, hp, n = state.shape
    gn = c_rows.shape[2]
    sb = STATE_SEQS_PER_STEP if bs % STATE_SEQS_PER_STEP == 0 else 1
    per_seq = lambda i: (i, 0, 0)
    pipelined = [((sb, hp, n), F32)] * 2 + [((sb, SLAB_ROWS, gn), F32)] * 2 + [((sb, SLAB_ROWS, hp), F32)] * 2
    kern = functools.partial(_ssd_state_kernel, n_heads=n_heads, n_groups=n_groups)
    return pl.pallas_call(
        kern,
        out_shape=(jax.ShapeDtypeStruct((bs, hp, n), F32), jax.ShapeDtypeStruct((bs, SLAB_ROWS, hp), F32)),
        grid=(bs // sb,),
        in_specs=[
            pl.BlockSpec(memory_space=pltpu.SMEM),
            pl.BlockSpec((sb, hp, n), per_seq),
            pl.BlockSpec((sb, SLAB_ROWS, gn), per_seq),
            pl.BlockSpec((sb, SLAB_ROWS, gn), per_seq),
            pl.BlockSpec((sb, SLAB_ROWS, hp), per_seq),
        ],
        out_specs=(pl.BlockSpec((sb, hp, n), per_seq), pl.BlockSpec((sb, SLAB_ROWS, hp), per_seq)),
        compiler_params=pltpu.CompilerParams(
            dimension_semantics=("parallel",),
            vmem_limit_bytes=_vmem_limit(pipelined, [])),
        name="ssd_state",
    )(cdec, state, c_rows, b_rows, xw_rows)


def _ssd_slab_post_kernel(ypart_ref, eace_ref, yoff_ref, z_ref, g_ref, o_ref, *, n_groups):
    ls, bs, ds = ypart_ref.shape
    gw = ds // n_groups
    for s in range(ls):
        for g in range(n_groups):
            gsl = slice(g * gw, (g + 1) * gw)
            y = ypart_ref[s, :, gsl] + eace_ref[s, :, gsl] * yoff_ref[:, s * ds + g * gw:s * ds + (g + 1) * gw]
            o_ref[s, :, gsl] = _gated_group_norm(y, z_ref[s, :, gsl], g_ref[:, gsl]).astype(o_ref.dtype)


def _ssd_slab_post(ypart, eace, yoff, proj3, g, *, n_groups, dl):
    ls, bs, ds = ypart.shape
    assert (2 * dl) % ds == 0
    c2 = lambda i: (0, 0)
    c3 = lambda i: (0, 0, 0)
    pipelined = [((ls, bs, ds), F32)] * 3 + [((bs, SLAB_ROWS * ds), F32), ((ls, bs, ds), BF16)]
    kern = functools.partial(_ssd_slab_post_kernel, n_groups=n_groups)
    return pl.pallas_call(
        kern,
        out_shape=jax.ShapeDtypeStruct((ls, bs, ds), BF16),
        grid=(1,),
        in_specs=[
            pl.BlockSpec((ls, bs, ds), c3), pl.BlockSpec((ls, bs, ds), c3),
            pl.BlockSpec((bs, SLAB_ROWS * ds), c2),
            pl.BlockSpec((ls, bs, ds), lambda i: (0, 0, (2 * dl) // ds)),
            pl.BlockSpec((1, ds), c2),
        ],
        out_specs=pl.BlockSpec((ls, bs, ds), c3),
        compiler_params=pltpu.CompilerParams(
            dimension_semantics=("arbitrary",),
            vmem_limit_bytes=_vmem_limit(pipelined, [])),
        name="ssd_slab_post",
    )(ypart, eace, yoff, proj3, g)


def _head_expansion(n_heads, head_dim):
    rows = lax.broadcasted_iota(jnp.int32, (HEAD_PAD, n_heads * head_dim), 0)
    cols = lax.broadcasted_iota(jnp.int32, (HEAD_PAD, n_heads * head_dim), 1)
    e = (cols // head_dim == rows).astype(BF16)
    return jnp.concatenate([e, e, e], axis=0)


def _pad_lanes(v, width):
    return jnp.pad(v, ((0, 0), (0, width - v.shape[1])))


def _tail_block(tail):
    return jnp.pad(tail, ((HALO - tail.shape[0], 0), (0, 0)))


def _mlp(x1, g_mlp, w_up, w_down, g_final, *, layer=None):
    m = x1.shape[0]
    tiles = (1024, 512, 256, 128)
    tm = m if layer is not None else _pick_tile(m, tiles)
    up_tiles = tiles if layer is not None else (2048,) + tiles
    up = _mlp_up(x1, g_mlp, w_up, tm=tm, tn=_pick_tile(w_up.shape[-1], up_tiles), layer=layer)
    hid = up[0]
    down = _mlp_down_final(hid, w_down, x1, g_final, tm=tm, tk=_pick_tile(w_down.shape[-2], tiles), layer=layer)
    if layer is None:
        return down[0]
    return down[0], up[1], down[1]


def kernel(x_prompt, x_sample, state_lru_h, state_lru_conv, state_ssd, state_ssd_conv, meta_tokens, g_mix, w_in, conv_lru_w, conv_lru_b, lru_wa, lru_ba, lru_wx, lru_bx, lru_lambda, g_lru_out, conv_ssd_w, conv_ssd_b, dt_bias, a_log, d_skip, g_ssd_out, w_out, g_mlp, w_up, w_down, g_final):
    depth = w_in.shape[0]
    assert depth == 1, "single-layer step"
    l = 0
    bp, lp, d = x_prompt.shape
    bs, ls, _ = x_sample.shape
    n_meta = meta_tokens.shape[0]
    dl = state_lru_h.shape[-1]
    n_heads, p, n = state_ssd.shape[-3:]
    ds = n_heads * p
    dc = state_ssd_conv.shape[-1]
    gn = (dc - ds) // 2
    n_groups = gn // n
    nw = 2 * dl + ds + dc
    ntail = CONV_TAPS - 1
    q = SSD_CHUNK
    meta_pad = (-n_meta) % q
    assert n_heads <= HEAD_PAD and (bs * ls) % q == 0 and lp % q == 0 and q % bs == 0

    row = lambda v: v.reshape(1, -1).astype(F32)
    w_in_t = jnp.swapaxes(w_in, 1, 2)
    lru_params = (conv_lru_w[l], row(conv_lru_b[l]), (0.5 * lru_wa[l]).astype(BF16), (0.5 * lru_wx[l]).astype(BF16),
                  row(0.5 * lru_ba[l]), row(0.5 * lru_bx[l]), row(lru_lambda[l]), row(g_lru_out[l]))
    ssd_params = (conv_ssd_w[l], row(conv_ssd_b[l]), _pad_lanes(row(dt_bias[l]), HEAD_PAD),
                  _pad_lanes(row(a_log[l]), HEAD_PAD), row(jnp.repeat(d_skip[l], p)), row(g_ssd_out[l]),
                  _head_expansion(n_heads, p))
    g_mix_r, g_mlp_r, g_final_r = row(g_mix[l]), row(g_mlp[l]), row(g_final)

    xs_tm = x_sample.transpose(1, 0, 2).reshape(ls * bs, d)
    x_side = jnp.concatenate([xs_tm, jnp.zeros((meta_pad, d), F32), meta_tokens.astype(F32)], axis=0)
    xp_rows = x_prompt.reshape(bp * lp, d)

    tiles = (1024, 512, 256, 128)
    tn_in = next(t for t in tiles if nw % t == 0 and dl % t == 0 and ds % t == 0)
    act_cols = dict(gelu_cols=(0, dl), silu_cols=(2 * dl, 2 * dl + ds))
    proj_side, dt_side, w_in_b, w_dt_b = _in_proj(x_side, g_mix_r, w_in_t, nw=nw, n_dt=n_heads, tm=x_side.shape[0],
                                                  tn=tn_in, layer=l, **act_cols)

    proj_s3 = proj_side.reshape(-1, bs, nw)
    dt_s3 = dt_side.reshape(-1, bs, HEAD_PAD)
    lru_s, s_h, s_ltail = _lru_slab(proj_s3, lru_params, state_lru_h[l],
                                    state_lru_conv[l].reshape(bs, ntail * dl), ls=ls)
    ypart, eace, c_rows, b_rows, xw_rows, cdec, s_stail = _ssd_slab_pre(
        proj_s3, dt_s3, state_ssd_conv[l].reshape(bs, ntail * dc), ssd_params,
        ls=ls, n_heads=n_heads, n_groups=n_groups, dl=dl)
    s_new, yoff = _ssd_state(cdec, state_ssd[l].reshape(bs, ds, n),
                             c_rows.reshape(bs, SLAB_ROWS, gn), b_rows.reshape(bs, SLAB_ROWS, gn),
                             xw_rows.reshape(bs, SLAB_ROWS, ds), n_heads=n_heads, n_groups=n_groups)
    ssd_s = _ssd_slab_post(ypart, eace, yoff.reshape(bs, SLAB_ROWS * ds), proj_s3, ssd_params[5],
                           n_groups=n_groups, dl=dl)
    x1_s, w_out_lru, w_out_ssd = _out_proj(lru_s.reshape(ls * bs, dl), ssd_s.reshape(ls * bs, ds), w_out, l, xs_tm,
                                           tn=_pick_tile(d, (512, 256, 128)))
    y_s, w_up_b, w_down_b = _mlp(x1_s, g_mlp_r, w_up, w_down, g_final_r, layer=l)

    meta_row0 = ls * bs
    _, m_h, m_ltail = _lru_seq(proj_side, lru_params, jnp.zeros((1, dl), F32), jnp.zeros((HALO, dl), F32),
                               n_seq=1, seq_len=q, t=q, row0=meta_row0, pad=meta_pad, reset_first=True)
    _, m_s, m_stail = _ssd_seq(proj_side, dt_side, ssd_params, jnp.zeros((n, ds), F32), jnp.zeros((HALO, dc), F32),
                               n_seq=1, seq_len=q, row0=meta_row0, n_heads=n_heads, n_groups=n_groups, dl=dl,
                               pad=meta_pad)

    proj_p, dt_p = _in_proj(xp_rows, g_mix_r, w_in_b, w_dt_b, nw=nw, n_dt=n_heads, tm=_pick_tile(bp * lp, tiles),
                            tn=tn_in, **act_cols)
    part_p, p_h, p_ltail = _lru_seq(proj_p, lru_params, m_h[0], _tail_block(m_ltail[0]),
                                    n_seq=bp, seq_len=lp, t=_pick_tile(lp, (256, 128)), row0=0, wo=w_out_lru)
    x1_p, p_s, p_stail = _ssd_seq(proj_p, dt_p, ssd_params, m_s[0], _tail_block(m_stail[0]),
                                  n_seq=bp, seq_len=lp, row0=0, n_heads=n_heads, n_groups=n_groups, dl=dl,
                                  fuse=(w_out_ssd, part_p, xp_rows))
    y_p = _mlp(x1_p, g_mlp_r, w_up_b, w_down_b, g_final_r)

    y_prompt = y_p.reshape(bp, lp, d)
    y_sample = y_s.reshape(ls, bs, d).transpose(1, 0, 2)
    p_lru_h = p_h.reshape(1, bp, dl)
    p_lru_conv = p_ltail.reshape(1, bp, ntail, dl)
    p_ssd = p_s.transpose(0, 2, 1).reshape(1, bp, n_heads, p, n)
    p_ssd_conv = p_stail.reshape(1, bp, ntail, dc)
    s_lru_h = s_h.reshape(1, bs, dl)
    s_lru_conv = s_ltail.reshape(1, bs, ntail, dl)
    s_ssd = s_new.reshape(1, bs, n_heads, p, n)
    s_ssd_conv = s_stail.reshape(1, bs, ntail, dc)
    return (y_prompt, y_sample, p_lru_h, p_lru_conv, p_ssd, p_ssd_conv, s_lru_h, s_lru_conv, s_ssd, s_ssd_conv)
```

```python
import functools

import jax
import jax.numpy as jnp
from jax import lax
from jax.experimental import pallas as pl
from jax.experimental.pallas import tpu as pltpu

F32 = jnp.float32
BF16 = jnp.bfloat16

EPS = 1e-6
LRU_C = 8.0
CONV_TAPS = 4

LANES = 128
SUBLANES = 8
MXU_COLS = 256
VMEM_BYTES_V7X = 64 * 1024 * 1024
VMEM_TEMP_BYTES = 10 * 1024 * 1024
VMEM_CEILING_BYTES = VMEM_BYTES_V7X - 6 * 1024 * 1024

HALO = 2 * SUBLANES
SSD_CHUNK = 128
SSD_CHUNKS_PER_STEP = 2
HEAD_PAD = LANES
SLAB_ROWS = SUBLANES
STATE_SEQS_PER_STEP = 8
CONV_LANE_BLOCK = 512
MATMUL_TILES = (1024, 512, 256, 128)


def _nbytes(shape, dtype):
    n = 1
    for s in shape:
        n *= s
    return n * jnp.dtype(dtype).itemsize


def _vmem_limit(pipelined, resident):
    est = 2 * sum(_nbytes(s, d) for s, d in pipelined) + sum(_nbytes(s, d) for s, d in resident)
    return int(min(est + VMEM_TEMP_BYTES, VMEM_CEILING_BYTES))


def _pick_tile(m, prefs):
    for t in prefs:
        if m % t == 0:
            return t
    return m


def _silu(x):
    h = 0.5 * x
    return h * jnp.tanh(h) + h


def _split3(x):
    hi = x.astype(BF16)
    r1 = x - hi.astype(F32)
    mid = r1.astype(BF16)
    lo = (r1 - mid.astype(F32)).astype(BF16)
    return jnp.concatenate([hi, mid, lo], axis=1)


def _expand_heads(x, e3_ref):
    return jnp.dot(_split3(x), e3_ref[...], preferred_element_type=F32)


def _rmsnorm_rows(x, g):
    ms = jnp.mean(x * x, axis=-1, keepdims=True)
    return x * lax.rsqrt(ms + EPS) * g


def _conv_rows(xe_ref, pair_ref, cw_ref, cb_ref, sl, t):
    lo = HALO - SUBLANES
    x0 = xe_ref[lo:HALO + t, sl]
    x1 = xe_ref[lo - 1:HALO + t - 1, sl]
    pair_ref[...] = x0 * cw_ref[1:2, sl] + x1 * cw_ref[0:1, sl]
    near = x0[SUBLANES:] * cw_ref[3:4, sl] + x1[SUBLANES:] * cw_ref[2:3, sl]
    return cb_ref[:, sl] + near + pair_ref[SUBLANES - 2:SUBLANES - 2 + t, :]


def _conv_taps(ext, s, cw_ref, cb_ref, sl):
    v = cb_ref[:, sl] + ext[s] * cw_ref[0:1, sl]
    for k in range(1, CONV_TAPS):
        v = v + ext[s + k] * cw_ref[k:k + 1, sl]
    return v


def _weight_tile(w_ref, wb_ref):
    if wb_ref is None:
        return w_ref[...]
    w = w_ref[...].astype(BF16)
    wb_ref[...] = w
    return w


def _weight_specs(layer, blk, idx):
    if layer is None:
        return pl.BlockSpec(blk, idx), None
    return pl.BlockSpec((None,) + blk, lambda i, j: (layer,) + idx(i, j)), pl.BlockSpec(blk, idx)


_NT = (((1,), (1,)), ((), ()))


def _in_proj_kernel(x_ref, g_ref, w_ref, wdt_ref, o_ref, dt_ref, *rest, gelu_tiles, silu_tiles, emit_w):
    if emit_w:
        wb_ref, wdtb_ref, xn_ref = rest
    else:
        wb_ref, wdtb_ref, xn_ref = None, None, rest[0]
    j = pl.program_id(1)

    @pl.when(j == 0)
    def _():
        xn = _rmsnorm_rows(x_ref[...], g_ref[...]).astype(BF16)
        xn_ref[...] = xn
        wdt = _weight_tile(wdt_ref, wdtb_ref)
        wdt = jnp.concatenate([wdt, jnp.zeros((HEAD_PAD - wdt.shape[0], wdt.shape[1]), BF16)], axis=0)
        dt_ref[...] = lax.dot_general(xn, wdt, _NT, preferred_element_type=F32)

    def tile():
        return lax.dot_general(xn_ref[...], _weight_tile(w_ref, wb_ref), _NT, preferred_element_type=F32)

    in_range = lambda r: jnp.logical_and(j >= r[0], j < r[1])
    is_gelu, is_silu = in_range(gelu_tiles), in_range(silu_tiles)

    @pl.when(is_gelu)
    def _():
        o_ref[...] = jax.nn.gelu(tile())

    @pl.when(is_silu)
    def _():
        o_ref[...] = _silu(tile())

    @pl.when(jnp.logical_not(jnp.logical_or(is_gelu, is_silu)))
    def _():
        o_ref[...] = tile()


def _in_proj(x, g, w_t, wdt_t=None, *, nw, n_dt, tm, tn, gelu_cols, silu_cols, layer=None):
    m, k = x.shape
    emit_w = layer is not None
    assert m % tm == 0 and nw % tn == 0 and all(c % tn == 0 for c in gelu_cols + silu_cols)
    assert n_dt % SUBLANES == 0 and nw % n_dt == 0 and (not emit_w or m == tm)
    out_shape = [jax.ShapeDtypeStruct((m, nw), F32), jax.ShapeDtypeStruct((m, HEAD_PAD), F32)]
    out_specs = [pl.BlockSpec((tm, tn), lambda i, j: (i, j)), pl.BlockSpec((tm, HEAD_PAD), lambda i, j: (i, 0))]
    pipelined = [((tm, k), F32), ((tn, k), w_t.dtype), ((n_dt, k), w_t.dtype), ((tm, tn), F32), ((tm, HEAD_PAD), F32)]
    if emit_w:
        w_spec = pl.BlockSpec((None, tn, k), lambda i, j: (layer, j, 0))
        wdt_spec = pl.BlockSpec((None, n_dt, k), lambda i, j: (layer, nw // n_dt, 0))
        wdt_t = w_t
        out_shape += [jax.ShapeDtypeStruct((nw, k), BF16), jax.ShapeDtypeStruct((n_dt, k), BF16)]
        out_specs += [pl.BlockSpec((tn, k), lambda i, j: (j, 0)), pl.BlockSpec((n_dt, k), lambda i, j: (0, 0))]
        pipelined += [((tn, k), BF16), ((n_dt, k), BF16)]
    else:
        w_spec = pl.BlockSpec((tn, k), lambda i, j: (j, 0))
        wdt_spec = pl.BlockSpec((n_dt, k), lambda i, j: (0, 0))
    kern = functools.partial(_in_proj_kernel, gelu_tiles=tuple(c // tn for c in gelu_cols),
                             silu_tiles=tuple(c // tn for c in silu_cols), emit_w=emit_w)
    return pl.pallas_call(
        kern,
        out_shape=tuple(out_shape),
        grid=(m // tm, nw // tn),
        in_specs=[
            pl.BlockSpec((tm, k), lambda i, j: (i, 0)),
            pl.BlockSpec((1, k), lambda i, j: (0, 0)),
            w_spec,
            wdt_spec,
        ],
        out_specs=tuple(out_specs),
        scratch_shapes=[pltpu.VMEM((tm, k), BF16)],
        compiler_params=pltpu.CompilerParams(
            dimension_semantics=("parallel", "arbitrary"),
            vmem_limit_bytes=_vmem_limit(pipelined, [((tm, k), BF16), ((tn, k), BF16)])),
        name="in_proj",
    )(x, g, w_t, wdt_t)


def _out_proj_kernel(a1_ref, a2_ref, w1_ref, w2_ref, res_ref, o_ref, wb1_ref, wb2_ref):
    acc = jnp.dot(a1_ref[...], _weight_tile(w1_ref, wb1_ref), preferred_element_type=F32)
    acc = acc + jnp.dot(a2_ref[...], _weight_tile(w2_ref, wb2_ref), preferred_element_type=F32)
    o_ref[...] = res_ref[...] + acc


def _out_proj(a1, a2, w, layer, res, *, tn):
    m, k1 = a1.shape
    k2 = a2.shape[1]
    n = w.shape[2]
    assert n % tn == 0 and k1 == k2 and w.shape[1] == k1 + k2
    pipelined = [((m, k1), BF16), ((m, k2), BF16), ((k1, tn), w.dtype), ((k2, tn), w.dtype), ((m, tn), F32),
                 ((m, tn), F32), ((k1, tn), BF16), ((k2, tn), BF16)]
    col = lambda j: (0, j)
    return pl.pallas_call(
        _out_proj_kernel,
        out_shape=(jax.ShapeDtypeStruct((m, n), F32), jax.ShapeDtypeStruct((k1, n), BF16),
                   jax.ShapeDtypeStruct((k2, n), BF16)),
        grid=(n // tn,),
        in_specs=[
            pl.BlockSpec((m, k1), lambda j: (0, 0)),
            pl.BlockSpec((m, k2), lambda j: (0, 0)),
            pl.BlockSpec((None, k1, tn), lambda j: (layer, 0, j)),
            pl.BlockSpec((None, k2, tn), lambda j: (layer, 1, j)),
            pl.BlockSpec((m, tn), col),
        ],
        out_specs=(pl.BlockSpec((m, tn), col), pl.BlockSpec((k1, tn), col), pl.BlockSpec((k2, tn), col)),
        compiler_params=pltpu.CompilerParams(
            dimension_semantics=("parallel",),
            vmem_limit_bytes=_vmem_limit(pipelined, [])),
        name="out_proj",
    )(a1, a2, w, w, res)


def _mlp_up_kernel(x_ref, g_ref, w_ref, o_ref, *rest, emit_w):
    wb_ref, xn_ref = rest if emit_w else (None, rest[0])

    @pl.when(pl.program_id(1) == 0)
    def _():
        xn_ref[...] = _rmsnorm_rows(x_ref[...], g_ref[...]).astype(BF16)

    acc = jnp.dot(xn_ref[...], _weight_tile(w_ref, wb_ref), preferred_element_type=F32)
    o_ref[...] = jnp.square(jnp.maximum(acc, 0.0)).astype(o_ref.dtype)


def _mlp_up(x, g, w, *, tm, tn, layer=None):
    m, k = x.shape
    n = w.shape[-1]
    emit_w = layer is not None
    assert m % tm == 0 and n % tn == 0 and (not emit_w or m == tm)
    w_spec, wb_spec = _weight_specs(layer, (k, tn), lambda i, j: (0, j))
    out_shape = [jax.ShapeDtypeStruct((m, n), BF16)]
    out_specs = [pl.BlockSpec((tm, tn), lambda i, j: (i, j))]
    pipelined = [((tm, k), F32), ((k, tn), w.dtype), ((tm, tn), BF16)]
    if emit_w:
        out_shape.append(jax.ShapeDtypeStruct((k, n), BF16))
        out_specs.append(wb_spec)
        pipelined.append(((k, tn), BF16))
    return pl.pallas_call(
        functools.partial(_mlp_up_kernel, emit_w=emit_w),
        out_shape=tuple(out_shape),
        grid=(m // tm, n // tn),
        in_specs=[
            pl.BlockSpec((tm, k), lambda i, j: (i, 0)),
            pl.BlockSpec((1, k), lambda i, j: (0, 0)),
            w_spec,
        ],
        out_specs=tuple(out_specs),
        scratch_shapes=[pltpu.VMEM((tm, k), BF16)],
        compiler_params=pltpu.CompilerParams(
            dimension_semantics=("parallel", "arbitrary"),
            vmem_limit_bytes=_vmem_limit(pipelined, [((tm, k), BF16), ((k, tn), BF16)])),
        name="mlp_up",
    )(x, g, w)


def _mlp_down_kernel(h_ref, w_ref, res_ref, g_ref, o_ref, *rest, emit_w):
    wb_ref = rest[0] if emit_w else None
    kk = pl.program_id(1)

    @pl.when(kk == 0)
    def _():
        o_ref[...] = res_ref[...]

    o_ref[...] += jnp.dot(h_ref[...], _weight_tile(w_ref, wb_ref), preferred_element_type=F32)

    @pl.when(kk == pl.num_programs(1) - 1)
    def _():
        o_ref[...] = _rmsnorm_rows(o_ref[...], g_ref[...])


def _mlp_down_final(h, w, res, g, *, tm, tk, layer=None):
    m, k = h.shape
    n = w.shape[-1]
    emit_w = layer is not None
    assert m % tm == 0 and k % tk == 0 and (not emit_w or m == tm)
    w_spec, wb_spec = _weight_specs(layer, (tk, n), lambda i, j: (j, 0))
    out_shape = [jax.ShapeDtypeStruct((m, n), F32)]
    out_specs = [pl.BlockSpec((tm, n), lambda i, j: (i, 0))]
    pipelined = [((tm, tk), BF16), ((tk, n), w.dtype), ((tm, n), F32), ((tm, n), F32)]
    if emit_w:
        out_shape.append(jax.ShapeDtypeStruct((k, n), BF16))
        out_specs.append(wb_spec)
        pipelined.append(((tk, n), BF16))
    return pl.pallas_call(
        functools.partial(_mlp_down_kernel, emit_w=emit_w),
        out_shape=tuple(out_shape),
        grid=(m // tm, k // tk),
        in_specs=[
            pl.BlockSpec((tm, tk), lambda i, j: (i, j)),
            w_spec,
            pl.BlockSpec((tm, n), lambda i, j: (i, 0)),
            pl.BlockSpec((1, n), lambda i, j: (0, 0)),
        ],
        out_specs=tuple(out_specs),
        compiler_params=pltpu.CompilerParams(
            dimension_semantics=("parallel", "arbitrary"),
            vmem_limit_bytes=_vmem_limit(pipelined, [((tk, n), BF16)] if emit_w else [])),
        name="mlp_down",
    )(h, w, res, g)


def _lru_gates(xh, wa_half, wx_half, ba_half, bx_half, hsp):
    xb = xh.astype(BF16)
    tr = jnp.tanh(jnp.dot(xb, wa_half, preferred_element_type=F32) + ba_half)
    ti = jnp.tanh(jnp.dot(xb, wx_half, preferred_element_type=F32) + bx_half)
    nla = tr * hsp + hsp
    a = jnp.exp(-nla)
    q = jnp.tanh(nla) * (1.0 + a * a)
    mult = jnp.where(q > 0.0, q * lax.rsqrt(q), 0.0)
    return a, mult, 0.5 * ti + 0.5


def _scan_rows(a, b, h_prev):
    t, hd = a.shape
    g = t // SUBLANES
    a3 = a.reshape(g, SUBLANES, hd)
    b3 = b.reshape(g, SUBLANES, hd)
    sub = lax.broadcasted_iota(jnp.int32, (g, SUBLANES, hd), 1)
    d = 1
    while d < SUBLANES:
        keep = sub >= d
        a_sh = jnp.where(keep, pltpu.roll(a3, d, axis=1), 1.0)
        b_sh = jnp.where(keep, pltpu.roll(b3, d, axis=1), 0.0)
        b3 = a3 * b_sh + b3
        a3 = a3 * a_sh
        d *= 2
    tiles = []
    h = h_prev
    for k in range(g):
        hk = a3[k] * h + b3[k]
        tiles.append(hk)
        h = hk[SUBLANES - 1:SUBLANES, :]
    return jnp.concatenate(tiles, axis=0), h


def _lru_seq_kernel(gate_ref, x_ref, cw_ref, cb_ref, wa_ref, wx_ref, ba_ref, bx_ref, lam_ref, g_ref,
                    h0_ref, tail0_ref, *rest, pad, reset_first, n_chunks, n_live, fuse_out):
    if fuse_out:
        wo_ref, o_ref, hfin_ref, tailfin_ref, xe_ref, hc_ref, y_ref, pair_ref, yn_ref = rest
    else:
        o_ref, hfin_ref, tailfin_ref, xe_ref, hc_ref, y_ref, pair_ref = rest
    step = pl.program_id(0)
    live = step < n_live
    c = lax.rem(jnp.minimum(step, n_live - 1), n_chunks)
    t, dl = x_ref.shape
    nh, hd = wa_ref.shape[0], wa_ref.shape[1]
    keep = (lambda new, old: jnp.where(live, new, old)) if fuse_out else (lambda new, old: new)

    @pl.when(jnp.logical_and(c == 0, live))
    def _():
        xe_ref[0:HALO, :] = tail0_ref[...]
        hc_ref[...] = h0_ref[...]

    if fuse_out:
        @pl.when(step == 0)
        def _():
            yn_ref[...] = jnp.zeros_like(yn_ref)

    xe_ref[HALO:HALO + t, :] = x_ref[...]
    grow = c * t + lax.broadcasted_iota(jnp.int32, (t, hd), 0)
    ssq = jnp.zeros((t, hd), F32)
    if fuse_out:
        n_pieces = max(1, min(nh, o_ref.shape[1] // MXU_COLS))
        while nh % n_pieces or o_ref.shape[1] % n_pieces:
            n_pieces -= 1
        heads_per_piece, piece = nh // n_pieces, o_ref.shape[1] // n_pieces
    for h in range(nh):
        sl = slice(h * hd, (h + 1) * hd)
        xh = _conv_rows(xe_ref, pair_ref.at[h % 2], cw_ref, cb_ref, sl, t)
        hsp = (0.5 * LRU_C) * jax.nn.softplus(-lam_ref[:, sl])
        a, mult, i = _lru_gates(xh, wa_ref[h], wx_ref[h], ba_ref[:, sl], bx_ref[:, sl], hsp)
        if reset_first:
            mult = jnp.where(grow == pad, 1.0, mult)
        b = mult * i * xh
        if pad:
            a = jnp.where(grow >= pad, a, 1.0)
            b = jnp.where(grow >= pad, b, 0.0)
        h_prev = hc_ref[:, sl]
        hs, h_last = _scan_rows(a, b, h_prev)
        hc_ref[:, sl] = keep(h_last, h_prev)
        y = hs * gate_ref[:, sl]
        y_ref[:, sl] = y
        ssq = ssq + y * y
        if fuse_out and (h + 1) % heads_per_piece == 0:
            k = (h + 1) // heads_per_piece - 1
            psl = slice(k * piece, (k + 1) * piece)
            o_ref[:, psl] = jnp.dot(yn_ref[...], wo_ref[:, psl], preferred_element_type=F32)
    scale = lax.rsqrt(jnp.sum(ssq, axis=-1, keepdims=True) / dl + EPS)
    yn = (y_ref[...] * scale * g_ref[...]).astype(BF16)
    if fuse_out:
        yn_ref[...] = yn
    else:
        o_ref[...] = yn
    xe_ref[0:HALO, :] = keep(xe_ref[t:t + HALO, :], xe_ref[0:HALO, :])
    hfin_ref[0] = hc_ref[...]
    tailfin_ref[0] = xe_ref[HALO - 3:HALO, :]


def _lru_seq(proj, params, h0, tail0, *, n_seq, seq_len, t, row0, pad=0, reset_first=False, wo=None):
    cw, cb, wa, wx, ba, bx, lam, g = params
    dl = cw.shape[1]
    nh, hd = wa.shape[0], wa.shape[1]
    assert seq_len % t == 0 and row0 % t == 0 and t % SUBLANES == 0
    n_chunks = seq_len // t
    n_live = n_seq * n_chunks
    blk0 = row0 // t
    fuse_out = wo is not None
    chunk = (lambda s: jnp.minimum(s, n_live - 1)) if fuse_out else (lambda s: s)
    out_chunk = (lambda s: jnp.maximum(s - 1, 0)) if fuse_out else (lambda s: s)
    const2 = lambda s: (0, 0)
    const3 = lambda s: (0, 0, 0)
    per_seq = lambda s: (chunk(s) // n_chunks, 0, 0)
    dout, out_dtype = (wo.shape[1], F32) if fuse_out else (dl, BF16)
    pipelined = [((t, dl), F32), ((t, dl), F32), ((t, dout), out_dtype)]
    resident = [((t + HALO, dl), F32), ((t, dl), F32), ((4 * nh, hd, hd), BF16), ((2, t + SUBLANES, hd), F32)]
    scratch = [pltpu.VMEM((t + HALO, dl), F32), pltpu.VMEM((1, dl), F32), pltpu.VMEM((t, dl), F32),
               pltpu.VMEM((2, t + SUBLANES, hd), F32)]
    extra_specs, extra_args = [], []
    if fuse_out:
        extra_specs.append(pl.BlockSpec((dl, dout), const2, pipeline_mode=pl.Buffered(1)))
        extra_args.append(wo)
        resident += [((dl, dout), BF16), ((t, dl), BF16)]
        scratch.append(pltpu.VMEM((t, dl), BF16))
    kern = functools.partial(_lru_seq_kernel, pad=pad, reset_first=reset_first, n_chunks=n_chunks, n_live=n_live,
                             fuse_out=fuse_out)
    return pl.pallas_call(
        kern,
        out_shape=(jax.ShapeDtypeStruct((n_seq * seq_len, dout), out_dtype),
                   jax.ShapeDtypeStruct((n_seq, 1, dl), F32),
                   jax.ShapeDtypeStruct((n_seq, CONV_TAPS - 1, dl), F32)),
        grid=(n_live + 1 if fuse_out else n_live,),
        in_specs=[
            pl.BlockSpec((t, dl), lambda s: (blk0 + chunk(s), 0)),
            pl.BlockSpec((t, dl), lambda s: (blk0 + chunk(s), 1)),
            pl.BlockSpec((CONV_TAPS, dl), const2),
            pl.BlockSpec((1, dl), const2),
            pl.BlockSpec((nh, hd, hd), const3),
            pl.BlockSpec((nh, hd, hd), const3),
            pl.BlockSpec((1, dl), const2),
            pl.BlockSpec((1, dl), const2),
            pl.BlockSpec((1, dl), const2),
            pl.BlockSpec((1, dl), const2),
            pl.BlockSpec((1, dl), const2),
            pl.BlockSpec((HALO, dl), const2),
        ] + extra_specs,
        out_specs=(
            pl.BlockSpec((t, dout), lambda s: (out_chunk(s), 0)),
            pl.BlockSpec((1, 1, dl), per_seq),
            pl.BlockSpec((1, CONV_TAPS - 1, dl), per_seq),
        ),
        scratch_shapes=scratch,
        compiler_params=pltpu.CompilerParams(
            dimension_semantics=("arbitrary",),
            vmem_limit_bytes=_vmem_limit(pipelined, resident)),
        name="lru_seq",
    )(proj, proj, cw, cb, wa, wx, ba, bx, lam, g, h0, tail0, *extra_args)


def _lru_slab_kernel(gate_ref, x_ref, cw_ref, cb_ref, wa_ref, wx_ref, ba_ref, bx_ref, lam_ref, g_ref,
                     h0_ref, tail_ref, o_ref, hfin_ref, tailfin_ref, y_ref):
    ls, bs, dl = x_ref.shape
    nh, hd = wa_ref.shape[0], wa_ref.shape[1]
    ntail = CONV_TAPS - 1
    for h in range(nh):
        sl = slice(h * hd, (h + 1) * hd)
        ext = [tail_ref[k, :, sl] for k in range(ntail)]
        ext += [x_ref[s, :, sl] for s in range(ls)]
        hsp = (0.5 * LRU_C) * jax.nn.softplus(-lam_ref[:, sl])
        hcur = h0_ref[:, sl]
        for s in range(ls):
            xh = _conv_taps(ext, s, cw_ref, cb_ref, sl)
            a, mult, i = _lru_gates(xh, wa_ref[h], wx_ref[h], ba_ref[:, sl], bx_ref[:, sl], hsp)
            hcur = a * hcur + mult * i * xh
            y_ref[s, :, sl] = hcur * gate_ref[s, :, sl]
        hfin_ref[:, sl] = hcur
        for k in range(ntail):
            tailfin_ref[k, :, sl] = ext[ls + k]
    for s in range(ls):
        y = y_ref[s]
        scale = lax.rsqrt(jnp.mean(y * y, axis=-1, keepdims=True) + EPS)
        o_ref[s] = (y * scale * g_ref[...]).astype(o_ref.dtype)


def _lru_slab(proj3, params, h0, tail, *, ls):
    cw, cb, wa, wx, ba, bx, lam, g = params
    bs = proj3.shape[1]
    dl = cw.shape[1]
    nh, hd = wa.shape[0], wa.shape[1]
    ntail = CONV_TAPS - 1
    c2 = lambda i: (0, 0)
    c3 = lambda i: (0, 0, 0)
    pipelined = [((ls, bs, dl), F32)] * 2 + [((ls, bs, dl), BF16)] + [((bs, (2 * ntail + 2) * dl), F32)]
    return pl.pallas_call(
        _lru_slab_kernel,
        out_shape=(jax.ShapeDtypeStruct((ls, bs, dl), BF16),
                   jax.ShapeDtypeStruct((bs, dl), F32),
                   jax.ShapeDtypeStruct((ntail, bs, dl), F32)),
        grid=(1,),
        in_specs=[
            pl.BlockSpec((ls, bs, dl), lambda i: (0, 0, 0)),
            pl.BlockSpec((ls, bs, dl), lambda i: (0, 0, 1)),
            pl.BlockSpec((CONV_TAPS, dl), c2), pl.BlockSpec((1, dl), c2),
            pl.BlockSpec((nh, hd, hd), c3), pl.BlockSpec((nh, hd, hd), c3),
            pl.BlockSpec((1, dl), c2), pl.BlockSpec((1, dl), c2), pl.BlockSpec((1, dl), c2), pl.BlockSpec((1, dl), c2),
            pl.BlockSpec((bs, dl), c2), pl.BlockSpec((ntail, bs, dl), c3),
        ],
        out_specs=(pl.BlockSpec((ls, bs, dl), c3), pl.BlockSpec((bs, dl), c2), pl.BlockSpec((ntail, bs, dl), c3)),
        scratch_shapes=[pltpu.VMEM((ls, bs, dl), F32)],
        compiler_params=pltpu.CompilerParams(
            dimension_semantics=("arbitrary",),
            vmem_limit_bytes=_vmem_limit(pipelined, [((ls, bs, dl), F32)])),
        name="lru_slab",
    )(proj3, proj3, cw, cb, wa, wx, ba, bx, lam, g, h0, tail)


def _head_lane_mask(n_heads):
    return lax.broadcasted_iota(jnp.int32, (1, HEAD_PAD), 1) < n_heads


def _gated_group_norm(y, z_act, g):
    yg = y * z_act
    scale = lax.rsqrt(jnp.mean(yg * yg, axis=-1, keepdims=True) + EPS)
    return yg * scale * g


def _ssd_seq_kernel(z_ref, xbc_ref, dt_ref, cw_ref, cb_ref, dtb_ref, alog_ref, dskip_ref, g_ref, e3_ref,
                    s0_ref, tail0_ref, *rest, pad, q, n_heads, n_groups, n_chunks, n_live, fuse_out):
    if fuse_out:
        wo_ref, part_ref, res_ref = rest[:3]
        o_ref, sfin_ref, tailfin_ref, xe_ref, xc_ref, ex_ref, y_ref, s_ref, pair_ref, yn_ref, ynp_ref = rest[3:]
    else:
        o_ref, sfin_ref, tailfin_ref, xe_ref, xc_ref, ex_ref, y_ref, s_ref, pair_ref = rest
        yn_ref = o_ref
    step = pl.program_id(0)
    live = step < n_live
    c = lax.rem(jnp.minimum(step, n_live - 1), n_chunks)
    keep = (lambda new, old: jnp.where(live, new, old)) if fuse_out else (lambda new, old: new)
    tq, ds = z_ref.shape
    cps = tq // q
    dc = xbc_ref.shape[1]
    gn = (dc - ds) // 2
    n = gn // n_groups
    p = ds // n_heads
    r = n_heads // n_groups
    gw = ds // n_groups
    hpb = LANES // p

    @pl.when(jnp.logical_and(c == 0, live))
    def _():
        xe_ref[0:HALO, :] = tail0_ref[...]
        s_ref[...] = s0_ref[...]

    if fuse_out:
        @pl.when(step == 0)
        def _():
            ynp_ref[...] = jnp.zeros_like(ynp_ref)

    xe_ref[HALO:HALO + tq, :] = xbc_ref[...]
    valid_all = (c * tq + lax.broadcasted_iota(jnp.int32, (tq, 1), 0)) >= pad

    cblk = CONV_LANE_BLOCK if dc % CONV_LANE_BLOCK == 0 else LANES
    for j in range(dc // cblk):
        sl = slice(j * cblk, (j + 1) * cblk)
        v = _silu(_conv_rows(xe_ref, pair_ref.at[j % 2], cw_ref, cb_ref, sl, tq))
        if pad and (j + 1) * cblk <= ds:
            v = jnp.where(valid_all, v, 0.0)
        xc_ref[:, sl] = v

    ri = lax.broadcasted_iota(jnp.int32, (q, q), 0)
    ci = lax.broadcasted_iota(jnp.int32, (q, q), 1)
    causal = ci <= ri
    tri = jnp.where(causal, 1.0, 0.0).astype(BF16)
    lane = lax.broadcasted_iota(jnp.int32, (q, LANES), 1)
    a_neg = -jnp.exp(alog_ref[...])
    n_pieces = cps * n_groups
    piece = o_ref.shape[1] // n_pieces if fuse_out else 0

    for u in range(cps):
        rows = slice(u * q, (u + 1) * q)
        dtv = jnp.where(_head_lane_mask(n_heads), jax.nn.softplus(dt_ref[rows, :] + dtb_ref[...]), 0.0)
        if pad:
            dtv = jnp.where(valid_all[u * q:(u + 1) * q], dtv, 0.0)
        ac3 = jnp.dot(tri, _split3(dtv * a_neg), preferred_element_type=F32)
        acum = ac3[:, 0:HEAD_PAD] + ac3[:, HEAD_PAD:2 * HEAD_PAD] + ac3[:, 2 * HEAD_PAD:3 * HEAD_PAD]
        alast = acum[q - 1:q, :]
        eac = jnp.exp(acum)
        wend = jnp.exp(alast - acum) * dtv
        cdec = jnp.broadcast_to(jnp.exp(alast), (SUBLANES, HEAD_PAD))
        ex_ref[...] = _expand_heads(jnp.concatenate([eac, wend, cdec], axis=0), e3_ref)
        acum_t = acum.T
        dt_t = dtv.T

        for g in range(n_groups):
            gsl = slice(g * gw, (g + 1) * gw)
            bg = xc_ref[rows, ds + g * n:ds + (g + 1) * n].astype(BF16)
            cg = xc_ref[rows, ds + gn + g * n:ds + gn + (g + 1) * n].astype(BF16)
            cbm = lax.dot_general(cg, bg, _NT, preferred_element_type=F32)
            yoff = jnp.dot(cg, s_ref[:, gsl].astype(BF16), preferred_element_type=F32)
            for k in range(gw // LANES):
                lsl = slice(g * gw + k * LANES, g * gw + (k + 1) * LANES)
                xs = xc_ref[rows, lsl]
                ms = []
                xparts = []
                for w in range(hpb):
                    h = g * r + k * hpb + w
                    seg = acum[:, h:h + 1] - acum_t[h:h + 1, :]
                    lm = jnp.where(causal, jnp.exp(seg), 0.0) * dt_t[h:h + 1, :]
                    ms.append((cbm * lm).astype(BF16))
                    inhead = (lane >= w * p) & (lane < (w + 1) * p)
                    xparts.append(jnp.where(inhead, xs, 0.0).astype(BF16))
                ydiag = jnp.dot(jnp.concatenate(ms, axis=1), jnp.concatenate(xparts, axis=0),
                                preferred_element_type=F32)
                y = ydiag + yoff[:, k * LANES:(k + 1) * LANES] * ex_ref[0:q, lsl]
                y_ref[rows, lsl] = y + dskip_ref[:, lsl] * xs
            yn_ref[rows, gsl] = _gated_group_norm(y_ref[rows, gsl], z_ref[rows, gsl], g_ref[:, gsl]).astype(BF16)
            xw = (xc_ref[rows, gsl] * ex_ref[q:2 * q, gsl]).astype(BF16)
            upd = lax.dot_general(bg, xw, (((0,), (0,)), ((), ())), preferred_element_type=F32)
            s_old = s_ref[:, gsl]
            s_ref[:, gsl] = keep(ex_ref[2 * q:2 * q + 1, gsl] * s_old + upd, s_old)
            if fuse_out:
                psl = slice((u * n_groups + g) * piece, (u * n_groups + g + 1) * piece)
                mix = part_ref[:, psl] + jnp.dot(ynp_ref[...], wo_ref[:, psl], preferred_element_type=F32)
                o_ref[:, psl] = res_ref[:, psl] + mix

    if fuse_out:
        ynp_ref[...] = yn_ref[...]
    xe_ref[0:HALO, :] = keep(xe_ref[tq:tq + HALO, :], xe_ref[0:HALO, :])
    sfin_ref[0] = s_ref[...]
    tailfin_ref[0] = xe_ref[HALO - 3:HALO, :]


def _ssd_seq(proj, dt, params, s0, tail0, *, n_seq, seq_len, row0, n_heads, n_groups, dl, pad=0, fuse=None):
    cw, cb, dtb, alog, dskip, g, e3 = params
    dc = cw.shape[1]
    ds = dskip.shape[1]
    n = s0.shape[0]
    q = SSD_CHUNK
    tq = q * (SSD_CHUNKS_PER_STEP if seq_len % (q * SSD_CHUNKS_PER_STEP) == 0 else 1)
    assert seq_len % tq == 0 and row0 % tq == 0
    assert (2 * dl) % ds == 0 and (2 * dl + ds) % dc == 0 and LANES % (ds // n_heads) == 0
    n_chunks = seq_len // tq
    n_live = n_seq * n_chunks
    blk0 = row0 // tq
    fuse_out = fuse is not None
    chunk = (lambda s: jnp.minimum(s, n_live - 1)) if fuse_out else (lambda s: s)
    rows = lambda s: blk0 + chunk(s)
    const2 = lambda s: (0, 0)
    out_rows = (lambda s: (jnp.maximum(s - 1, 0), 0)) if fuse_out else (lambda s: (s, 0))
    per_seq = lambda s: (chunk(s) // n_chunks, 0, 0)
    dout, out_dtype = (fuse[0].shape[1], F32) if fuse_out else (ds, BF16)
    pipelined = [((tq, ds), F32), ((tq, dc), F32), ((tq, HEAD_PAD), F32), ((tq, dout), out_dtype)]
    resident = [((tq + HALO, dc), F32), ((tq, dc), F32), ((2 * q + SUBLANES, ds), F32), ((tq, ds), F32),
                ((3 * n, ds), F32), ((6 * HEAD_PAD, ds), BF16), ((2, tq + SUBLANES, CONV_LANE_BLOCK), F32)]
    cblk = CONV_LANE_BLOCK if dc % CONV_LANE_BLOCK == 0 else LANES
    scratch = [pltpu.VMEM((tq + HALO, dc), F32), pltpu.VMEM((tq, dc), F32),
               pltpu.VMEM((2 * q + SUBLANES, ds), F32), pltpu.VMEM((tq, ds), F32), pltpu.VMEM((n, ds), F32),
               pltpu.VMEM((2, tq + SUBLANES, cblk), F32)]
    extra_specs, extra_args = [], []
    if fuse_out:
        extra_specs = [pl.BlockSpec((ds, dout), const2, pipeline_mode=pl.Buffered(1)),
                       pl.BlockSpec((tq, dout), out_rows), pl.BlockSpec((tq, dout), out_rows)]
        extra_args = list(fuse)
        pipelined += [((tq, dout), F32)] * 2
        resident += [((ds, dout), BF16), ((2 * tq, ds), BF16)]
        scratch += [pltpu.VMEM((tq, ds), BF16), pltpu.VMEM((tq, ds), BF16)]
    kern = functools.partial(_ssd_seq_kernel, pad=pad, q=q, n_heads=n_heads, n_groups=n_groups, n_chunks=n_chunks,
                             n_live=n_live, fuse_out=fuse_out)
    return pl.pallas_call(
        kern,
        out_shape=(jax.ShapeDtypeStruct((n_seq * seq_len, dout), out_dtype),
                   jax.ShapeDtypeStruct((n_seq, n, ds), F32),
                   jax.ShapeDtypeStruct((n_seq, CONV_TAPS - 1, dc), F32)),
        grid=(n_live + 1 if fuse_out else n_live,),
        in_specs=[
            pl.BlockSpec((tq, ds), lambda s: (rows(s), (2 * dl) // ds)),
            pl.BlockSpec((tq, dc), lambda s: (rows(s), (2 * dl + ds) // dc)),
            pl.BlockSpec((tq, HEAD_PAD), lambda s: (rows(s), 0)),
            pl.BlockSpec((CONV_TAPS, dc), const2),
            pl.BlockSpec((1, dc), const2),
            pl.BlockSpec((1, HEAD_PAD), const2),
            pl.BlockSpec((1, HEAD_PAD), const2),
            pl.BlockSpec((1, ds), const2),
            pl.BlockSpec((1, ds), const2),
            pl.BlockSpec((3 * HEAD_PAD, ds), const2),
            pl.BlockSpec((n, ds), const2),
            pl.BlockSpec((HALO, dc), const2),
        ] + extra_specs,
        out_specs=(
            pl.BlockSpec((tq, dout), out_rows),
            pl.BlockSpec((1, n, ds), per_seq),
            pl.BlockSpec((1, CONV_TAPS - 1, dc), per_seq),
        ),
        scratch_shapes=scratch,
        compiler_params=pltpu.CompilerParams(
            dimension_semantics=("arbitrary",),
            vmem_limit_bytes=_vmem_limit(pipelined, resident)),
        name="ssd_seq",
    )(proj, proj, dt, cw, cb, dtb, alog, dskip, g, e3, s0, tail0, *extra_args)


def _ssd_slab_pre_kernel(xbc_ref, dt_ref, tail_ref, cw_ref, cb_ref, dtb_ref, alog_ref, dskip_ref, e3_ref,
                         ypart_ref, eace_ref, c_ref, b_ref, xw_ref, cdec_ref, tailfin_ref, xc_ref,
                         *, n_heads, n_groups):
    ls, bs, dc = xbc_ref.shape
    ds = dskip_ref.shape[1]
    gn = (dc - ds) // 2
    n = gn // n_groups
    r = n_heads // n_groups
    ntail = CONV_TAPS - 1

    cblk = CONV_LANE_BLOCK if dc % CONV_LANE_BLOCK == 0 else LANES
    for j in range(dc // cblk):
        sl = slice(j * cblk, (j + 1) * cblk)
        ext = [tail_ref[k, :, sl] for k in range(ntail)]
        ext += [xbc_ref[s, :, sl] for s in range(ls)]
        for s in range(ls):
            xc_ref[s, :, sl] = _silu(_conv_taps(ext, s, cw_ref, cb_ref, sl))
        for k in range(ntail):
            tailfin_ref[k, :, sl] = ext[ls + k]

    hmask = _head_lane_mask(n_heads)
    a_neg = -jnp.exp(alog_ref[...])
    dtv, acum = [], []
    run = jnp.zeros((bs, HEAD_PAD), F32)
    for s in range(ls):
        d = jnp.where(hmask, jax.nn.softplus(dt_ref[s] + dtb_ref[...]), 0.0)
        run = run + d * a_neg
        dtv.append(d)
        acum.append(run)
    alast = acum[ls - 1]
    cdec_ref[...] = jnp.exp(alast)
    head_group = lax.broadcasted_iota(jnp.int32, (1, HEAD_PAD), 1) // r

    for s in range(ls):
        eace_ref[s] = _expand_heads(jnp.exp(acum[s]), e3_ref)
        wend_e = _expand_heads(jnp.exp(alast - acum[s]) * dtv[s], e3_ref)
        xw_ref[:, s * ds:(s + 1) * ds] = xc_ref[s, :, 0:ds] * wend_e
        b_ref[:, s * gn:(s + 1) * gn] = xc_ref[s, :, ds:ds + gn]
        c_ref[:, s * gn:(s + 1) * gn] = xc_ref[s, :, ds + gn:ds + 2 * gn]
        ypart = dskip_ref[...] * xc_ref[s, :, 0:ds]
        for j in range(s + 1):
            cbh = jnp.zeros((bs, HEAD_PAD), F32)
            for g in range(n_groups):
                cs = xc_ref[s, :, ds + gn + g * n:ds + gn + (g + 1) * n]
                bj = xc_ref[j, :, ds + g * n:ds + (g + 1) * n]
                cbg = jnp.sum(cs * bj, axis=-1, keepdims=True)
                cbh = cbh + jnp.where(head_group == g, cbg, 0.0)
            coef = cbh * (jnp.exp(acum[s] - acum[j]) * dtv[j])
            ypart = ypart + _expand_heads(coef, e3_ref) * xc_ref[j, :, 0:ds]
        ypart_ref[s] = ypart
    for s in range(ls, SLAB_ROWS):
        xw_ref[:, s * ds:(s + 1) * ds] = jnp.zeros((bs, ds), F32)
        b_ref[:, s * gn:(s + 1) * gn] = jnp.zeros((bs, gn), F32)
        c_ref[:, s * gn:(s + 1) * gn] = jnp.zeros((bs, gn), F32)


def _ssd_slab_pre(proj3, dt3, tail, params, *, ls, n_heads, n_groups, dl):
    cw, cb, dtb, alog, dskip, _, e3 = params
    bs = proj3.shape[1]
    dc = cw.shape[1]
    ds = dskip.shape[1]
    gn = (dc - ds) // 2
    ntail = CONV_TAPS - 1
    assert ls <= SLAB_ROWS and (2 * dl + ds) % dc == 0
    c2 = lambda i: (0, 0)
    c3 = lambda i: (0, 0, 0)
    pipelined = [((ls, bs, dc), F32), ((ls, bs, HEAD_PAD), F32), ((bs, 2 * ntail * dc), F32),
                 ((2 * ls, bs, ds), F32), ((bs, SLAB_ROWS * (2 * gn + ds)), F32), ((3 * HEAD_PAD, ds), BF16)]
    kern = functools.partial(_ssd_slab_pre_kernel, n_heads=n_heads, n_groups=n_groups)
    return pl.pallas_call(
        kern,
        out_shape=(jax.ShapeDtypeStruct((ls, bs, ds), F32),
                   jax.ShapeDtypeStruct((ls, bs, ds), F32),
                   jax.ShapeDtypeStruct((bs, SLAB_ROWS * gn), F32),
                   jax.ShapeDtypeStruct((bs, SLAB_ROWS * gn), F32),
                   jax.ShapeDtypeStruct((bs, SLAB_ROWS * ds), F32),
                   jax.ShapeDtypeStruct((bs, HEAD_PAD), F32),
                   jax.ShapeDtypeStruct((ntail, bs, dc), F32)),
        grid=(1,),
        in_specs=[
            pl.BlockSpec((ls, bs, dc), lambda i: (0, 0, (2 * dl + ds) // dc)),
            pl.BlockSpec((ls, bs, HEAD_PAD), c3),
            pl.BlockSpec((ntail, bs, dc), c3),
            pl.BlockSpec((CONV_TAPS, dc), c2), pl.BlockSpec((1, dc), c2),
            pl.BlockSpec((1, HEAD_PAD), c2), pl.BlockSpec((1, HEAD_PAD), c2),
            pl.BlockSpec((1, ds), c2), pl.BlockSpec((3 * HEAD_PAD, ds), c2),
        ],
        out_specs=(pl.BlockSpec((ls, bs, ds), c3), pl.BlockSpec((ls, bs, ds), c3),
                   pl.BlockSpec((bs, SLAB_ROWS * gn), c2), pl.BlockSpec((bs, SLAB_ROWS * gn), c2),
                   pl.BlockSpec((bs, SLAB_ROWS * ds), c2), pl.BlockSpec((bs, HEAD_PAD), c2),
                   pl.BlockSpec((ntail, bs, dc), c3)),
        scratch_shapes=[pltpu.VMEM((ls, bs, dc), F32)],
        compiler_params=pltpu.CompilerParams(
            dimension_semantics=("arbitrary",),
            vmem_limit_bytes=_vmem_limit(pipelined, [((ls, bs, dc), F32)])),
        name="ssd_slab_pre",
    )(proj3, dt3, tail, cw, cb, dtb, alog, dskip, e3)


def _ssd_state_kernel(cdec_ref, s_ref, c_ref, b_ref, xw_ref, snew_ref, yoff_ref, *, n_heads, n_groups):
    i = pl.program_id(0)
    sb, hp, n = s_ref.shape
    p = hp // n_heads
    r = n_heads // n_groups
    gw = hp // n_groups
    for q in range(sb):
        for g in range(n_groups):
            gsl = slice(g * gw, (g + 1) * gw)
            sg = s_ref[q, gsl, :]
            cg = c_ref[q, :, g * n:(g + 1) * n].astype(BF16)
            yoff_ref[q, :, gsl] = lax.dot_general(cg, sg.astype(BF16), (((1,), (1,)), ((), ())),
                                                  preferred_element_type=F32)
            upd = lax.dot_general(xw_ref[q, :, gsl].astype(BF16), b_ref[q, :, g * n:(g + 1) * n].astype(BF16),
                                  (((0,), (0,)), ((), ())), preferred_element_type=F32)
            for u in range(r):
                h = g * r + u
                rows = slice(h * p, (h + 1) * p)
                snew_ref[q, rows, :] = cdec_ref[i * sb + q, h] * s_ref[q, rows, :] + upd[u * p:(u + 1) * p, :]


def _ssd_state(cdec, state, c_rows, b_rows, xw_rows, *, n_heads, n_groups):
    bs, hp, n = state.shape
    gn = c_rows.shape[2]
    sb = STATE_SEQS_PER_STEP if bs % STATE_SEQS_PER_STEP == 0 else 1
    per_seq = lambda i: (i, 0, 0)
    pipelined = [((sb, hp, n), F32)] * 2 + [((sb, SLAB_ROWS, gn), F32)] * 2 + [((sb, SLAB_ROWS, hp), F32)] * 2
    kern = functools.partial(_ssd_state_kernel, n_heads=n_heads, n_groups=n_groups)
    return pl.pallas_call(
        kern,
        out_shape=(jax.ShapeDtypeStruct((bs, hp, n), F32), jax.ShapeDtypeStruct((bs, SLAB_ROWS, hp), F32)),
        grid=(bs // sb,),
        in_specs=[
            pl.BlockSpec(memory_space=pltpu.SMEM),
            pl.BlockSpec((sb, hp, n), per_seq),
            pl.BlockSpec((sb, SLAB_ROWS, gn), per_seq),
            pl.BlockSpec((sb, SLAB_ROWS, gn), per_seq),
            pl.BlockSpec((sb, SLAB_ROWS, hp), per_seq),
        ],
        out_specs=(pl.BlockSpec((sb, hp, n), per_seq), pl.BlockSpec((sb, SLAB_ROWS, hp), per_seq)),
        compiler_params=pltpu.CompilerParams(
            dimension_semantics=("parallel",),
            vmem_limit_bytes=_vmem_limit(pipelined, [])),
        name="ssd_state",
    )(cdec, state, c_rows, b_rows, xw_rows)


def _ssd_slab_post_kernel(ypart_ref, eace_ref, yoff_ref, z_ref, g_ref, o_ref, *, n_groups):
    ls, bs, ds = ypart_ref.shape
    gw = ds // n_groups
    for s in range(ls):
        for g in range(n_groups):
            gsl = slice(g * gw, (g + 1) * gw)
            y = ypart_ref[s, :, gsl] + eace_ref[s, :, gsl] * yoff_ref[:, s * ds + g * gw:s * ds + (g + 1) * gw]
            o_ref[s, :, gsl] = _gated_group_norm(y, z_ref[s, :, gsl], g_ref[:, gsl]).astype(o_ref.dtype)


def _ssd_slab_post(ypart, eace, yoff, proj3, g, *, n_groups, dl):
    ls, bs, ds = ypart.shape
    assert (2 * dl) % ds == 0
    c2 = lambda i: (0, 0)
    c3 = lambda i: (0, 0, 0)
    pipelined = [((ls, bs, ds), F32)] * 3 + [((bs, SLAB_ROWS * ds), F32), ((ls, bs, ds), BF16)]
    kern = functools.partial(_ssd_slab_post_kernel, n_groups=n_groups)
    return pl.pallas_call(
        kern,
        out_shape=jax.ShapeDtypeStruct((ls, bs, ds), BF16),
        grid=(1,),
        in_specs=[
            pl.BlockSpec((ls, bs, ds), c3), pl.BlockSpec((ls, bs, ds), c3),
            pl.BlockSpec((bs, SLAB_ROWS * ds), c2),
            pl.BlockSpec((ls, bs, ds), lambda i: (0, 0, (2 * dl) // ds)),
            pl.BlockSpec((1, ds), c2),
        ],
        out_specs=pl.BlockSpec((ls, bs, ds), c3),
        compiler_params=pltpu.CompilerParams(
            dimension_semantics=("arbitrary",),
            vmem_limit_bytes=_vmem_limit(pipelined, [])),
        name="ssd_slab_post",
    )(ypart, eace, yoff, proj3, g)


def _head_expansion(n_heads, head_dim):
    rows = lax.broadcasted_iota(jnp.int32, (HEAD_PAD, n_heads * head_dim), 0)
    cols = lax.broadcasted_iota(jnp.int32, (HEAD_PAD, n_heads * head_dim), 1)
    e = (cols // head_dim == rows).astype(BF16)
    return jnp.concatenate([e, e, e], axis=0)


def _pad_lanes(v, width):
    return jnp.pad(v, ((0, 0), (0, width - v.shape[1])))


def _tail_block(tail):
    return jnp.pad(tail, ((HALO - tail.shape[0], 0), (0, 0)))


def _mlp(x1, g_mlp, w_up, w_down, g_final, *, layer=None):
    m = x1.shape[0]
    tiles = MATMUL_TILES
    tm = m if layer is not None else _pick_tile(m, tiles)
    up_tiles = tiles if layer is not None else (2048,) + tiles
    up = _mlp_up(x1, g_mlp, w_up, tm=tm, tn=_pick_tile(w_up.shape[-1], up_tiles), layer=layer)
    hid = up[0]
    down = _mlp_down_final(hid, w_down, x1, g_final, tm=tm, tk=_pick_tile(w_down.shape[-2], tiles), layer=layer)
    if layer is None:
        return down[0]
    return down[0], up[1], down[1]


def kernel(x_prompt, x_sample, state_lru_h, state_lru_conv, state_ssd, state_ssd_conv, meta_tokens, g_mix, w_in, conv_lru_w, conv_lru_b, lru_wa, lru_ba, lru_wx, lru_bx, lru_lambda, g_lru_out, conv_ssd_w, conv_ssd_b, dt_bias, a_log, d_skip, g_ssd_out, w_out, g_mlp, w_up, w_down, g_final):
    depth = w_in.shape[0]
    assert depth == 1, "single-layer step"
    l = 0
    bp, lp, d = x_prompt.shape
    bs, ls, _ = x_sample.shape
    n_meta = meta_tokens.shape[0]
    dl = state_lru_h.shape[-1]
    n_heads, p, n = state_ssd.shape[-3:]
    ds = n_heads * p
    dc = state_ssd_conv.shape[-1]
    gn = (dc - ds) // 2
    n_groups = gn // n
    nw = 2 * dl + ds + dc
    ntail = CONV_TAPS - 1
    q = SSD_CHUNK
    meta_pad = (-n_meta) % q
    assert n_heads <= HEAD_PAD and (bs * ls) % q == 0 and lp % q == 0 and q % bs == 0

    row = lambda v: v.reshape(1, -1).astype(F32)
    w_in_t = jnp.swapaxes(w_in, 1, 2)
    lru_params = (conv_lru_w[l], row(conv_lru_b[l]), (0.5 * lru_wa[l]).astype(BF16), (0.5 * lru_wx[l]).astype(BF16),
                  row(0.5 * lru_ba[l]), row(0.5 * lru_bx[l]), row(lru_lambda[l]), row(g_lru_out[l]))
    ssd_params = (conv_ssd_w[l], row(conv_ssd_b[l]), _pad_lanes(row(dt_bias[l]), HEAD_PAD),
                  _pad_lanes(row(a_log[l]), HEAD_PAD), row(jnp.repeat(d_skip[l], p)), row(g_ssd_out[l]),
                  _head_expansion(n_heads, p))
    g_mix_r, g_mlp_r, g_final_r = row(g_mix[l]), row(g_mlp[l]), row(g_final)

    xs_tm = x_sample.transpose(1, 0, 2).reshape(ls * bs, d)
    x_side = jnp.concatenate([xs_tm, jnp.zeros((meta_pad, d), F32), meta_tokens.astype(F32)], axis=0)
    xp_rows = x_prompt.reshape(bp * lp, d)

    tiles = MATMUL_TILES
    tn_in = next(t for t in tiles if nw % t == 0 and dl % t == 0 and ds % t == 0)
    act_cols = dict(gelu_cols=(0, dl), silu_cols=(2 * dl, 2 * dl + ds))
    proj_side, dt_side, w_in_b, w_dt_b = _in_proj(x_side, g_mix_r, w_in_t, nw=nw, n_dt=n_heads, tm=x_side.shape[0],
                                                  tn=tn_in, layer=l, **act_cols)

    proj_s3 = proj_side.reshape(-1, bs, nw)
    dt_s3 = dt_side.reshape(-1, bs, HEAD_PAD)
    lru_s, s_h, s_ltail = _lru_slab(proj_s3, lru_params, state_lru_h[l],
                                    state_lru_conv[l].transpose(1, 0, 2), ls=ls)
    ypart, eace, c_rows, b_rows, xw_rows, cdec, s_stail = _ssd_slab_pre(
        proj_s3, dt_s3, state_ssd_conv[l].transpose(1, 0, 2), ssd_params,
        ls=ls, n_heads=n_heads, n_groups=n_groups, dl=dl)
    s_new, yoff = _ssd_state(cdec, state_ssd[l].reshape(bs, ds, n),
                             c_rows.reshape(bs, SLAB_ROWS, gn), b_rows.reshape(bs, SLAB_ROWS, gn),
                             xw_rows.reshape(bs, SLAB_ROWS, ds), n_heads=n_heads, n_groups=n_groups)
    ssd_s = _ssd_slab_post(ypart, eace, yoff.reshape(bs, SLAB_ROWS * ds), proj_s3, ssd_params[5],
                           n_groups=n_groups, dl=dl)
    x1_s, w_out_lru, w_out_ssd = _out_proj(lru_s.reshape(ls * bs, dl), ssd_s.reshape(ls * bs, ds), w_out, l, xs_tm,
                                           tn=_pick_tile(d, (512, 256, 128)))
    y_s, w_up_b, w_down_b = _mlp(x1_s, g_mlp_r, w_up, w_down, g_final_r, layer=l)

    meta_row0 = ls * bs
    _, m_h, m_ltail = _lru_seq(proj_side, lru_params, jnp.zeros((1, dl), F32), jnp.zeros((HALO, dl), F32),
                               n_seq=1, seq_len=q, t=q, row0=meta_row0, pad=meta_pad, reset_first=True)
    _, m_s, m_stail = _ssd_seq(proj_side, dt_side, ssd_params, jnp.zeros((n, ds), F32), jnp.zeros((HALO, dc), F32),
                               n_seq=1, seq_len=q, row0=meta_row0, n_heads=n_heads, n_groups=n_groups, dl=dl,
                               pad=meta_pad)

    proj_p, dt_p = _in_proj(xp_rows, g_mix_r, w_in_b, w_dt_b, nw=nw, n_dt=n_heads, tm=_pick_tile(bp * lp, tiles),
                            tn=tn_in, **act_cols)
    part_p, p_h, p_ltail = _lru_seq(proj_p, lru_params, m_h[0], _tail_block(m_ltail[0]),
                                    n_seq=bp, seq_len=lp, t=_pick_tile(lp, (256, 128)), row0=0, wo=w_out_lru)
    x1_p, p_s, p_stail = _ssd_seq(proj_p, dt_p, ssd_params, m_s[0], _tail_block(m_stail[0]),
                                  n_seq=bp, seq_len=lp, row0=0, n_heads=n_heads, n_groups=n_groups, dl=dl,
                                  fuse=(w_out_ssd, part_p, xp_rows))
    y_p = _mlp(x1_p, g_mlp_r, w_up_b, w_down_b, g_final_r)

    y_prompt = y_p.reshape(bp, lp, d)
    y_sample = y_s.reshape(ls, bs, d).transpose(1, 0, 2)
    p_lru_h = p_h.reshape(1, bp, dl)
    p_lru_conv = p_ltail.reshape(1, bp, ntail, dl)
    p_ssd = p_s.transpose(0, 2, 1).reshape(1, bp, n_heads, p, n)
    p_ssd_conv = p_stail.reshape(1, bp, ntail, dc)
    s_lru_h = s_h.reshape(1, bs, dl)
    s_lru_conv = s_ltail.transpose(1, 0, 2)[None]
    s_ssd = s_new.reshape(1, bs, n_heads, p, n)
    s_ssd_conv = s_stail.transpose(1, 0, 2)[None]
    return (y_prompt, y_sample, p_lru_h, p_lru_conv, p_ssd, p_ssd_conv, s_lru_h, s_lru_conv, s_ssd, s_ssd_conv)
```

```python
import functools

import jax
import jax.numpy as jnp
from jax import lax
from jax.experimental import pallas as pl
from jax.experimental.pallas import tpu as pltpu

F32 = jnp.float32
BF16 = jnp.bfloat16

EPS = 1e-6
LRU_C = 8.0
CONV_TAPS = 4

LANES = 128
SUBLANES = 8
MXU_COLS = 256
VMEM_BYTES_V7X = 64 * 1024 * 1024
VMEM_TEMP_BYTES = 10 * 1024 * 1024
VMEM_CEILING_BYTES = VMEM_BYTES_V7X - 6 * 1024 * 1024

HALO = 2 * SUBLANES
SSD_CHUNK = 128
SSD_CHUNKS_PER_STEP = 2
HEAD_PAD = LANES
SLAB_ROWS = SUBLANES
STATE_SEQS_PER_STEP = 8
CONV_LANE_BLOCK = 512
MATMUL_TILES = (1024, 512, 256, 128)


def _nbytes(shape, dtype):
    n = 1
    for s in shape:
        n *= s
    return n * jnp.dtype(dtype).itemsize


def _vmem_limit(pipelined, resident):
    est = 2 * sum(_nbytes(s, d) for s, d in pipelined) + sum(_nbytes(s, d) for s, d in resident)
    return int(min(est + VMEM_TEMP_BYTES, VMEM_CEILING_BYTES))


def _pick_tile(m, prefs):
    for t in prefs:
        if m % t == 0:
            return t
    return m


def _silu(x):
    h = 0.5 * x
    return h * jnp.tanh(h) + h


def _split3(x):
    hi = x.astype(BF16)
    r1 = x - hi.astype(F32)
    mid = r1.astype(BF16)
    lo = (r1 - mid.astype(F32)).astype(BF16)
    return jnp.concatenate([hi, mid, lo], axis=1)


def _expand_heads(x, e3_ref):
    return jnp.dot(_split3(x), e3_ref[...], preferred_element_type=F32)


def _rmsnorm_rows(x, g):
    ms = jnp.mean(x * x, axis=-1, keepdims=True)
    return x * lax.rsqrt(ms + EPS) * g


def _conv_rows(xe_ref, pair_ref, cw_ref, cb_ref, sl, t):
    lo = HALO - SUBLANES
    x0 = xe_ref[lo:HALO + t, sl]
    x1 = xe_ref[lo - 1:HALO + t - 1, sl]
    pair_ref[...] = x0 * cw_ref[1:2, sl] + x1 * cw_ref[0:1, sl]
    near = x0[SUBLANES:] * cw_ref[3:4, sl] + x1[SUBLANES:] * cw_ref[2:3, sl]
    return cb_ref[:, sl] + near + pair_ref[SUBLANES - 2:SUBLANES - 2 + t, :]


def _conv_taps(ext, s, cw_ref, cb_ref, sl):
    v = cb_ref[:, sl] + ext[s] * cw_ref[0:1, sl]
    for k in range(1, CONV_TAPS):
        v = v + ext[s + k] * cw_ref[k:k + 1, sl]
    return v


def _weight_tile(w_ref, wb_ref):
    if wb_ref is None:
        return w_ref[...]
    w = w_ref[...].astype(BF16)
    wb_ref[...] = w
    return w


def _weight_specs(layer, blk, idx):
    if layer is None:
        return pl.BlockSpec(blk, idx), None
    return pl.BlockSpec((None,) + blk, lambda i, j: (layer,) + idx(i, j)), pl.BlockSpec(blk, idx)


_NT = (((1,), (1,)), ((), ()))


def _in_proj_kernel(x_ref, g_ref, w_ref, wdt_ref, o_ref, dt_ref, *rest, gelu_tiles, silu_tiles, emit_w):
    if emit_w:
        wb_ref, wdtb_ref, xn_ref = rest
    else:
        wb_ref, wdtb_ref, xn_ref = None, None, rest[0]
    j = pl.program_id(1)

    @pl.when(j == 0)
    def _():
        xn = _rmsnorm_rows(x_ref[...], g_ref[...]).astype(BF16)
        xn_ref[...] = xn
        wdt = _weight_tile(wdt_ref, wdtb_ref)
        wdt = jnp.concatenate([wdt, jnp.zeros((HEAD_PAD - wdt.shape[0], wdt.shape[1]), BF16)], axis=0)
        dt_ref[...] = lax.dot_general(xn, wdt, _NT, preferred_element_type=F32)

    def tile():
        return lax.dot_general(xn_ref[...], _weight_tile(w_ref, wb_ref), _NT, preferred_element_type=F32)

    in_range = lambda r: jnp.logical_and(j >= r[0], j < r[1])
    is_gelu, is_silu = in_range(gelu_tiles), in_range(silu_tiles)

    @pl.when(is_gelu)
    def _():
        o_ref[...] = jax.nn.gelu(tile())

    @pl.when(is_silu)
    def _():
        o_ref[...] = _silu(tile())

    @pl.when(jnp.logical_not(jnp.logical_or(is_gelu, is_silu)))
    def _():
        o_ref[...] = tile()


def _in_proj(x, g, w_t, wdt_t=None, *, nw, n_dt, tm, tn, gelu_cols, silu_cols, layer=None):
    m, k = x.shape
    emit_w = layer is not None
    assert m % tm == 0 and nw % tn == 0 and all(c % tn == 0 for c in gelu_cols + silu_cols)
    assert n_dt % SUBLANES == 0 and nw % n_dt == 0 and (not emit_w or m == tm)
    out_shape = [jax.ShapeDtypeStruct((m, nw), F32), jax.ShapeDtypeStruct((m, HEAD_PAD), F32)]
    out_specs = [pl.BlockSpec((tm, tn), lambda i, j: (i, j)), pl.BlockSpec((tm, HEAD_PAD), lambda i, j: (i, 0))]
    pipelined = [((tm, k), F32), ((tn, k), w_t.dtype), ((n_dt, k), w_t.dtype), ((tm, tn), F32), ((tm, HEAD_PAD), F32)]
    if emit_w:
        w_spec = pl.BlockSpec((None, tn, k), lambda i, j: (layer, j, 0))
        wdt_spec = pl.BlockSpec((None, n_dt, k), lambda i, j: (layer, nw // n_dt, 0))
        wdt_t = w_t
        out_shape += [jax.ShapeDtypeStruct((nw, k), BF16), jax.ShapeDtypeStruct((n_dt, k), BF16)]
        out_specs += [pl.BlockSpec((tn, k), lambda i, j: (j, 0)), pl.BlockSpec((n_dt, k), lambda i, j: (0, 0))]
        pipelined += [((tn, k), BF16), ((n_dt, k), BF16)]
    else:
        w_spec = pl.BlockSpec((tn, k), lambda i, j: (j, 0))
        wdt_spec = pl.BlockSpec((n_dt, k), lambda i, j: (0, 0))
    kern = functools.partial(_in_proj_kernel, gelu_tiles=tuple(c // tn for c in gelu_cols),
                             silu_tiles=tuple(c // tn for c in silu_cols), emit_w=emit_w)
    return pl.pallas_call(
        kern,
        out_shape=tuple(out_shape),
        grid=(m // tm, nw // tn),
        in_specs=[
            pl.BlockSpec((tm, k), lambda i, j: (i, 0)),
            pl.BlockSpec((1, k), lambda i, j: (0, 0)),
            w_spec,
            wdt_spec,
        ],
        out_specs=tuple(out_specs),
        scratch_shapes=[pltpu.VMEM((tm, k), BF16)],
        compiler_params=pltpu.CompilerParams(
            dimension_semantics=("parallel", "arbitrary"),
            vmem_limit_bytes=_vmem_limit(pipelined, [((tm, k), BF16), ((tn, k), BF16)])),
        name="in_proj",
    )(x, g, w_t, wdt_t)


def _out_proj_kernel(a1_ref, a2_ref, w1_ref, w2_ref, res_ref, o_ref, wb1_ref, wb2_ref):
    acc = jnp.dot(a1_ref[...], _weight_tile(w1_ref, wb1_ref), preferred_element_type=F32)
    acc = acc + jnp.dot(a2_ref[...], _weight_tile(w2_ref, wb2_ref), preferred_element_type=F32)
    o_ref[...] = res_ref[...] + acc


def _out_proj(a1, a2, w, layer, res, *, tn):
    m, k1 = a1.shape
    k2 = a2.shape[1]
    n = w.shape[2]
    assert n % tn == 0 and k1 == k2 and w.shape[1] == k1 + k2
    pipelined = [((m, k1), BF16), ((m, k2), BF16), ((k1, tn), w.dtype), ((k2, tn), w.dtype), ((m, tn), F32),
                 ((m, tn), F32), ((k1, tn), BF16), ((k2, tn), BF16)]
    col = lambda j: (0, j)
    return pl.pallas_call(
        _out_proj_kernel,
        out_shape=(jax.ShapeDtypeStruct((m, n), F32), jax.ShapeDtypeStruct((k1, n), BF16),
                   jax.ShapeDtypeStruct((k2, n), BF16)),
        grid=(n // tn,),
        in_specs=[
            pl.BlockSpec((m, k1), lambda j: (0, 0)),
            pl.BlockSpec((m, k2), lambda j: (0, 0)),
            pl.BlockSpec((None, k1, tn), lambda j: (layer, 0, j)),
            pl.BlockSpec((None, k2, tn), lambda j: (layer, 1, j)),
            pl.BlockSpec((m, tn), col),
        ],
        out_specs=(pl.BlockSpec((m, tn), col), pl.BlockSpec((k1, tn), col), pl.BlockSpec((k2, tn), col)),
        compiler_params=pltpu.CompilerParams(
            dimension_semantics=("parallel",),
            vmem_limit_bytes=_vmem_limit(pipelined, [])),
        name="out_proj",
    )(a1, a2, w, w, res)


def _mlp_up_kernel(x_ref, g_ref, w_ref, o_ref, *rest, emit_w):
    wb_ref, xn_ref = rest if emit_w else (None, rest[0])

    @pl.when(pl.program_id(1) == 0)
    def _():
        xn_ref[...] = _rmsnorm_rows(x_ref[...], g_ref[...]).astype(BF16)

    acc = jnp.dot(xn_ref[...], _weight_tile(w_ref, wb_ref), preferred_element_type=F32)
    o_ref[...] = jnp.square(jnp.maximum(acc, 0.0)).astype(o_ref.dtype)


def _mlp_up(x, g, w, *, tm, tn, layer=None):
    m, k = x.shape
    n = w.shape[-1]
    emit_w = layer is not None
    assert m % tm == 0 and n % tn == 0 and (not emit_w or m == tm)
    w_spec, wb_spec = _weight_specs(layer, (k, tn), lambda i, j: (0, j))
    out_shape = [jax.ShapeDtypeStruct((m, n), BF16)]
    out_specs = [pl.BlockSpec((tm, tn), lambda i, j: (i, j))]
    pipelined = [((tm, k), F32), ((k, tn), w.dtype), ((tm, tn), BF16)]
    if emit_w:
        out_shape.append(jax.ShapeDtypeStruct((k, n), BF16))
        out_specs.append(wb_spec)
        pipelined.append(((k, tn), BF16))
    return pl.pallas_call(
        functools.partial(_mlp_up_kernel, emit_w=emit_w),
        out_shape=tuple(out_shape),
        grid=(m // tm, n // tn),
        in_specs=[
            pl.BlockSpec((tm, k), lambda i, j: (i, 0)),
            pl.BlockSpec((1, k), lambda i, j: (0, 0)),
            w_spec,
        ],
        out_specs=tuple(out_specs),
        scratch_shapes=[pltpu.VMEM((tm, k), BF16)],
        compiler_params=pltpu.CompilerParams(
            dimension_semantics=("parallel", "arbitrary"),
            vmem_limit_bytes=_vmem_limit(pipelined, [((tm, k), BF16), ((k, tn), BF16)])),
        name="mlp_up",
    )(x, g, w)


def _mlp_down_kernel(h_ref, w_ref, res_ref, g_ref, o_ref, *rest, emit_w):
    wb_ref = rest[0] if emit_w else None
    kk = pl.program_id(1)

    def partial_product():
        return jnp.dot(h_ref[...], _weight_tile(w_ref, wb_ref), preferred_element_type=F32)

    @pl.when(kk == 0)
    def _():
        o_ref[...] = res_ref[...] + partial_product()

    @pl.when(kk != 0)
    def _():
        o_ref[...] += partial_product()

    @pl.when(kk == pl.num_programs(1) - 1)
    def _():
        o_ref[...] = _rmsnorm_rows(o_ref[...], g_ref[...])


def _mlp_down_final(h, w, res, g, *, tm, tk, layer=None):
    m, k = h.shape
    n = w.shape[-1]
    emit_w = layer is not None
    assert m % tm == 0 and k % tk == 0 and (not emit_w or m == tm)
    w_spec, wb_spec = _weight_specs(layer, (tk, n), lambda i, j: (j, 0))
    out_shape = [jax.ShapeDtypeStruct((m, n), F32)]
    out_specs = [pl.BlockSpec((tm, n), lambda i, j: (i, 0))]
    pipelined = [((tm, tk), BF16), ((tk, n), w.dtype), ((tm, n), F32), ((tm, n), F32)]
    if emit_w:
        out_shape.append(jax.ShapeDtypeStruct((k, n), BF16))
        out_specs.append(wb_spec)
        pipelined.append(((tk, n), BF16))
    return pl.pallas_call(
        functools.partial(_mlp_down_kernel, emit_w=emit_w),
        out_shape=tuple(out_shape),
        grid=(m // tm, k // tk),
        in_specs=[
            pl.BlockSpec((tm, tk), lambda i, j: (i, j)),
            w_spec,
            pl.BlockSpec((tm, n), lambda i, j: (i, 0)),
            pl.BlockSpec((1, n), lambda i, j: (0, 0)),
        ],
        out_specs=tuple(out_specs),
        compiler_params=pltpu.CompilerParams(
            dimension_semantics=("parallel", "arbitrary"),
            vmem_limit_bytes=_vmem_limit(pipelined, [((tk, n), BF16)] if emit_w else [])),
        name="mlp_down",
    )(h, w, res, g)


def _lru_gates(xh, wa_half, wx_half, ba_half, bx_half, hsp):
    xb = xh.astype(BF16)
    tr = jnp.tanh(jnp.dot(xb, wa_half, preferred_element_type=F32) + ba_half)
    ti = jnp.tanh(jnp.dot(xb, wx_half, preferred_element_type=F32) + bx_half)
    nla = tr * hsp + hsp
    a = jnp.exp(-nla)
    q = jnp.tanh(nla) * (1.0 + a * a)
    mult = jnp.where(q > 0.0, q * lax.rsqrt(q), 0.0)
    return a, mult, 0.5 * ti + 0.5


def _scan_rows(a, b, h_prev):
    t, hd = a.shape
    g = t // SUBLANES
    a3 = a.reshape(g, SUBLANES, hd)
    b3 = b.reshape(g, SUBLANES, hd)
    sub = lax.broadcasted_iota(jnp.int32, (g, SUBLANES, hd), 1)
    d = 1
    while d < SUBLANES:
        keep = sub >= d
        a_sh = jnp.where(keep, pltpu.roll(a3, d, axis=1), 1.0)
        b_sh = jnp.where(keep, pltpu.roll(b3, d, axis=1), 0.0)
        b3 = a3 * b_sh + b3
        a3 = a3 * a_sh
        d *= 2
    tiles = []
    h = h_prev
    for k in range(g):
        hk = a3[k] * h + b3[k]
        tiles.append(hk)
        h = hk[SUBLANES - 1:SUBLANES, :]
    return jnp.concatenate(tiles, axis=0), h


def _lru_seq_kernel(gate_ref, x_ref, cw_ref, cb_ref, wa_ref, wx_ref, ba_ref, bx_ref, lam_ref, g_ref,
                    h0_ref, tail0_ref, *rest, pad, reset_first, n_chunks, n_live, fuse_out):
    if fuse_out:
        wo_ref, o_ref, hfin_ref, tailfin_ref, xe_ref, hc_ref, y_ref, pair_ref, yn_ref = rest
    else:
        o_ref, hfin_ref, tailfin_ref, xe_ref, hc_ref, y_ref, pair_ref = rest
    step = pl.program_id(0)
    live = step < n_live
    c = lax.rem(jnp.minimum(step, n_live - 1), n_chunks)
    t, dl = x_ref.shape
    nh, hd = wa_ref.shape[0], wa_ref.shape[1]
    keep = (lambda new, old: jnp.where(live, new, old)) if fuse_out else (lambda new, old: new)

    @pl.when(jnp.logical_and(c == 0, live))
    def _():
        xe_ref[0:HALO, :] = tail0_ref[...]
        hc_ref[...] = h0_ref[...]

    if fuse_out:
        @pl.when(step == 0)
        def _():
            yn_ref[...] = jnp.zeros_like(yn_ref)

    xe_ref[HALO:HALO + t, :] = x_ref[...]
    grow = c * t + lax.broadcasted_iota(jnp.int32, (t, hd), 0)
    ssq = jnp.zeros((t, hd), F32)
    if fuse_out:
        n_pieces = max(1, min(nh, o_ref.shape[1] // MXU_COLS))
        while nh % n_pieces or o_ref.shape[1] % n_pieces:
            n_pieces -= 1
        heads_per_piece, piece = nh // n_pieces, o_ref.shape[1] // n_pieces
    for h in range(nh):
        sl = slice(h * hd, (h + 1) * hd)
        xh = _conv_rows(xe_ref, pair_ref.at[h % 2], cw_ref, cb_ref, sl, t)
        hsp = (0.5 * LRU_C) * jax.nn.softplus(-lam_ref[:, sl])
        a, mult, i = _lru_gates(xh, wa_ref[h], wx_ref[h], ba_ref[:, sl], bx_ref[:, sl], hsp)
        if reset_first:
            mult = jnp.where(grow == pad, 1.0, mult)
        b = mult * i * xh
        if pad:
            a = jnp.where(grow >= pad, a, 1.0)
            b = jnp.where(grow >= pad, b, 0.0)
        h_prev = hc_ref[:, sl]
        hs, h_last = _scan_rows(a, b, h_prev)
        hc_ref[:, sl] = keep(h_last, h_prev)
        y = hs * gate_ref[:, sl]
        y_ref[:, sl] = y
        ssq = ssq + y * y
        if fuse_out and (h + 1) % heads_per_piece == 0:
            k = (h + 1) // heads_per_piece - 1
            psl = slice(k * piece, (k + 1) * piece)
            o_ref[:, psl] = jnp.dot(yn_ref[...], wo_ref[:, psl], preferred_element_type=F32)
    scale = lax.rsqrt(jnp.sum(ssq, axis=-1, keepdims=True) / dl + EPS)
    yn = (y_ref[...] * scale * g_ref[...]).astype(BF16)
    if fuse_out:
        yn_ref[...] = yn
    else:
        o_ref[...] = yn
    xe_ref[0:HALO, :] = keep(xe_ref[t:t + HALO, :], xe_ref[0:HALO, :])
    hfin_ref[0] = hc_ref[...]
    tailfin_ref[0] = xe_ref[HALO - 3:HALO, :]


def _lru_seq(proj, params, h0, tail0, *, n_seq, seq_len, t, row0, pad=0, reset_first=False, wo=None):
    cw, cb, wa, wx, ba, bx, lam, g = params
    dl = cw.shape[1]
    nh, hd = wa.shape[0], wa.shape[1]
    assert seq_len % t == 0 and row0 % t == 0 and t % SUBLANES == 0
    n_chunks = seq_len // t
    n_live = n_seq * n_chunks
    blk0 = row0 // t
    fuse_out = wo is not None
    chunk = (lambda s: jnp.minimum(s, n_live - 1)) if fuse_out else (lambda s: s)
    out_chunk = (lambda s: jnp.maximum(s - 1, 0)) if fuse_out else (lambda s: s)
    const2 = lambda s: (0, 0)
    const3 = lambda s: (0, 0, 0)
    per_seq = lambda s: (chunk(s) // n_chunks, 0, 0)
    dout, out_dtype = (wo.shape[1], F32) if fuse_out else (dl, BF16)
    pipelined = [((t, dl), F32), ((t, dl), F32), ((t, dout), out_dtype)]
    resident = [((t + HALO, dl), F32), ((t, dl), F32), ((4 * nh, hd, hd), BF16), ((2, t + SUBLANES, hd), F32)]
    scratch = [pltpu.VMEM((t + HALO, dl), F32), pltpu.VMEM((1, dl), F32), pltpu.VMEM((t, dl), F32),
               pltpu.VMEM((2, t + SUBLANES, hd), F32)]
    extra_specs, extra_args = [], []
    if fuse_out:
        extra_specs.append(pl.BlockSpec((dl, dout), const2, pipeline_mode=pl.Buffered(1)))
        extra_args.append(wo)
        resident += [((dl, dout), BF16), ((t, dl), BF16)]
        scratch.append(pltpu.VMEM((t, dl), BF16))
    kern = functools.partial(_lru_seq_kernel, pad=pad, reset_first=reset_first, n_chunks=n_chunks, n_live=n_live,
                             fuse_out=fuse_out)
    return pl.pallas_call(
        kern,
        out_shape=(jax.ShapeDtypeStruct((n_seq * seq_len, dout), out_dtype),
                   jax.ShapeDtypeStruct((n_seq, 1, dl), F32),
                   jax.ShapeDtypeStruct((n_seq, CONV_TAPS - 1, dl), F32)),
        grid=(n_live + 1 if fuse_out else n_live,),
        in_specs=[
            pl.BlockSpec((t, dl), lambda s: (blk0 + chunk(s), 0)),
            pl.BlockSpec((t, dl), lambda s: (blk0 + chunk(s), 1)),
            pl.BlockSpec((CONV_TAPS, dl), const2),
            pl.BlockSpec((1, dl), const2),
            pl.BlockSpec((nh, hd, hd), const3),
            pl.BlockSpec((nh, hd, hd), const3),
            pl.BlockSpec((1, dl), const2),
            pl.BlockSpec((1, dl), const2),
            pl.BlockSpec((1, dl), const2),
            pl.BlockSpec((1, dl), const2),
            pl.BlockSpec((1, dl), const2),
            pl.BlockSpec((HALO, dl), const2),
        ] + extra_specs,
        out_specs=(
            pl.BlockSpec((t, dout), lambda s: (out_chunk(s), 0)),
            pl.BlockSpec((1, 1, dl), per_seq),
            pl.BlockSpec((1, CONV_TAPS - 1, dl), per_seq),
        ),
        scratch_shapes=scratch,
        compiler_params=pltpu.CompilerParams(
            dimension_semantics=("arbitrary",),
            vmem_limit_bytes=_vmem_limit(pipelined, resident)),
        name="lru_seq",
    )(proj, proj, cw, cb, wa, wx, ba, bx, lam, g, h0, tail0, *extra_args)


def _lru_slab_kernel(gate_ref, x_ref, cw_ref, cb_ref, wa_ref, wx_ref, ba_ref, bx_ref, lam_ref, g_ref,
                     h0_ref, tail_ref, o_ref, hfin_ref, tailfin_ref, y_ref):
    ls, bs, dl = x_ref.shape
    nh, hd = wa_ref.shape[0], wa_ref.shape[1]
    ntail = CONV_TAPS - 1
    for h in range(nh):
        sl = slice(h * hd, (h + 1) * hd)
        ext = [tail_ref[k, :, sl] for k in range(ntail)]
        ext += [x_ref[s, :, sl] for s in range(ls)]
        hsp = (0.5 * LRU_C) * jax.nn.softplus(-lam_ref[:, sl])
        hcur = h0_ref[:, sl]
        for s in range(ls):
            xh = _conv_taps(ext, s, cw_ref, cb_ref, sl)
            a, mult, i = _lru_gates(xh, wa_ref[h], wx_ref[h], ba_ref[:, sl], bx_ref[:, sl], hsp)
            hcur = a * hcur + mult * i * xh
            y_ref[s, :, sl] = hcur * gate_ref[s, :, sl]
        hfin_ref[:, sl] = hcur
        for k in range(ntail):
            tailfin_ref[k, :, sl] = ext[ls + k]
    for s in range(ls):
        y = y_ref[s]
        scale = lax.rsqrt(jnp.mean(y * y, axis=-1, keepdims=True) + EPS)
        o_ref[s] = (y * scale * g_ref[...]).astype(o_ref.dtype)


def _lru_slab(proj3, params, h0, tail, *, ls):
    cw, cb, wa, wx, ba, bx, lam, g = params
    bs = proj3.shape[1]
    dl = cw.shape[1]
    nh, hd = wa.shape[0], wa.shape[1]
    ntail = CONV_TAPS - 1
    c2 = lambda i: (0, 0)
    c3 = lambda i: (0, 0, 0)
    pipelined = [((ls, bs, dl), F32)] * 2 + [((ls, bs, dl), BF16)] + [((bs, (2 * ntail + 2) * dl), F32)]
    return pl.pallas_call(
        _lru_slab_kernel,
        out_shape=(jax.ShapeDtypeStruct((ls, bs, dl), BF16),
                   jax.ShapeDtypeStruct((bs, dl), F32),
                   jax.ShapeDtypeStruct((ntail, bs, dl), F32)),
        grid=(1,),
        in_specs=[
            pl.BlockSpec((ls, bs, dl), lambda i: (0, 0, 0)),
            pl.BlockSpec((ls, bs, dl), lambda i: (0, 0, 1)),
            pl.BlockSpec((CONV_TAPS, dl), c2), pl.BlockSpec((1, dl), c2),
            pl.BlockSpec((nh, hd, hd), c3), pl.BlockSpec((nh, hd, hd), c3),
            pl.BlockSpec((1, dl), c2), pl.BlockSpec((1, dl), c2), pl.BlockSpec((1, dl), c2), pl.BlockSpec((1, dl), c2),
            pl.BlockSpec((bs, dl), c2), pl.BlockSpec((ntail, bs, dl), c3),
        ],
        out_specs=(pl.BlockSpec((ls, bs, dl), c3), pl.BlockSpec((bs, dl), c2), pl.BlockSpec((ntail, bs, dl), c3)),
        scratch_shapes=[pltpu.VMEM((ls, bs, dl), F32)],
        compiler_params=pltpu.CompilerParams(
            dimension_semantics=("arbitrary",),
            vmem_limit_bytes=_vmem_limit(pipelined, [((ls, bs, dl), F32)])),
        name="lru_slab",
    )(proj3, proj3, cw, cb, wa, wx, ba, bx, lam, g, h0, tail)


def _head_lane_mask(n_heads):
    return lax.broadcasted_iota(jnp.int32, (1, HEAD_PAD), 1) < n_heads


def _gated_group_norm(y, z_act, g):
    yg = y * z_act
    scale = lax.rsqrt(jnp.mean(yg * yg, axis=-1, keepdims=True) + EPS)
    return yg * scale * g


def _ssd_seq_kernel(z_ref, xbc_ref, dt_ref, cw_ref, cb_ref, dtb_ref, alog_ref, dskip_ref, g_ref, e3_ref,
                    s0_ref, tail0_ref, *rest, pad, q, n_heads, n_groups, n_chunks, n_live, fuse_out):
    if fuse_out:
        wo_ref, part_ref, res_ref = rest[:3]
        o_ref, sfin_ref, tailfin_ref, xe_ref, xc_ref, ex_ref, y_ref, s_ref, pair_ref, yn_ref, ynp_ref = rest[3:]
    else:
        o_ref, sfin_ref, tailfin_ref, xe_ref, xc_ref, ex_ref, y_ref, s_ref, pair_ref = rest
        yn_ref = o_ref
    step = pl.program_id(0)
    live = step < n_live
    c = lax.rem(jnp.minimum(step, n_live - 1), n_chunks)
    keep = (lambda new, old: jnp.where(live, new, old)) if fuse_out else (lambda new, old: new)
    tq, ds = z_ref.shape
    cps = tq // q
    dc = xbc_ref.shape[1]
    gn = (dc - ds) // 2
    n = gn // n_groups
    p = ds // n_heads
    r = n_heads // n_groups
    gw = ds // n_groups
    hpb = LANES // p

    @pl.when(jnp.logical_and(c == 0, live))
    def _():
        xe_ref[0:HALO, :] = tail0_ref[...]
        s_ref[...] = s0_ref[...]

    if fuse_out:
        @pl.when(step == 0)
        def _():
            ynp_ref[...] = jnp.zeros_like(ynp_ref)

    xe_ref[HALO:HALO + tq, :] = xbc_ref[...]
    valid_all = (c * tq + lax.broadcasted_iota(jnp.int32, (tq, 1), 0)) >= pad

    cblk = CONV_LANE_BLOCK if dc % CONV_LANE_BLOCK == 0 else LANES
    for j in range(dc // cblk):
        sl = slice(j * cblk, (j + 1) * cblk)
        v = _silu(_conv_rows(xe_ref, pair_ref.at[j % 2], cw_ref, cb_ref, sl, tq))
        if pad and (j + 1) * cblk <= ds:
            v = jnp.where(valid_all, v, 0.0)
        xc_ref[:, sl] = v

    ri = lax.broadcasted_iota(jnp.int32, (q, q), 0)
    ci = lax.broadcasted_iota(jnp.int32, (q, q), 1)
    causal = ci <= ri
    tri = jnp.where(causal, 1.0, 0.0).astype(BF16)
    lane = lax.broadcasted_iota(jnp.int32, (q, LANES), 1)
    a_neg = -jnp.exp(alog_ref[...])
    n_pieces = cps * n_groups
    piece = o_ref.shape[1] // n_pieces if fuse_out else 0

    for u in range(cps):
        rows = slice(u * q, (u + 1) * q)
        dtv = jnp.where(_head_lane_mask(n_heads), jax.nn.softplus(dt_ref[rows, :] + dtb_ref[...]), 0.0)
        if pad:
            dtv = jnp.where(valid_all[u * q:(u + 1) * q], dtv, 0.0)
        ac3 = jnp.dot(tri, _split3(dtv * a_neg), preferred_element_type=F32)
        acum = ac3[:, 0:HEAD_PAD] + ac3[:, HEAD_PAD:2 * HEAD_PAD] + ac3[:, 2 * HEAD_PAD:3 * HEAD_PAD]
        alast = acum[q - 1:q, :]
        eac = jnp.exp(acum)
        wend = jnp.exp(alast - acum) * dtv
        cdec = jnp.broadcast_to(jnp.exp(alast), (SUBLANES, HEAD_PAD))
        ex_ref[...] = _expand_heads(jnp.concatenate([eac, wend, cdec], axis=0), e3_ref)
        acum_t = acum.T
        dt_t = dtv.T

        for g in range(n_groups):
            gsl = slice(g * gw, (g + 1) * gw)
            bg = xc_ref[rows, ds + g * n:ds + (g + 1) * n].astype(BF16)
            cg = xc_ref[rows, ds + gn + g * n:ds + gn + (g + 1) * n].astype(BF16)
            cbm = lax.dot_general(cg, bg, _NT, preferred_element_type=F32)
            yoff = jnp.dot(cg, s_ref[:, gsl].astype(BF16), preferred_element_type=F32)
            for k in range(gw // LANES):
                lsl = slice(g * gw + k * LANES, g * gw + (k + 1) * LANES)
                xs = xc_ref[rows, lsl]
                ms = []
                xparts = []
                for w in range(hpb):
                    h = g * r + k * hpb + w
                    seg = acum[:, h:h + 1] - acum_t[h:h + 1, :]
                    lm = jnp.where(causal, jnp.exp(seg), 0.0) * dt_t[h:h + 1, :]
                    ms.append((cbm * lm).astype(BF16))
                    inhead = (lane >= w * p) & (lane < (w + 1) * p)
                    xparts.append(jnp.where(inhead, xs, 0.0).astype(BF16))
                ydiag = jnp.dot(jnp.concatenate(ms, axis=1), jnp.concatenate(xparts, axis=0),
                                preferred_element_type=F32)
                y = ydiag + yoff[:, k * LANES:(k + 1) * LANES] * ex_ref[0:q, lsl]
                y_ref[rows, lsl] = y + dskip_ref[:, lsl] * xs
            yn_ref[rows, gsl] = _gated_group_norm(y_ref[rows, gsl], z_ref[rows, gsl], g_ref[:, gsl]).astype(BF16)
            xw = (xc_ref[rows, gsl] * ex_ref[q:2 * q, gsl]).astype(BF16)
            upd = lax.dot_general(bg, xw, (((0,), (0,)), ((), ())), preferred_element_type=F32)
            s_old = s_ref[:, gsl]
            s_ref[:, gsl] = keep(ex_ref[2 * q:2 * q + 1, gsl] * s_old + upd, s_old)
            if fuse_out:
                psl = slice((u * n_groups + g) * piece, (u * n_groups + g + 1) * piece)
                mix = part_ref[:, psl] + jnp.dot(ynp_ref[...], wo_ref[:, psl], preferred_element_type=F32)
                o_ref[:, psl] = res_ref[:, psl] + mix

    if fuse_out:
        ynp_ref[...] = yn_ref[...]
    xe_ref[0:HALO, :] = keep(xe_ref[tq:tq + HALO, :], xe_ref[0:HALO, :])
    if fuse_out:
        @pl.when(c == n_chunks - 1)
        def _():
            sfin_ref[0] = s_ref[...].T
    else:
        sfin_ref[0] = s_ref[...]
    tailfin_ref[0] = xe_ref[HALO - 3:HALO, :]


def _ssd_seq(proj, dt, params, s0, tail0, *, n_seq, seq_len, row0, n_heads, n_groups, dl, pad=0, fuse=None):
    cw, cb, dtb, alog, dskip, g, e3 = params
    dc = cw.shape[1]
    ds = dskip.shape[1]
    n = s0.shape[0]
    q = SSD_CHUNK
    tq = q * (SSD_CHUNKS_PER_STEP if seq_len % (q * SSD_CHUNKS_PER_STEP) == 0 else 1)
    assert seq_len % tq == 0 and row0 % tq == 0
    assert (2 * dl) % ds == 0 and (2 * dl + ds) % dc == 0 and LANES % (ds // n_heads) == 0
    n_chunks = seq_len // tq
    n_live = n_seq * n_chunks
    blk0 = row0 // tq
    fuse_out = fuse is not None
    chunk = (lambda s: jnp.minimum(s, n_live - 1)) if fuse_out else (lambda s: s)
    rows = lambda s: blk0 + chunk(s)
    const2 = lambda s: (0, 0)
    out_rows = (lambda s: (jnp.maximum(s - 1, 0), 0)) if fuse_out else (lambda s: (s, 0))
    per_seq = lambda s: (chunk(s) // n_chunks, 0, 0)
    dout, out_dtype = (fuse[0].shape[1], F32) if fuse_out else (ds, BF16)
    state_shape = (ds, n) if fuse_out else (n, ds)
    pipelined = [((tq, ds), F32), ((tq, dc), F32), ((tq, HEAD_PAD), F32), ((tq, dout), out_dtype)]
    resident = [((tq + HALO, dc), F32), ((tq, dc), F32), ((2 * q + SUBLANES, ds), F32), ((tq, ds), F32),
                ((3 * n, ds), F32), ((6 * HEAD_PAD, ds), BF16), ((2, tq + SUBLANES, CONV_LANE_BLOCK), F32)]
    cblk = CONV_LANE_BLOCK if dc % CONV_LANE_BLOCK == 0 else LANES
    scratch = [pltpu.VMEM((tq + HALO, dc), F32), pltpu.VMEM((tq, dc), F32),
               pltpu.VMEM((2 * q + SUBLANES, ds), F32), pltpu.VMEM((tq, ds), F32), pltpu.VMEM((n, ds), F32),
               pltpu.VMEM((2, tq + SUBLANES, cblk), F32)]
    extra_specs, extra_args = [], []
    if fuse_out:
        extra_specs = [pl.BlockSpec((ds, dout), const2, pipeline_mode=pl.Buffered(1)),
                       pl.BlockSpec((tq, dout), out_rows), pl.BlockSpec((tq, dout), out_rows)]
        extra_args = list(fuse)
        pipelined += [((tq, dout), F32)] * 2
        resident += [((ds, dout), BF16), ((2 * tq, ds), BF16)]
        scratch += [pltpu.VMEM((tq, ds), BF16), pltpu.VMEM((tq, ds), BF16)]
    kern = functools.partial(_ssd_seq_kernel, pad=pad, q=q, n_heads=n_heads, n_groups=n_groups, n_chunks=n_chunks,
                             n_live=n_live, fuse_out=fuse_out)
    return pl.pallas_call(
        kern,
        out_shape=(jax.ShapeDtypeStruct((n_seq * seq_len, dout), out_dtype),
                   jax.ShapeDtypeStruct((n_seq,) + state_shape, F32),
                   jax.ShapeDtypeStruct((n_seq, CONV_TAPS - 1, dc), F32)),
        grid=(n_live + 1 if fuse_out else n_live,),
        in_specs=[
            pl.BlockSpec((tq, ds), lambda s: (rows(s), (2 * dl) // ds)),
            pl.BlockSpec((tq, dc), lambda s: (rows(s), (2 * dl + ds) // dc)),
            pl.BlockSpec((tq, HEAD_PAD), lambda s: (rows(s), 0)),
            pl.BlockSpec((CONV_TAPS, dc), const2),
            pl.BlockSpec((1, dc), const2),
            pl.BlockSpec((1, HEAD_PAD), const2),
            pl.BlockSpec((1, HEAD_PAD), const2),
            pl.BlockSpec((1, ds), const2),
            pl.BlockSpec((1, ds), const2),
            pl.BlockSpec((3 * HEAD_PAD, ds), const2),
            pl.BlockSpec((n, ds), const2),
            pl.BlockSpec((HALO, dc), const2),
        ] + extra_specs,
        out_specs=(
            pl.BlockSpec((tq, dout), out_rows),
            pl.BlockSpec((1,) + state_shape, per_seq),
            pl.BlockSpec((1, CONV_TAPS - 1, dc), per_seq),
        ),
        scratch_shapes=scratch,
        compiler_params=pltpu.CompilerParams(
            dimension_semantics=("arbitrary",),
            vmem_limit_bytes=_vmem_limit(pipelined, resident)),
        name="ssd_seq",
    )(proj, proj, dt, cw, cb, dtb, alog, dskip, g, e3, s0, tail0, *extra_args)


def _ssd_slab_pre_kernel(xbc_ref, dt_ref, tail_ref, cw_ref, cb_ref, dtb_ref, alog_ref, dskip_ref, e3_ref,
                         ypart_ref, eace_ref, c_ref, b_ref, xw_ref, cdec_ref, tailfin_ref, xc_ref,
                         *, n_heads, n_groups):
    ls, bs, dc = xbc_ref.shape
    ds = dskip_ref.shape[1]
    gn = (dc - ds) // 2
    n = gn // n_groups
    r = n_heads // n_groups
    ntail = CONV_TAPS - 1

    cblk = CONV_LANE_BLOCK if dc % CONV_LANE_BLOCK == 0 else LANES
    for j in range(dc // cblk):
        sl = slice(j * cblk, (j + 1) * cblk)
        ext = [tail_ref[k, :, sl] for k in range(ntail)]
        ext += [xbc_ref[s, :, sl] for s in range(ls)]
        for s in range(ls):
            xc_ref[s, :, sl] = _silu(_conv_taps(ext, s, cw_ref, cb_ref, sl))
        for k in range(ntail):
            tailfin_ref[k, :, sl] = ext[ls + k]

    hmask = _head_lane_mask(n_heads)
    a_neg = -jnp.exp(alog_ref[...])
    dtv, acum = [], []
    run = jnp.zeros((bs, HEAD_PAD), F32)
    for s in range(ls):
        d = jnp.where(hmask, jax.nn.softplus(dt_ref[s] + dtb_ref[...]), 0.0)
        run = run + d * a_neg
        dtv.append(d)
        acum.append(run)
    alast = acum[ls - 1]
    cdec_ref[...] = jnp.exp(alast)
    head_group = lax.broadcasted_iota(jnp.int32, (1, HEAD_PAD), 1) // r

    for s in range(ls):
        eace_ref[s] = _expand_heads(jnp.exp(acum[s]), e3_ref)
        wend_e = _expand_heads(jnp.exp(alast - acum[s]) * dtv[s], e3_ref)
        xw_ref[:, s * ds:(s + 1) * ds] = xc_ref[s, :, 0:ds] * wend_e
        b_ref[:, s * gn:(s + 1) * gn] = xc_ref[s, :, ds:ds + gn]
        c_ref[:, s * gn:(s + 1) * gn] = xc_ref[s, :, ds + gn:ds + 2 * gn]
        ypart = dskip_ref[...] * xc_ref[s, :, 0:ds]
        for j in range(s + 1):
            cbh = jnp.zeros((bs, HEAD_PAD), F32)
            for g in range(n_groups):
                cs = xc_ref[s, :, ds + gn + g * n:ds + gn + (g + 1) * n]
                bj = xc_ref[j, :, ds + g * n:ds + (g + 1) * n]
                cbg = jnp.sum(cs * bj, axis=-1, keepdims=True)
                cbh = cbh + jnp.where(head_group == g, cbg, 0.0)
            coef = cbh * (jnp.exp(acum[s] - acum[j]) * dtv[j])
            ypart = ypart + _expand_heads(coef, e3_ref) * xc_ref[j, :, 0:ds]
        ypart_ref[s] = ypart
    for s in range(ls, SLAB_ROWS):
        xw_ref[:, s * ds:(s + 1) * ds] = jnp.zeros((bs, ds), F32)
        b_ref[:, s * gn:(s + 1) * gn] = jnp.zeros((bs, gn), F32)
        c_ref[:, s * gn:(s + 1) * gn] = jnp.zeros((bs, gn), F32)


def _ssd_slab_pre(proj3, dt3, tail, params, *, ls, n_heads, n_groups, dl):
    cw, cb, dtb, alog, dskip, _, e3 = params
    bs = proj3.shape[1]
    dc = cw.shape[1]
    ds = dskip.shape[1]
    gn = (dc - ds) // 2
    ntail = CONV_TAPS - 1
    assert ls <= SLAB_ROWS and (2 * dl + ds) % dc == 0
    c2 = lambda i: (0, 0)
    c3 = lambda i: (0, 0, 0)
    pipelined = [((ls, bs, dc), F32), ((ls, bs, HEAD_PAD), F32), ((bs, 2 * ntail * dc), F32),
                 ((2 * ls, bs, ds), F32), ((bs, SLAB_ROWS * (2 * gn + ds)), F32), ((3 * HEAD_PAD, ds), BF16)]
    kern = functools.partial(_ssd_slab_pre_kernel, n_heads=n_heads, n_groups=n_groups)
    return pl.pallas_call(
        kern,
        out_shape=(jax.ShapeDtypeStruct((ls, bs, ds), F32),
                   jax.ShapeDtypeStruct((ls, bs, ds), F32),
                   jax.ShapeDtypeStruct((bs, SLAB_ROWS * gn), F32),
                   jax.ShapeDtypeStruct((bs, SLAB_ROWS * gn), F32),
                   jax.ShapeDtypeStruct((bs, SLAB_ROWS * ds), F32),
                   jax.ShapeDtypeStruct((bs, HEAD_PAD), F32),
                   jax.ShapeDtypeStruct((ntail, bs, dc), F32)),
        grid=(1,),
        in_specs=[
            pl.BlockSpec((ls, bs, dc), lambda i: (0, 0, (2 * dl + ds) // dc)),
            pl.BlockSpec((ls, bs, HEAD_PAD), c3),
            pl.BlockSpec((ntail, bs, dc), c3),
            pl.BlockSpec((CONV_TAPS, dc), c2), pl.BlockSpec((1, dc), c2),
            pl.BlockSpec((1, HEAD_PAD), c2), pl.BlockSpec((1, HEAD_PAD), c2),
            pl.BlockSpec((1, ds), c2), pl.BlockSpec((3 * HEAD_PAD, ds), c2),
        ],
        out_specs=(pl.BlockSpec((ls, bs, ds), c3), pl.BlockSpec((ls, bs, ds), c3),
                   pl.BlockSpec((bs, SLAB_ROWS * gn), c2), pl.BlockSpec((bs, SLAB_ROWS * gn), c2),
                   pl.BlockSpec((bs, SLAB_ROWS * ds), c2), pl.BlockSpec((bs, HEAD_PAD), c2),
                   pl.BlockSpec((ntail, bs, dc), c3)),
        scratch_shapes=[pltpu.VMEM((ls, bs, dc), F32)],
        compiler_params=pltpu.CompilerParams(
            dimension_semantics=("arbitrary",),
            vmem_limit_bytes=_vmem_limit(pipelined, [((ls, bs, dc), F32)])),
        name="ssd_slab_pre",
    )(proj3, dt3, tail, cw, cb, dtb, alog, dskip, e3)


def _ssd_state_kernel(cdec_ref, s_ref, c_ref, b_ref, xw_ref, snew_ref, yoff_ref, *, n_heads, n_groups):
    i = pl.program_id(0)
    sb, hp, n = s_ref.shape
    p = hp // n_heads
    r = n_heads // n_groups
    gw = hp // n_groups
    for q in range(sb):
        for g in range(n_groups):
            gsl = slice(g * gw, (g + 1) * gw)
            sg = s_ref[q, gsl, :]
            cg = c_ref[q, :, g * n:(g + 1) * n].astype(BF16)
            yoff_ref[q, :, gsl] = lax.dot_general(cg, sg.astype(BF16), (((1,), (1,)), ((), ())),
                                                  preferred_element_type=F32)
            upd = lax.dot_general(xw_ref[q, :, gsl].astype(BF16), b_ref[q, :, g * n:(g + 1) * n].astype(BF16),
                                  (((0,), (0,)), ((), ())), preferred_element_type=F32)
            for u in range(r):
                h = g * r + u
                rows = slice(h * p, (h + 1) * p)
                snew_ref[q, rows, :] = cdec_ref[i * sb + q, h] * s_ref[q, rows, :] + upd[u * p:(u + 1) * p, :]


def _ssd_state(cdec, state, c_rows, b_rows, xw_rows, *, n_heads, n_groups):
    bs, hp, n = state.shape
    gn = c_rows.shape[2]
    sb = STATE_SEQS_PER_STEP if bs % STATE_SEQS_PER_STEP == 0 else 1
    per_seq = lambda i: (i, 0, 0)
    pipelined = [((sb, hp, n), F32)] * 2 + [((sb, SLAB_ROWS, gn), F32)] * 2 + [((sb, SLAB_ROWS, hp), F32)] * 2
    kern = functools.partial(_ssd_state_kernel, n_heads=n_heads, n_groups=n_groups)
    return pl.pallas_call(
        kern,
        out_shape=(jax.ShapeDtypeStruct((bs, hp, n), F32), jax.ShapeDtypeStruct((bs, SLAB_ROWS, hp), F32)),
        grid=(bs // sb,),
        in_specs=[
            pl.BlockSpec(memory_space=pltpu.SMEM),
            pl.BlockSpec((sb, hp, n), per_seq),
            pl.BlockSpec((sb, SLAB_ROWS, gn), per_seq),
            pl.BlockSpec((sb, SLAB_ROWS, gn), per_seq),
            pl.BlockSpec((sb, SLAB_ROWS, hp), per_seq),
        ],
        out_specs=(pl.BlockSpec((sb, hp, n), per_seq), pl.BlockSpec((sb, SLAB_ROWS, hp), per_seq)),
        compiler_params=pltpu.CompilerParams(
            dimension_semantics=("parallel",),
            vmem_limit_bytes=_vmem_limit(pipelined, [])),
        name="ssd_state",
    )(cdec, state, c_rows, b_rows, xw_rows)


def _ssd_slab_post_kernel(ypart_ref, eace_ref, yoff_ref, z_ref, g_ref, o_ref, *, n_groups):
    ls, bs, ds = ypart_ref.shape
    gw = ds // n_groups
    for s in range(ls):
        for g in range(n_groups):
            gsl = slice(g * gw, (g + 1) * gw)
            y = ypart_ref[s, :, gsl] + eace_ref[s, :, gsl] * yoff_ref[:, s * ds + g * gw:s * ds + (g + 1) * gw]
            o_ref[s, :, gsl] = _gated_group_norm(y, z_ref[s, :, gsl], g_ref[:, gsl]).astype(o_ref.dtype)


def _ssd_slab_post(ypart, eace, yoff, proj3, g, *, n_groups, dl):
    ls, bs, ds = ypart.shape
    assert (2 * dl) % ds == 0
    c2 = lambda i: (0, 0)
    c3 = lambda i: (0, 0, 0)
    pipelined = [((ls, bs, ds), F32)] * 3 + [((bs, SLAB_ROWS * ds), F32), ((ls, bs, ds), BF16)]
    kern = functools.partial(_ssd_slab_post_kernel, n_groups=n_groups)
    return pl.pallas_call(
        kern,
        out_shape=jax.ShapeDtypeStruct((ls, bs, ds), BF16),
        grid=(1,),
        in_specs=[
            pl.BlockSpec((ls, bs, ds), c3), pl.BlockSpec((ls, bs, ds), c3),
            pl.BlockSpec((bs, SLAB_ROWS * ds), c2),
            pl.BlockSpec((ls, bs, ds), lambda i: (0, 0, (2 * dl) // ds)),
            pl.BlockSpec((1, ds), c2),
        ],
        out_specs=pl.BlockSpec((ls, bs, ds), c3),
        compiler_params=pltpu.CompilerParams(
            dimension_semantics=("arbitrary",),
            vmem_limit_bytes=_vmem_limit(pipelined, [])),
        name="ssd_slab_post",
    )(ypart, eace, yoff, proj3, g)


def _head_expansion(n_heads, head_dim):
    rows = lax.broadcasted_iota(jnp.int32, (HEAD_PAD, n_heads * head_dim), 0)
    cols = lax.broadcasted_iota(jnp.int32, (HEAD_PAD, n_heads * head_dim), 1)
    e = (cols // head_dim == rows).astype(BF16)
    return jnp.concatenate([e, e, e], axis=0)


def _pad_lanes(v, width):
    return jnp.pad(v, ((0, 0), (0, width - v.shape[1])))


def _tail_block(tail):
    return jnp.pad(tail, ((HALO - tail.shape[0], 0), (0, 0)))


def _mlp(x1, g_mlp, w_up, w_down, g_final, *, layer=None):
    m = x1.shape[0]
    tiles = MATMUL_TILES
    tm = m if layer is not None else _pick_tile(m, tiles)
    up_tiles = tiles if layer is not None else (2048,) + tiles
    up = _mlp_up(x1, g_mlp, w_up, tm=tm, tn=_pick_tile(w_up.shape[-1], up_tiles), layer=layer)
    hid = up[0]
    down = _mlp_down_final(hid, w_down, x1, g_final, tm=tm, tk=_pick_tile(w_down.shape[-2], tiles), layer=layer)
    if layer is None:
        return down[0]
    return down[0], up[1], down[1]


def kernel(x_prompt, x_sample, state_lru_h, state_lru_conv, state_ssd, state_ssd_conv, meta_tokens, g_mix, w_in, conv_lru_w, conv_lru_b, lru_wa, lru_ba, lru_wx, lru_bx, lru_lambda, g_lru_out, conv_ssd_w, conv_ssd_b, dt_bias, a_log, d_skip, g_ssd_out, w_out, g_mlp, w_up, w_down, g_final):
    depth = w_in.shape[0]
    assert depth == 1, "single-layer step"
    l = 0
    bp, lp, d = x_prompt.shape
    bs, ls, _ = x_sample.shape
    n_meta = meta_tokens.shape[0]
    dl = state_lru_h.shape[-1]
    n_heads, p, n = state_ssd.shape[-3:]
    ds = n_heads * p
    dc = state_ssd_conv.shape[-1]
    gn = (dc - ds) // 2
    n_groups = gn // n
    nw = 2 * dl + ds + dc
    ntail = CONV_TAPS - 1
    q = SSD_CHUNK
    meta_pad = (-n_meta) % q
    assert n_heads <= HEAD_PAD and (bs * ls) % q == 0 and lp % q == 0 and q % bs == 0

    row = lambda v: v.reshape(1, -1).astype(F32)
    w_in_t = jnp.swapaxes(w_in, 1, 2)
    lru_params = (conv_lru_w[l], row(conv_lru_b[l]), (0.5 * lru_wa[l]).astype(BF16), (0.5 * lru_wx[l]).astype(BF16),
                  row(0.5 * lru_ba[l]), row(0.5 * lru_bx[l]), row(lru_lambda[l]), row(g_lru_out[l]))
    ssd_params = (conv_ssd_w[l], row(conv_ssd_b[l]), _pad_lanes(row(dt_bias[l]), HEAD_PAD),
                  _pad_lanes(row(a_log[l]), HEAD_PAD), row(jnp.repeat(d_skip[l], p)), row(g_ssd_out[l]),
                  _head_expansion(n_heads, p))
    g_mix_r, g_mlp_r, g_final_r = row(g_mix[l]), row(g_mlp[l]), row(g_final)

    xs_tm = x_sample.transpose(1, 0, 2).reshape(ls * bs, d)
    x_side = jnp.concatenate([xs_tm, jnp.zeros((meta_pad, d), F32), meta_tokens.astype(F32)], axis=0)
    xp_rows = x_prompt.reshape(bp * lp, d)

    tiles = MATMUL_TILES
    tn_in = next(t for t in tiles if nw % t == 0 and dl % t == 0 and ds % t == 0)
    act_cols = dict(gelu_cols=(0, dl), silu_cols=(2 * dl, 2 * dl + ds))
    proj_side, dt_side, w_in_b, w_dt_b = _in_proj(x_side, g_mix_r, w_in_t, nw=nw, n_dt=n_heads, tm=x_side.shape[0],
                                                  tn=tn_in, layer=l, **act_cols)

    proj_s3 = proj_side.reshape(-1, bs, nw)
    dt_s3 = dt_side.reshape(-1, bs, HEAD_PAD)
    lru_s, s_h, s_ltail = _lru_slab(proj_s3, lru_params, state_lru_h[l],
                                    state_lru_conv[l].transpose(1, 0, 2), ls=ls)
    ypart, eace, c_rows, b_rows, xw_rows, cdec, s_stail = _ssd_slab_pre(
        proj_s3, dt_s3, state_ssd_conv[l].transpose(1, 0, 2), ssd_params,
        ls=ls, n_heads=n_heads, n_groups=n_groups, dl=dl)
    s_new, yoff = _ssd_state(cdec, state_ssd[l].reshape(bs, ds, n),
                             c_rows.reshape(bs, SLAB_ROWS, gn), b_rows.reshape(bs, SLAB_ROWS, gn),
                             xw_rows.reshape(bs, SLAB_ROWS, ds), n_heads=n_heads, n_groups=n_groups)
    ssd_s = _ssd_slab_post(ypart, eace, yoff.reshape(bs, SLAB_ROWS * ds), proj_s3, ssd_params[5],
                           n_groups=n_groups, dl=dl)
    x1_s, w_out_lru, w_out_ssd = _out_proj(lru_s.reshape(ls * bs, dl), ssd_s.reshape(ls * bs, ds), w_out, l, xs_tm,
                                           tn=_pick_tile(d, (512, 256, 128)))
    y_s, w_up_b, w_down_b = _mlp(x1_s, g_mlp_r, w_up, w_down, g_final_r, layer=l)

    meta_row0 = ls * bs
    _, m_h, m_ltail = _lru_seq(proj_side, lru_params, jnp.zeros((1, dl), F32), jnp.zeros((HALO, dl), F32),
                               n_seq=1, seq_len=q, t=q, row0=meta_row0, pad=meta_pad, reset_first=True)
    _, m_s, m_stail = _ssd_seq(proj_side, dt_side, ssd_params, jnp.zeros((n, ds), F32), jnp.zeros((HALO, dc), F32),
                               n_seq=1, seq_len=q, row0=meta_row0, n_heads=n_heads, n_groups=n_groups, dl=dl,
                               pad=meta_pad)

    proj_p, dt_p = _in_proj(xp_rows, g_mix_r, w_in_b, w_dt_b, nw=nw, n_dt=n_heads, tm=_pick_tile(bp * lp, tiles),
                            tn=tn_in, **act_cols)
    part_p, p_h, p_ltail = _lru_seq(proj_p, lru_params, m_h[0], _tail_block(m_ltail[0]),
                                    n_seq=bp, seq_len=lp, t=_pick_tile(lp, (256, 128)), row0=0, wo=w_out_lru)
    x1_p, p_s, p_stail = _ssd_seq(proj_p, dt_p, ssd_params, m_s[0], _tail_block(m_stail[0]),
                                  n_seq=bp, seq_len=lp, row0=0, n_heads=n_heads, n_groups=n_groups, dl=dl,
                                  fuse=(w_out_ssd, part_p, xp_rows))
    y_p = _mlp(x1_p, g_mlp_r, w_up_b, w_down_b, g_final_r)

    y_prompt = y_p.reshape(bp, lp, d)
    y_sample = y_s.reshape(ls, bs, d).transpose(1, 0, 2)
    p_lru_h = p_h.reshape(1, bp, dl)
    p_lru_conv = p_ltail.reshape(1, bp, ntail, dl)
    p_ssd = p_s.reshape(1, bp, n_heads, p, n)
    p_ssd_conv = p_stail.reshape(1, bp, ntail, dc)
    s_lru_h = s_h.reshape(1, bs, dl)
    s_lru_conv = s_ltail.transpose(1, 0, 2)[None]
    s_ssd = s_new.reshape(1, bs, n_heads, p, n)
    s_ssd_conv = s_stail.transpose(1, 0, 2)[None]
    return (y_prompt, y_sample, p_lru_h, p_lru_conv, p_ssd, p_ssd_conv, s_lru_h, s_lru_conv, s_ssd, s_ssd_conv)
```

```python
import functools

import jax
import jax.numpy as jnp
from jax import lax
from jax.experimental import pallas as pl
from jax.experimental.pallas import tpu as pltpu

F32 = jnp.float32
BF16 = jnp.bfloat16

EPS = 1e-6
LRU_C = 8.0
CONV_TAPS = 4

LANES = 128
SUBLANES = 8
MXU_COLS = 256
VMEM_BYTES_V7X = 64 * 1024 * 1024
VMEM_TEMP_BYTES = 10 * 1024 * 1024
VMEM_CEILING_BYTES = VMEM_BYTES_V7X - 6 * 1024 * 1024

HALO = 2 * SUBLANES
SSD_CHUNK = 128
SSD_CHUNKS_PER_STEP = 2
HEAD_PAD = LANES
SLAB_ROWS = SUBLANES
STATE_SEQS_PER_STEP = 8
CONV_LANE_BLOCK = 128
LRU_ROW_BLOCK = 128
MATMUL_TILES = (1024, 512, 256, 128)


def _nbytes(shape, dtype):
    n = 1
    for s in shape:
        n *= s
    return n * jnp.dtype(dtype).itemsize


def _vmem_limit(pipelined, resident):
    est = 2 * sum(_nbytes(s, d) for s, d in pipelined) + sum(_nbytes(s, d) for s, d in resident)
    return int(min(est + VMEM_TEMP_BYTES, VMEM_CEILING_BYTES))


def _pick_tile(m, prefs):
    for t in prefs:
        if m % t == 0:
            return t
    return m


def _silu(x):
    h = 0.5 * x
    return h * jnp.tanh(h) + h


def _split3(x):
    hi = x.astype(BF16)
    r1 = x - hi.astype(F32)
    mid = r1.astype(BF16)
    lo = (r1 - mid.astype(F32)).astype(BF16)
    return jnp.concatenate([hi, mid, lo], axis=1)


def _expand_heads(x, e3_ref):
    return jnp.dot(_split3(x), e3_ref[...], preferred_element_type=F32)


def _rmsnorm_rows(x, g):
    ms = jnp.mean(x * x, axis=-1, keepdims=True)
    return x * lax.rsqrt(ms + EPS) * g


def _conv_rows(xe_ref, pair_ref, cw_ref, cb_ref, sl, t, r0=0):
    lo = HALO - SUBLANES + r0
    x0 = xe_ref[lo:lo + SUBLANES + t, sl]
    x1 = xe_ref[lo - 1:lo - 1 + SUBLANES + t, sl]
    pair_ref[...] = x0 * cw_ref[1:2, sl] + x1 * cw_ref[0:1, sl]
    near = x0[SUBLANES:] * cw_ref[3:4, sl] + x1[SUBLANES:] * cw_ref[2:3, sl]
    return cb_ref[:, sl] + near + pair_ref[SUBLANES - 2:SUBLANES - 2 + t, :]


def _conv_taps(ext, s, cw_ref, cb_ref, sl):
    v = cb_ref[:, sl] + ext[s] * cw_ref[0:1, sl]
    for k in range(1, CONV_TAPS):
        v = v + ext[s + k] * cw_ref[k:k + 1, sl]
    return v


def _weight_tile(w_ref, wb_ref):
    if wb_ref is None:
        return w_ref[...]
    w = w_ref[...].astype(BF16)
    wb_ref[...] = w
    return w


def _weight_specs(layer, blk, idx):
    if layer is None:
        return pl.BlockSpec(blk, idx), None
    return pl.BlockSpec((None,) + blk, lambda i, j: (layer,) + idx(i, j)), pl.BlockSpec(blk, idx)


_NT = (((1,), (1,)), ((), ()))


def _in_proj_kernel(x_ref, g_ref, w_ref, wdt_ref, o_ref, dt_ref, *rest, gelu_tiles, silu_tiles, emit_w):
    if emit_w:
        wb_ref, wdtb_ref, xn_ref = rest
    else:
        wb_ref, wdtb_ref, xn_ref = None, None, rest[0]
    j = pl.program_id(1)

    @pl.when(j == 0)
    def _():
        xn = _rmsnorm_rows(x_ref[...], g_ref[...]).astype(BF16)
        xn_ref[...] = xn
        wdt = _weight_tile(wdt_ref, wdtb_ref)
        wdt = jnp.concatenate([wdt, jnp.zeros((HEAD_PAD - wdt.shape[0], wdt.shape[1]), BF16)], axis=0)
        dt_ref[...] = lax.dot_general(xn, wdt, _NT, preferred_element_type=F32)

    def tile():
        return lax.dot_general(xn_ref[...], _weight_tile(w_ref, wb_ref), _NT, preferred_element_type=F32)

    in_range = lambda r: jnp.logical_and(j >= r[0], j < r[1])
    is_gelu, is_silu = in_range(gelu_tiles), in_range(silu_tiles)

    @pl.when(is_gelu)
    def _():
        o_ref[...] = jax.nn.gelu(tile())

    @pl.when(is_silu)
    def _():
        o_ref[...] = _silu(tile())

    @pl.when(jnp.logical_not(jnp.logical_or(is_gelu, is_silu)))
    def _():
        o_ref[...] = tile()


def _in_proj(x, g, w_t, wdt_t=None, *, nw, n_dt, tm, tn, gelu_cols, silu_cols, layer=None):
    m, k = x.shape
    emit_w = layer is not None
    assert m % tm == 0 and nw % tn == 0 and all(c % tn == 0 for c in gelu_cols + silu_cols)
    assert n_dt % SUBLANES == 0 and nw % n_dt == 0 and (not emit_w or m == tm)
    out_shape = [jax.ShapeDtypeStruct((m, nw), F32), jax.ShapeDtypeStruct((m, HEAD_PAD), F32)]
    out_specs = [pl.BlockSpec((tm, tn), lambda i, j: (i, j)), pl.BlockSpec((tm, HEAD_PAD), lambda i, j: (i, 0))]
    pipelined = [((tm, k), F32), ((tn, k), w_t.dtype), ((n_dt, k), w_t.dtype), ((tm, tn), F32), ((tm, HEAD_PAD), F32)]
    if emit_w:
        w_spec = pl.BlockSpec((None, tn, k), lambda i, j: (layer, j, 0))
        wdt_spec = pl.BlockSpec((None, n_dt, k), lambda i, j: (layer, nw // n_dt, 0))
        wdt_t = w_t
        out_shape += [jax.ShapeDtypeStruct((nw, k), BF16), jax.ShapeDtypeStruct((n_dt, k), BF16)]
        out_specs += [pl.BlockSpec((tn, k), lambda i, j: (j, 0)), pl.BlockSpec((n_dt, k), lambda i, j: (0, 0))]
        pipelined += [((tn, k), BF16), ((n_dt, k), BF16)]
    else:
        w_spec = pl.BlockSpec((tn, k), lambda i, j: (j, 0))
        wdt_spec = pl.BlockSpec((n_dt, k), lambda i, j: (0, 0))
    kern = functools.partial(_in_proj_kernel, gelu_tiles=tuple(c // tn for c in gelu_cols),
                             silu_tiles=tuple(c // tn for c in silu_cols), emit_w=emit_w)
    return pl.pallas_call(
        kern,
        out_shape=tuple(out_shape),
        grid=(m // tm, nw // tn),
        in_specs=[
            pl.BlockSpec((tm, k), lambda i, j: (i, 0)),
            pl.BlockSpec((1, k), lambda i, j: (0, 0)),
            w_spec,
            wdt_spec,
        ],
        out_specs=tuple(out_specs),
        scratch_shapes=[pltpu.VMEM((tm, k), BF16)],
        compiler_params=pltpu.CompilerParams(
            dimension_semantics=("parallel", "arbitrary"),
            vmem_limit_bytes=_vmem_limit(pipelined, [((tm, k), BF16), ((tn, k), BF16)])),
        name="in_proj",
    )(x, g, w_t, wdt_t)


def _out_proj_kernel(a1_ref, a2_ref, w1_ref, w2_ref, res_ref, o_ref, wb1_ref, wb2_ref):
    acc = jnp.dot(a1_ref[...], _weight_tile(w1_ref, wb1_ref), preferred_element_type=F32)
    acc = acc + jnp.dot(a2_ref[...], _weight_tile(w2_ref, wb2_ref), preferred_element_type=F32)
    o_ref[...] = res_ref[...] + acc


def _out_proj(a1, a2, w, layer, res, *, tn):
    m, k1 = a1.shape
    k2 = a2.shape[1]
    n = w.shape[2]
    assert n % tn == 0 and k1 == k2 and w.shape[1] == k1 + k2
    pipelined = [((m, k1), BF16), ((m, k2), BF16), ((k1, tn), w.dtype), ((k2, tn), w.dtype), ((m, tn), F32),
                 ((m, tn), F32), ((k1, tn), BF16), ((k2, tn), BF16)]
    col = lambda j: (0, j)
    return pl.pallas_call(
        _out_proj_kernel,
        out_shape=(jax.ShapeDtypeStruct((m, n), F32), jax.ShapeDtypeStruct((k1, n), BF16),
                   jax.ShapeDtypeStruct((k2, n), BF16)),
        grid=(n // tn,),
        in_specs=[
            pl.BlockSpec((m, k1), lambda j: (0, 0)),
            pl.BlockSpec((m, k2), lambda j: (0, 0)),
            pl.BlockSpec((None, k1, tn), lambda j: (layer, 0, j)),
            pl.BlockSpec((None, k2, tn), lambda j: (layer, 1, j)),
            pl.BlockSpec((m, tn), col),
        ],
        out_specs=(pl.BlockSpec((m, tn), col), pl.BlockSpec((k1, tn), col), pl.BlockSpec((k2, tn), col)),
        compiler_params=pltpu.CompilerParams(
            dimension_semantics=("parallel",),
            vmem_limit_bytes=_vmem_limit(pipelined, [])),
        name="out_proj",
    )(a1, a2, w, w, res)


def _mlp_up_kernel(x_ref, g_ref, w_ref, o_ref, *rest, emit_w):
    wb_ref, xn_ref = rest if emit_w else (None, rest[0])

    @pl.when(pl.program_id(1) == 0)
    def _():
        xn_ref[...] = _rmsnorm_rows(x_ref[...], g_ref[...]).astype(BF16)

    acc = jnp.dot(xn_ref[...], _weight_tile(w_ref, wb_ref), preferred_element_type=F32)
    o_ref[...] = jnp.square(jnp.maximum(acc, 0.0)).astype(o_ref.dtype)


def _mlp_up(x, g, w, *, tm, tn, layer=None):
    m, k = x.shape
    n = w.shape[-1]
    emit_w = layer is not None
    assert m % tm == 0 and n % tn == 0 and (not emit_w or m == tm)
    w_spec, wb_spec = _weight_specs(layer, (k, tn), lambda i, j: (0, j))
    out_shape = [jax.ShapeDtypeStruct((m, n), BF16)]
    out_specs = [pl.BlockSpec((tm, tn), lambda i, j: (i, j))]
    pipelined = [((tm, k), F32), ((k, tn), w.dtype), ((tm, tn), BF16)]
    if emit_w:
        out_shape.append(jax.ShapeDtypeStruct((k, n), BF16))
        out_specs.append(wb_spec)
        pipelined.append(((k, tn), BF16))
    return pl.pallas_call(
        functools.partial(_mlp_up_kernel, emit_w=emit_w),
        out_shape=tuple(out_shape),
        grid=(m // tm, n // tn),
        in_specs=[
            pl.BlockSpec((tm, k), lambda i, j: (i, 0)),
            pl.BlockSpec((1, k), lambda i, j: (0, 0)),
            w_spec,
        ],
        out_specs=tuple(out_specs),
        scratch_shapes=[pltpu.VMEM((tm, k), BF16)],
        compiler_params=pltpu.CompilerParams(
            dimension_semantics=("parallel", "arbitrary"),
            vmem_limit_bytes=_vmem_limit(pipelined, [((tm, k), BF16), ((k, tn), BF16)])),
        name="mlp_up",
    )(x, g, w)


def _mlp_down_kernel(h_ref, w_ref, res_ref, g_ref, o_ref, *rest, emit_w):
    wb_ref = rest[0] if emit_w else None
    kk = pl.program_id(1)

    @pl.when(kk == 0)
    def _():
        o_ref[...] = res_ref[...]

    o_ref[...] += jnp.dot(h_ref[...], _weight_tile(w_ref, wb_ref), preferred_element_type=F32)

    @pl.when(kk == pl.num_programs(1) - 1)
    def _():
        o_ref[...] = _rmsnorm_rows(o_ref[...], g_ref[...])


def _mlp_down_final(h, w, res, g, *, tm, tk, layer=None):
    m, k = h.shape
    n = w.shape[-1]
    emit_w = layer is not None
    assert m % tm == 0 and k % tk == 0 and (not emit_w or m == tm)
    w_spec, wb_spec = _weight_specs(layer, (tk, n), lambda i, j: (j, 0))
    out_shape = [jax.ShapeDtypeStruct((m, n), F32)]
    out_specs = [pl.BlockSpec((tm, n), lambda i, j: (i, 0))]
    pipelined = [((tm, tk), BF16), ((tk, n), w.dtype), ((tm, n), F32), ((tm, n), F32)]
    if emit_w:
        out_shape.append(jax.ShapeDtypeStruct((k, n), BF16))
        out_specs.append(wb_spec)
        pipelined.append(((tk, n), BF16))
    return pl.pallas_call(
        functools.partial(_mlp_down_kernel, emit_w=emit_w),
        out_shape=tuple(out_shape),
        grid=(m // tm, k // tk),
        in_specs=[
            pl.BlockSpec((tm, tk), lambda i, j: (i, j)),
            w_spec,
            pl.BlockSpec((tm, n), lambda i, j: (i, 0)),
            pl.BlockSpec((1, n), lambda i, j: (0, 0)),
        ],
        out_specs=tuple(out_specs),
        compiler_params=pltpu.CompilerParams(
            dimension_semantics=("parallel", "arbitrary"),
            vmem_limit_bytes=_vmem_limit(pipelined, [((tk, n), BF16)] if emit_w else [])),
        name="mlp_down",
    )(h, w, res, g)


def _lru_gates(xh, wa_half, wx_half, ba_half, bx_half, hsp):
    xb = xh.astype(BF16)
    tr = jnp.tanh(jnp.dot(xb, wa_half, preferred_element_type=F32) + ba_half)
    ti = jnp.tanh(jnp.dot(xb, wx_half, preferred_element_type=F32) + bx_half)
    nla = tr * hsp + hsp
    a = jnp.exp(-nla)
    q = jnp.tanh(nla) * (1.0 + a * a)
    mult = jnp.where(q > 0.0, q * lax.rsqrt(q), 0.0)
    return a, mult, 0.5 * ti + 0.5


def _scan_rows(a, b, h_prev):
    t, hd = a.shape
    g = t // SUBLANES
    a3 = a.reshape(g, SUBLANES, hd)
    b3 = b.reshape(g, SUBLANES, hd)
    sub = lax.broadcasted_iota(jnp.int32, (g, SUBLANES, hd), 1)
    d = 1
    while d < SUBLANES:
        keep = sub >= d
        a_sh = jnp.where(keep, pltpu.roll(a3, d, axis=1), 1.0)
        b_sh = jnp.where(keep, pltpu.roll(b3, d, axis=1), 0.0)
        b3 = a3 * b_sh + b3
        a3 = a3 * a_sh
        d *= 2
    tiles = []
    h = h_prev
    for k in range(g):
        hk = a3[k] * h + b3[k]
        tiles.append(hk)
        h = hk[SUBLANES - 1:SUBLANES, :]
    return jnp.concatenate(tiles, axis=0), h


def _lru_seq_kernel(gate_ref, x_ref, cw_ref, cb_ref, wa_ref, wx_ref, ba_ref, bx_ref, lam_ref, g_ref,
                    h0_ref, tail0_ref, *rest, pad, reset_first, n_chunks, n_live, fuse_out):
    if fuse_out:
        wo_ref, o_ref, hfin_ref, tailfin_ref, xe_ref, hc_ref, y_ref, pair_ref, yn_ref = rest
    else:
        o_ref, hfin_ref, tailfin_ref, xe_ref, hc_ref, y_ref, pair_ref = rest
    step = pl.program_id(0)
    live = step < n_live
    c = lax.rem(jnp.minimum(step, n_live - 1), n_chunks)
    t, dl = x_ref.shape
    nh, hd = wa_ref.shape[0], wa_ref.shape[1]
    keep = (lambda new, old: jnp.where(live, new, old)) if fuse_out else (lambda new, old: new)

    @pl.when(jnp.logical_and(c == 0, live))
    def _():
        xe_ref[0:HALO, :] = tail0_ref[...]
        hc_ref[...] = h0_ref[...]

    if fuse_out:
        @pl.when(step == 0)
        def _():
            yn_ref[...] = jnp.zeros_like(yn_ref)

    xe_ref[HALO:HALO + t, :] = x_ref[...]
    grow = c * t + lax.broadcasted_iota(jnp.int32, (t, hd), 0)
    if fuse_out:
        n_pieces = max(1, min(nh, o_ref.shape[1] // MXU_COLS))
        while nh % n_pieces or o_ref.shape[1] % n_pieces:
            n_pieces -= 1
        heads_per_piece, piece = nh // n_pieces, o_ref.shape[1] // n_pieces
    rb = min(t, LRU_ROW_BLOCK)
    ssq = [jnp.zeros((rb, hd), F32) for _ in range(t // rb)]
    for h in range(nh):
        sl = slice(h * hd, (h + 1) * hd)
        hsp = (0.5 * LRU_C) * jax.nn.softplus(-lam_ref[:, sl])
        h_prev = hc_ref[:, sl]
        h_run = h_prev
        for bi in range(t // rb):
            rows = slice(bi * rb, (bi + 1) * rb)
            xh = _conv_rows(xe_ref, pair_ref.at[(h * (t // rb) + bi) % 2], cw_ref, cb_ref, sl, rb, bi * rb)
            a, mult, i = _lru_gates(xh, wa_ref[h], wx_ref[h], ba_ref[:, sl], bx_ref[:, sl], hsp)
            if reset_first:
                mult = jnp.where(grow[rows] == pad, 1.0, mult)
            b = mult * i * xh
            if pad:
                a = jnp.where(grow[rows] >= pad, a, 1.0)
                b = jnp.where(grow[rows] >= pad, b, 0.0)
            hs, h_run = _scan_rows(a, b, h_run)
            y = hs * gate_ref[rows, sl]
            y_ref[rows, sl] = y
            ssq[bi] = ssq[bi] + y * y
        hc_ref[:, sl] = keep(h_run, h_prev)
        if fuse_out and (h + 1) % heads_per_piece == 0:
            k = (h + 1) // heads_per_piece - 1
            psl = slice(k * piece, (k + 1) * piece)
            o_ref[:, psl] = jnp.dot(yn_ref[...], wo_ref[:, psl], preferred_element_type=F32)
    scale = lax.rsqrt(jnp.sum(jnp.concatenate(ssq, axis=0), axis=-1, keepdims=True) / dl + EPS)
    yn = (y_ref[...] * scale * g_ref[...]).astype(BF16)
    if fuse_out:
        yn_ref[...] = yn
    else:
        o_ref[...] = yn
    xe_ref[0:HALO, :] = keep(xe_ref[t:t + HALO, :], xe_ref[0:HALO, :])
    hfin_ref[0] = hc_ref[...]
    tailfin_ref[0] = xe_ref[HALO - 3:HALO, :]


def _lru_seq(proj, params, h0, tail0, *, n_seq, seq_len, t, row0, pad=0, reset_first=False, wo=None):
    cw, cb, wa, wx, ba, bx, lam, g = params
    dl = cw.shape[1]
    nh, hd = wa.shape[0], wa.shape[1]
    assert seq_len % t == 0 and row0 % t == 0 and t % SUBLANES == 0
    n_chunks = seq_len // t
    n_live = n_seq * n_chunks
    blk0 = row0 // t
    fuse_out = wo is not None
    chunk = (lambda s: jnp.minimum(s, n_live - 1)) if fuse_out else (lambda s: s)
    out_chunk = (lambda s: jnp.maximum(s - 1, 0)) if fuse_out else (lambda s: s)
    const2 = lambda s: (0, 0)
    const3 = lambda s: (0, 0, 0)
    per_seq = lambda s: (chunk(s) // n_chunks, 0, 0)
    dout, out_dtype = (wo.shape[1], F32) if fuse_out else (dl, BF16)
    pipelined = [((t, dl), F32), ((t, dl), F32), ((t, dout), out_dtype)]
    pair_shape = (2, min(t, LRU_ROW_BLOCK) + SUBLANES, hd)
    resident = [((t + HALO, dl), F32), ((t, dl), F32), ((4 * nh, hd, hd), BF16), (pair_shape, F32)]
    scratch = [pltpu.VMEM((t + HALO, dl), F32), pltpu.VMEM((1, dl), F32), pltpu.VMEM((t, dl), F32),
               pltpu.VMEM(pair_shape, F32)]
    extra_specs, extra_args = [], []
    if fuse_out:
        extra_specs.append(pl.BlockSpec((dl, dout), const2, pipeline_mode=pl.Buffered(1)))
        extra_args.append(wo)
        resident += [((dl, dout), BF16), ((t, dl), BF16)]
        scratch.append(pltpu.VMEM((t, dl), BF16))
    kern = functools.partial(_lru_seq_kernel, pad=pad, reset_first=reset_first, n_chunks=n_chunks, n_live=n_live,
                             fuse_out=fuse_out)
    return pl.pallas_call(
        kern,
        out_shape=(jax.ShapeDtypeStruct((n_seq * seq_len, dout), out_dtype),
                   jax.ShapeDtypeStruct((n_seq, 1, dl), F32),
                   jax.ShapeDtypeStruct((n_seq, CONV_TAPS - 1, dl), F32)),
        grid=(n_live + 1 if fuse_out else n_live,),
        in_specs=[
            pl.BlockSpec((t, dl), lambda s: (blk0 + chunk(s), 0)),
            pl.BlockSpec((t, dl), lambda s: (blk0 + chunk(s), 1)),
            pl.BlockSpec((CONV_TAPS, dl), const2),
            pl.BlockSpec((1, dl), const2),
            pl.BlockSpec((nh, hd, hd), const3),
            pl.BlockSpec((nh, hd, hd), const3),
            pl.BlockSpec((1, dl), const2),
            pl.BlockSpec((1, dl), const2),
            pl.BlockSpec((1, dl), const2),
            pl.BlockSpec((1, dl), const2),
            pl.BlockSpec((1, dl), const2),
            pl.BlockSpec((HALO, dl), const2),
        ] + extra_specs,
        out_specs=(
            pl.BlockSpec((t, dout), lambda s: (out_chunk(s), 0)),
            pl.BlockSpec((1, 1, dl), per_seq),
            pl.BlockSpec((1, CONV_TAPS - 1, dl), per_seq),
        ),
        scratch_shapes=scratch,
        compiler_params=pltpu.CompilerParams(
            dimension_semantics=("arbitrary",),
            vmem_limit_bytes=_vmem_limit(pipelined, resident)),
        name="lru_seq",
    )(proj, proj, cw, cb, wa, wx, ba, bx, lam, g, h0, tail0, *extra_args)


def _lru_slab_kernel(gate_ref, x_ref, cw_ref, cb_ref, wa_ref, wx_ref, ba_ref, bx_ref, lam_ref, g_ref,
                     h0_ref, tail_ref, o_ref, hfin_ref, tailfin_ref, y_ref):
    ls, bs, dl = x_ref.shape
    nh, hd = wa_ref.shape[0], wa_ref.shape[1]
    ntail = CONV_TAPS - 1
    for h in range(nh):
        sl = slice(h * hd, (h + 1) * hd)
        ext = [tail_ref[k, :, sl] for k in range(ntail)]
        ext += [x_ref[s, :, sl] for s in range(ls)]
        hsp = (0.5 * LRU_C) * jax.nn.softplus(-lam_ref[:, sl])
        hcur = h0_ref[:, sl]
        for s in range(ls):
            xh = _conv_taps(ext, s, cw_ref, cb_ref, sl)
            a, mult, i = _lru_gates(xh, wa_ref[h], wx_ref[h], ba_ref[:, sl], bx_ref[:, sl], hsp)
            hcur = a * hcur + mult * i * xh
            y_ref[s, :, sl] = hcur * gate_ref[s, :, sl]
        hfin_ref[:, sl] = hcur
        for k in range(ntail):
            tailfin_ref[k, :, sl] = ext[ls + k]
    for s in range(ls):
        y = y_ref[s]
        scale = lax.rsqrt(jnp.mean(y * y, axis=-1, keepdims=True) + EPS)
        o_ref[s] = (y * scale * g_ref[...]).astype(o_ref.dtype)


def _lru_slab(proj3, params, h0, tail, *, ls):
    cw, cb, wa, wx, ba, bx, lam, g = params
    bs = proj3.shape[1]
    dl = cw.shape[1]
    nh, hd = wa.shape[0], wa.shape[1]
    ntail = CONV_TAPS - 1
    c2 = lambda i: (0, 0)
    c3 = lambda i: (0, 0, 0)
    pipelined = [((ls, bs, dl), F32)] * 2 + [((ls, bs, dl), BF16)] + [((bs, (2 * ntail + 2) * dl), F32)]
    return pl.pallas_call(
        _lru_slab_kernel,
        out_shape=(jax.ShapeDtypeStruct((ls, bs, dl), BF16),
                   jax.ShapeDtypeStruct((bs, dl), F32),
                   jax.ShapeDtypeStruct((ntail, bs, dl), F32)),
        grid=(1,),
        in_specs=[
            pl.BlockSpec((ls, bs, dl), lambda i: (0, 0, 0)),
            pl.BlockSpec((ls, bs, dl), lambda i: (0, 0, 1)),
            pl.BlockSpec((CONV_TAPS, dl), c2), pl.BlockSpec((1, dl), c2),
            pl.BlockSpec((nh, hd, hd), c3), pl.BlockSpec((nh, hd, hd), c3),
            pl.BlockSpec((1, dl), c2), pl.BlockSpec((1, dl), c2), pl.BlockSpec((1, dl), c2), pl.BlockSpec((1, dl), c2),
            pl.BlockSpec((bs, dl), c2), pl.BlockSpec((ntail, bs, dl), c3),
        ],
        out_specs=(pl.BlockSpec((ls, bs, dl), c3), pl.BlockSpec((bs, dl), c2), pl.BlockSpec((ntail, bs, dl), c3)),
        scratch_shapes=[pltpu.VMEM((ls, bs, dl), F32)],
        compiler_params=pltpu.CompilerParams(
            dimension_semantics=("arbitrary",),
            vmem_limit_bytes=_vmem_limit(pipelined, [((ls, bs, dl), F32)])),
        name="lru_slab",
    )(proj3, proj3, cw, cb, wa, wx, ba, bx, lam, g, h0, tail)


def _head_lane_mask(n_heads):
    return lax.broadcasted_iota(jnp.int32, (1, HEAD_PAD), 1) < n_heads


def _gated_group_norm(y, z_act, g):
    yg = y * z_act
    scale = lax.rsqrt(jnp.mean(yg * yg, axis=-1, keepdims=True) + EPS)
    return yg * scale * g


def _ssd_seq_kernel(z_ref, xbc_ref, dt_ref, cw_ref, cb_ref, dtb_ref, alog_ref, dskip_ref, g_ref, e3_ref,
                    s0_ref, tail0_ref, *rest, pad, q, n_heads, n_groups, n_chunks, n_live, fuse_out):
    if fuse_out:
        wo_ref, part_ref, res_ref = rest[:3]
        o_ref, sfin_ref, tailfin_ref, xe_ref, xc_ref, ex_ref, y_ref, s_ref, pair_ref, hv_ref, yn_ref, ynp_ref = rest[3:]
    else:
        o_ref, sfin_ref, tailfin_ref, xe_ref, xc_ref, ex_ref, y_ref, s_ref, pair_ref, hv_ref = rest
        yn_ref = o_ref
    step = pl.program_id(0)
    live = step < n_live
    c = lax.rem(jnp.minimum(step, n_live - 1), n_chunks)
    keep = (lambda new, old: jnp.where(live, new, old)) if fuse_out else (lambda new, old: new)
    tq, ds = z_ref.shape
    cps = tq // q
    dc = xbc_ref.shape[1]
    gn = (dc - ds) // 2
    n = gn // n_groups
    p = ds // n_heads
    r = n_heads // n_groups
    gw = ds // n_groups
    hpb = LANES // p

    @pl.when(jnp.logical_and(c == 0, live))
    def _():
        xe_ref[0:HALO, :] = tail0_ref[...]
        s_ref[...] = s0_ref[...]

    if fuse_out:
        @pl.when(step == 0)
        def _():
            ynp_ref[...] = jnp.zeros_like(ynp_ref)

    xe_ref[HALO:HALO + tq, :] = xbc_ref[...]
    valid_all = (c * tq + lax.broadcasted_iota(jnp.int32, (tq, 1), 0)) >= pad

    cblk = CONV_LANE_BLOCK if dc % CONV_LANE_BLOCK == 0 else LANES
    for j in range(dc // cblk):
        sl = slice(j * cblk, (j + 1) * cblk)
        v = _silu(_conv_rows(xe_ref, pair_ref.at[j % 2], cw_ref, cb_ref, sl, tq))
        if pad and (j + 1) * cblk <= ds:
            v = jnp.where(valid_all, v, 0.0)
        xc_ref[:, sl] = v

    ri = lax.broadcasted_iota(jnp.int32, (q, q), 0)
    ci = lax.broadcasted_iota(jnp.int32, (q, q), 1)
    causal = ci <= ri
    tri = jnp.where(causal, 1.0, 0.0).astype(BF16)
    lane = lax.broadcasted_iota(jnp.int32, (q, LANES), 1)
    a_neg = -jnp.exp(alog_ref[...])
    n_pieces = cps * n_groups
    piece = o_ref.shape[1] // n_pieces if fuse_out else 0

    for u in range(cps):
        rows = slice(u * q, (u + 1) * q)
        dtv = jnp.where(_head_lane_mask(n_heads), jax.nn.softplus(dt_ref[rows, :] + dtb_ref[...]), 0.0)
        if pad:
            dtv = jnp.where(valid_all[u * q:(u + 1) * q], dtv, 0.0)
        ac3 = jnp.dot(tri, _split3(dtv * a_neg), preferred_element_type=F32)
        acum = ac3[:, 0:HEAD_PAD] + ac3[:, HEAD_PAD:2 * HEAD_PAD] + ac3[:, 2 * HEAD_PAD:3 * HEAD_PAD]
        alast = acum[q - 1:q, :]
        eac = jnp.exp(acum)
        wend = jnp.exp(alast - acum) * dtv
        cdec = jnp.broadcast_to(jnp.exp(alast), (SUBLANES, HEAD_PAD))
        ex_ref[...] = _expand_heads(jnp.concatenate([eac, wend, cdec], axis=0), e3_ref)
        hv_ref[0] = acum
        hv_ref[1] = acum.T
        hv_ref[2] = dtv.T

        for g in range(n_groups):
            gsl = slice(g * gw, (g + 1) * gw)
            bg = xc_ref[rows, ds + g * n:ds + (g + 1) * n].astype(BF16)
            cg = xc_ref[rows, ds + gn + g * n:ds + gn + (g + 1) * n].astype(BF16)
            cbm = lax.dot_general(cg, bg, _NT, preferred_element_type=F32)
            yoff = jnp.dot(cg, s_ref[:, gsl].astype(BF16), preferred_element_type=F32)
            for k in range(gw // LANES):
                lsl = slice(g * gw + k * LANES, g * gw + (k + 1) * LANES)
                xs = xc_ref[rows, lsl]
                ms = []
                xparts = []
                for w in range(hpb):
                    h = g * r + k * hpb + w
                    seg = hv_ref[0, :, h:h + 1] - hv_ref[1, h:h + 1, :]
                    lm = jnp.where(causal, jnp.exp(seg), 0.0) * hv_ref[2, h:h + 1, :]
                    ms.append((cbm * lm).astype(BF16))
                    inhead = (lane >= w * p) & (lane < (w + 1) * p)
                    xparts.append(jnp.where(inhead, xs, 0.0).astype(BF16))
                ydiag = jnp.dot(jnp.concatenate(ms, axis=1), jnp.concatenate(xparts, axis=0),
                                preferred_element_type=F32)
                y = ydiag + yoff[:, k * LANES:(k + 1) * LANES] * ex_ref[0:q, lsl]
                y_ref[rows, lsl] = y + dskip_ref[:, lsl] * xs
            yn_ref[rows, gsl] = _gated_group_norm(y_ref[rows, gsl], z_ref[rows, gsl], g_ref[:, gsl]).astype(BF16)
            xw = (xc_ref[rows, gsl] * ex_ref[q:2 * q, gsl]).astype(BF16)
            upd = lax.dot_general(bg, xw, (((0,), (0,)), ((), ())), preferred_element_type=F32)
            s_old = s_ref[:, gsl]
            s_ref[:, gsl] = keep(ex_ref[2 * q:2 * q + 1, gsl] * s_old + upd, s_old)
            if fuse_out:
                psl = slice((u * n_groups + g) * piece, (u * n_groups + g + 1) * piece)
                mix = part_ref[:, psl] + jnp.dot(ynp_ref[...], wo_ref[:, psl], preferred_element_type=F32)
                o_ref[:, psl] = res_ref[:, psl] + mix

    if fuse_out:
        ynp_ref[...] = yn_ref[...]
    xe_ref[0:HALO, :] = keep(xe_ref[tq:tq + HALO, :], xe_ref[0:HALO, :])
    sfin_ref[0] = s_ref[...]
    tailfin_ref[0] = xe_ref[HALO - 3:HALO, :]


def _ssd_seq(proj, dt, params, s0, tail0, *, n_seq, seq_len, row0, n_heads, n_groups, dl, pad=0, fuse=None):
    cw, cb, dtb, alog, dskip, g, e3 = params
    dc = cw.shape[1]
    ds = dskip.shape[1]
    n = s0.shape[0]
    q = SSD_CHUNK
    tq = q * (SSD_CHUNKS_PER_STEP if seq_len % (q * SSD_CHUNKS_PER_STEP) == 0 else 1)
    assert seq_len % tq == 0 and row0 % tq == 0
    assert (2 * dl) % ds == 0 and (2 * dl + ds) % dc == 0 and LANES % (ds // n_heads) == 0
    n_chunks = seq_len // tq
    n_live = n_seq * n_chunks
    blk0 = row0 // tq
    fuse_out = fuse is not None
    chunk = (lambda s: jnp.minimum(s, n_live - 1)) if fuse_out else (lambda s: s)
    rows = lambda s: blk0 + chunk(s)
    const2 = lambda s: (0, 0)
    out_rows = (lambda s: (jnp.maximum(s - 1, 0), 0)) if fuse_out else (lambda s: (s, 0))
    per_seq = lambda s: (chunk(s) // n_chunks, 0, 0)
    dout, out_dtype = (fuse[0].shape[1], F32) if fuse_out else (ds, BF16)
    pipelined = [((tq, ds), F32), ((tq, dc), F32), ((tq, HEAD_PAD), F32), ((tq, dout), out_dtype)]
    resident = [((tq + HALO, dc), F32), ((tq, dc), F32), ((2 * q + SUBLANES, ds), F32), ((tq, ds), F32),
                ((3 * n, ds), F32), ((6 * HEAD_PAD, ds), BF16), ((2, tq + SUBLANES, CONV_LANE_BLOCK), F32)]
    cblk = CONV_LANE_BLOCK if dc % CONV_LANE_BLOCK == 0 else LANES
    scratch = [pltpu.VMEM((tq + HALO, dc), F32), pltpu.VMEM((tq, dc), F32),
               pltpu.VMEM((2 * q + SUBLANES, ds), F32), pltpu.VMEM((tq, ds), F32), pltpu.VMEM((n, ds), F32),
               pltpu.VMEM((2, tq + SUBLANES, cblk), F32), pltpu.VMEM((3, q, HEAD_PAD), F32)]
    extra_specs, extra_args = [], []
    if fuse_out:
        extra_specs = [pl.BlockSpec((ds, dout), const2, pipeline_mode=pl.Buffered(1)),
                       pl.BlockSpec((tq, dout), out_rows), pl.BlockSpec((tq, dout), out_rows)]
        extra_args = list(fuse)
        pipelined += [((tq, dout), F32)] * 2
        resident += [((ds, dout), BF16), ((2 * tq, ds), BF16)]
        scratch += [pltpu.VMEM((tq, ds), BF16), pltpu.VMEM((tq, ds), BF16)]
    kern = functools.partial(_ssd_seq_kernel, pad=pad, q=q, n_heads=n_heads, n_groups=n_groups, n_chunks=n_chunks,
                             n_live=n_live, fuse_out=fuse_out)
    return pl.pallas_call(
        kern,
        out_shape=(jax.ShapeDtypeStruct((n_seq * seq_len, dout), out_dtype),
                   jax.ShapeDtypeStruct((n_seq, n, ds), F32),
                   jax.ShapeDtypeStruct((n_seq, CONV_TAPS - 1, dc), F32)),
        grid=(n_live + 1 if fuse_out else n_live,),
        in_specs=[
            pl.BlockSpec((tq, ds), lambda s: (rows(s), (2 * dl) // ds)),
            pl.BlockSpec((tq, dc), lambda s: (rows(s), (2 * dl + ds) // dc)),
            pl.BlockSpec((tq, HEAD_PAD), lambda s: (rows(s), 0)),
            pl.BlockSpec((CONV_TAPS, dc), const2),
            pl.BlockSpec((1, dc), const2),
            pl.BlockSpec((1, HEAD_PAD), const2),
            pl.BlockSpec((1, HEAD_PAD), const2),
            pl.BlockSpec((1, ds), const2),
            pl.BlockSpec((1, ds), const2),
            pl.BlockSpec((3 * HEAD_PAD, ds), const2),
            pl.BlockSpec((n, ds), const2),
            pl.BlockSpec((HALO, dc), const2),
        ] + extra_specs,
        out_specs=(
            pl.BlockSpec((tq, dout), out_rows),
            pl.BlockSpec((1, n, ds), per_seq),
            pl.BlockSpec((1, CONV_TAPS - 1, dc), per_seq),
        ),
        scratch_shapes=scratch,
        compiler_params=pltpu.CompilerParams(
            dimension_semantics=("arbitrary",),
            vmem_limit_bytes=_vmem_limit(pipelined, resident)),
        name="ssd_seq",
    )(proj, proj, dt, cw, cb, dtb, alog, dskip, g, e3, s0, tail0, *extra_args)


def _ssd_slab_pre_kernel(xbc_ref, dt_ref, tail_ref, cw_ref, cb_ref, dtb_ref, alog_ref, dskip_ref, e3_ref,
                         ypart_ref, eace_ref, c_ref, b_ref, xw_ref, cdec_ref, tailfin_ref, xc_ref,
                         *, n_heads, n_groups):
    ls, bs, dc = xbc_ref.shape
    ds = dskip_ref.shape[1]
    gn = (dc - ds) // 2
    n = gn // n_groups
    r = n_heads // n_groups
    ntail = CONV_TAPS - 1

    cblk = CONV_LANE_BLOCK if dc % CONV_LANE_BLOCK == 0 else LANES
    for j in range(dc // cblk):
        sl = slice(j * cblk, (j + 1) * cblk)
        ext = [tail_ref[k, :, sl] for k in range(ntail)]
        ext += [xbc_ref[s, :, sl] for s in range(ls)]
        for s in range(ls):
            xc_ref[s, :, sl] = _silu(_conv_taps(ext, s, cw_ref, cb_ref, sl))
        for k in range(ntail):
            tailfin_ref[k, :, sl] = ext[ls + k]

    hmask = _head_lane_mask(n_heads)
    a_neg = -jnp.exp(alog_ref[...])
    dtv, acum = [], []
    run = jnp.zeros((bs, HEAD_PAD), F32)
    for s in range(ls):
        d = jnp.where(hmask, jax.nn.softplus(dt_ref[s] + dtb_ref[...]), 0.0)
        run = run + d * a_neg
        dtv.append(d)
        acum.append(run)
    alast = acum[ls - 1]
    cdec_ref[...] = jnp.exp(alast)
    head_group = lax.broadcasted_iota(jnp.int32, (1, HEAD_PAD), 1) // r

    for s in range(ls):
        eace_ref[s] = _expand_heads(jnp.exp(acum[s]), e3_ref)
        wend_e = _expand_heads(jnp.exp(alast - acum[s]) * dtv[s], e3_ref)
        xw_ref[:, s * ds:(s + 1) * ds] = xc_ref[s, :, 0:ds] * wend_e
        b_ref[:, s * gn:(s + 1) * gn] = xc_ref[s, :, ds:ds + gn]
        c_ref[:, s * gn:(s + 1) * gn] = xc_ref[s, :, ds + gn:ds + 2 * gn]
        ypart = dskip_ref[...] * xc_ref[s, :, 0:ds]
        for j in range(s + 1):
            cbh = jnp.zeros((bs, HEAD_PAD), F32)
            for g in range(n_groups):
                cs = xc_ref[s, :, ds + gn + g * n:ds + gn + (g + 1) * n]
                bj = xc_ref[j, :, ds + g * n:ds + (g + 1) * n]
                cbg = jnp.sum(cs * bj, axis=-1, keepdims=True)
                cbh = cbh + jnp.where(head_group == g, cbg, 0.0)
            coef = cbh * (jnp.exp(acum[s] - acum[j]) * dtv[j])
            ypart = ypart + _expand_heads(coef, e3_ref) * xc_ref[j, :, 0:ds]
        ypart_ref[s] = ypart
    for s in range(ls, SLAB_ROWS):
        xw_ref[:, s * ds:(s + 1) * ds] = jnp.zeros((bs, ds), F32)
        b_ref[:, s * gn:(s + 1) * gn] = jnp.zeros((bs, gn), F32)
        c_ref[:, s * gn:(s + 1) * gn] = jnp.zeros((bs, gn), F32)


def _ssd_slab_pre(proj3, dt3, tail, params, *, ls, n_heads, n_groups, dl):
    cw, cb, dtb, alog, dskip, _, e3 = params
    bs = proj3.shape[1]
    dc = cw.shape[1]
    ds = dskip.shape[1]
    gn = (dc - ds) // 2
    ntail = CONV_TAPS - 1
    assert ls <= SLAB_ROWS and (2 * dl + ds) % dc == 0
    c2 = lambda i: (0, 0)
    c3 = lambda i: (0, 0, 0)
    pipelined = [((ls, bs, dc), F32), ((ls, bs, HEAD_PAD), F32), ((bs, 2 * ntail * dc), F32),
                 ((2 * ls, bs, ds), F32), ((bs, SLAB_ROWS * (2 * gn + ds)), F32), ((3 * HEAD_PAD, ds), BF16)]
    kern = functools.partial(_ssd_slab_pre_kernel, n_heads=n_heads, n_groups=n_groups)
    return pl.pallas_call(
        kern,
        out_shape=(jax.ShapeDtypeStruct((ls, bs, ds), F32),
                   jax.ShapeDtypeStruct((ls, bs, ds), F32),
                   jax.ShapeDtypeStruct((bs, SLAB_ROWS * gn), F32),
                   jax.ShapeDtypeStruct((bs, SLAB_ROWS * gn), F32),
                   jax.ShapeDtypeStruct((bs, SLAB_ROWS * ds), F32),
                   jax.ShapeDtypeStruct((bs, HEAD_PAD), F32),
                   jax.ShapeDtypeStruct((ntail, bs, dc), F32)),
        grid=(1,),
        in_specs=[
            pl.BlockSpec((ls, bs, dc), lambda i: (0, 0, (2 * dl + ds) // dc)),
            pl.BlockSpec((ls, bs, HEAD_PAD), c3),
            pl.BlockSpec((ntail, bs, dc), c3),
            pl.BlockSpec((CONV_TAPS, dc), c2), pl.BlockSpec((1, dc), c2),
            pl.BlockSpec((1, HEAD_PAD), c2), pl.BlockSpec((1, HEAD_PAD), c2),
            pl.BlockSpec((1, ds), c2), pl.BlockSpec((3 * HEAD_PAD, ds), c2),
        ],
        out_specs=(pl.BlockSpec((ls, bs, ds), c3), pl.BlockSpec((ls, bs, ds), c3),
                   pl.BlockSpec((bs, SLAB_ROWS * gn), c2), pl.BlockSpec((bs, SLAB_ROWS * gn), c2),
                   pl.BlockSpec((bs, SLAB_ROWS * ds), c2), pl.BlockSpec((bs, HEAD_PAD), c2),
                   pl.BlockSpec((ntail, bs, dc), c3)),
        scratch_shapes=[pltpu.VMEM((ls, bs, dc), F32)],
        compiler_params=pltpu.CompilerParams(
            dimension_semantics=("arbitrary",),
            vmem_limit_bytes=_vmem_limit(pipelined, [((ls, bs, dc), F32)])),
        name="ssd_slab_pre",
    )(proj3, dt3, tail, cw, cb, dtb, alog, dskip, e3)


def _ssd_state_kernel(cdec_ref, s_ref, c_ref, b_ref, xw_ref, snew_ref, yoff_ref, *, n_heads, n_groups):
    i = pl.program_id(0)
    sb, hp, n = s_ref.shape
    p = hp // n_heads
    r = n_heads // n_groups
    gw = hp // n_groups
    for q in range(sb):
        for g in range(n_groups):
            gsl = slice(g * gw, (g + 1) * gw)
            sg = s_ref[q, gsl, :]
            cg = c_ref[q, :, g * n:(g + 1) * n].astype(BF16)
            yoff_ref[q, :, gsl] = lax.dot_general(cg, sg.astype(BF16), (((1,), (1,)), ((), ())),
                                                  preferred_element_type=F32)
            upd = lax.dot_general(xw_ref[q, :, gsl].astype(BF16), b_ref[q, :, g * n:(g + 1) * n].astype(BF16),
                                  (((0,), (0,)), ((), ())), preferred_element_type=F32)
            for u in range(r):
                h = g * r + u
                rows = slice(h * p, (h + 1) * p)
                snew_ref[q, rows, :] = cdec_ref[i * sb + q, h] * s_ref[q, rows, :] + upd[u * p:(u + 1) * p, :]


def _ssd_state(cdec, state, c_rows, b_rows, xw_rows, *, n_heads, n_groups):
    bs, hp, n = state.shape
    gn = c_rows.shape[2]
    sb = STATE_SEQS_PER_STEP if bs % STATE_SEQS_PER_STEP == 0 else 1
    per_seq = lambda i: (i, 0, 0)
    pipelined = [((sb, hp, n), F32)] * 2 + [((sb, SLAB_ROWS, gn), F32)] * 2 + [((sb, SLAB_ROWS, hp), F32)] * 2
    kern = functools.partial(_ssd_state_kernel, n_heads=n_heads, n_groups=n_groups)
    return pl.pallas_call(
        kern,
        out_shape=(jax.ShapeDtypeStruct((bs, hp, n), F32), jax.ShapeDtypeStruct((bs, SLAB_ROWS, hp), F32)),
        grid=(bs // sb,),
        in_specs=[
            pl.BlockSpec(memory_space=pltpu.SMEM),
            pl.BlockSpec((sb, hp, n), per_seq),
            pl.BlockSpec((sb, SLAB_ROWS, gn), per_seq),
            pl.BlockSpec((sb, SLAB_ROWS, gn), per_seq),
            pl.BlockSpec((sb, SLAB_ROWS, hp), per_seq),
        ],
        out_specs=(pl.BlockSpec((sb, hp, n), per_seq), pl.BlockSpec((sb, SLAB_ROWS, hp), per_seq)),
        compiler_params=pltpu.CompilerParams(
            dimension_semantics=("parallel",),
            vmem_limit_bytes=_vmem_limit(pipelined, [])),
        name="ssd_state",
    )(cdec, state, c_rows, b_rows, xw_rows)


def _ssd_slab_post_kernel(ypart_ref, eace_ref, yoff_ref, z_ref, g_ref, o_ref, *, n_groups):
    ls, bs, ds = ypart_ref.shape
    gw = ds // n_groups
    for s in range(ls):
        for g in range(n_groups):
            gsl = slice(g * gw, (g + 1) * gw)
            y = ypart_ref[s, :, gsl] + eace_ref[s, :, gsl] * yoff_ref[:, s * ds + g * gw:s * ds + (g + 1) * gw]
            o_ref[s, :, gsl] = _gated_group_norm(y, z_ref[s, :, gsl], g_ref[:, gsl]).astype(o_ref.dtype)


def _ssd_slab_post(ypart, eace, yoff, proj3, g, *, n_groups, dl):
    ls, bs, ds = ypart.shape
    assert (2 * dl) % ds == 0
    c2 = lambda i: (0, 0)
    c3 = lambda i: (0, 0, 0)
    pipelined = [((ls, bs, ds), F32)] * 3 + [((bs, SLAB_ROWS * ds), F32), ((ls, bs, ds), BF16)]
    kern = functools.partial(_ssd_slab_post_kernel, n_groups=n_groups)
    return pl.pallas_call(
        kern,
        out_shape=jax.ShapeDtypeStruct((ls, bs, ds), BF16),
        grid=(1,),
        in_specs=[
            pl.BlockSpec((ls, bs, ds), c3), pl.BlockSpec((ls, bs, ds), c3),
            pl.BlockSpec((bs, SLAB_ROWS * ds), c2),
            pl.BlockSpec((ls, bs, ds), lambda i: (0, 0, (2 * dl) // ds)),
            pl.BlockSpec((1, ds), c2),
        ],
        out_specs=pl.BlockSpec((ls, bs, ds), c3),
        compiler_params=pltpu.CompilerParams(
            dimension_semantics=("arbitrary",),
            vmem_limit_bytes=_vmem_limit(pipelined, [])),
        name="ssd_slab_post",
    )(ypart, eace, yoff, proj3, g)


def _head_expansion(n_heads, head_dim):
    rows = lax.broadcasted_iota(jnp.int32, (HEAD_PAD, n_heads * head_dim), 0)
    cols = lax.broadcasted_iota(jnp.int32, (HEAD_PAD, n_heads * head_dim), 1)
    e = (cols // head_dim == rows).astype(BF16)
    return jnp.concatenate([e, e, e], axis=0)


def _pad_lanes(v, width):
    return jnp.pad(v, ((0, 0), (0, width - v.shape[1])))


def _tail_block(tail):
    return jnp.pad(tail, ((HALO - tail.shape[0], 0), (0, 0)))


def _mlp(x1, g_mlp, w_up, w_down, g_final, *, layer=None):
    m = x1.shape[0]
    tiles = MATMUL_TILES
    tm = m if layer is not None else _pick_tile(m, tiles)
    up_tiles = tiles if layer is not None else (2048,) + tiles
    up = _mlp_up(x1, g_mlp, w_up, tm=tm, tn=_pick_tile(w_up.shape[-1], up_tiles), layer=layer)
    hid = up[0]
    down = _mlp_down_final(hid, w_down, x1, g_final, tm=tm, tk=_pick_tile(w_down.shape[-2], tiles), layer=layer)
    if layer is None:
        return down[0]
    return down[0], up[1], down[1]


def kernel(x_prompt, x_sample, state_lru_h, state_lru_conv, state_ssd, state_ssd_conv, meta_tokens, g_mix, w_in, conv_lru_w, conv_lru_b, lru_wa, lru_ba, lru_wx, lru_bx, lru_lambda, g_lru_out, conv_ssd_w, conv_ssd_b, dt_bias, a_log, d_skip, g_ssd_out, w_out, g_mlp, w_up, w_down, g_final):
    depth = w_in.shape[0]
    assert depth == 1, "single-layer step"
    l = 0
    bp, lp, d = x_prompt.shape
    bs, ls, _ = x_sample.shape
    n_meta = meta_tokens.shape[0]
    dl = state_lru_h.shape[-1]
    n_heads, p, n = state_ssd.shape[-3:]
    ds = n_heads * p
    dc = state_ssd_conv.shape[-1]
    gn = (dc - ds) // 2
    n_groups = gn // n
    nw = 2 * dl + ds + dc
    ntail = CONV_TAPS - 1
    q = SSD_CHUNK
    meta_pad = (-n_meta) % q
    assert n_heads <= HEAD_PAD and (bs * ls) % q == 0 and lp % q == 0 and q % bs == 0

    row = lambda v: v.reshape(1, -1).astype(F32)
    w_in_t = jnp.swapaxes(w_in, 1, 2)
    lru_params = (conv_lru_w[l], row(conv_lru_b[l]), (0.5 * lru_wa[l]).astype(BF16), (0.5 * lru_wx[l]).astype(BF16),
                  row(0.5 * lru_ba[l]), row(0.5 * lru_bx[l]), row(lru_lambda[l]), row(g_lru_out[l]))
    ssd_params = (conv_ssd_w[l], row(conv_ssd_b[l]), _pad_lanes(row(dt_bias[l]), HEAD_PAD),
                  _pad_lanes(row(a_log[l]), HEAD_PAD), row(jnp.repeat(d_skip[l], p)), row(g_ssd_out[l]),
                  _head_expansion(n_heads, p))
    g_mix_r, g_mlp_r, g_final_r = row(g_mix[l]), row(g_mlp[l]), row(g_final)

    xs_tm = x_sample.transpose(1, 0, 2).reshape(ls * bs, d)
    x_side = jnp.concatenate([xs_tm, jnp.zeros((meta_pad, d), F32), meta_tokens.astype(F32)], axis=0)
    xp_rows = x_prompt.reshape(bp * lp, d)

    tiles = MATMUL_TILES
    tn_in = next(t for t in tiles if nw % t == 0 and dl % t == 0 and ds % t == 0)
    act_cols = dict(gelu_cols=(0, dl), silu_cols=(2 * dl, 2 * dl + ds))
    proj_side, dt_side, w_in_b, w_dt_b = _in_proj(x_side, g_mix_r, w_in_t, nw=nw, n_dt=n_heads, tm=x_side.shape[0],
                                                  tn=tn_in, layer=l, **act_cols)

    proj_s3 = proj_side.reshape(-1, bs, nw)
    dt_s3 = dt_side.reshape(-1, bs, HEAD_PAD)
    lru_s, s_h, s_ltail = _lru_slab(proj_s3, lru_params, state_lru_h[l],
                                    state_lru_conv[l].transpose(1, 0, 2), ls=ls)
    ypart, eace, c_rows, b_rows, xw_rows, cdec, s_stail = _ssd_slab_pre(
        proj_s3, dt_s3, state_ssd_conv[l].transpose(1, 0, 2), ssd_params,
        ls=ls, n_heads=n_heads, n_groups=n_groups, dl=dl)
    s_new, yoff = _ssd_state(cdec, state_ssd[l].reshape(bs, ds, n),
                             c_rows.reshape(bs, SLAB_ROWS, gn), b_rows.reshape(bs, SLAB_ROWS, gn),
                             xw_rows.reshape(bs, SLAB_ROWS, ds), n_heads=n_heads, n_groups=n_groups)
    ssd_s = _ssd_slab_post(ypart, eace, yoff.reshape(bs, SLAB_ROWS * ds), proj_s3, ssd_params[5],
                           n_groups=n_groups, dl=dl)
    x1_s, w_out_lru, w_out_ssd = _out_proj(lru_s.reshape(ls * bs, dl), ssd_s.reshape(ls * bs, ds), w_out, l, xs_tm,
                                           tn=_pick_tile(d, (512, 256, 128)))
    y_s, w_up_b, w_down_b = _mlp(x1_s, g_mlp_r, w_up, w_down, g_final_r, layer=l)

    meta_row0 = ls * bs
    _, m_h, m_ltail = _lru_seq(proj_side, lru_params, jnp.zeros((1, dl), F32), jnp.zeros((HALO, dl), F32),
                               n_seq=1, seq_len=q, t=q, row0=meta_row0, pad=meta_pad, reset_first=True)
    _, m_s, m_stail = _ssd_seq(proj_side, dt_side, ssd_params, jnp.zeros((n, ds), F32), jnp.zeros((HALO, dc), F32),
                               n_seq=1, seq_len=q, row0=meta_row0, n_heads=n_heads, n_groups=n_groups, dl=dl,
                               pad=meta_pad)

    proj_p, dt_p = _in_proj(xp_rows, g_mix_r, w_in_b, w_dt_b, nw=nw, n_dt=n_heads, tm=_pick_tile(bp * lp, tiles),
                            tn=tn_in, **act_cols)
    part_p, p_h, p_ltail = _lru_seq(proj_p, lru_params, m_h[0], _tail_block(m_ltail[0]),
                                    n_seq=bp, seq_len=lp, t=_pick_tile(lp, (256, 128)), row0=0, wo=w_out_lru)
    x1_p, p_s, p_stail = _ssd_seq(proj_p, dt_p, ssd_params, m_s[0], _tail_block(m_stail[0]),
                                  n_seq=bp, seq_len=lp, row0=0, n_heads=n_heads, n_groups=n_groups, dl=dl,
                                  fuse=(w_out_ssd, part_p, xp_rows))
    y_p = _mlp(x1_p, g_mlp_r, w_up_b, w_down_b, g_final_r)

    y_prompt = y_p.reshape(bp, lp, d)
    y_sample = y_s.reshape(ls, bs, d).transpose(1, 0, 2)
    p_lru_h = p_h.reshape(1, bp, dl)
    p_lru_conv = p_ltail.reshape(1, bp, ntail, dl)
    p_ssd = p_s.transpose(0, 2, 1).reshape(1, bp, n_heads, p, n)
    p_ssd_conv = p_stail.reshape(1, bp, ntail, dc)
    s_lru_h = s_h.reshape(1, bs, dl)
    s_lru_conv = s_ltail.transpose(1, 0, 2)[None]
    s_ssd = s_new.reshape(1, bs, n_heads, p, n)
    s_ssd_conv = s_stail.transpose(1, 0, 2)[None]
    return (y_prompt, y_sample, p_lru_h, p_lru_conv, p_ssd, p_ssd_conv, s_lru_h, s_lru_conv, s_ssd, s_ssd_conv)
```

```python
import functools

import jax
import jax.numpy as jnp
from jax import lax
from jax.experimental import pallas as pl
from jax.experimental.pallas import tpu as pltpu

F32 = jnp.float32
BF16 = jnp.bfloat16

EPS = 1e-6
LRU_C = 8.0
CONV_TAPS = 4

LANES = 128
SUBLANES = 8
MXU_COLS = 256
VMEM_BYTES_V7X = 64 * 1024 * 1024
VMEM_TEMP_BYTES = 10 * 1024 * 1024
VMEM_CEILING_BYTES = VMEM_BYTES_V7X - 6 * 1024 * 1024

HALO = 2 * SUBLANES
SSD_CHUNK = 128
SSD_CHUNKS_PER_STEP = 2
HEAD_PAD = LANES
SLAB_ROWS = SUBLANES
STATE_SEQS_PER_STEP = 8
CONV_LANE_BLOCK = 128
LRU_ROW_BLOCK = 128
MATMUL_TILES = (1024, 512, 256, 128)


def _nbytes(shape, dtype):
    n = 1
    for s in shape:
        n *= s
    return n * jnp.dtype(dtype).itemsize


def _vmem_limit(pipelined, resident):
    est = 2 * sum(_nbytes(s, d) for s, d in pipelined) + sum(_nbytes(s, d) for s, d in resident)
    return int(min(est + VMEM_TEMP_BYTES, VMEM_CEILING_BYTES))


def _pick_tile(m, prefs):
    for t in prefs:
        if m % t == 0:
            return t
    return m


def _silu(x):
    h = 0.5 * x
    return h * jnp.tanh(h) + h


def _split3(x):
    hi = x.astype(BF16)
    r1 = x - hi.astype(F32)
    mid = r1.astype(BF16)
    lo = (r1 - mid.astype(F32)).astype(BF16)
    return jnp.concatenate([hi, mid, lo], axis=1)


def _expand_heads(x, e3_ref):
    return jnp.dot(_split3(x), e3_ref[...], preferred_element_type=F32)


def _rmsnorm_rows(x, g):
    ms = jnp.mean(x * x, axis=-1, keepdims=True)
    return x * lax.rsqrt(ms + EPS) * g


def _conv_rows(xe_ref, pair_ref, cw_ref, cb_ref, sl, t, r0=0):
    lo = HALO - SUBLANES + r0
    x0 = xe_ref[lo:lo + SUBLANES + t, sl]
    x1 = xe_ref[lo - 1:lo - 1 + SUBLANES + t, sl]
    pair_ref[...] = x0 * cw_ref[1:2, sl] + x1 * cw_ref[0:1, sl]
    near = x0[SUBLANES:] * cw_ref[3:4, sl] + x1[SUBLANES:] * cw_ref[2:3, sl]
    return cb_ref[:, sl] + near + pair_ref[SUBLANES - 2:SUBLANES - 2 + t, :]


def _conv_taps(ext, s, cw_ref, cb_ref, sl):
    v = cb_ref[:, sl] + ext[s] * cw_ref[0:1, sl]
    for k in range(1, CONV_TAPS):
        v = v + ext[s + k] * cw_ref[k:k + 1, sl]
    return v


def _weight_tile(w_ref, wb_ref):
    if wb_ref is None:
        return w_ref[...]
    w = w_ref[...].astype(BF16)
    wb_ref[...] = w
    return w


def _weight_specs(layer, blk, idx):
    if layer is None:
        return pl.BlockSpec(blk, idx), None
    return pl.BlockSpec((None,) + blk, lambda i, j: (layer,) + idx(i, j)), pl.BlockSpec(blk, idx)


_NT = (((1,), (1,)), ((), ()))


def _in_proj_kernel(x_ref, g_ref, w_ref, wdt_ref, o_ref, dt_ref, *rest, gelu_tiles, silu_tiles, emit_w):
    if emit_w:
        wb_ref, wdtb_ref, xn_ref = rest
    else:
        wb_ref, wdtb_ref, xn_ref = None, None, rest[0]
    j = pl.program_id(1)

    @pl.when(j == 0)
    def _():
        xn = _rmsnorm_rows(x_ref[...], g_ref[...]).astype(BF16)
        xn_ref[...] = xn
        wdt = _weight_tile(wdt_ref, wdtb_ref)
        wdt = jnp.concatenate([wdt, jnp.zeros((HEAD_PAD - wdt.shape[0], wdt.shape[1]), BF16)], axis=0)
        dt_ref[...] = lax.dot_general(xn, wdt, _NT, preferred_element_type=F32)

    def tile():
        return lax.dot_general(xn_ref[...], _weight_tile(w_ref, wb_ref), _NT, preferred_element_type=F32)

    in_range = lambda r: jnp.logical_and(j >= r[0], j < r[1])
    is_gelu, is_silu = in_range(gelu_tiles), in_range(silu_tiles)

    @pl.when(is_gelu)
    def _():
        o_ref[...] = jax.nn.gelu(tile())

    @pl.when(is_silu)
    def _():
        o_ref[...] = _silu(tile())

    @pl.when(jnp.logical_not(jnp.logical_or(is_gelu, is_silu)))
    def _():
        o_ref[...] = tile()


def _in_proj(x, g, w_t, wdt_t=None, *, nw, n_dt, tm, tn, gelu_cols, silu_cols, layer=None):
    m, k = x.shape
    emit_w = layer is not None
    assert m % tm == 0 and nw % tn == 0 and all(c % tn == 0 for c in gelu_cols + silu_cols)
    assert n_dt % SUBLANES == 0 and nw % n_dt == 0 and (not emit_w or m == tm)
    out_shape = [jax.ShapeDtypeStruct((m, nw), F32), jax.ShapeDtypeStruct((m, HEAD_PAD), F32)]
    out_specs = [pl.BlockSpec((tm, tn), lambda i, j: (i, j)), pl.BlockSpec((tm, HEAD_PAD), lambda i, j: (i, 0))]
    pipelined = [((tm, k), F32), ((tn, k), w_t.dtype), ((n_dt, k), w_t.dtype), ((tm, tn), F32), ((tm, HEAD_PAD), F32)]
    if emit_w:
        w_spec = pl.BlockSpec((None, tn, k), lambda i, j: (layer, j, 0))
        wdt_spec = pl.BlockSpec((None, n_dt, k), lambda i, j: (layer, nw // n_dt, 0))
        wdt_t = w_t
        out_shape += [jax.ShapeDtypeStruct((nw, k), BF16), jax.ShapeDtypeStruct((n_dt, k), BF16)]
        out_specs += [pl.BlockSpec((tn, k), lambda i, j: (j, 0)), pl.BlockSpec((n_dt, k), lambda i, j: (0, 0))]
        pipelined += [((tn, k), BF16), ((n_dt, k), BF16)]
    else:
        w_spec = pl.BlockSpec((tn, k), lambda i, j: (j, 0))
        wdt_spec = pl.BlockSpec((n_dt, k), lambda i, j: (0, 0))
    kern = functools.partial(_in_proj_kernel, gelu_tiles=tuple(c // tn for c in gelu_cols),
                             silu_tiles=tuple(c // tn for c in silu_cols), emit_w=emit_w)
    return pl.pallas_call(
        kern,
        out_shape=tuple(out_shape),
        grid=(m // tm, nw // tn),
        in_specs=[
            pl.BlockSpec((tm, k), lambda i, j: (i, 0)),
            pl.BlockSpec((1, k), lambda i, j: (0, 0)),
            w_spec,
            wdt_spec,
        ],
        out_specs=tuple(out_specs),
        scratch_shapes=[pltpu.VMEM((tm, k), BF16)],
        compiler_params=pltpu.CompilerParams(
            dimension_semantics=("parallel", "arbitrary"),
            vmem_limit_bytes=_vmem_limit(pipelined, [((tm, k), BF16), ((tn, k), BF16)])),
        name="in_proj",
    )(x, g, w_t, wdt_t)


def _out_proj_kernel(a1_ref, a2_ref, w1_ref, w2_ref, res_ref, o_ref, wb1_ref, wb2_ref):
    acc = jnp.dot(a1_ref[...], _weight_tile(w1_ref, wb1_ref), preferred_element_type=F32)
    acc = acc + jnp.dot(a2_ref[...], _weight_tile(w2_ref, wb2_ref), preferred_element_type=F32)
    o_ref[...] = res_ref[...] + acc


def _out_proj(a1, a2, w, layer, res, *, tn):
    m, k1 = a1.shape
    k2 = a2.shape[1]
    n = w.shape[2]
    assert n % tn == 0 and k1 == k2 and w.shape[1] == k1 + k2
    pipelined = [((m, k1), BF16), ((m, k2), BF16), ((k1, tn), w.dtype), ((k2, tn), w.dtype), ((m, tn), F32),
                 ((m, tn), F32), ((k1, tn), BF16), ((k2, tn), BF16)]
    col = lambda j: (0, j)
    return pl.pallas_call(
        _out_proj_kernel,
        out_shape=(jax.ShapeDtypeStruct((m, n), F32), jax.ShapeDtypeStruct((k1, n), BF16),
                   jax.ShapeDtypeStruct((k2, n), BF16)),
        grid=(n // tn,),
        in_specs=[
            pl.BlockSpec((m, k1), lambda j: (0, 0)),
            pl.BlockSpec((m, k2), lambda j: (0, 0)),
            pl.BlockSpec((None, k1, tn), lambda j: (layer, 0, j)),
            pl.BlockSpec((None, k2, tn), lambda j: (layer, 1, j)),
            pl.BlockSpec((m, tn), col),
        ],
        out_specs=(pl.BlockSpec((m, tn), col), pl.BlockSpec((k1, tn), col), pl.BlockSpec((k2, tn), col)),
        compiler_params=pltpu.CompilerParams(
            dimension_semantics=("parallel",),
            vmem_limit_bytes=_vmem_limit(pipelined, [])),
        name="out_proj",
    )(a1, a2, w, w, res)


def _mlp_up_kernel(x_ref, g_ref, w_ref, o_ref, *rest, emit_w):
    wb_ref, xn_ref = rest if emit_w else (None, rest[0])

    @pl.when(pl.program_id(1) == 0)
    def _():
        xn_ref[...] = _rmsnorm_rows(x_ref[...], g_ref[...]).astype(BF16)

    acc = jnp.dot(xn_ref[...], _weight_tile(w_ref, wb_ref), preferred_element_type=F32)
    o_ref[...] = jnp.square(jnp.maximum(acc, 0.0)).astype(o_ref.dtype)


def _mlp_up(x, g, w, *, tm, tn, layer=None):
    m, k = x.shape
    n = w.shape[-1]
    emit_w = layer is not None
    assert m % tm == 0 and n % tn == 0 and (not emit_w or m == tm)
    w_spec, wb_spec = _weight_specs(layer, (k, tn), lambda i, j: (0, j))
    out_shape = [jax.ShapeDtypeStruct((m, n), BF16)]
    out_specs = [pl.BlockSpec((tm, tn), lambda i, j: (i, j))]
    pipelined = [((tm, k), F32), ((k, tn), w.dtype), ((tm, tn), BF16)]
    if emit_w:
        out_shape.append(jax.ShapeDtypeStruct((k, n), BF16))
        out_specs.append(wb_spec)
        pipelined.append(((k, tn), BF16))
    return pl.pallas_call(
        functools.partial(_mlp_up_kernel, emit_w=emit_w),
        out_shape=tuple(out_shape),
        grid=(m // tm, n // tn),
        in_specs=[
            pl.BlockSpec((tm, k), lambda i, j: (i, 0)),
            pl.BlockSpec((1, k), lambda i, j: (0, 0)),
            w_spec,
        ],
        out_specs=tuple(out_specs),
        scratch_shapes=[pltpu.VMEM((tm, k), BF16)],
        compiler_params=pltpu.CompilerParams(
            dimension_semantics=("parallel", "arbitrary"),
            vmem_limit_bytes=_vmem_limit(pipelined, [((tm, k), BF16), ((k, tn), BF16)])),
        name="mlp_up",
    )(x, g, w)


def _mlp_down_kernel(h_ref, w_ref, res_ref, g_ref, o_ref, *rest, emit_w):
    wb_ref = rest[0] if emit_w else None
    kk = pl.program_id(1)

    @pl.when(kk == 0)
    def _():
        o_ref[...] = res_ref[...]

    o_ref[...] += jnp.dot(h_ref[...], _weight_tile(w_ref, wb_ref), preferred_element_type=F32)

    @pl.when(kk == pl.num_programs(1) - 1)
    def _():
        o_ref[...] = _rmsnorm_rows(o_ref[...], g_ref[...])


def _mlp_down_final(h, w, res, g, *, tm, tk, layer=None):
    m, k = h.shape
    n = w.shape[-1]
    emit_w = layer is not None
    assert m % tm == 0 and k % tk == 0 and (not emit_w or m == tm)
    w_spec, wb_spec = _weight_specs(layer, (tk, n), lambda i, j: (j, 0))
    out_shape = [jax.ShapeDtypeStruct((m, n), F32)]
    out_specs = [pl.BlockSpec((tm, n), lambda i, j: (i, 0))]
    pipelined = [((tm, tk), BF16), ((tk, n), w.dtype), ((tm, n), F32), ((tm, n), F32)]
    if emit_w:
        out_shape.append(jax.ShapeDtypeStruct((k, n), BF16))
        out_specs.append(wb_spec)
        pipelined.append(((tk, n), BF16))
    return pl.pallas_call(
        functools.partial(_mlp_down_kernel, emit_w=emit_w),
        out_shape=tuple(out_shape),
        grid=(m // tm, k // tk),
        in_specs=[
            pl.BlockSpec((tm, tk), lambda i, j: (i, j)),
            w_spec,
            pl.BlockSpec((tm, n), lambda i, j: (i, 0)),
            pl.BlockSpec((1, n), lambda i, j: (0, 0)),
        ],
        out_specs=tuple(out_specs),
        compiler_params=pltpu.CompilerParams(
            dimension_semantics=("parallel", "arbitrary"),
            vmem_limit_bytes=_vmem_limit(pipelined, [((tk, n), BF16)] if emit_w else [])),
        name="mlp_down",
    )(h, w, res, g)


def _lru_gates(xh, wa_half, wx_half, ba_half, bx_half, hsp):
    xb = xh.astype(BF16)
    tr = jnp.tanh(jnp.dot(xb, wa_half, preferred_element_type=F32) + ba_half)
    ti = jnp.tanh(jnp.dot(xb, wx_half, preferred_element_type=F32) + bx_half)
    nla = tr * hsp + hsp
    a = jnp.exp(-nla)
    q = jnp.tanh(nla) * (1.0 + a * a)
    mult = jnp.where(q > 0.0, q * lax.rsqrt(q), 0.0)
    return a, mult, 0.5 * ti + 0.5


def _scan_rows(a, b, h_prev):
    t, hd = a.shape
    g = t // SUBLANES
    a3 = a.reshape(g, SUBLANES, hd)
    b3 = b.reshape(g, SUBLANES, hd)
    sub = lax.broadcasted_iota(jnp.int32, (g, SUBLANES, hd), 1)
    d = 1
    while d < SUBLANES:
        keep = sub >= d
        a_sh = jnp.where(keep, pltpu.roll(a3, d, axis=1), 1.0)
        b_sh = jnp.where(keep, pltpu.roll(b3, d, axis=1), 0.0)
        b3 = a3 * b_sh + b3
        a3 = a3 * a_sh
        d *= 2
    tiles = []
    h = h_prev
    for k in range(g):
        hk = a3[k] * h + b3[k]
        tiles.append(hk)
        h = hk[SUBLANES - 1:SUBLANES, :]
    return jnp.concatenate(tiles, axis=0), h


def _lru_seq_kernel(gate_ref, x_ref, cw_ref, cb_ref, wa_ref, wx_ref, ba_ref, bx_ref, lam_ref, g_ref,
                    h0_ref, tail0_ref, *rest, pad, reset_first, n_chunks, n_live, fuse_out):
    if fuse_out:
        wo_ref, o_ref, hfin_ref, tailfin_ref, xe_ref, hc_ref, y_ref, pair_ref, yn_ref = rest
    else:
        o_ref, hfin_ref, tailfin_ref, xe_ref, hc_ref, y_ref, pair_ref = rest
    step = pl.program_id(0)
    live = step < n_live
    c = lax.rem(jnp.minimum(step, n_live - 1), n_chunks)
    t, dl = x_ref.shape
    nh, hd = wa_ref.shape[0], wa_ref.shape[1]
    keep = (lambda new, old: jnp.where(live, new, old)) if fuse_out else (lambda new, old: new)

    @pl.when(jnp.logical_and(c == 0, live))
    def _():
        xe_ref[0:HALO, :] = tail0_ref[...]
        hc_ref[...] = h0_ref[...]

    if fuse_out:
        @pl.when(step == 0)
        def _():
            yn_ref[...] = jnp.zeros_like(yn_ref)

    xe_ref[HALO:HALO + t, :] = x_ref[...]
    grow = c * t + lax.broadcasted_iota(jnp.int32, (t, hd), 0)
    if fuse_out:
        n_pieces = max(1, min(nh, o_ref.shape[1] // MXU_COLS))
        while nh % n_pieces or o_ref.shape[1] % n_pieces:
            n_pieces -= 1
        heads_per_piece, piece = nh // n_pieces, o_ref.shape[1] // n_pieces
    rb = min(t, LRU_ROW_BLOCK)
    ssq = [jnp.zeros((rb, hd), F32) for _ in range(t // rb)]
    for h in range(nh):
        sl = slice(h * hd, (h + 1) * hd)
        hsp = (0.5 * LRU_C) * jax.nn.softplus(-lam_ref[:, sl])
        h_prev = hc_ref[:, sl]
        h_run = h_prev
        for bi in range(t // rb):
            rows = slice(bi * rb, (bi + 1) * rb)
            xh = _conv_rows(xe_ref, pair_ref.at[(h * (t // rb) + bi) % 2], cw_ref, cb_ref, sl, rb, bi * rb)
            a, mult, i = _lru_gates(xh, wa_ref[h], wx_ref[h], ba_ref[:, sl], bx_ref[:, sl], hsp)
            if reset_first:
                mult = jnp.where(grow[rows] == pad, 1.0, mult)
            b = mult * i * xh
            if pad:
                a = jnp.where(grow[rows] >= pad, a, 1.0)
                b = jnp.where(grow[rows] >= pad, b, 0.0)
            hs, h_run = _scan_rows(a, b, h_run)
            y = hs * gate_ref[rows, sl]
            y_ref[rows, sl] = y
            ssq[bi] = ssq[bi] + y * y
        hc_ref[:, sl] = keep(h_run, h_prev)
        if fuse_out and (h + 1) % heads_per_piece == 0:
            k = (h + 1) // heads_per_piece - 1
            psl = slice(k * piece, (k + 1) * piece)
            o_ref[:, psl] = jnp.dot(yn_ref[...], wo_ref[:, psl], preferred_element_type=F32)
    scale = lax.rsqrt(jnp.sum(jnp.concatenate(ssq, axis=0), axis=-1, keepdims=True) / dl + EPS)
    yn = (y_ref[...] * scale * g_ref[...]).astype(BF16)
    if fuse_out:
        yn_ref[...] = yn
    else:
        o_ref[...] = yn
    xe_ref[0:HALO, :] = keep(xe_ref[t:t + HALO, :], xe_ref[0:HALO, :])
    hfin_ref[0] = hc_ref[...]
    tailfin_ref[0] = xe_ref[HALO - 3:HALO, :]


def _lru_seq(proj, params, h0, tail0, *, n_seq, seq_len, t, row0, pad=0, reset_first=False, wo=None):
    cw, cb, wa, wx, ba, bx, lam, g = params
    dl = cw.shape[1]
    nh, hd = wa.shape[0], wa.shape[1]
    assert seq_len % t == 0 and row0 % t == 0 and t % SUBLANES == 0
    n_chunks = seq_len // t
    n_live = n_seq * n_chunks
    blk0 = row0 // t
    fuse_out = wo is not None
    chunk = (lambda s: jnp.minimum(s, n_live - 1)) if fuse_out else (lambda s: s)
    out_chunk = (lambda s: jnp.maximum(s - 1, 0)) if fuse_out else (lambda s: s)
    const2 = lambda s: (0, 0)
    const3 = lambda s: (0, 0, 0)
    per_seq = lambda s: (chunk(s) // n_chunks, 0, 0)
    dout, out_dtype = (wo.shape[1], F32) if fuse_out else (dl, BF16)
    pipelined = [((t, dl), F32), ((t, dl), F32), ((t, dout), out_dtype)]
    pair_shape = (2, min(t, LRU_ROW_BLOCK) + SUBLANES, hd)
    resident = [((t + HALO, dl), F32), ((t, dl), F32), ((4 * nh, hd, hd), BF16), (pair_shape, F32)]
    scratch = [pltpu.VMEM((t + HALO, dl), F32), pltpu.VMEM((1, dl), F32), pltpu.VMEM((t, dl), F32),
               pltpu.VMEM(pair_shape, F32)]
    extra_specs, extra_args = [], []
    if fuse_out:
        extra_specs.append(pl.BlockSpec((dl, dout), const2, pipeline_mode=pl.Buffered(1)))
        extra_args.append(wo)
        resident += [((dl, dout), BF16), ((t, dl), BF16)]
        scratch.append(pltpu.VMEM((t, dl), BF16))
    kern = functools.partial(_lru_seq_kernel, pad=pad, reset_first=reset_first, n_chunks=n_chunks, n_live=n_live,
                             fuse_out=fuse_out)
    return pl.pallas_call(
        kern,
        out_shape=(jax.ShapeDtypeStruct((n_seq * seq_len, dout), out_dtype),
                   jax.ShapeDtypeStruct((n_seq, 1, dl), F32),
                   jax.ShapeDtypeStruct((n_seq, CONV_TAPS - 1, dl), F32)),
        grid=(n_live + 1 if fuse_out else n_live,),
        in_specs=[
            pl.BlockSpec((t, dl), lambda s: (blk0 + chunk(s), 0)),
            pl.BlockSpec((t, dl), lambda s: (blk0 + chunk(s), 1)),
            pl.BlockSpec((CONV_TAPS, dl), const2),
            pl.BlockSpec((1, dl), const2),
            pl.BlockSpec((nh, hd, hd), const3),
            pl.BlockSpec((nh, hd, hd), const3),
            pl.BlockSpec((1, dl), const2),
            pl.BlockSpec((1, dl), const2),
            pl.BlockSpec((1, dl), const2),
            pl.BlockSpec((1, dl), const2),
            pl.BlockSpec((1, dl), const2),
            pl.BlockSpec((HALO, dl), const2),
        ] + extra_specs,
        out_specs=(
            pl.BlockSpec((t, dout), lambda s: (out_chunk(s), 0)),
            pl.BlockSpec((1, 1, dl), per_seq),
            pl.BlockSpec((1, CONV_TAPS - 1, dl), per_seq),
        ),
        scratch_shapes=scratch,
        compiler_params=pltpu.CompilerParams(
            dimension_semantics=("arbitrary",),
            vmem_limit_bytes=_vmem_limit(pipelined, resident)),
        name="lru_seq",
    )(proj, proj, cw, cb, wa, wx, ba, bx, lam, g, h0, tail0, *extra_args)


def _lru_slab_kernel(gate_ref, x_ref, cw_ref, cb_ref, wa_ref, wx_ref, ba_ref, bx_ref, lam_ref, g_ref,
                     h0_ref, tail_ref, o_ref, hfin_ref, tailfin_ref, y_ref):
    ls, bs, dl = x_ref.shape
    nh, hd = wa_ref.shape[0], wa_ref.shape[1]
    ntail = CONV_TAPS - 1
    for h in range(nh):
        sl = slice(h * hd, (h + 1) * hd)
        ext = [tail_ref[k, :, sl] for k in range(ntail)]
        ext += [x_ref[s, :, sl] for s in range(ls)]
        hsp = (0.5 * LRU_C) * jax.nn.softplus(-lam_ref[:, sl])
        hcur = h0_ref[:, sl]
        for s in range(ls):
            xh = _conv_taps(ext, s, cw_ref, cb_ref, sl)
            a, mult, i = _lru_gates(xh, wa_ref[h], wx_ref[h], ba_ref[:, sl], bx_ref[:, sl], hsp)
            hcur = a * hcur + mult * i * xh
            y_ref[s, :, sl] = hcur * gate_ref[s, :, sl]
        hfin_ref[:, sl] = hcur
        for k in range(ntail):
            tailfin_ref[k, :, sl] = ext[ls + k]
    for s in range(ls):
        y = y_ref[s]
        scale = lax.rsqrt(jnp.mean(y * y, axis=-1, keepdims=True) + EPS)
        o_ref[s] = (y * scale * g_ref[...]).astype(o_ref.dtype)


def _lru_slab(proj3, params, h0, tail, *, ls):
    cw, cb, wa, wx, ba, bx, lam, g = params
    bs = proj3.shape[1]
    dl = cw.shape[1]
    nh, hd = wa.shape[0], wa.shape[1]
    ntail = CONV_TAPS - 1
    c2 = lambda i: (0, 0)
    c3 = lambda i: (0, 0, 0)
    pipelined = [((ls, bs, dl), F32)] * 2 + [((ls, bs, dl), BF16)] + [((bs, (2 * ntail + 2) * dl), F32)]
    return pl.pallas_call(
        _lru_slab_kernel,
        out_shape=(jax.ShapeDtypeStruct((ls, bs, dl), BF16),
                   jax.ShapeDtypeStruct((bs, dl), F32),
                   jax.ShapeDtypeStruct((ntail, bs, dl), F32)),
        grid=(1,),
        in_specs=[
            pl.BlockSpec((ls, bs, dl), lambda i: (0, 0, 0)),
            pl.BlockSpec((ls, bs, dl), lambda i: (0, 0, 1)),
            pl.BlockSpec((CONV_TAPS, dl), c2), pl.BlockSpec((1, dl), c2),
            pl.BlockSpec((nh, hd, hd), c3), pl.BlockSpec((nh, hd, hd), c3),
            pl.BlockSpec((1, dl), c2), pl.BlockSpec((1, dl), c2), pl.BlockSpec((1, dl), c2), pl.BlockSpec((1, dl), c2),
            pl.BlockSpec((bs, dl), c2), pl.BlockSpec((ntail, bs, dl), c3),
        ],
        out_specs=(pl.BlockSpec((ls, bs, dl), c3), pl.BlockSpec((bs, dl), c2), pl.BlockSpec((ntail, bs, dl), c3)),
        scratch_shapes=[pltpu.VMEM((ls, bs, dl), F32)],
        compiler_params=pltpu.CompilerParams(
            dimension_semantics=("arbitrary",),
            vmem_limit_bytes=_vmem_limit(pipelined, [((ls, bs, dl), F32)])),
        name="lru_slab",
    )(proj3, proj3, cw, cb, wa, wx, ba, bx, lam, g, h0, tail)


def _head_lane_mask(n_heads):
    return lax.broadcasted_iota(jnp.int32, (1, HEAD_PAD), 1) < n_heads


def _gated_group_norm(y, z_act, g):
    yg = y * z_act
    scale = lax.rsqrt(jnp.mean(yg * yg, axis=-1, keepdims=True) + EPS)
    return yg * scale * g


def _ssd_seq_kernel(z_ref, xbc_ref, dt_ref, cw_ref, cb_ref, dtb_ref, alog_ref, dskip_ref, g_ref, e3_ref,
                    s0_ref, tail0_ref, *rest, pad, q, n_heads, n_groups, n_chunks, n_live, fuse_out):
    if fuse_out:
        wo_ref, part_ref, res_ref = rest[:3]
        o_ref, sfin_ref, tailfin_ref, xe_ref, xc_ref, ex_ref, y_ref, s_ref, pair_ref, hv_ref, yn_ref, ynp_ref = rest[3:]
    else:
        o_ref, sfin_ref, tailfin_ref, xe_ref, xc_ref, ex_ref, y_ref, s_ref, pair_ref, hv_ref = rest
        yn_ref = o_ref
    step = pl.program_id(0)
    live = step < n_live
    c = lax.rem(jnp.minimum(step, n_live - 1), n_chunks)
    keep = (lambda new, old: jnp.where(live, new, old)) if fuse_out else (lambda new, old: new)
    tq, ds = z_ref.shape
    cps = tq // q
    dc = xbc_ref.shape[1]
    gn = (dc - ds) // 2
    n = gn // n_groups
    p = ds // n_heads
    r = n_heads // n_groups
    gw = ds // n_groups
    hpb = LANES // p

    @pl.when(jnp.logical_and(c == 0, live))
    def _():
        xe_ref[0:HALO, :] = tail0_ref[...]
        s_ref[...] = s0_ref[...]

    if fuse_out:
        @pl.when(step == 0)
        def _():
            ynp_ref[...] = jnp.zeros_like(ynp_ref)

    xe_ref[HALO:HALO + tq, :] = xbc_ref[...]
    valid_all = (c * tq + lax.broadcasted_iota(jnp.int32, (tq, 1), 0)) >= pad

    cblk = CONV_LANE_BLOCK if dc % CONV_LANE_BLOCK == 0 else LANES
    for j in range(dc // cblk):
        sl = slice(j * cblk, (j + 1) * cblk)
        for u in range(cps):
            rows = slice(u * q, (u + 1) * q)
            v = _silu(_conv_rows(xe_ref, pair_ref.at[(j * cps + u) % 2], cw_ref, cb_ref, sl, q, u * q))
            if pad and (j + 1) * cblk <= ds:
                v = jnp.where(valid_all[rows], v, 0.0)
            xc_ref[rows, sl] = v

    ri = lax.broadcasted_iota(jnp.int32, (q, q), 0)
    ci = lax.broadcasted_iota(jnp.int32, (q, q), 1)
    causal = ci <= ri
    tri = jnp.where(causal, 1.0, 0.0).astype(BF16)
    lane = lax.broadcasted_iota(jnp.int32, (q, LANES), 1)
    a_neg = -jnp.exp(alog_ref[...])
    n_pieces = cps * n_groups
    piece = o_ref.shape[1] // n_pieces if fuse_out else 0

    for u in range(cps):
        rows = slice(u * q, (u + 1) * q)
        dtv = jnp.where(_head_lane_mask(n_heads), jax.nn.softplus(dt_ref[rows, :] + dtb_ref[...]), 0.0)
        if pad:
            dtv = jnp.where(valid_all[u * q:(u + 1) * q], dtv, 0.0)
        ac3 = jnp.dot(tri, _split3(dtv * a_neg), preferred_element_type=F32)
        acum = ac3[:, 0:HEAD_PAD] + ac3[:, HEAD_PAD:2 * HEAD_PAD] + ac3[:, 2 * HEAD_PAD:3 * HEAD_PAD]
        alast = acum[q - 1:q, :]
        eac = jnp.exp(acum)
        wend = jnp.exp(alast - acum) * dtv
        cdec = jnp.broadcast_to(jnp.exp(alast), (SUBLANES, HEAD_PAD))
        ex_ref[...] = _expand_heads(jnp.concatenate([eac, wend, cdec], axis=0), e3_ref)
        hv_ref[0] = acum
        hv_ref[1] = acum.T
        hv_ref[2] = dtv.T

        for g in range(n_groups):
            gsl = slice(g * gw, (g + 1) * gw)
            bg = xc_ref[rows, ds + g * n:ds + (g + 1) * n].astype(BF16)
            cg = xc_ref[rows, ds + gn + g * n:ds + gn + (g + 1) * n].astype(BF16)
            cbm = lax.dot_general(cg, bg, _NT, preferred_element_type=F32)
            yoff = jnp.dot(cg, s_ref[:, gsl].astype(BF16), preferred_element_type=F32)
            for k in range(gw // LANES):
                lsl = slice(g * gw + k * LANES, g * gw + (k + 1) * LANES)
                xs = xc_ref[rows, lsl]
                ms = []
                xparts = []
                for w in range(hpb):
                    h = g * r + k * hpb + w
                    seg = hv_ref[0, :, h:h + 1] - hv_ref[1, h:h + 1, :]
                    lm = jnp.where(causal, jnp.exp(seg), 0.0) * hv_ref[2, h:h + 1, :]
                    ms.append((cbm * lm).astype(BF16))
                    inhead = (lane >= w * p) & (lane < (w + 1) * p)
                    xparts.append(jnp.where(inhead, xs, 0.0).astype(BF16))
                ydiag = jnp.dot(jnp.concatenate(ms, axis=1), jnp.concatenate(xparts, axis=0),
                                preferred_element_type=F32)
                y = ydiag + yoff[:, k * LANES:(k + 1) * LANES] * ex_ref[0:q, lsl]
                yg = (y + dskip_ref[:, lsl] * xs) * z_ref[rows, lsl]
                y_ref[rows, lsl] = yg
                ssq = yg * yg if k == 0 else ssq + yg * yg
            scale = lax.rsqrt(jnp.sum(ssq, axis=-1, keepdims=True) / gw + EPS)
            for k in range(gw // LANES):
                lsl = slice(g * gw + k * LANES, g * gw + (k + 1) * LANES)
                yn_ref[rows, lsl] = (y_ref[rows, lsl] * scale * g_ref[:, lsl]).astype(BF16)
            xw = (xc_ref[rows, gsl] * ex_ref[q:2 * q, gsl]).astype(BF16)
            upd = lax.dot_general(bg, xw, (((0,), (0,)), ((), ())), preferred_element_type=F32)
            s_old = s_ref[:, gsl]
            s_ref[:, gsl] = keep(ex_ref[2 * q:2 * q + 1, gsl] * s_old + upd, s_old)
            if fuse_out:
                psl = slice((u * n_groups + g) * piece, (u * n_groups + g + 1) * piece)
                mix = part_ref[:, psl] + jnp.dot(ynp_ref[...], wo_ref[:, psl], preferred_element_type=F32)
                o_ref[:, psl] = res_ref[:, psl] + mix

    if fuse_out:
        ynp_ref[...] = yn_ref[...]
    xe_ref[0:HALO, :] = keep(xe_ref[tq:tq + HALO, :], xe_ref[0:HALO, :])
    sfin_ref[0] = s_ref[...]
    tailfin_ref[0] = xe_ref[HALO - 3:HALO, :]


def _ssd_seq(proj, dt, params, s0, tail0, *, n_seq, seq_len, row0, n_heads, n_groups, dl, pad=0, fuse=None):
    cw, cb, dtb, alog, dskip, g, e3 = params
    dc = cw.shape[1]
    ds = dskip.shape[1]
    n = s0.shape[0]
    q = SSD_CHUNK
    tq = q * (SSD_CHUNKS_PER_STEP if seq_len % (q * SSD_CHUNKS_PER_STEP) == 0 else 1)
    assert seq_len % tq == 0 and row0 % tq == 0
    assert (2 * dl) % ds == 0 and (2 * dl + ds) % dc == 0 and LANES % (ds // n_heads) == 0
    n_chunks = seq_len // tq
    n_live = n_seq * n_chunks
    blk0 = row0 // tq
    fuse_out = fuse is not None
    chunk = (lambda s: jnp.minimum(s, n_live - 1)) if fuse_out else (lambda s: s)
    rows = lambda s: blk0 + chunk(s)
    const2 = lambda s: (0, 0)
    out_rows = (lambda s: (jnp.maximum(s - 1, 0), 0)) if fuse_out else (lambda s: (s, 0))
    per_seq = lambda s: (chunk(s) // n_chunks, 0, 0)
    dout, out_dtype = (fuse[0].shape[1], F32) if fuse_out else (ds, BF16)
    pipelined = [((tq, ds), F32), ((tq, dc), F32), ((tq, HEAD_PAD), F32), ((tq, dout), out_dtype)]
    resident = [((tq + HALO, dc), F32), ((tq, dc), F32), ((2 * q + SUBLANES, ds), F32), ((tq, ds), F32),
                ((3 * n, ds), F32), ((6 * HEAD_PAD, ds), BF16), ((2, q + SUBLANES, CONV_LANE_BLOCK), F32)]
    cblk = CONV_LANE_BLOCK if dc % CONV_LANE_BLOCK == 0 else LANES
    scratch = [pltpu.VMEM((tq + HALO, dc), F32), pltpu.VMEM((tq, dc), F32),
               pltpu.VMEM((2 * q + SUBLANES, ds), F32), pltpu.VMEM((tq, ds), F32), pltpu.VMEM((n, ds), F32),
               pltpu.VMEM((2, q + SUBLANES, cblk), F32), pltpu.VMEM((3, q, HEAD_PAD), F32)]
    extra_specs, extra_args = [], []
    if fuse_out:
        extra_specs = [pl.BlockSpec((ds, dout), const2, pipeline_mode=pl.Buffered(1)),
                       pl.BlockSpec((tq, dout), out_rows), pl.BlockSpec((tq, dout), out_rows)]
        extra_args = list(fuse)
        pipelined += [((tq, dout), F32)] * 2
        resident += [((ds, dout), BF16), ((2 * tq, ds), BF16)]
        scratch += [pltpu.VMEM((tq, ds), BF16), pltpu.VMEM((tq, ds), BF16)]
    kern = functools.partial(_ssd_seq_kernel, pad=pad, q=q, n_heads=n_heads, n_groups=n_groups, n_chunks=n_chunks,
                             n_live=n_live, fuse_out=fuse_out)
    return pl.pallas_call(
        kern,
        out_shape=(jax.ShapeDtypeStruct((n_seq * seq_len, dout), out_dtype),
                   jax.ShapeDtypeStruct((n_seq, n, ds), F32),
                   jax.ShapeDtypeStruct((n_seq, CONV_TAPS - 1, dc), F32)),
        grid=(n_live + 1 if fuse_out else n_live,),
        in_specs=[
            pl.BlockSpec((tq, ds), lambda s: (rows(s), (2 * dl) // ds)),
            pl.BlockSpec((tq, dc), lambda s: (rows(s), (2 * dl + ds) // dc)),
            pl.BlockSpec((tq, HEAD_PAD), lambda s: (rows(s), 0)),
            pl.BlockSpec((CONV_TAPS, dc), const2),
            pl.BlockSpec((1, dc), const2),
            pl.BlockSpec((1, HEAD_PAD), const2),
            pl.BlockSpec((1, HEAD_PAD), const2),
            pl.BlockSpec((1, ds), const2),
            pl.BlockSpec((1, ds), const2),
            pl.BlockSpec((3 * HEAD_PAD, ds), const2),
            pl.BlockSpec((n, ds), const2),
            pl.BlockSpec((HALO, dc), const2),
        ] + extra_specs,
        out_specs=(
            pl.BlockSpec((tq, dout), out_rows),
            pl.BlockSpec((1, n, ds), per_seq),
            pl.BlockSpec((1, CONV_TAPS - 1, dc), per_seq),
        ),
        scratch_shapes=scratch,
        compiler_params=pltpu.CompilerParams(
            dimension_semantics=("arbitrary",),
            vmem_limit_bytes=_vmem_limit(pipelined, resident)),
        name="ssd_seq",
    )(proj, proj, dt, cw, cb, dtb, alog, dskip, g, e3, s0, tail0, *extra_args)


def _ssd_slab_pre_kernel(xbc_ref, dt_ref, tail_ref, cw_ref, cb_ref, dtb_ref, alog_ref, dskip_ref, e3_ref,
                         ypart_ref, eace_ref, c_ref, b_ref, xw_ref, cdec_ref, tailfin_ref, xc_ref,
                         *, n_heads, n_groups):
    ls, bs, dc = xbc_ref.shape
    ds = dskip_ref.shape[1]
    gn = (dc - ds) // 2
    n = gn // n_groups
    r = n_heads // n_groups
    ntail = CONV_TAPS - 1

    cblk = CONV_LANE_BLOCK if dc % CONV_LANE_BLOCK == 0 else LANES
    for j in range(dc // cblk):
        sl = slice(j * cblk, (j + 1) * cblk)
        ext = [tail_ref[k, :, sl] for k in range(ntail)]
        ext += [xbc_ref[s, :, sl] for s in range(ls)]
        for s in range(ls):
            xc_ref[s, :, sl] = _silu(_conv_taps(ext, s, cw_ref, cb_ref, sl))
        for k in range(ntail):
            tailfin_ref[k, :, sl] = ext[ls + k]

    hmask = _head_lane_mask(n_heads)
    a_neg = -jnp.exp(alog_ref[...])
    dtv, acum = [], []
    run = jnp.zeros((bs, HEAD_PAD), F32)
    for s in range(ls):
        d = jnp.where(hmask, jax.nn.softplus(dt_ref[s] + dtb_ref[...]), 0.0)
        run = run + d * a_neg
        dtv.append(d)
        acum.append(run)
    alast = acum[ls - 1]
    cdec_ref[...] = jnp.exp(alast)
    head_group = lax.broadcasted_iota(jnp.int32, (1, HEAD_PAD), 1) // r

    for s in range(ls):
        eace_ref[s] = _expand_heads(jnp.exp(acum[s]), e3_ref)
        wend_e = _expand_heads(jnp.exp(alast - acum[s]) * dtv[s], e3_ref)
        xw_ref[:, s * ds:(s + 1) * ds] = xc_ref[s, :, 0:ds] * wend_e
        b_ref[:, s * gn:(s + 1) * gn] = xc_ref[s, :, ds:ds + gn]
        c_ref[:, s * gn:(s + 1) * gn] = xc_ref[s, :, ds + gn:ds + 2 * gn]
        ypart = dskip_ref[...] * xc_ref[s, :, 0:ds]
        for j in range(s + 1):
            cbh = jnp.zeros((bs, HEAD_PAD), F32)
            for g in range(n_groups):
                cs = xc_ref[s, :, ds + gn + g * n:ds + gn + (g + 1) * n]
                bj = xc_ref[j, :, ds + g * n:ds + (g + 1) * n]
                cbg = jnp.sum(cs * bj, axis=-1, keepdims=True)
                cbh = cbh + jnp.where(head_group == g, cbg, 0.0)
            coef = cbh * (jnp.exp(acum[s] - acum[j]) * dtv[j])
            ypart = ypart + _expand_heads(coef, e3_ref) * xc_ref[j, :, 0:ds]
        ypart_ref[s] = ypart
    for s in range(ls, SLAB_ROWS):
        xw_ref[:, s * ds:(s + 1) * ds] = jnp.zeros((bs, ds), F32)
        b_ref[:, s * gn:(s + 1) * gn] = jnp.zeros((bs, gn), F32)
        c_ref[:, s * gn:(s + 1) * gn] = jnp.zeros((bs, gn), F32)


def _ssd_slab_pre(proj3, dt3, tail, params, *, ls, n_heads, n_groups, dl):
    cw, cb, dtb, alog, dskip, _, e3 = params
    bs = proj3.shape[1]
    dc = cw.shape[1]
    ds = dskip.shape[1]
    gn = (dc - ds) // 2
    ntail = CONV_TAPS - 1
    assert ls <= SLAB_ROWS and (2 * dl + ds) % dc == 0
    c2 = lambda i: (0, 0)
    c3 = lambda i: (0, 0, 0)
    pipelined = [((ls, bs, dc), F32), ((ls, bs, HEAD_PAD), F32), ((bs, 2 * ntail * dc), F32),
                 ((2 * ls, bs, ds), F32), ((bs, SLAB_ROWS * (2 * gn + ds)), F32), ((3 * HEAD_PAD, ds), BF16)]
    kern = functools.partial(_ssd_slab_pre_kernel, n_heads=n_heads, n_groups=n_groups)
    return pl.pallas_call(
        kern,
        out_shape=(jax.ShapeDtypeStruct((ls, bs, ds), F32),
                   jax.ShapeDtypeStruct((ls, bs, ds), F32),
                   jax.ShapeDtypeStruct((bs, SLAB_ROWS * gn), F32),
                   jax.ShapeDtypeStruct((bs, SLAB_ROWS * gn), F32),
                   jax.ShapeDtypeStruct((bs, SLAB_ROWS * ds), F32),
                   jax.ShapeDtypeStruct((bs, HEAD_PAD), F32),
                   jax.ShapeDtypeStruct((ntail, bs, dc), F32)),
        grid=(1,),
        in_specs=[
            pl.BlockSpec((ls, bs, dc), lambda i: (0, 0, (2 * dl + ds) // dc)),
            pl.BlockSpec((ls, bs, HEAD_PAD), c3),
            pl.BlockSpec((ntail, bs, dc), c3),
            pl.BlockSpec((CONV_TAPS, dc), c2), pl.BlockSpec((1, dc), c2),
            pl.BlockSpec((1, HEAD_PAD), c2), pl.BlockSpec((1, HEAD_PAD), c2),
            pl.BlockSpec((1, ds), c2), pl.BlockSpec((3 * HEAD_PAD, ds), c2),
        ],
        out_specs=(pl.BlockSpec((ls, bs, ds), c3), pl.BlockSpec((ls, bs, ds), c3),
                   pl.BlockSpec((bs, SLAB_ROWS * gn), c2), pl.BlockSpec((bs, SLAB_ROWS * gn), c2),
                   pl.BlockSpec((bs, SLAB_ROWS * ds), c2), pl.BlockSpec((bs, HEAD_PAD), c2),
                   pl.BlockSpec((ntail, bs, dc), c3)),
        scratch_shapes=[pltpu.VMEM((ls, bs, dc), F32)],
        compiler_params=pltpu.CompilerParams(
            dimension_semantics=("arbitrary",),
            vmem_limit_bytes=_vmem_limit(pipelined, [((ls, bs, dc), F32)])),
        name="ssd_slab_pre",
    )(proj3, dt3, tail, cw, cb, dtb, alog, dskip, e3)


def _ssd_state_kernel(cdec_ref, s_ref, c_ref, b_ref, xw_ref, snew_ref, yoff_ref, *, n_heads, n_groups):
    i = pl.program_id(0)
    sb, hp, n = s_ref.shape
    p = hp // n_heads
    r = n_heads // n_groups
    gw = hp // n_groups
    for q in range(sb):
        for g in range(n_groups):
            gsl = slice(g * gw, (g + 1) * gw)
            sg = s_ref[q, gsl, :]
            cg = c_ref[q, :, g * n:(g + 1) * n].astype(BF16)
            yoff_ref[q, :, gsl] = lax.dot_general(cg, sg.astype(BF16), (((1,), (1,)), ((), ())),
                                                  preferred_element_type=F32)
            upd = lax.dot_general(xw_ref[q, :, gsl].astype(BF16), b_ref[q, :, g * n:(g + 1) * n].astype(BF16),
                                  (((0,), (0,)), ((), ())), preferred_element_type=F32)
            for u in range(r):
                h = g * r + u
                rows = slice(h * p, (h + 1) * p)
                snew_ref[q, rows, :] = cdec_ref[i * sb + q, h] * s_ref[q, rows, :] + upd[u * p:(u + 1) * p, :]


def _ssd_state(cdec, state, c_rows, b_rows, xw_rows, *, n_heads, n_groups):
    bs, hp, n = state.shape
    gn = c_rows.shape[2]
    sb = STATE_SEQS_PER_STEP if bs % STATE_SEQS_PER_STEP == 0 else 1
    per_seq = lambda i: (i, 0, 0)
    pipelined = [((sb, hp, n), F32)] * 2 + [((sb, SLAB_ROWS, gn), F32)] * 2 + [((sb, SLAB_ROWS, hp), F32)] * 2
    kern = functools.partial(_ssd_state_kernel, n_heads=n_heads, n_groups=n_groups)
    return pl.pallas_call(
        kern,
        out_shape=(jax.ShapeDtypeStruct((bs, hp, n), F32), jax.ShapeDtypeStruct((bs, SLAB_ROWS, hp), F32)),
        grid=(bs // sb,),
        in_specs=[
            pl.BlockSpec(memory_space=pltpu.SMEM),
            pl.BlockSpec((sb, hp, n), per_seq),
            pl.BlockSpec((sb, SLAB_ROWS, gn), per_seq),
            pl.BlockSpec((sb, SLAB_ROWS, gn), per_seq),
            pl.BlockSpec((sb, SLAB_ROWS, hp), per_seq),
        ],
        out_specs=(pl.BlockSpec((sb, hp, n), per_seq), pl.BlockSpec((sb, SLAB_ROWS, hp), per_seq)),
        compiler_params=pltpu.CompilerParams(
            dimension_semantics=("parallel",),
            vmem_limit_bytes=_vmem_limit(pipelined, [])),
        name="ssd_state",
    )(cdec, state, c_rows, b_rows, xw_rows)


def _ssd_slab_post_kernel(ypart_ref, eace_ref, yoff_ref, z_ref, g_ref, o_ref, *, n_groups):
    ls, bs, ds = ypart_ref.shape
    gw = ds // n_groups
    for s in range(ls):
        for g in range(n_groups):
            gsl = slice(g * gw, (g + 1) * gw)
            y = ypart_ref[s, :, gsl] + eace_ref[s, :, gsl] * yoff_ref[:, s * ds + g * gw:s * ds + (g + 1) * gw]
            o_ref[s, :, gsl] = _gated_group_norm(y, z_ref[s, :, gsl], g_ref[:, gsl]).astype(o_ref.dtype)


def _ssd_slab_post(ypart, eace, yoff, proj3, g, *, n_groups, dl):
    ls, bs, ds = ypart.shape
    assert (2 * dl) % ds == 0
    c2 = lambda i: (0, 0)
    c3 = lambda i: (0, 0, 0)
    pipelined = [((ls, bs, ds), F32)] * 3 + [((bs, SLAB_ROWS * ds), F32), ((ls, bs, ds), BF16)]
    kern = functools.partial(_ssd_slab_post_kernel, n_groups=n_groups)
    return pl.pallas_call(
        kern,
        out_shape=jax.ShapeDtypeStruct((ls, bs, ds), BF16),
        grid=(1,),
        in_specs=[
            pl.BlockSpec((ls, bs, ds), c3), pl.BlockSpec((ls, bs, ds), c3),
            pl.BlockSpec((bs, SLAB_ROWS * ds), c2),
            pl.BlockSpec((ls, bs, ds), lambda i: (0, 0, (2 * dl) // ds)),
            pl.BlockSpec((1, ds), c2),
        ],
        out_specs=pl.BlockSpec((ls, bs, ds), c3),
        compiler_params=pltpu.CompilerParams(
            dimension_semantics=("arbitrary",),
            vmem_limit_bytes=_vmem_limit(pipelined, [])),
        name="ssd_slab_post",
    )(ypart, eace, yoff, proj3, g)


def _head_expansion(n_heads, head_dim):
    rows = lax.broadcasted_iota(jnp.int32, (HEAD_PAD, n_heads * head_dim), 0)
    cols = lax.broadcasted_iota(jnp.int32, (HEAD_PAD, n_heads * head_dim), 1)
    e = (cols // head_dim == rows).astype(BF16)
    return jnp.concatenate([e, e, e], axis=0)


def _pad_lanes(v, width):
    return jnp.pad(v, ((0, 0), (0, width - v.shape[1])))


def _tail_block(tail):
    return jnp.pad(tail, ((HALO - tail.shape[0], 0), (0, 0)))


def _mlp(x1, g_mlp, w_up, w_down, g_final, *, layer=None):
    m = x1.shape[0]
    tiles = MATMUL_TILES
    tm = m if layer is not None else _pick_tile(m, tiles)
    up_tiles = tiles if layer is not None else (2048,) + tiles
    up = _mlp_up(x1, g_mlp, w_up, tm=tm, tn=_pick_tile(w_up.shape[-1], up_tiles), layer=layer)
    hid = up[0]
    down = _mlp_down_final(hid, w_down, x1, g_final, tm=tm, tk=_pick_tile(w_down.shape[-2], tiles), layer=layer)
    if layer is None:
        return down[0]
    return down[0], up[1], down[1]


def kernel(x_prompt, x_sample, state_lru_h, state_lru_conv, state_ssd, state_ssd_conv, meta_tokens, g_mix, w_in, conv_lru_w, conv_lru_b, lru_wa, lru_ba, lru_wx, lru_bx, lru_lambda, g_lru_out, conv_ssd_w, conv_ssd_b, dt_bias, a_log, d_skip, g_ssd_out, w_out, g_mlp, w_up, w_down, g_final):
    depth = w_in.shape[0]
    assert depth == 1, "single-layer step"
    l = 0
    bp, lp, d = x_prompt.shape
    bs, ls, _ = x_sample.shape
    n_meta = meta_tokens.shape[0]
    dl = state_lru_h.shape[-1]
    n_heads, p, n = state_ssd.shape[-3:]
    ds = n_heads * p
    dc = state_ssd_conv.shape[-1]
    gn = (dc - ds) // 2
    n_groups = gn // n
    nw = 2 * dl + ds + dc
    ntail = CONV_TAPS - 1
    q = SSD_CHUNK
    meta_pad = (-n_meta) % q
    assert n_heads <= HEAD_PAD and (bs * ls) % q == 0 and lp % q == 0 and q % bs == 0

    row = lambda v: v.reshape(1, -1).astype(F32)
    w_in_t = jnp.swapaxes(w_in, 1, 2)
    lru_params = (conv_lru_w[l], row(conv_lru_b[l]), (0.5 * lru_wa[l]).astype(BF16), (0.5 * lru_wx[l]).astype(BF16),
                  row(0.5 * lru_ba[l]), row(0.5 * lru_bx[l]), row(lru_lambda[l]), row(g_lru_out[l]))
    ssd_params = (conv_ssd_w[l], row(conv_ssd_b[l]), _pad_lanes(row(dt_bias[l]), HEAD_PAD),
                  _pad_lanes(row(a_log[l]), HEAD_PAD), row(jnp.repeat(d_skip[l], p)), row(g_ssd_out[l]),
                  _head_expansion(n_heads, p))
    g_mix_r, g_mlp_r, g_final_r = row(g_mix[l]), row(g_mlp[l]), row(g_final)

    xs_tm = x_sample.transpose(1, 0, 2).reshape(ls * bs, d)
    x_side = jnp.concatenate([xs_tm, jnp.zeros((meta_pad, d), F32), meta_tokens.astype(F32)], axis=0)
    xp_rows = x_prompt.reshape(bp * lp, d)

    tiles = MATMUL_TILES
    tn_in = next(t for t in tiles if nw % t == 0 and dl % t == 0 and ds % t == 0)
    act_cols = dict(gelu_cols=(0, dl), silu_cols=(2 * dl, 2 * dl + ds))
    proj_side, dt_side, w_in_b, w_dt_b = _in_proj(x_side, g_mix_r, w_in_t, nw=nw, n_dt=n_heads, tm=x_side.shape[0],
                                                  tn=tn_in, layer=l, **act_cols)

    proj_s3 = proj_side.reshape(-1, bs, nw)
    dt_s3 = dt_side.reshape(-1, bs, HEAD_PAD)
    lru_s, s_h, s_ltail = _lru_slab(proj_s3, lru_params, state_lru_h[l],
                                    state_lru_conv[l].transpose(1, 0, 2), ls=ls)
    ypart, eace, c_rows, b_rows, xw_rows, cdec, s_stail = _ssd_slab_pre(
        proj_s3, dt_s3, state_ssd_conv[l].transpose(1, 0, 2), ssd_params,
        ls=ls, n_heads=n_heads, n_groups=n_groups, dl=dl)
    s_new, yoff = _ssd_state(cdec, state_ssd[l].reshape(bs, ds, n),
                             c_rows.reshape(bs, SLAB_ROWS, gn), b_rows.reshape(bs, SLAB_ROWS, gn),
                             xw_rows.reshape(bs, SLAB_ROWS, ds), n_heads=n_heads, n_groups=n_groups)
    ssd_s = _ssd_slab_post(ypart, eace, yoff.reshape(bs, SLAB_ROWS * ds), proj_s3, ssd_params[5],
                           n_groups=n_groups, dl=dl)
    x1_s, w_out_lru, w_out_ssd = _out_proj(lru_s.reshape(ls * bs, dl), ssd_s.reshape(ls * bs, ds), w_out, l, xs_tm,
                                           tn=_pick_tile(d, (512, 256, 128)))
    y_s, w_up_b, w_down_b = _mlp(x1_s, g_mlp_r, w_up, w_down, g_final_r, layer=l)

    meta_row0 = ls * bs
    _, m_h, m_ltail = _lru_seq(proj_side, lru_params, jnp.zeros((1, dl), F32), jnp.zeros((HALO, dl), F32),
                               n_seq=1, seq_len=q, t=q, row0=meta_row0, pad=meta_pad, reset_first=True)
    _, m_s, m_stail = _ssd_seq(proj_side, dt_side, ssd_params, jnp.zeros((n, ds), F32), jnp.zeros((HALO, dc), F32),
                               n_seq=1, seq_len=q, row0=meta_row0, n_heads=n_heads, n_groups=n_groups, dl=dl,
                               pad=meta_pad)

    proj_p, dt_p = _in_proj(xp_rows, g_mix_r, w_in_b, w_dt_b, nw=nw, n_dt=n_heads, tm=_pick_tile(bp * lp, tiles),
                            tn=tn_in, **act_cols)
    part_p, p_h, p_ltail = _lru_seq(proj_p, lru_params, m_h[0], _tail_block(m_ltail[0]),
                                    n_seq=bp, seq_len=lp, t=_pick_tile(lp, (256, 128)), row0=0, wo=w_out_lru)
    x1_p, p_s, p_stail = _ssd_seq(proj_p, dt_p, ssd_params, m_s[0], _tail_block(m_stail[0]),
                                  n_seq=bp, seq_len=lp, row0=0, n_heads=n_heads, n_groups=n_groups, dl=dl,
                                  fuse=(w_out_ssd, part_p, xp_rows))
    y_p = _mlp(x1_p, g_mlp_r, w_up_b, w_down_b, g_final_r)

    y_prompt = y_p.reshape(bp, lp, d)
    y_sample = y_s.reshape(ls, bs, d).transpose(1, 0, 2)
    p_lru_h = p_h.reshape(1, bp, dl)
    p_lru_conv = p_ltail.reshape(1, bp, ntail, dl)
    p_ssd = p_s.transpose(0, 2, 1).reshape(1, bp, n_heads, p, n)
    p_ssd_conv = p_stail.reshape(1, bp, ntail, dc)
    s_lru_h = s_h.reshape(1, bs, dl)
    s_lru_conv = s_ltail.transpose(1, 0, 2)[None]
    s_ssd = s_new.reshape(1, bs, n_heads, p, n)
    s_ssd_conv = s_stail.transpose(1, 0, 2)[None]
    return (y_prompt, y_sample, p_lru_h, p_lru_conv, p_ssd, p_ssd_conv, s_lru_h, s_lru_conv, s_ssd, s_ssd_conv)
```

```python
import functools

import jax
import jax.numpy as jnp
from jax import lax
from jax.experimental import pallas as pl
from jax.experimental.pallas import tpu as pltpu

F32 = jnp.float32
BF16 = jnp.bfloat16

EPS = 1e-6
LRU_C = 8.0
CONV_TAPS = 4

LANES = 128
SUBLANES = 8
MXU_COLS = 256
VMEM_BYTES_V7X = 64 * 1024 * 1024
VMEM_TEMP_BYTES = 10 * 1024 * 1024
VMEM_CEILING_BYTES = VMEM_BYTES_V7X - 6 * 1024 * 1024

HALO = 2 * SUBLANES
SSD_CHUNK = 128
SSD_CHUNKS_PER_STEP = 2
HEAD_PAD = LANES
SLAB_ROWS = SUBLANES
STATE_SEQS_PER_STEP = 8
CONV_LANE_BLOCK = 128
LRU_ROW_BLOCK = 128
MATMUL_TILES = (1024, 512, 256, 128)


def _nbytes(shape, dtype):
    n = 1
    for s in shape:
        n *= s
    return n * jnp.dtype(dtype).itemsize


def _vmem_limit(pipelined, resident):
    est = 2 * sum(_nbytes(s, d) for s, d in pipelined) + sum(_nbytes(s, d) for s, d in resident)
    return int(min(est + VMEM_TEMP_BYTES, VMEM_CEILING_BYTES))


def _pick_tile(m, prefs):
    for t in prefs:
        if m % t == 0:
            return t
    return m


def _silu(x):
    h = 0.5 * x
    return h * jnp.tanh(h) + h


def _split3(x):
    hi = x.astype(BF16)
    r1 = x - hi.astype(F32)
    mid = r1.astype(BF16)
    lo = (r1 - mid.astype(F32)).astype(BF16)
    return jnp.concatenate([hi, mid, lo], axis=1)


def _expand_heads(x, e3_ref):
    return jnp.dot(_split3(x), e3_ref[...], preferred_element_type=F32)


def _rmsnorm_rows(x, g):
    ms = jnp.mean(x * x, axis=-1, keepdims=True)
    return x * lax.rsqrt(ms + EPS) * g


def _conv_rows(xe_ref, pair_ref, cw_ref, cb_ref, sl, t, r0=0):
    lo = HALO - SUBLANES + r0
    x0 = xe_ref[lo:lo + SUBLANES + t, sl]
    x1 = xe_ref[lo - 1:lo - 1 + SUBLANES + t, sl]
    pair_ref[...] = x0 * cw_ref[1:2, sl] + x1 * cw_ref[0:1, sl]
    near = x0[SUBLANES:] * cw_ref[3:4, sl] + x1[SUBLANES:] * cw_ref[2:3, sl]
    return cb_ref[:, sl] + near + pair_ref[SUBLANES - 2:SUBLANES - 2 + t, :]


def _conv_taps(ext, s, cw_ref, cb_ref, sl):
    v = cb_ref[:, sl] + ext[s] * cw_ref[0:1, sl]
    for k in range(1, CONV_TAPS):
        v = v + ext[s + k] * cw_ref[k:k + 1, sl]
    return v


def _weight_tile(w_ref, wb_ref):
    if wb_ref is None:
        return w_ref[...]
    w = w_ref[...].astype(BF16)
    wb_ref[...] = w
    return w


def _weight_specs(layer, blk, idx):
    if layer is None:
        return pl.BlockSpec(blk, idx), None
    return pl.BlockSpec((None,) + blk, lambda i, j: (layer,) + idx(i, j)), pl.BlockSpec(blk, idx)


_NT = (((1,), (1,)), ((), ()))


def _in_proj_kernel(x_ref, g_ref, w_ref, wdt_ref, o_ref, dt_ref, *rest, gelu_tiles, silu_tiles, emit_w):
    if emit_w:
        wb_ref, wdtb_ref, xn_ref = rest
    else:
        wb_ref, wdtb_ref, xn_ref = None, None, rest[0]
    j = pl.program_id(1)

    @pl.when(j == 0)
    def _():
        xn = _rmsnorm_rows(x_ref[...], g_ref[...]).astype(BF16)
        xn_ref[...] = xn
        wdt = _weight_tile(wdt_ref, wdtb_ref)
        wdt = jnp.concatenate([wdt, jnp.zeros((HEAD_PAD - wdt.shape[0], wdt.shape[1]), BF16)], axis=0)
        dt_ref[...] = lax.dot_general(xn, wdt, _NT, preferred_element_type=F32)

    def tile():
        return lax.dot_general(xn_ref[...], _weight_tile(w_ref, wb_ref), _NT, preferred_element_type=F32)

    in_range = lambda r: jnp.logical_and(j >= r[0], j < r[1])
    is_gelu, is_silu = in_range(gelu_tiles), in_range(silu_tiles)

    @pl.when(is_gelu)
    def _():
        o_ref[...] = jax.nn.gelu(tile())

    @pl.when(is_silu)
    def _():
        o_ref[...] = _silu(tile())

    @pl.when(jnp.logical_not(jnp.logical_or(is_gelu, is_silu)))
    def _():
        o_ref[...] = tile()


def _in_proj(x, g, w_t, wdt_t=None, *, nw, n_dt, tm, tn, gelu_cols, silu_cols, layer=None):
    m, k = x.shape
    emit_w = layer is not None
    assert m % tm == 0 and nw % tn == 0 and all(c % tn == 0 for c in gelu_cols + silu_cols)
    assert n_dt % SUBLANES == 0 and nw % n_dt == 0 and (not emit_w or m == tm)
    out_shape = [jax.ShapeDtypeStruct((m, nw), F32), jax.ShapeDtypeStruct((m, HEAD_PAD), F32)]
    out_specs = [pl.BlockSpec((tm, tn), lambda i, j: (i, j)), pl.BlockSpec((tm, HEAD_PAD), lambda i, j: (i, 0))]
    pipelined = [((tm, k), F32), ((tn, k), w_t.dtype), ((n_dt, k), w_t.dtype), ((tm, tn), F32), ((tm, HEAD_PAD), F32)]
    if emit_w:
        w_spec = pl.BlockSpec((None, tn, k), lambda i, j: (layer, j, 0))
        wdt_spec = pl.BlockSpec((None, n_dt, k), lambda i, j: (layer, nw // n_dt, 0))
        wdt_t = w_t
        out_shape += [jax.ShapeDtypeStruct((nw, k), BF16), jax.ShapeDtypeStruct((n_dt, k), BF16)]
        out_specs += [pl.BlockSpec((tn, k), lambda i, j: (j, 0)), pl.BlockSpec((n_dt, k), lambda i, j: (0, 0))]
        pipelined += [((tn, k), BF16), ((n_dt, k), BF16)]
    else:
        w_spec = pl.BlockSpec((tn, k), lambda i, j: (j, 0))
        wdt_spec = pl.BlockSpec((n_dt, k), lambda i, j: (0, 0))
    kern = functools.partial(_in_proj_kernel, gelu_tiles=tuple(c // tn for c in gelu_cols),
                             silu_tiles=tuple(c // tn for c in silu_cols), emit_w=emit_w)
    return pl.pallas_call(
        kern,
        out_shape=tuple(out_shape),
        grid=(m // tm, nw // tn),
        in_specs=[
            pl.BlockSpec((tm, k), lambda i, j: (i, 0)),
            pl.BlockSpec((1, k), lambda i, j: (0, 0)),
            w_spec,
            wdt_spec,
        ],
        out_specs=tuple(out_specs),
        scratch_shapes=[pltpu.VMEM((tm, k), BF16)],
        compiler_params=pltpu.CompilerParams(
            dimension_semantics=("parallel", "arbitrary"),
            vmem_limit_bytes=_vmem_limit(pipelined, [((tm, k), BF16), ((tn, k), BF16)])),
        name="in_proj",
    )(x, g, w_t, wdt_t)


def _out_proj_kernel(a1_ref, a2_ref, w1_ref, w2_ref, res_ref, o_ref, wb1_ref, wb2_ref):
    acc = jnp.dot(a1_ref[...], _weight_tile(w1_ref, wb1_ref), preferred_element_type=F32)
    acc = acc + jnp.dot(a2_ref[...], _weight_tile(w2_ref, wb2_ref), preferred_element_type=F32)
    o_ref[...] = res_ref[...] + acc


def _out_proj(a1, a2, w, layer, res, *, tn):
    m, k1 = a1.shape
    k2 = a2.shape[1]
    n = w.shape[2]
    assert n % tn == 0 and k1 == k2 and w.shape[1] == k1 + k2
    pipelined = [((m, k1), BF16), ((m, k2), BF16), ((k1, tn), w.dtype), ((k2, tn), w.dtype), ((m, tn), F32),
                 ((m, tn), F32), ((k1, tn), BF16), ((k2, tn), BF16)]
    col = lambda j: (0, j)
    return pl.pallas_call(
        _out_proj_kernel,
        out_shape=(jax.ShapeDtypeStruct((m, n), F32), jax.ShapeDtypeStruct((k1, n), BF16),
                   jax.ShapeDtypeStruct((k2, n), BF16)),
        grid=(n // tn,),
        in_specs=[
            pl.BlockSpec((m, k1), lambda j: (0, 0)),
            pl.BlockSpec((m, k2), lambda j: (0, 0)),
            pl.BlockSpec((None, k1, tn), lambda j: (layer, 0, j)),
            pl.BlockSpec((None, k2, tn), lambda j: (layer, 1, j)),
            pl.BlockSpec((m, tn), col),
        ],
        out_specs=(pl.BlockSpec((m, tn), col), pl.BlockSpec((k1, tn), col), pl.BlockSpec((k2, tn), col)),
        compiler_params=pltpu.CompilerParams(
            dimension_semantics=("parallel",),
            vmem_limit_bytes=_vmem_limit(pipelined, [])),
        name="out_proj",
    )(a1, a2, w, w, res)


def _mlp_up_kernel(x_ref, g_ref, w_ref, o_ref, *rest, emit_w):
    wb_ref, xn_ref = rest if emit_w else (None, rest[0])

    @pl.when(pl.program_id(1) == 0)
    def _():
        xn_ref[...] = _rmsnorm_rows(x_ref[...], g_ref[...]).astype(BF16)

    acc = jnp.dot(xn_ref[...], _weight_tile(w_ref, wb_ref), preferred_element_type=F32)
    o_ref[...] = jnp.square(jnp.maximum(acc, 0.0)).astype(o_ref.dtype)


def _mlp_up(x, g, w, *, tm, tn, layer=None):
    m, k = x.shape
    n = w.shape[-1]
    emit_w = layer is not None
    assert m % tm == 0 and n % tn == 0 and (not emit_w or m == tm)
    w_spec, wb_spec = _weight_specs(layer, (k, tn), lambda i, j: (0, j))
    out_shape = [jax.ShapeDtypeStruct((m, n), BF16)]
    out_specs = [pl.BlockSpec((tm, tn), lambda i, j: (i, j))]
    pipelined = [((tm, k), F32), ((k, tn), w.dtype), ((tm, tn), BF16)]
    if emit_w:
        out_shape.append(jax.ShapeDtypeStruct((k, n), BF16))
        out_specs.append(wb_spec)
        pipelined.append(((k, tn), BF16))
    return pl.pallas_call(
        functools.partial(_mlp_up_kernel, emit_w=emit_w),
        out_shape=tuple(out_shape),
        grid=(m // tm, n // tn),
        in_specs=[
            pl.BlockSpec((tm, k), lambda i, j: (i, 0)),
            pl.BlockSpec((1, k), lambda i, j: (0, 0)),
            w_spec,
        ],
        out_specs=tuple(out_specs),
        scratch_shapes=[pltpu.VMEM((tm, k), BF16)],
        compiler_params=pltpu.CompilerParams(
            dimension_semantics=("parallel", "arbitrary"),
            vmem_limit_bytes=_vmem_limit(pipelined, [((tm, k), BF16), ((k, tn), BF16)])),
        name="mlp_up",
    )(x, g, w)


def _mlp_down_kernel(h_ref, w_ref, res_ref, g_ref, o_ref, *rest, emit_w):
    wb_ref = rest[0] if emit_w else None
    kk = pl.program_id(1)

    @pl.when(kk == 0)
    def _():
        o_ref[...] = res_ref[...]

    o_ref[...] += jnp.dot(h_ref[...], _weight_tile(w_ref, wb_ref), preferred_element_type=F32)

    @pl.when(kk == pl.num_programs(1) - 1)
    def _():
        o_ref[...] = _rmsnorm_rows(o_ref[...], g_ref[...])


def _mlp_down_final(h, w, res, g, *, tm, tk, layer=None):
    m, k = h.shape
    n = w.shape[-1]
    emit_w = layer is not None
    assert m % tm == 0 and k % tk == 0 and (not emit_w or m == tm)
    w_spec, wb_spec = _weight_specs(layer, (tk, n), lambda i, j: (j, 0))
    out_shape = [jax.ShapeDtypeStruct((m, n), F32)]
    out_specs = [pl.BlockSpec((tm, n), lambda i, j: (i, 0))]
    pipelined = [((tm, tk), BF16), ((tk, n), w.dtype), ((tm, n), F32), ((tm, n), F32)]
    if emit_w:
        out_shape.append(jax.ShapeDtypeStruct((k, n), BF16))
        out_specs.append(wb_spec)
        pipelined.append(((tk, n), BF16))
    return pl.pallas_call(
        functools.partial(_mlp_down_kernel, emit_w=emit_w),
        out_shape=tuple(out_shape),
        grid=(m // tm, k // tk),
        in_specs=[
            pl.BlockSpec((tm, tk), lambda i, j: (i, j)),
            w_spec,
            pl.BlockSpec((tm, n), lambda i, j: (i, 0)),
            pl.BlockSpec((1, n), lambda i, j: (0, 0)),
        ],
        out_specs=tuple(out_specs),
        compiler_params=pltpu.CompilerParams(
            dimension_semantics=("parallel", "arbitrary"),
            vmem_limit_bytes=_vmem_limit(pipelined, [((tk, n), BF16)] if emit_w else [])),
        name="mlp_down",
    )(h, w, res, g)


def _lru_gates(xh, wa_half, wx_half, ba_half, bx_half, hsp):
    xb = xh.astype(BF16)
    tr = jnp.tanh(jnp.dot(xb, wa_half, preferred_element_type=F32) + ba_half)
    ti = jnp.tanh(jnp.dot(xb, wx_half, preferred_element_type=F32) + bx_half)
    nla = tr * hsp + hsp
    a = jnp.exp(-nla)
    q = jnp.tanh(nla) * (1.0 + a * a)
    mult = jnp.where(q > 0.0, q * lax.rsqrt(q), 0.0)
    return a, mult, 0.5 * ti + 0.5


def _scan_rows(a, b, h_prev):
    t, hd = a.shape
    g = t // SUBLANES
    a3 = a.reshape(g, SUBLANES, hd)
    b3 = b.reshape(g, SUBLANES, hd)
    sub = lax.broadcasted_iota(jnp.int32, (g, SUBLANES, hd), 1)
    d = 1
    while d < SUBLANES:
        keep = sub >= d
        a_sh = jnp.where(keep, pltpu.roll(a3, d, axis=1), 1.0)
        b_sh = jnp.where(keep, pltpu.roll(b3, d, axis=1), 0.0)
        b3 = a3 * b_sh + b3
        a3 = a3 * a_sh
        d *= 2
    tiles = []
    h = h_prev
    for k in range(g):
        hk = a3[k] * h + b3[k]
        tiles.append(hk)
        h = hk[SUBLANES - 1:SUBLANES, :]
    return jnp.concatenate(tiles, axis=0), h


def _lru_seq_kernel(gate_ref, x_ref, cw_ref, cb_ref, wa_ref, wx_ref, ba_ref, bx_ref, lam_ref, g_ref,
                    h0_ref, tail0_ref, *rest, pad, reset_first, n_chunks, n_live, fuse_out):
    if fuse_out:
        wo_ref, o_ref, hfin_ref, tailfin_ref, xe_ref, hc_ref, y_ref, pair_ref, yn_ref = rest
    else:
        o_ref, hfin_ref, tailfin_ref, xe_ref, hc_ref, y_ref, pair_ref = rest
    step = pl.program_id(0)
    live = step < n_live
    c = lax.rem(jnp.minimum(step, n_live - 1), n_chunks)
    t, dl = x_ref.shape
    nh, hd = wa_ref.shape[0], wa_ref.shape[1]
    keep = (lambda new, old: jnp.where(live, new, old)) if fuse_out else (lambda new, old: new)

    @pl.when(jnp.logical_and(c == 0, live))
    def _():
        xe_ref[0:HALO, :] = tail0_ref[...]
        hc_ref[...] = h0_ref[...]

    if fuse_out:
        @pl.when(step == 0)
        def _():
            yn_ref[...] = jnp.zeros_like(yn_ref)

    xe_ref[HALO:HALO + t, :] = x_ref[...]
    grow = c * t + lax.broadcasted_iota(jnp.int32, (t, hd), 0)
    if fuse_out:
        n_pieces = max(1, min(nh, o_ref.shape[1] // MXU_COLS))
        while nh % n_pieces or o_ref.shape[1] % n_pieces:
            n_pieces -= 1
        heads_per_piece, piece = nh // n_pieces, o_ref.shape[1] // n_pieces
    rb = min(t, LRU_ROW_BLOCK)
    ssq = [jnp.zeros((rb, hd), F32) for _ in range(t // rb)]
    for h in range(nh):
        sl = slice(h * hd, (h + 1) * hd)
        hsp = (0.5 * LRU_C) * jax.nn.softplus(-lam_ref[:, sl])
        h_prev = hc_ref[:, sl]
        h_run = h_prev
        for bi in range(t // rb):
            rows = slice(bi * rb, (bi + 1) * rb)
            xh = _conv_rows(xe_ref, pair_ref.at[(h * (t // rb) + bi) % 2], cw_ref, cb_ref, sl, rb, bi * rb)
            a, mult, i = _lru_gates(xh, wa_ref[h], wx_ref[h], ba_ref[:, sl], bx_ref[:, sl], hsp)
            if reset_first:
                mult = jnp.where(grow[rows] == pad, 1.0, mult)
            b = mult * i * xh
            if pad:
                a = jnp.where(grow[rows] >= pad, a, 1.0)
                b = jnp.where(grow[rows] >= pad, b, 0.0)
            hs, h_run = _scan_rows(a, b, h_run)
            y = hs * gate_ref[rows, sl]
            y_ref[rows, sl] = y
            ssq[bi] = ssq[bi] + y * y
        hc_ref[:, sl] = keep(h_run, h_prev)
        if fuse_out and (h + 1) % heads_per_piece == 0:
            k = (h + 1) // heads_per_piece - 1
            psl = slice(k * piece, (k + 1) * piece)
            o_ref[:, psl] = jnp.dot(yn_ref[...], wo_ref[:, psl], preferred_element_type=F32)
    scale = lax.rsqrt(jnp.sum(jnp.concatenate(ssq, axis=0), axis=-1, keepdims=True) / dl + EPS)
    yn = (y_ref[...] * scale * g_ref[...]).astype(BF16)
    if fuse_out:
        yn_ref[...] = yn
    else:
        o_ref[...] = yn
    xe_ref[0:HALO, :] = keep(xe_ref[t:t + HALO, :], xe_ref[0:HALO, :])
    hfin_ref[0] = hc_ref[...]
    tailfin_ref[0] = xe_ref[HALO - 3:HALO, :]


def _lru_seq(proj, params, h0, tail0, *, n_seq, seq_len, t, row0, pad=0, reset_first=False, wo=None):
    cw, cb, wa, wx, ba, bx, lam, g = params
    dl = cw.shape[1]
    nh, hd = wa.shape[0], wa.shape[1]
    assert seq_len % t == 0 and row0 % t == 0 and t % SUBLANES == 0
    n_chunks = seq_len // t
    n_live = n_seq * n_chunks
    blk0 = row0 // t
    fuse_out = wo is not None
    chunk = (lambda s: jnp.minimum(s, n_live - 1)) if fuse_out else (lambda s: s)
    out_chunk = (lambda s: jnp.maximum(s - 1, 0)) if fuse_out else (lambda s: s)
    const2 = lambda s: (0, 0)
    const3 = lambda s: (0, 0, 0)
    per_seq = lambda s: (chunk(s) // n_chunks, 0, 0)
    dout, out_dtype = (wo.shape[1], F32) if fuse_out else (dl, BF16)
    pipelined = [((t, dl), F32), ((t, dl), F32), ((t, dout), out_dtype)]
    pair_shape = (2, min(t, LRU_ROW_BLOCK) + SUBLANES, hd)
    resident = [((t + HALO, dl), F32), ((t, dl), F32), ((4 * nh, hd, hd), BF16), (pair_shape, F32)]
    scratch = [pltpu.VMEM((t + HALO, dl), F32), pltpu.VMEM((1, dl), F32), pltpu.VMEM((t, dl), F32),
               pltpu.VMEM(pair_shape, F32)]
    extra_specs, extra_args = [], []
    if fuse_out:
        extra_specs.append(pl.BlockSpec((dl, dout), const2))
        extra_args.append(wo)
        pipelined.append(((dl, dout), BF16))
        resident += [((t, dl), BF16)]
        scratch.append(pltpu.VMEM((t, dl), BF16))
    kern = functools.partial(_lru_seq_kernel, pad=pad, reset_first=reset_first, n_chunks=n_chunks, n_live=n_live,
                             fuse_out=fuse_out)
    return pl.pallas_call(
        kern,
        out_shape=(jax.ShapeDtypeStruct((n_seq * seq_len, dout), out_dtype),
                   jax.ShapeDtypeStruct((n_seq, 1, dl), F32),
                   jax.ShapeDtypeStruct((n_seq, CONV_TAPS - 1, dl), F32)),
        grid=(n_live + 1 if fuse_out else n_live,),
        in_specs=[
            pl.BlockSpec((t, dl), lambda s: (blk0 + chunk(s), 0)),
            pl.BlockSpec((t, dl), lambda s: (blk0 + chunk(s), 1)),
            pl.BlockSpec((CONV_TAPS, dl), const2),
            pl.BlockSpec((1, dl), const2),
            pl.BlockSpec((nh, hd, hd), const3),
            pl.BlockSpec((nh, hd, hd), const3),
            pl.BlockSpec((1, dl), const2),
            pl.BlockSpec((1, dl), const2),
            pl.BlockSpec((1, dl), const2),
            pl.BlockSpec((1, dl), const2),
            pl.BlockSpec((1, dl), const2),
            pl.BlockSpec((HALO, dl), const2),
        ] + extra_specs,
        out_specs=(
            pl.BlockSpec((t, dout), lambda s: (out_chunk(s), 0)),
            pl.BlockSpec((1, 1, dl), per_seq),
            pl.BlockSpec((1, CONV_TAPS - 1, dl), per_seq),
        ),
        scratch_shapes=scratch,
        compiler_params=pltpu.CompilerParams(
            dimension_semantics=("arbitrary",),
            vmem_limit_bytes=_vmem_limit(pipelined, resident)),
        name="lru_seq",
    )(proj, proj, cw, cb, wa, wx, ba, bx, lam, g, h0, tail0, *extra_args)


def _lru_slab_kernel(gate_ref, x_ref, cw_ref, cb_ref, wa_ref, wx_ref, ba_ref, bx_ref, lam_ref, g_ref,
                     h0_ref, tail_ref, o_ref, hfin_ref, tailfin_ref, y_ref):
    ls, bs, dl = x_ref.shape
    nh, hd = wa_ref.shape[0], wa_ref.shape[1]
    ntail = CONV_TAPS - 1
    for h in range(nh):
        sl = slice(h * hd, (h + 1) * hd)
        ext = [tail_ref[k, :, sl] for k in range(ntail)]
        ext += [x_ref[s, :, sl] for s in range(ls)]
        hsp = (0.5 * LRU_C) * jax.nn.softplus(-lam_ref[:, sl])
        hcur = h0_ref[:, sl]
        for s in range(ls):
            xh = _conv_taps(ext, s, cw_ref, cb_ref, sl)
            a, mult, i = _lru_gates(xh, wa_ref[h], wx_ref[h], ba_ref[:, sl], bx_ref[:, sl], hsp)
            hcur = a * hcur + mult * i * xh
            y_ref[s, :, sl] = hcur * gate_ref[s, :, sl]
        hfin_ref[:, sl] = hcur
        for k in range(ntail):
            tailfin_ref[k, :, sl] = ext[ls + k]
    for s in range(ls):
        y = y_ref[s]
        scale = lax.rsqrt(jnp.mean(y * y, axis=-1, keepdims=True) + EPS)
        o_ref[s] = (y * scale * g_ref[...]).astype(o_ref.dtype)


def _lru_slab(proj3, params, h0, tail, *, ls):
    cw, cb, wa, wx, ba, bx, lam, g = params
    bs = proj3.shape[1]
    dl = cw.shape[1]
    nh, hd = wa.shape[0], wa.shape[1]
    ntail = CONV_TAPS - 1
    c2 = lambda i: (0, 0)
    c3 = lambda i: (0, 0, 0)
    pipelined = [((ls, bs, dl), F32)] * 2 + [((ls, bs, dl), BF16)] + [((bs, (2 * ntail + 2) * dl), F32)]
    return pl.pallas_call(
        _lru_slab_kernel,
        out_shape=(jax.ShapeDtypeStruct((ls, bs, dl), BF16),
                   jax.ShapeDtypeStruct((bs, dl), F32),
                   jax.ShapeDtypeStruct((ntail, bs, dl), F32)),
        grid=(1,),
        in_specs=[
            pl.BlockSpec((ls, bs, dl), lambda i: (0, 0, 0)),
            pl.BlockSpec((ls, bs, dl), lambda i: (0, 0, 1)),
            pl.BlockSpec((CONV_TAPS, dl), c2), pl.BlockSpec((1, dl), c2),
            pl.BlockSpec((nh, hd, hd), c3), pl.BlockSpec((nh, hd, hd), c3),
            pl.BlockSpec((1, dl), c2), pl.BlockSpec((1, dl), c2), pl.BlockSpec((1, dl), c2), pl.BlockSpec((1, dl), c2),
            pl.BlockSpec((bs, dl), c2), pl.BlockSpec((ntail, bs, dl), c3),
        ],
        out_specs=(pl.BlockSpec((ls, bs, dl), c3), pl.BlockSpec((bs, dl), c2), pl.BlockSpec((ntail, bs, dl), c3)),
        scratch_shapes=[pltpu.VMEM((ls, bs, dl), F32)],
        compiler_params=pltpu.CompilerParams(
            dimension_semantics=("arbitrary",),
            vmem_limit_bytes=_vmem_limit(pipelined, [((ls, bs, dl), F32)])),
        name="lru_slab",
    )(proj3, proj3, cw, cb, wa, wx, ba, bx, lam, g, h0, tail)


def _head_lane_mask(n_heads):
    return lax.broadcasted_iota(jnp.int32, (1, HEAD_PAD), 1) < n_heads


def _gated_group_norm(y, z_act, g):
    yg = y * z_act
    scale = lax.rsqrt(jnp.mean(yg * yg, axis=-1, keepdims=True) + EPS)
    return yg * scale * g


def _ssd_seq_kernel(z_ref, xbc_ref, dt_ref, cw_ref, cb_ref, dtb_ref, alog_ref, dskip_ref, g_ref, e3_ref,
                    s0_ref, tail0_ref, *rest, pad, q, n_heads, n_groups, n_chunks, n_live, fuse_out):
    if fuse_out:
        wo_ref, part_ref, res_ref = rest[:3]
        o_ref, sfin_ref, tailfin_ref, xe_ref, xc_ref, ex_ref, y_ref, s_ref, pair_ref, hv_ref, yn_ref, ynp_ref = rest[3:]
    else:
        o_ref, sfin_ref, tailfin_ref, xe_ref, xc_ref, ex_ref, y_ref, s_ref, pair_ref, hv_ref = rest
        yn_ref = o_ref
    step = pl.program_id(0)
    live = step < n_live
    c = lax.rem(jnp.minimum(step, n_live - 1), n_chunks)
    keep = (lambda new, old: jnp.where(live, new, old)) if fuse_out else (lambda new, old: new)
    tq, ds = z_ref.shape
    cps = tq // q
    dc = xbc_ref.shape[1]
    gn = (dc - ds) // 2
    n = gn // n_groups
    p = ds // n_heads
    r = n_heads // n_groups
    gw = ds // n_groups
    hpb = LANES // p

    @pl.when(jnp.logical_and(c == 0, live))
    def _():
        xe_ref[0:HALO, :] = tail0_ref[...]
        s_ref[...] = s0_ref[...]

    if fuse_out:
        @pl.when(step == 0)
        def _():
            ynp_ref[...] = jnp.zeros_like(ynp_ref)

    xe_ref[HALO:HALO + tq, :] = xbc_ref[...]
    valid_all = (c * tq + lax.broadcasted_iota(jnp.int32, (tq, 1), 0)) >= pad

    cblk = CONV_LANE_BLOCK if dc % CONV_LANE_BLOCK == 0 else LANES
    for j in range(dc // cblk):
        sl = slice(j * cblk, (j + 1) * cblk)
        for u in range(cps):
            rows = slice(u * q, (u + 1) * q)
            v = _silu(_conv_rows(xe_ref, pair_ref.at[(j * cps + u) % 2], cw_ref, cb_ref, sl, q, u * q))
            if pad and (j + 1) * cblk <= ds:
                v = jnp.where(valid_all[rows], v, 0.0)
            xc_ref[rows, sl] = v

    ri = lax.broadcasted_iota(jnp.int32, (q, q), 0)
    ci = lax.broadcasted_iota(jnp.int32, (q, q), 1)
    causal = ci <= ri
    tri = jnp.where(causal, 1.0, 0.0).astype(BF16)
    lane = lax.broadcasted_iota(jnp.int32, (q, LANES), 1)
    a_neg = -jnp.exp(alog_ref[...])
    n_pieces = cps * n_groups
    piece = o_ref.shape[1] // n_pieces if fuse_out else 0

    for u in range(cps):
        rows = slice(u * q, (u + 1) * q)
        dtv = jnp.where(_head_lane_mask(n_heads), jax.nn.softplus(dt_ref[rows, :] + dtb_ref[...]), 0.0)
        if pad:
            dtv = jnp.where(valid_all[u * q:(u + 1) * q], dtv, 0.0)
        ac3 = jnp.dot(tri, _split3(dtv * a_neg), preferred_element_type=F32)
        acum = ac3[:, 0:HEAD_PAD] + ac3[:, HEAD_PAD:2 * HEAD_PAD] + ac3[:, 2 * HEAD_PAD:3 * HEAD_PAD]
        alast = acum[q - 1:q, :]
        eac = jnp.exp(acum)
        wend = jnp.exp(alast - acum) * dtv
        cdec = jnp.broadcast_to(jnp.exp(alast), (SUBLANES, HEAD_PAD))
        ex_ref[...] = _expand_heads(jnp.concatenate([eac, wend, cdec], axis=0), e3_ref)
        hv_ref[0] = acum
        hv_ref[1] = acum.T
        hv_ref[2] = dtv.T

        for g in range(n_groups):
            gsl = slice(g * gw, (g + 1) * gw)
            bg = xc_ref[rows, ds + g * n:ds + (g + 1) * n].astype(BF16)
            cg = xc_ref[rows, ds + gn + g * n:ds + gn + (g + 1) * n].astype(BF16)
            cbm = lax.dot_general(cg, bg, _NT, preferred_element_type=F32)
            yoff = jnp.dot(cg, s_ref[:, gsl].astype(BF16), preferred_element_type=F32)
            for k in range(gw // LANES):
                lsl = slice(g * gw + k * LANES, g * gw + (k + 1) * LANES)
                xs = xc_ref[rows, lsl]
                ms = []
                xparts = []
                for w in range(hpb):
                    h = g * r + k * hpb + w
                    seg = hv_ref[0, :, h:h + 1] - hv_ref[1, h:h + 1, :]
                    lm = jnp.where(causal, jnp.exp(seg), 0.0) * hv_ref[2, h:h + 1, :]
                    ms.append((cbm * lm).astype(BF16))
                    inhead = (lane >= w * p) & (lane < (w + 1) * p)
                    xparts.append(jnp.where(inhead, xs, 0.0).astype(BF16))
                ydiag = jnp.dot(jnp.concatenate(ms, axis=1), jnp.concatenate(xparts, axis=0),
                                preferred_element_type=F32)
                y = ydiag + yoff[:, k * LANES:(k + 1) * LANES] * ex_ref[0:q, lsl]
                yg = (y + dskip_ref[:, lsl] * xs) * z_ref[rows, lsl]
                y_ref[rows, lsl] = yg
                ssq = yg * yg if k == 0 else ssq + yg * yg
            scale = lax.rsqrt(jnp.sum(ssq, axis=-1, keepdims=True) / gw + EPS)
            for k in range(gw // LANES):
                lsl = slice(g * gw + k * LANES, g * gw + (k + 1) * LANES)
                yn_ref[rows, lsl] = (y_ref[rows, lsl] * scale * g_ref[:, lsl]).astype(BF16)
            xw = (xc_ref[rows, gsl] * ex_ref[q:2 * q, gsl]).astype(BF16)
            upd = lax.dot_general(bg, xw, (((0,), (0,)), ((), ())), preferred_element_type=F32)
            s_old = s_ref[:, gsl]
            s_ref[:, gsl] = keep(ex_ref[2 * q:2 * q + 1, gsl] * s_old + upd, s_old)
            if fuse_out:
                psl = slice((u * n_groups + g) * piece, (u * n_groups + g + 1) * piece)
                mix = part_ref[:, psl] + jnp.dot(ynp_ref[...], wo_ref[:, psl], preferred_element_type=F32)
                o_ref[:, psl] = res_ref[:, psl] + mix

    if fuse_out:
        ynp_ref[...] = yn_ref[...]
    xe_ref[0:HALO, :] = keep(xe_ref[tq:tq + HALO, :], xe_ref[0:HALO, :])
    sfin_ref[0] = s_ref[...]
    tailfin_ref[0] = xe_ref[HALO - 3:HALO, :]


def _ssd_seq(proj, dt, params, s0, tail0, *, n_seq, seq_len, row0, n_heads, n_groups, dl, pad=0, fuse=None):
    cw, cb, dtb, alog, dskip, g, e3 = params
    dc = cw.shape[1]
    ds = dskip.shape[1]
    n = s0.shape[0]
    q = SSD_CHUNK
    tq = q * (SSD_CHUNKS_PER_STEP if seq_len % (q * SSD_CHUNKS_PER_STEP) == 0 else 1)
    assert seq_len % tq == 0 and row0 % tq == 0
    assert (2 * dl) % ds == 0 and (2 * dl + ds) % dc == 0 and LANES % (ds // n_heads) == 0
    n_chunks = seq_len // tq
    n_live = n_seq * n_chunks
    blk0 = row0 // tq
    fuse_out = fuse is not None
    chunk = (lambda s: jnp.minimum(s, n_live - 1)) if fuse_out else (lambda s: s)
    rows = lambda s: blk0 + chunk(s)
    const2 = lambda s: (0, 0)
    out_rows = (lambda s: (jnp.maximum(s - 1, 0), 0)) if fuse_out else (lambda s: (s, 0))
    per_seq = lambda s: (chunk(s) // n_chunks, 0, 0)
    dout, out_dtype = (fuse[0].shape[1], F32) if fuse_out else (ds, BF16)
    pipelined = [((tq, ds), F32), ((tq, dc), F32), ((tq, HEAD_PAD), F32), ((tq, dout), out_dtype)]
    resident = [((tq + HALO, dc), F32), ((tq, dc), F32), ((2 * q + SUBLANES, ds), F32), ((tq, ds), F32),
                ((3 * n, ds), F32), ((6 * HEAD_PAD, ds), BF16), ((2, q + SUBLANES, CONV_LANE_BLOCK), F32)]
    cblk = CONV_LANE_BLOCK if dc % CONV_LANE_BLOCK == 0 else LANES
    scratch = [pltpu.VMEM((tq + HALO, dc), F32), pltpu.VMEM((tq, dc), F32),
               pltpu.VMEM((2 * q + SUBLANES, ds), F32), pltpu.VMEM((tq, ds), F32), pltpu.VMEM((n, ds), F32),
               pltpu.VMEM((2, q + SUBLANES, cblk), F32), pltpu.VMEM((3, q, HEAD_PAD), F32)]
    extra_specs, extra_args = [], []
    if fuse_out:
        extra_specs = [pl.BlockSpec((ds, dout), const2, pipeline_mode=pl.Buffered(1)),
                       pl.BlockSpec((tq, dout), out_rows), pl.BlockSpec((tq, dout), out_rows)]
        extra_args = list(fuse)
        pipelined += [((tq, dout), F32)] * 2
        resident += [((ds, dout), BF16), ((2 * tq, ds), BF16)]
        scratch += [pltpu.VMEM((tq, ds), BF16), pltpu.VMEM((tq, ds), BF16)]
    kern = functools.partial(_ssd_seq_kernel, pad=pad, q=q, n_heads=n_heads, n_groups=n_groups, n_chunks=n_chunks,
                             n_live=n_live, fuse_out=fuse_out)
    return pl.pallas_call(
        kern,
        out_shape=(jax.ShapeDtypeStruct((n_seq * seq_len, dout), out_dtype),
                   jax.ShapeDtypeStruct((n_seq, n, ds), F32),
                   jax.ShapeDtypeStruct((n_seq, CONV_TAPS - 1, dc), F32)),
        grid=(n_live + 1 if fuse_out else n_live,),
        in_specs=[
            pl.BlockSpec((tq, ds), lambda s: (rows(s), (2 * dl) // ds)),
            pl.BlockSpec((tq, dc), lambda s: (rows(s), (2 * dl + ds) // dc)),
            pl.BlockSpec((tq, HEAD_PAD), lambda s: (rows(s), 0)),
            pl.BlockSpec((CONV_TAPS, dc), const2),
            pl.BlockSpec((1, dc), const2),
            pl.BlockSpec((1, HEAD_PAD), const2),
            pl.BlockSpec((1, HEAD_PAD), const2),
            pl.BlockSpec((1, ds), const2),
            pl.BlockSpec((1, ds), const2),
            pl.BlockSpec((3 * HEAD_PAD, ds), const2),
            pl.BlockSpec((n, ds), const2),
            pl.BlockSpec((HALO, dc), const2),
        ] + extra_specs,
        out_specs=(
            pl.BlockSpec((tq, dout), out_rows),
            pl.BlockSpec((1, n, ds), per_seq),
            pl.BlockSpec((1, CONV_TAPS - 1, dc), per_seq),
        ),
        scratch_shapes=scratch,
        compiler_params=pltpu.CompilerParams(
            dimension_semantics=("arbitrary",),
            vmem_limit_bytes=_vmem_limit(pipelined, resident)),
        name="ssd_seq",
    )(proj, proj, dt, cw, cb, dtb, alog, dskip, g, e3, s0, tail0, *extra_args)


def _ssd_slab_pre_kernel(xbc_ref, dt_ref, tail_ref, cw_ref, cb_ref, dtb_ref, alog_ref, dskip_ref, e3_ref,
                         ypart_ref, eace_ref, c_ref, b_ref, xw_ref, cdec_ref, tailfin_ref, xc_ref,
                         *, n_heads, n_groups):
    ls, bs, dc = xbc_ref.shape
    ds = dskip_ref.shape[1]
    gn = (dc - ds) // 2
    n = gn // n_groups
    r = n_heads // n_groups
    ntail = CONV_TAPS - 1

    cblk = CONV_LANE_BLOCK if dc % CONV_LANE_BLOCK == 0 else LANES
    for j in range(dc // cblk):
        sl = slice(j * cblk, (j + 1) * cblk)
        ext = [tail_ref[k, :, sl] for k in range(ntail)]
        ext += [xbc_ref[s, :, sl] for s in range(ls)]
        for s in range(ls):
            xc_ref[s, :, sl] = _silu(_conv_taps(ext, s, cw_ref, cb_ref, sl))
        for k in range(ntail):
            tailfin_ref[k, :, sl] = ext[ls + k]

    hmask = _head_lane_mask(n_heads)
    a_neg = -jnp.exp(alog_ref[...])
    dtv, acum = [], []
    run = jnp.zeros((bs, HEAD_PAD), F32)
    for s in range(ls):
        d = jnp.where(hmask, jax.nn.softplus(dt_ref[s] + dtb_ref[...]), 0.0)
        run = run + d * a_neg
        dtv.append(d)
        acum.append(run)
    alast = acum[ls - 1]
    cdec_ref[...] = jnp.exp(alast)
    head_group = lax.broadcasted_iota(jnp.int32, (1, HEAD_PAD), 1) // r

    for s in range(ls):
        eace_ref[s] = _expand_heads(jnp.exp(acum[s]), e3_ref)
        wend_e = _expand_heads(jnp.exp(alast - acum[s]) * dtv[s], e3_ref)
        xw_ref[:, s * ds:(s + 1) * ds] = xc_ref[s, :, 0:ds] * wend_e
        b_ref[:, s * gn:(s + 1) * gn] = xc_ref[s, :, ds:ds + gn]
        c_ref[:, s * gn:(s + 1) * gn] = xc_ref[s, :, ds + gn:ds + 2 * gn]
        ypart = dskip_ref[...] * xc_ref[s, :, 0:ds]
        for j in range(s + 1):
            cbh = jnp.zeros((bs, HEAD_PAD), F32)
            for g in range(n_groups):
                cs = xc_ref[s, :, ds + gn + g * n:ds + gn + (g + 1) * n]
                bj = xc_ref[j, :, ds + g * n:ds + (g + 1) * n]
                cbg = jnp.sum(cs * bj, axis=-1, keepdims=True)
                cbh = cbh + jnp.where(head_group == g, cbg, 0.0)
            coef = cbh * (jnp.exp(acum[s] - acum[j]) * dtv[j])
            ypart = ypart + _expand_heads(coef, e3_ref) * xc_ref[j, :, 0:ds]
        ypart_ref[s] = ypart
    for s in range(ls, SLAB_ROWS):
        xw_ref[:, s * ds:(s + 1) * ds] = jnp.zeros((bs, ds), F32)
        b_ref[:, s * gn:(s + 1) * gn] = jnp.zeros((bs, gn), F32)
        c_ref[:, s * gn:(s + 1) * gn] = jnp.zeros((bs, gn), F32)


def _ssd_slab_pre(proj3, dt3, tail, params, *, ls, n_heads, n_groups, dl):
    cw, cb, dtb, alog, dskip, _, e3 = params
    bs = proj3.shape[1]
    dc = cw.shape[1]
    ds = dskip.shape[1]
    gn = (dc - ds) // 2
    ntail = CONV_TAPS - 1
    assert ls <= SLAB_ROWS and (2 * dl + ds) % dc == 0
    c2 = lambda i: (0, 0)
    c3 = lambda i: (0, 0, 0)
    pipelined = [((ls, bs, dc), F32), ((ls, bs, HEAD_PAD), F32), ((bs, 2 * ntail * dc), F32),
                 ((2 * ls, bs, ds), F32), ((bs, SLAB_ROWS * (2 * gn + ds)), F32), ((3 * HEAD_PAD, ds), BF16)]
    kern = functools.partial(_ssd_slab_pre_kernel, n_heads=n_heads, n_groups=n_groups)
    return pl.pallas_call(
        kern,
        out_shape=(jax.ShapeDtypeStruct((ls, bs, ds), F32),
                   jax.ShapeDtypeStruct((ls, bs, ds), F32),
                   jax.ShapeDtypeStruct((bs, SLAB_ROWS * gn), F32),
                   jax.ShapeDtypeStruct((bs, SLAB_ROWS * gn), F32),
                   jax.ShapeDtypeStruct((bs, SLAB_ROWS * ds), F32),
                   jax.ShapeDtypeStruct((bs, HEAD_PAD), F32),
                   jax.ShapeDtypeStruct((ntail, bs, dc), F32)),
        grid=(1,),
        in_specs=[
            pl.BlockSpec((ls, bs, dc), lambda i: (0, 0, (2 * dl + ds) // dc)),
            pl.BlockSpec((ls, bs, HEAD_PAD), c3),
            pl.BlockSpec((ntail, bs, dc), c3),
            pl.BlockSpec((CONV_TAPS, dc), c2), pl.BlockSpec((1, dc), c2),
            pl.BlockSpec((1, HEAD_PAD), c2), pl.BlockSpec((1, HEAD_PAD), c2),
            pl.BlockSpec((1, ds), c2), pl.BlockSpec((3 * HEAD_PAD, ds), c2),
        ],
        out_specs=(pl.BlockSpec((ls, bs, ds), c3), pl.BlockSpec((ls, bs, ds), c3),
                   pl.BlockSpec((bs, SLAB_ROWS * gn), c2), pl.BlockSpec((bs, SLAB_ROWS * gn), c2),
                   pl.BlockSpec((bs, SLAB_ROWS * ds), c2), pl.BlockSpec((bs, HEAD_PAD), c2),
                   pl.BlockSpec((ntail, bs, dc), c3)),
        scratch_shapes=[pltpu.VMEM((ls, bs, dc), F32)],
        compiler_params=pltpu.CompilerParams(
            dimension_semantics=("arbitrary",),
            vmem_limit_bytes=_vmem_limit(pipelined, [((ls, bs, dc), F32)])),
        name="ssd_slab_pre",
    )(proj3, dt3, tail, cw, cb, dtb, alog, dskip, e3)


def _ssd_state_kernel(cdec_ref, s_ref, c_ref, b_ref, xw_ref, snew_ref, yoff_ref, *, n_heads, n_groups):
    i = pl.program_id(0)
    sb, hp, n = s_ref.shape
    p = hp // n_heads
    r = n_heads // n_groups
    gw = hp // n_groups
    for q in range(sb):
        for g in range(n_groups):
            gsl = slice(g * gw, (g + 1) * gw)
            sg = s_ref[q, gsl, :]
            cg = c_ref[q, :, g * n:(g + 1) * n].astype(BF16)
            yoff_ref[q, :, gsl] = lax.dot_general(cg, sg.astype(BF16), (((1,), (1,)), ((), ())),
                                                  preferred_element_type=F32)
            upd = lax.dot_general(xw_ref[q, :, gsl].astype(BF16), b_ref[q, :, g * n:(g + 1) * n].astype(BF16),
                                  (((0,), (0,)), ((), ())), preferred_element_type=F32)
            for u in range(r):
                h = g * r + u
                rows = slice(h * p, (h + 1) * p)
                snew_ref[q, rows, :] = cdec_ref[i * sb + q, h] * s_ref[q, rows, :] + upd[u * p:(u + 1) * p, :]


def _ssd_state(cdec, state, c_rows, b_rows, xw_rows, *, n_heads, n_groups):
    bs, hp, n = state.shape
    gn = c_rows.shape[2]
    sb = STATE_SEQS_PER_STEP if bs % STATE_SEQS_PER_STEP == 0 else 1
    per_seq = lambda i: (i, 0, 0)
    pipelined = [((sb, hp, n), F32)] * 2 + [((sb, SLAB_ROWS, gn), F32)] * 2 + [((sb, SLAB_ROWS, hp), F32)] * 2
    kern = functools.partial(_ssd_state_kernel, n_heads=n_heads, n_groups=n_groups)
    return pl.pallas_call(
        kern,
        out_shape=(jax.ShapeDtypeStruct((bs, hp, n), F32), jax.ShapeDtypeStruct((bs, SLAB_ROWS, hp), F32)),
        grid=(bs // sb,),
        in_specs=[
            pl.BlockSpec(memory_space=pltpu.SMEM),
            pl.BlockSpec((sb, hp, n), per_seq),
            pl.BlockSpec((sb, SLAB_ROWS, gn), per_seq),
            pl.BlockSpec((sb, SLAB_ROWS, gn), per_seq),
            pl.BlockSpec((sb, SLAB_ROWS, hp), per_seq),
        ],
        out_specs=(pl.BlockSpec((sb, hp, n), per_seq), pl.BlockSpec((sb, SLAB_ROWS, hp), per_seq)),
        compiler_params=pltpu.CompilerParams(
            dimension_semantics=("parallel",),
            vmem_limit_bytes=_vmem_limit(pipelined, [])),
        name="ssd_state",
    )(cdec, state, c_rows, b_rows, xw_rows)


def _ssd_slab_post_kernel(ypart_ref, eace_ref, yoff_ref, z_ref, g_ref, o_ref, *, n_groups):
    ls, bs, ds = ypart_ref.shape
    gw = ds // n_groups
    for s in range(ls):
        for g in range(n_groups):
            gsl = slice(g * gw, (g + 1) * gw)
            y = ypart_ref[s, :, gsl] + eace_ref[s, :, gsl] * yoff_ref[:, s * ds + g * gw:s * ds + (g + 1) * gw]
            o_ref[s, :, gsl] = _gated_group_norm(y, z_ref[s, :, gsl], g_ref[:, gsl]).astype(o_ref.dtype)


def _ssd_slab_post(ypart, eace, yoff, proj3, g, *, n_groups, dl):
    ls, bs, ds = ypart.shape
    assert (2 * dl) % ds == 0
    c2 = lambda i: (0, 0)
    c3 = lambda i: (0, 0, 0)
    pipelined = [((ls, bs, ds), F32)] * 3 + [((bs, SLAB_ROWS * ds), F32), ((ls, bs, ds), BF16)]
    kern = functools.partial(_ssd_slab_post_kernel, n_groups=n_groups)
    return pl.pallas_call(
        kern,
        out_shape=jax.ShapeDtypeStruct((ls, bs, ds), BF16),
        grid=(1,),
        in_specs=[
            pl.BlockSpec((ls, bs, ds), c3), pl.BlockSpec((ls, bs, ds), c3),
            pl.BlockSpec((bs, SLAB_ROWS * ds), c2),
            pl.BlockSpec((ls, bs, ds), lambda i: (0, 0, (2 * dl) // ds)),
            pl.BlockSpec((1, ds), c2),
        ],
        out_specs=pl.BlockSpec((ls, bs, ds), c3),
        compiler_params=pltpu.CompilerParams(
            dimension_semantics=("arbitrary",),
            vmem_limit_bytes=_vmem_limit(pipelined, [])),
        name="ssd_slab_post",
    )(ypart, eace, yoff, proj3, g)


def _head_expansion(n_heads, head_dim):
    rows = lax.broadcasted_iota(jnp.int32, (HEAD_PAD, n_heads * head_dim), 0)
    cols = lax.broadcasted_iota(jnp.int32, (HEAD_PAD, n_heads * head_dim), 1)
    e = (cols // head_dim == rows).astype(BF16)
    return jnp.concatenate([e, e, e], axis=0)


def _pad_lanes(v, width):
    return jnp.pad(v, ((0, 0), (0, width - v.shape[1])))


def _tail_block(tail):
    return jnp.pad(tail, ((HALO - tail.shape[0], 0), (0, 0)))


def _mlp(x1, g_mlp, w_up, w_down, g_final, *, layer=None):
    m = x1.shape[0]
    tiles = MATMUL_TILES
    tm = m if layer is not None else _pick_tile(m, tiles)
    up_tiles = tiles if layer is not None else (2048,) + tiles
    up = _mlp_up(x1, g_mlp, w_up, tm=tm, tn=_pick_tile(w_up.shape[-1], up_tiles), layer=layer)
    hid = up[0]
    down = _mlp_down_final(hid, w_down, x1, g_final, tm=tm, tk=_pick_tile(w_down.shape[-2], tiles), layer=layer)
    if layer is None:
        return down[0]
    return down[0], up[1], down[1]


def kernel(x_prompt, x_sample, state_lru_h, state_lru_conv, state_ssd, state_ssd_conv, meta_tokens, g_mix, w_in, conv_lru_w, conv_lru_b, lru_wa, lru_ba, lru_wx, lru_bx, lru_lambda, g_lru_out, conv_ssd_w, conv_ssd_b, dt_bias, a_log, d_skip, g_ssd_out, w_out, g_mlp, w_up, w_down, g_final):
    depth = w_in.shape[0]
    assert depth == 1, "single-layer step"
    l = 0
    bp, lp, d = x_prompt.shape
    bs, ls, _ = x_sample.shape
    n_meta = meta_tokens.shape[0]
    dl = state_lru_h.shape[-1]
    n_heads, p, n = state_ssd.shape[-3:]
    ds = n_heads * p
    dc = state_ssd_conv.shape[-1]
    gn = (dc - ds) // 2
    n_groups = gn // n
    nw = 2 * dl + ds + dc
    ntail = CONV_TAPS - 1
    q = SSD_CHUNK
    meta_pad = (-n_meta) % q
    assert n_heads <= HEAD_PAD and (bs * ls) % q == 0 and lp % q == 0 and q % bs == 0

    row = lambda v: v.reshape(1, -1).astype(F32)
    w_in_t = jnp.swapaxes(w_in, 1, 2)
    lru_params = (conv_lru_w[l], row(conv_lru_b[l]), (0.5 * lru_wa[l]).astype(BF16), (0.5 * lru_wx[l]).astype(BF16),
                  row(0.5 * lru_ba[l]), row(0.5 * lru_bx[l]), row(lru_lambda[l]), row(g_lru_out[l]))
    ssd_params = (conv_ssd_w[l], row(conv_ssd_b[l]), _pad_lanes(row(dt_bias[l]), HEAD_PAD),
                  _pad_lanes(row(a_log[l]), HEAD_PAD), row(jnp.repeat(d_skip[l], p)), row(g_ssd_out[l]),
                  _head_expansion(n_heads, p))
    g_mix_r, g_mlp_r, g_final_r = row(g_mix[l]), row(g_mlp[l]), row(g_final)

    xs_tm = x_sample.transpose(1, 0, 2).reshape(ls * bs, d)
    x_side = jnp.concatenate([xs_tm, jnp.zeros((meta_pad, d), F32), meta_tokens.astype(F32)], axis=0)
    xp_rows = x_prompt.reshape(bp * lp, d)

    tiles = MATMUL_TILES
    tn_in = next(t for t in tiles if nw % t == 0 and dl % t == 0 and ds % t == 0)
    act_cols = dict(gelu_cols=(0, dl), silu_cols=(2 * dl, 2 * dl + ds))
    proj_side, dt_side, w_in_b, w_dt_b = _in_proj(x_side, g_mix_r, w_in_t, nw=nw, n_dt=n_heads, tm=x_side.shape[0],
                                                  tn=tn_in, layer=l, **act_cols)

    proj_s3 = proj_side.reshape(-1, bs, nw)
    dt_s3 = dt_side.reshape(-1, bs, HEAD_PAD)
    lru_s, s_h, s_ltail = _lru_slab(proj_s3, lru_params, state_lru_h[l],
                                    state_lru_conv[l].transpose(1, 0, 2), ls=ls)
    ypart, eace, c_rows, b_rows, xw_rows, cdec, s_stail = _ssd_slab_pre(
        proj_s3, dt_s3, state_ssd_conv[l].transpose(1, 0, 2), ssd_params,
        ls=ls, n_heads=n_heads, n_groups=n_groups, dl=dl)
    s_new, yoff = _ssd_state(cdec, state_ssd[l].reshape(bs, ds, n),
                             c_rows.reshape(bs, SLAB_ROWS, gn), b_rows.reshape(bs, SLAB_ROWS, gn),
                             xw_rows.reshape(bs, SLAB_ROWS, ds), n_heads=n_heads, n_groups=n_groups)
    ssd_s = _ssd_slab_post(ypart, eace, yoff.reshape(bs, SLAB_ROWS * ds), proj_s3, ssd_params[5],
                           n_groups=n_groups, dl=dl)
    x1_s, w_out_lru, w_out_ssd = _out_proj(lru_s.reshape(ls * bs, dl), ssd_s.reshape(ls * bs, ds), w_out, l, xs_tm,
                                           tn=_pick_tile(d, (512, 256, 128)))
    y_s, w_up_b, w_down_b = _mlp(x1_s, g_mlp_r, w_up, w_down, g_final_r, layer=l)

    meta_row0 = ls * bs
    _, m_h, m_ltail = _lru_seq(proj_side, lru_params, jnp.zeros((1, dl), F32), jnp.zeros((HALO, dl), F32),
                               n_seq=1, seq_len=q, t=q, row0=meta_row0, pad=meta_pad, reset_first=True)
    _, m_s, m_stail = _ssd_seq(proj_side, dt_side, ssd_params, jnp.zeros((n, ds), F32), jnp.zeros((HALO, dc), F32),
                               n_seq=1, seq_len=q, row0=meta_row0, n_heads=n_heads, n_groups=n_groups, dl=dl,
                               pad=meta_pad)

    proj_p, dt_p = _in_proj(xp_rows, g_mix_r, w_in_b, w_dt_b, nw=nw, n_dt=n_heads, tm=_pick_tile(bp * lp, tiles),
                            tn=tn_in, **act_cols)
    part_p, p_h, p_ltail = _lru_seq(proj_p, lru_params, m_h[0], _tail_block(m_ltail[0]),
                                    n_seq=bp, seq_len=lp, t=_pick_tile(lp, (256, 128)), row0=0, wo=w_out_lru)
    x1_p, p_s, p_stail = _ssd_seq(proj_p, dt_p, ssd_params, m_s[0], _tail_block(m_stail[0]),
                                  n_seq=bp, seq_len=lp, row0=0, n_heads=n_heads, n_groups=n_groups, dl=dl,
                                  fuse=(w_out_ssd, part_p, xp_rows))
    y_p = _mlp(x1_p, g_mlp_r, w_up_b, w_down_b, g_final_r)

    y_prompt = y_p.reshape(bp, lp, d)
    y_sample = y_s.reshape(ls, bs, d).transpose(1, 0, 2)
    p_lru_h = p_h.reshape(1, bp, dl)
    p_lru_conv = p_ltail.reshape(1, bp, ntail, dl)
    p_ssd = p_s.transpose(0, 2, 1).reshape(1, bp, n_heads, p, n)
    p_ssd_conv = p_stail.reshape(1, bp, ntail, dc)
    s_lru_h = s_h.reshape(1, bs, dl)
    s_lru_conv = s_ltail.transpose(1, 0, 2)[None]
    s_ssd = s_new.reshape(1, bs, n_heads, p, n)
    s_ssd_conv = s_stail.transpose(1, 0, 2)[None]
    return (y_prompt, y_sample, p_lru_h, p_lru_conv, p_ssd, p_ssd_conv, s_lru_h, s_lru_conv, s_ssd, s_ssd_conv)
```
